```python
import math
import jax
import jax.numpy as jnp
from jax import lax
import numpy as np

D_MODEL = 2048
BATCH = 8
SEQ = 8192
DEPTH = 2

CTX_LEN = 256
GRID_W = 64
N_BRANCH = 3
BRANCH_WIDTH = 1024
N_HEADS = 8
N_KV_HEADS = 2
HEAD_DIM = 128
GROUP = N_HEADS // N_KV_HEADS
WINDOW = 128
Q_BLOCK = 128
ROPE_BASE = 10000.0
RNN_WIDTH = BRANCH_WIDTH
RNN_BLOCKS = 8
RNN_BLOCK = RNN_WIDTH // RNN_BLOCKS
RNN_CONV = 4
RNN_CONV_LEFT = 2
LRU_C = 8.0
SC_WIDTH = BRANCH_WIDTH
SC_CONV = 3
SC_CONV_LEFT = 1
D_FF = 5504
N_MOD = 9
EPS = 1e-6
NEG_INF = -1e30
IN_SIZES = (RNN_WIDTH, RNN_WIDTH, SC_WIDTH, SC_WIDTH, SC_WIDTH,
            N_HEADS * HEAD_DIM, N_KV_HEADS * HEAD_DIM, N_KV_HEADS * HEAD_DIM,
            N_BRANCH * D_MODEL)
IN_COLS = sum(IN_SIZES)

kernel_name = "hybrid_rglru_shortconv_swa_diffusion_block"


def rmsnorm(x, g):
    xf = x.astype(jnp.float32)
    y = xf * lax.rsqrt(jnp.mean(xf * xf, axis=-1, keepdims=True) + EPS)
    return (y * g.astype(jnp.float32)).astype(x.dtype)


def modulate(x, shift, scale):
    return x * (1 + scale) + shift


def swiglu(u, w13, w2):
    gu = u @ w13
    gate, up = gu[..., :D_FF], gu[..., D_FF:]
    return (jax.nn.silu(gate) * up) @ w2


def split_in(z):
    out, start = [], 0
    for n in IN_SIZES:
        out.append(z[..., start:start + n])
        start += n
    return out


def dwconv(x, w, left):
    k_w, ch = w.shape
    return lax.conv_general_dilated(
        x, w[:, None, :], window_strides=(1,), padding=[(left, k_w - 1 - left)],
        dimension_numbers=("NWC", "WIO", "NWC"), feature_group_count=ch)


def axial_rope(n_tok):
    rows = n_tok // GRID_W
    row = jnp.repeat(jnp.arange(rows), GRID_W).astype(jnp.float32)
    col = jnp.tile(jnp.arange(GRID_W), rows).astype(jnp.float32)
    half = HEAD_DIM // 2
    inv = ROPE_BASE ** (-jnp.arange(0, half, 2, dtype=jnp.float32) / half)
    ang = jnp.concatenate([row[:, None] * inv, col[:, None] * inv], axis=-1)
    ang = ang.reshape(n_tok, 2, half // 2)
    return jnp.cos(ang), jnp.sin(ang)


def apply_rope(x, cos, sin):
    b, l, h, d = x.shape
    xr = x.astype(jnp.float32).reshape(b, l, h, 2, 2, d // 4)
    x1, x2 = xr[..., 0, :], xr[..., 1, :]
    cs, sn = cos[None, :, None], sin[None, :, None]
    out = jnp.stack([x1 * cs - x2 * sn, x2 * cs + x1 * sn], axis=-2)
    return out.reshape(b, l, h, d).astype(x.dtype)


def linear_scan(a, b, h0):
    def combine(e1, e2):
        a1, b1 = e1
        a2, b2 = e2
        return a1 * a2, a2 * b1 + b2
    a_cum, b_cum = lax.associative_scan(combine, (a, b), axis=1)
    return b_cum + a_cum * h0[:, None, :]


def rglru(x, w_a, b_a, w_x, b_x, lam, h0, reverse):
    b, l, r = x.shape
    xb = x.reshape(b, l, RNN_BLOCKS, RNN_BLOCK)
    rg = jax.nn.sigmoid((jnp.einsum("blnd,nde->blne", xb, w_a).reshape(b, l, r) + b_a).astype(jnp.float32))
    ig = jax.nn.sigmoid((jnp.einsum("blnd,nde->blne", xb, w_x).reshape(b, l, r) + b_x).astype(jnp.float32))
    log_a = -LRU_C * rg * jax.nn.softplus(-lam.astype(jnp.float32))
    a = jnp.exp(log_a)
    u = jnp.sqrt(-jnp.expm1(2.0 * log_a)) * (ig * x.astype(jnp.float32))
    if reverse:
        a, u = jnp.flip(a, axis=1), jnp.flip(u, axis=1)
    h = linear_scan(a, u, h0)
    h_last = h[:, -1]
    if reverse:
        h = jnp.flip(h, axis=1)
    return h.astype(x.dtype), h_last


def sink_softmax(logits, sink):
    s = jnp.broadcast_to(sink.astype(jnp.float32)[None, :, :, None, None], logits.shape[:-1] + (1,))
    p = jax.nn.softmax(jnp.concatenate([s, logits], axis=-1), axis=-1)
    return p[..., 1:]


def banded_attention(q, k, v, kc, vc, sink):
    b, l = q.shape[0], q.shape[1]
    nblk = l // Q_BLOCK
    scale = HEAD_DIM ** -0.5
    qb = q.reshape(b, nblk, Q_BLOCK, N_KV_HEADS, GROUP, HEAD_DIM)
    pad = ((0, 0), (Q_BLOCK, Q_BLOCK), (0, 0), (0, 0))
    kp, vp = jnp.pad(k, pad), jnp.pad(v, pad)
    span = 3 * Q_BLOCK
    offs_q = jnp.arange(Q_BLOCK)
    offs_k = jnp.arange(span) - Q_BLOCK

    def block(n):
        qn = lax.dynamic_index_in_dim(qb, n, axis=1, keepdims=False)
        kn = lax.dynamic_slice_in_dim(kp, n * Q_BLOCK, span, axis=1)
        vn = lax.dynamic_slice_in_dim(vp, n * Q_BLOCK, span, axis=1)
        qpos = n * Q_BLOCK + offs_q
        kpos = n * Q_BLOCK + offs_k
        valid = (jnp.abs(qpos[:, None] - kpos[None, :]) <= WINDOW) & (kpos >= 0)[None, :] & (kpos < l)[None, :]
        s_loc = jnp.einsum("bqkgd,bskd->bkgqs", qn, kn).astype(jnp.float32) * scale
        s_loc = jnp.where(valid, s_loc, NEG_INF)
        s_ctx = jnp.einsum("bqkgd,bckd->bkgqc", qn, kc).astype(jnp.float32) * scale
        p = sink_softmax(jnp.concatenate([s_loc, s_ctx], axis=-1), sink).astype(v.dtype)
        return (jnp.einsum("bkgqs,bskd->bqkgd", p[..., :span], vn)
                + jnp.einsum("bkgqc,bckd->bqkgd", p[..., span:], vc))

    out = lax.map(block, jnp.arange(nblk))
    return jnp.moveaxis(out, 0, 1).reshape(b, l, N_HEADS * HEAD_DIM)


def context_attention(qc, kc, vc, sink):
    b, n = qc.shape[0], qc.shape[1]
    s = jnp.einsum("bqkgd,bckd->bkgqc", qc, kc).astype(jnp.float32) * (HEAD_DIM ** -0.5)
    p = sink_softmax(s, sink).astype(vc.dtype)
    return jnp.einsum("bkgqc,bckd->bqkgd", p, vc).reshape(b, n, N_HEADS * HEAD_DIM)


def merge_branches(ys, g, b_merge, w_branch, w_out):
    gates = jax.nn.sigmoid(g.reshape(g.shape[:-1] + (N_BRANCH, D_MODEL)) + b_merge)
    merged = gates[..., 0, :] * (ys[0] @ w_branch[0])
    for i in range(1, N_BRANCH):
        merged = merged + gates[..., i, :] * (ys[i] @ w_branch[i])
    return merged @ w_out


def token_mixer(u, uc, cos, sin, w_in, b_merge, rnn_conv_w, rnn_conv_b, lru_w_a, lru_b_a,
                lru_w_x, lru_b_x, lru_lambda, sc_conv_w, attn_sink, w_branch, w_out, with_ctx_out):
    b, l, _ = u.shape
    n_ctx = uc.shape[1]
    rx, rg, sb, scg, sx, q, k, v, g = split_in(u @ w_in)
    rxc, rgc, sbc, scgc, sxc, qc, kc, vc, gc = split_in(uc @ w_in)

    xa = dwconv(rx, rnn_conv_w, RNN_CONV_LEFT) + rnn_conv_b
    xac = dwconv(rxc, rnn_conv_w, RNN_CONV_LEFT) + rnn_conv_b
    h0 = jnp.zeros((b, RNN_WIDTH), jnp.float32)
    hc_f, last_f = rglru(xac, lru_w_a[0], lru_b_a[0], lru_w_x[0], lru_b_x[0], lru_lambda[0], h0, False)
    hc_b, last_b = rglru(xac, lru_w_a[1], lru_b_a[1], lru_w_x[1], lru_b_x[1], lru_lambda[1], h0, True)
    hl_f, _ = rglru(xa, lru_w_a[0], lru_b_a[0], lru_w_x[0], lru_b_x[0], lru_lambda[0], last_f, False)
    hl_b, _ = rglru(xa, lru_w_a[1], lru_b_a[1], lru_w_x[1], lru_b_x[1], lru_lambda[1], last_b, True)
    ya = (hl_f + hl_b) * jax.nn.gelu(rg)

    yb = sb * dwconv(scg * sx, sc_conv_w, SC_CONV_LEFT)

    sink = attn_sink.reshape(N_KV_HEADS, GROUP)
    q = apply_rope(q.reshape(b, l, N_HEADS, HEAD_DIM), cos, sin).reshape(b, l, N_KV_HEADS, GROUP, HEAD_DIM)
    k = apply_rope(k.reshape(b, l, N_KV_HEADS, HEAD_DIM), cos, sin)
    v = v.reshape(b, l, N_KV_HEADS, HEAD_DIM)
    kc = kc.reshape(b, n_ctx, N_KV_HEADS, HEAD_DIM)
    vc = vc.reshape(b, n_ctx, N_KV_HEADS, HEAD_DIM)
    yatt = banded_attention(q, k, v, kc, vc, sink)

    y = merge_branches((ya, yb, yatt), g, b_merge, w_branch, w_out)
    if not with_ctx_out:
        return y, None

    yac = (hc_f + hc_b) * jax.nn.gelu(rgc)
    ybc = sbc * dwconv(scgc * sxc, sc_conv_w, SC_CONV_LEFT)
    yattc = context_attention(qc.reshape(b, n_ctx, N_KV_HEADS, GROUP, HEAD_DIM), kc, vc, sink)
    yc = merge_branches((yac, ybc, yattc), gc, b_merge, w_branch, w_out)
    return y, yc


def _fwd_setup_inputs(seed: int = 0) -> dict:
    key = jax.random.key(seed)
    ks = jax.random.split(key, 32)
    f32 = jnp.float32

    def nrm(k, shape, scale):
        return jax.random.normal(k, shape, f32) * scale

    a_c = jax.random.uniform(ks[15], (DEPTH, 2, RNN_WIDTH), f32, minval=0.9, maxval=0.999)
    s_l = a_c ** (1.0 / LRU_C)
    lam = jnp.log(s_l) - jnp.log1p(-s_l)
    return {
        "x": nrm(ks[0], (BATCH, SEQ, D_MODEL), 1.0),
        "c": nrm(ks[1], (BATCH, D_MODEL), 1.0),
        "ctx": nrm(ks[2], (BATCH, CTX_LEN, D_MODEL), 1.0),
        "c_ctx": nrm(ks[3], (D_MODEL,), 1.0),
        "ada_w": nrm(ks[4], (DEPTH, D_MODEL, N_MOD * D_MODEL), 0.5 * D_MODEL ** -0.5),
        "ada_b": nrm(ks[5], (DEPTH, N_MOD * D_MODEL), 0.02),
        "norm_g": 1.0 + nrm(ks[6], (DEPTH, 3, D_MODEL), 0.02),
        "ffn1_w13": nrm(ks[7], (DEPTH, D_MODEL, 2 * D_FF), D_MODEL ** -0.5),
        "ffn1_w2": nrm(ks[8], (DEPTH, D_FF, D_MODEL), D_FF ** -0.5),
        "w_in": nrm(ks[9], (DEPTH, D_MODEL, IN_COLS), D_MODEL ** -0.5),
        "b_merge": nrm(ks[10], (DEPTH, N_BRANCH, D_MODEL), 0.02),
        "rnn_conv_w": nrm(ks[11], (DEPTH, RNN_CONV, RNN_WIDTH), RNN_CONV ** -0.5),
        "rnn_conv_b": nrm(ks[12], (DEPTH, RNN_WIDTH), 0.02),
        "lru_w_a": nrm(ks[13], (DEPTH, 2, RNN_BLOCKS, RNN_BLOCK, RNN_BLOCK), RNN_BLOCK ** -0.5),
        "lru_b_a": nrm(ks[14], (DEPTH, 2, RNN_WIDTH), 0.02),
        "lru_w_x": nrm(ks[16], (DEPTH, 2, RNN_BLOCKS, RNN_BLOCK, RNN_BLOCK), RNN_BLOCK ** -0.5),
        "lru_b_x": nrm(ks[17], (DEPTH, 2, RNN_WIDTH), 0.02),
        "lru_lambda": lam,
        "sc_conv_w": nrm(ks[18], (DEPTH, SC_CONV, SC_WIDTH), SC_CONV ** -0.5),
        "attn_sink": nrm(ks[19], (DEPTH, N_HEADS), 0.5),
        "w_branch": nrm(ks[20], (DEPTH, N_BRANCH, BRANCH_WIDTH, D_MODEL), BRANCH_WIDTH ** -0.5),
        "w_out": nrm(ks[21], (DEPTH, D_MODEL, D_MODEL), D_MODEL ** -0.5),
        "ffn2_w13": nrm(ks[22], (DEPTH, D_MODEL, 2 * D_FF), D_MODEL ** -0.5),
        "ffn2_w2": nrm(ks[23], (DEPTH, D_FF, D_MODEL), D_FF ** -0.5),
        "final_norm_g": 1.0 + nrm(ks[24], (D_MODEL,), 0.02),
    }


def _fwd_reference(x, c, ctx, c_ctx, ada_w, ada_b, norm_g, ffn1_w13, ffn1_w2, w_in, b_merge,
              rnn_conv_w, rnn_conv_b, lru_w_a, lru_b_a, lru_w_x, lru_b_x, lru_lambda,
              sc_conv_w, attn_sink, w_branch, w_out, ffn2_w13, ffn2_w2, final_norm_g):
    b, l, _ = x.shape
    cos, sin = axial_rope(l)
    silu_c = jax.nn.silu(c)
    silu_cc = jax.nn.silu(c_ctx)
    h, hc = x, ctx
    for layer in range(DEPTH):
        last = layer == DEPTH - 1
        mod = (silu_c @ ada_w[layer] + ada_b[layer]).reshape(b, N_MOD, 1, D_MODEL)
        modc = (silu_cc @ ada_w[layer] + ada_b[layer]).reshape(N_MOD, D_MODEL)

        u = modulate(rmsnorm(h, norm_g[layer, 0]), mod[:, 0], mod[:, 1])
        h = h + 0.5 * mod[:, 2] * swiglu(u, ffn1_w13[layer], ffn1_w2[layer])
        uc = modulate(rmsnorm(hc, norm_g[layer, 0]), modc[0], modc[1])
        hc = hc + 0.5 * modc[2] * swiglu(uc, ffn1_w13[layer], ffn1_w2[layer])

        u = modulate(rmsnorm(h, norm_g[layer, 1]), mod[:, 3], mod[:, 4])
        uc = modulate(rmsnorm(hc, norm_g[layer, 1]), modc[3], modc[4])
        y, yc = token_mixer(u, uc, cos, sin, w_in[layer], b_merge[layer], rnn_conv_w[layer],
                            rnn_conv_b[layer], lru_w_a[layer], lru_b_a[layer], lru_w_x[layer],
                            lru_b_x[layer], lru_lambda[layer], sc_conv_w[layer], attn_sink[layer],
                            w_branch[layer], w_out[layer], not last)
        h = h + mod[:, 5] * y

        u = modulate(rmsnorm(h, norm_g[layer, 2]), mod[:, 6], mod[:, 7])
        h = h + 0.5 * mod[:, 8] * swiglu(u, ffn2_w13[layer], ffn2_w2[layer])
        if not last:
            hc = hc + modc[5] * yc
            uc = modulate(rmsnorm(hc, norm_g[layer, 2]), modc[6], modc[7])
            hc = hc + 0.5 * modc[8] * swiglu(uc, ffn2_w13[layer], ffn2_w2[layer])
    return rmsnorm(h, final_norm_g)


import jax as _jax
import jax.numpy as _jnp

TWIN_FORMAT = 'train_step'
FWD_PARAMS = ['x', 'c', 'ctx', 'c_ctx', 'ada_w', 'ada_b', 'norm_g', 'ffn1_w13', 'ffn1_w2', 'w_in', 'b_merge', 'rnn_conv_w', 'rnn_conv_b', 'lru_w_a', 'lru_b_a', 'lru_w_x', 'lru_b_x', 'lru_lambda', 'sc_conv_w', 'attn_sink', 'w_branch', 'w_out', 'ffn2_w13', 'ffn2_w2', 'final_norm_g']
TWIN_WEIGHTS = ['c_ctx', 'ada_w', 'ada_b', 'norm_g', 'ffn1_w13', 'ffn1_w2', 'w_in', 'b_merge', 'rnn_conv_w', 'rnn_conv_b', 'lru_w_a', 'lru_b_a', 'lru_w_x', 'lru_b_x', 'lru_lambda', 'sc_conv_w', 'attn_sink', 'w_branch', 'w_out', 'ffn2_w13', 'ffn2_w2', 'final_norm_g']
TWIN_DIFF_INPUT = 'x'
TWIN_INPUTS = ['x', 'c', 'ctx', 'c_ctx', 'ada_w', 'ada_b', 'norm_g', 'ffn1_w13', 'ffn1_w2', 'w_in', 'b_merge', 'rnn_conv_w', 'rnn_conv_b', 'lru_w_a', 'lru_b_a', 'lru_w_x', 'lru_b_x', 'lru_lambda', 'sc_conv_w', 'attn_sink', 'w_branch', 'w_out', 'ffn2_w13', 'ffn2_w2', 'final_norm_g', 'loss_target', 'm_c_ctx', 'm_ada_w', 'm_ada_b', 'm_norm_g', 'm_ffn1_w13', 'm_ffn1_w2', 'm_w_in', 'm_b_merge', 'm_rnn_conv_w', 'm_rnn_conv_b', 'm_lru_w_a', 'm_lru_b_a', 'm_lru_w_x', 'm_lru_b_x', 'm_lru_lambda', 'm_sc_conv_w', 'm_attn_sink', 'm_w_branch', 'm_w_out', 'm_ffn2_w13', 'm_ffn2_w2', 'm_final_norm_g', 'v_c_ctx', 'v_ada_w', 'v_ada_b', 'v_norm_g', 'v_ffn1_w13', 'v_ffn1_w2', 'v_w_in', 'v_b_merge', 'v_rnn_conv_w', 'v_rnn_conv_b', 'v_lru_w_a', 'v_lru_b_a', 'v_lru_w_x', 'v_lru_b_x', 'v_lru_lambda', 'v_sc_conv_w', 'v_attn_sink', 'v_w_branch', 'v_w_out', 'v_ffn2_w13', 'v_ffn2_w2', 'v_final_norm_g']
TWIN_OUTPUTS = ['loss', 'grad_x', 'grad_c_ctx', 'grad_ada_w', 'grad_ada_b', 'grad_norm_g', 'grad_ffn1_w13', 'grad_ffn1_w2', 'grad_w_in', 'grad_b_merge', 'grad_rnn_conv_w', 'grad_rnn_conv_b', 'grad_lru_w_a', 'grad_lru_b_a', 'grad_lru_w_x', 'grad_lru_b_x', 'grad_lru_lambda', 'grad_sc_conv_w', 'grad_attn_sink', 'grad_w_branch', 'grad_w_out', 'grad_ffn2_w13', 'grad_ffn2_w2', 'grad_final_norm_g', 'delta_c_ctx', 'delta_ada_w', 'delta_ada_b', 'delta_norm_g', 'delta_ffn1_w13', 'delta_ffn1_w2', 'delta_w_in', 'delta_b_merge', 'delta_rnn_conv_w', 'delta_rnn_conv_b', 'delta_lru_w_a', 'delta_lru_b_a', 'delta_lru_w_x', 'delta_lru_b_x', 'delta_lru_lambda', 'delta_sc_conv_w', 'delta_attn_sink', 'delta_w_branch', 'delta_w_out', 'delta_ffn2_w13', 'delta_ffn2_w2', 'delta_final_norm_g', 'new_m_c_ctx', 'new_m_ada_w', 'new_m_ada_b', 'new_m_norm_g', 'new_m_ffn1_w13', 'new_m_ffn1_w2', 'new_m_w_in', 'new_m_b_merge', 'new_m_rnn_conv_w', 'new_m_rnn_conv_b', 'new_m_lru_w_a', 'new_m_lru_b_a', 'new_m_lru_w_x', 'new_m_lru_b_x', 'new_m_lru_lambda', 'new_m_sc_conv_w', 'new_m_attn_sink', 'new_m_w_branch', 'new_m_w_out', 'new_m_ffn2_w13', 'new_m_ffn2_w2', 'new_m_final_norm_g', 'new_v_c_ctx', 'new_v_ada_w', 'new_v_ada_b', 'new_v_norm_g', 'new_v_ffn1_w13', 'new_v_ffn1_w2', 'new_v_w_in', 'new_v_b_merge', 'new_v_rnn_conv_w', 'new_v_rnn_conv_b', 'new_v_lru_w_a', 'new_v_lru_b_a', 'new_v_lru_w_x', 'new_v_lru_b_x', 'new_v_lru_lambda', 'new_v_sc_conv_w', 'new_v_attn_sink', 'new_v_w_branch', 'new_v_w_out', 'new_v_ffn2_w13', 'new_v_ffn2_w2', 'new_v_final_norm_g']
TWIN_LEAF_KINDS = {'loss': 'loss', 'grad_x': 'grad_x', 'grad_c_ctx': 'grad_w', 'grad_ada_w': 'grad_w', 'grad_ada_b': 'grad_w', 'grad_norm_g': 'grad_w', 'grad_ffn1_w13': 'grad_w', 'grad_ffn1_w2': 'grad_w', 'grad_w_in': 'grad_w', 'grad_b_merge': 'grad_w', 'grad_rnn_conv_w': 'grad_w', 'grad_rnn_conv_b': 'grad_w', 'grad_lru_w_a': 'grad_w', 'grad_lru_b_a': 'grad_w', 'grad_lru_w_x': 'grad_w', 'grad_lru_b_x': 'grad_w', 'grad_lru_lambda': 'grad_w', 'grad_sc_conv_w': 'grad_w', 'grad_attn_sink': 'grad_w', 'grad_w_branch': 'grad_w', 'grad_w_out': 'grad_w', 'grad_ffn2_w13': 'grad_w', 'grad_ffn2_w2': 'grad_w', 'grad_final_norm_g': 'grad_w', 'delta_c_ctx': 'delta_w', 'delta_ada_w': 'delta_w', 'delta_ada_b': 'delta_w', 'delta_norm_g': 'delta_w', 'delta_ffn1_w13': 'delta_w', 'delta_ffn1_w2': 'delta_w', 'delta_w_in': 'delta_w', 'delta_b_merge': 'delta_w', 'delta_rnn_conv_w': 'delta_w', 'delta_rnn_conv_b': 'delta_w', 'delta_lru_w_a': 'delta_w', 'delta_lru_b_a': 'delta_w', 'delta_lru_w_x': 'delta_w', 'delta_lru_b_x': 'delta_w', 'delta_lru_lambda': 'delta_w', 'delta_sc_conv_w': 'delta_w', 'delta_attn_sink': 'delta_w', 'delta_w_branch': 'delta_w', 'delta_w_out': 'delta_w', 'delta_ffn2_w13': 'delta_w', 'delta_ffn2_w2': 'delta_w', 'delta_final_norm_g': 'delta_w', 'new_m_c_ctx': 'new_m', 'new_m_ada_w': 'new_m', 'new_m_ada_b': 'new_m', 'new_m_norm_g': 'new_m', 'new_m_ffn1_w13': 'new_m', 'new_m_ffn1_w2': 'new_m', 'new_m_w_in': 'new_m', 'new_m_b_merge': 'new_m', 'new_m_rnn_conv_w': 'new_m', 'new_m_rnn_conv_b': 'new_m', 'new_m_lru_w_a': 'new_m', 'new_m_lru_b_a': 'new_m', 'new_m_lru_w_x': 'new_m', 'new_m_lru_b_x': 'new_m', 'new_m_lru_lambda': 'new_m', 'new_m_sc_conv_w': 'new_m', 'new_m_attn_sink': 'new_m', 'new_m_w_branch': 'new_m', 'new_m_w_out': 'new_m', 'new_m_ffn2_w13': 'new_m', 'new_m_ffn2_w2': 'new_m', 'new_m_final_norm_g': 'new_m', 'new_v_c_ctx': 'new_v', 'new_v_ada_w': 'new_v', 'new_v_ada_b': 'new_v', 'new_v_norm_g': 'new_v', 'new_v_ffn1_w13': 'new_v', 'new_v_ffn1_w2': 'new_v', 'new_v_w_in': 'new_v', 'new_v_b_merge': 'new_v', 'new_v_rnn_conv_w': 'new_v', 'new_v_rnn_conv_b': 'new_v', 'new_v_lru_w_a': 'new_v', 'new_v_lru_b_a': 'new_v', 'new_v_lru_w_x': 'new_v', 'new_v_lru_b_x': 'new_v', 'new_v_lru_lambda': 'new_v', 'new_v_sc_conv_w': 'new_v', 'new_v_attn_sink': 'new_v', 'new_v_w_branch': 'new_v', 'new_v_w_out': 'new_v', 'new_v_ffn2_w13': 'new_v', 'new_v_ffn2_w2': 'new_v', 'new_v_final_norm_g': 'new_v'}


def _forward(args):
    return _fwd_reference(*[args[k] for k in FWD_PARAMS])


def _output_shape():
    def fwd():
        inp = _fwd_setup_inputs(0)
        return _fwd_reference(*[inp[k] for k in FWD_PARAMS])
    out = _jax.eval_shape(fwd)
    return out.shape, out.dtype

N_MICROBATCH = 1
ADAM_LR = 0.001
ADAM_B1 = 0.9
ADAM_B2 = 0.999
ADAM_EPS = 1e-08
ADAM_WD = 0.01
ADAM_STEP = 10
PER_EXAMPLE_BATCH_AXIS = {'x': 0, 'c': 0, 'ctx': 0, 'loss_target': 0}
SHARED_INPUTS = []
_WEIGHT_DTYPES = {'c_ctx': _jnp.float32, 'ada_w': _jnp.float32, 'ada_b': _jnp.float32, 'norm_g': _jnp.float32, 'ffn1_w13': _jnp.float32, 'ffn1_w2': _jnp.float32, 'w_in': _jnp.float32, 'b_merge': _jnp.float32, 'rnn_conv_w': _jnp.float32, 'rnn_conv_b': _jnp.float32, 'lru_w_a': _jnp.float32, 'lru_b_a': _jnp.float32, 'lru_w_x': _jnp.float32, 'lru_b_x': _jnp.float32, 'lru_lambda': _jnp.float32, 'sc_conv_w': _jnp.float32, 'attn_sink': _jnp.float32, 'w_branch': _jnp.float32, 'w_out': _jnp.float32, 'ffn2_w13': _jnp.float32, 'ffn2_w2': _jnp.float32, 'final_norm_g': _jnp.float32}
MOMENT_SCALE = {'c_ctx': 2.119206e-02, 'ada_w': 7.043434e-02, 'ada_b': 1.212648e-01, 'norm_g': 4.545663e-02, 'ffn1_w13': 8.606361e-03, 'ffn1_w2': 1.395139e-02, 'w_in': 3.902648e-02, 'b_merge': 1.464783e-02, 'rnn_conv_w': 8.998158e-02, 'rnn_conv_b': 2.396756e-01, 'lru_w_a': 4.856197e-03, 'lru_b_a': 7.996804e-03, 'lru_w_x': 1.006059e-02, 'lru_b_x': 2.082818e-02, 'lru_lambda': 2.038120e-02, 'sc_conv_w': 3.444446e-02, 'attn_sink': 1.444001e-04, 'w_branch': 3.778085e-02, 'w_out': 6.593378e-02, 'ffn2_w13': 7.629833e-03, 'ffn2_w2': 1.232248e-02, 'final_norm_g': 3.220353e+01}


def _to_microbatches(a, axis):
    t = _jnp.moveaxis(a, axis, 0)
    t = t.reshape((N_MICROBATCH, t.shape[0] // N_MICROBATCH) + t.shape[1:])
    return _jnp.moveaxis(t, 1, axis + 1)


def setup_inputs(seed: int = 0) -> dict:
    inp = _fwd_setup_inputs(seed)
    key = _jax.random.fold_in(_jax.random.key(seed), 7919)
    shape, _ = _output_shape()
    out = dict(inp)
    out["loss_target"] = _jax.random.normal(_jax.random.fold_in(key, 0), shape, _jnp.float32)
    for i, name in enumerate(TWIN_WEIGHTS):
        w = inp[name].astype(_jnp.float32)
        if MOMENT_SCALE is None:
            s = _jnp.sqrt(_jnp.mean(_jnp.square(w)) + 1e-30)
        else:
            s = MOMENT_SCALE[name]
        km, kv = _jax.random.split(_jax.random.fold_in(key, i + 1))
        out[name] = w
        out["m_" + name] = s * _jax.random.normal(km, w.shape, _jnp.float32)
        out["v_" + name] = (s * s) * _jax.random.uniform(kv, w.shape, _jnp.float32, 0.5, 1.5)
    if N_MICROBATCH > 1:
        for name, axis in PER_EXAMPLE_BATCH_AXIS.items():
            out[name] = _to_microbatches(out[name], axis)
    return {'x': out['x'], 'c': out['c'], 'ctx': out['ctx'], 'c_ctx': out['c_ctx'], 'ada_w': out['ada_w'], 'ada_b': out['ada_b'], 'norm_g': out['norm_g'], 'ffn1_w13': out['ffn1_w13'], 'ffn1_w2': out['ffn1_w2'], 'w_in': out['w_in'], 'b_merge': out['b_merge'], 'rnn_conv_w': out['rnn_conv_w'], 'rnn_conv_b': out['rnn_conv_b'], 'lru_w_a': out['lru_w_a'], 'lru_b_a': out['lru_b_a'], 'lru_w_x': out['lru_w_x'], 'lru_b_x': out['lru_b_x'], 'lru_lambda': out['lru_lambda'], 'sc_conv_w': out['sc_conv_w'], 'attn_sink': out['attn_sink'], 'w_branch': out['w_branch'], 'w_out': out['w_out'], 'ffn2_w13': out['ffn2_w13'], 'ffn2_w2': out['ffn2_w2'], 'final_norm_g': out['final_norm_g'], 'loss_target': out['loss_target'], 'm_c_ctx': out['m_c_ctx'], 'm_ada_w': out['m_ada_w'], 'm_ada_b': out['m_ada_b'], 'm_norm_g': out['m_norm_g'], 'm_ffn1_w13': out['m_ffn1_w13'], 'm_ffn1_w2': out['m_ffn1_w2'], 'm_w_in': out['m_w_in'], 'm_b_merge': out['m_b_merge'], 'm_rnn_conv_w': out['m_rnn_conv_w'], 'm_rnn_conv_b': out['m_rnn_conv_b'], 'm_lru_w_a': out['m_lru_w_a'], 'm_lru_b_a': out['m_lru_b_a'], 'm_lru_w_x': out['m_lru_w_x'], 'm_lru_b_x': out['m_lru_b_x'], 'm_lru_lambda': out['m_lru_lambda'], 'm_sc_conv_w': out['m_sc_conv_w'], 'm_attn_sink': out['m_attn_sink'], 'm_w_branch': out['m_w_branch'], 'm_w_out': out['m_w_out'], 'm_ffn2_w13': out['m_ffn2_w13'], 'm_ffn2_w2': out['m_ffn2_w2'], 'm_final_norm_g': out['m_final_norm_g'], 'v_c_ctx': out['v_c_ctx'], 'v_ada_w': out['v_ada_w'], 'v_ada_b': out['v_ada_b'], 'v_norm_g': out['v_norm_g'], 'v_ffn1_w13': out['v_ffn1_w13'], 'v_ffn1_w2': out['v_ffn1_w2'], 'v_w_in': out['v_w_in'], 'v_b_merge': out['v_b_merge'], 'v_rnn_conv_w': out['v_rnn_conv_w'], 'v_rnn_conv_b': out['v_rnn_conv_b'], 'v_lru_w_a': out['v_lru_w_a'], 'v_lru_b_a': out['v_lru_b_a'], 'v_lru_w_x': out['v_lru_w_x'], 'v_lru_b_x': out['v_lru_b_x'], 'v_lru_lambda': out['v_lru_lambda'], 'v_sc_conv_w': out['v_sc_conv_w'], 'v_attn_sink': out['v_attn_sink'], 'v_w_branch': out['v_w_branch'], 'v_w_out': out['v_w_out'], 'v_ffn2_w13': out['v_ffn2_w13'], 'v_ffn2_w2': out['v_ffn2_w2'], 'v_final_norm_g': out['v_final_norm_g']}


def _loss(weights, diff, rest, loss_target):
    with _jax.named_scope("forward"):
        args = {**rest, TWIN_DIFF_INPUT: diff, **{k: w.astype(_WEIGHT_DTYPES[k]) for k, w in weights.items()}}
        y = _forward(args)
    with _jax.named_scope("loss_head"):
        err = _jnp.square(y.astype(_jnp.float32) - loss_target)
        return 0.5 * _jnp.sum(_jnp.mean(err, axis=-1)) if err.ndim else 0.5 * err


def _adamw(w, g, m, v):
    m = ADAM_B1 * m + (1.0 - ADAM_B1) * g
    v = ADAM_B2 * v + (1.0 - ADAM_B2) * _jnp.square(g)
    m_hat = m / (1.0 - ADAM_B1 ** ADAM_STEP)
    v_hat = v / (1.0 - ADAM_B2 ** ADAM_STEP)
    delta = -ADAM_LR * (m_hat / (_jnp.sqrt(v_hat) + ADAM_EPS) + ADAM_WD * w)
    return delta, m, v


def reference(x, c, ctx, c_ctx, ada_w, ada_b, norm_g, ffn1_w13, ffn1_w2, w_in, b_merge, rnn_conv_w, rnn_conv_b, lru_w_a, lru_b_a, lru_w_x, lru_b_x, lru_lambda, sc_conv_w, attn_sink, w_branch, w_out, ffn2_w13, ffn2_w2, final_norm_g, loss_target, m_c_ctx, m_ada_w, m_ada_b, m_norm_g, m_ffn1_w13, m_ffn1_w2, m_w_in, m_b_merge, m_rnn_conv_w, m_rnn_conv_b, m_lru_w_a, m_lru_b_a, m_lru_w_x, m_lru_b_x, m_lru_lambda, m_sc_conv_w, m_attn_sink, m_w_branch, m_w_out, m_ffn2_w13, m_ffn2_w2, m_final_norm_g, v_c_ctx, v_ada_w, v_ada_b, v_norm_g, v_ffn1_w13, v_ffn1_w2, v_w_in, v_b_merge, v_rnn_conv_w, v_rnn_conv_b, v_lru_w_a, v_lru_b_a, v_lru_w_x, v_lru_b_x, v_lru_lambda, v_sc_conv_w, v_attn_sink, v_w_branch, v_w_out, v_ffn2_w13, v_ffn2_w2, v_final_norm_g):
    given = dict(x=x, c=c, ctx=ctx, c_ctx=c_ctx, ada_w=ada_w, ada_b=ada_b, norm_g=norm_g, ffn1_w13=ffn1_w13, ffn1_w2=ffn1_w2, w_in=w_in, b_merge=b_merge, rnn_conv_w=rnn_conv_w, rnn_conv_b=rnn_conv_b, lru_w_a=lru_w_a, lru_b_a=lru_b_a, lru_w_x=lru_w_x, lru_b_x=lru_b_x, lru_lambda=lru_lambda, sc_conv_w=sc_conv_w, attn_sink=attn_sink, w_branch=w_branch, w_out=w_out, ffn2_w13=ffn2_w13, ffn2_w2=ffn2_w2, final_norm_g=final_norm_g, loss_target=loss_target, m_c_ctx=m_c_ctx, m_ada_w=m_ada_w, m_ada_b=m_ada_b, m_norm_g=m_norm_g, m_ffn1_w13=m_ffn1_w13, m_ffn1_w2=m_ffn1_w2, m_w_in=m_w_in, m_b_merge=m_b_merge, m_rnn_conv_w=m_rnn_conv_w, m_rnn_conv_b=m_rnn_conv_b, m_lru_w_a=m_lru_w_a, m_lru_b_a=m_lru_b_a, m_lru_w_x=m_lru_w_x, m_lru_b_x=m_lru_b_x, m_lru_lambda=m_lru_lambda, m_sc_conv_w=m_sc_conv_w, m_attn_sink=m_attn_sink, m_w_branch=m_w_branch, m_w_out=m_w_out, m_ffn2_w13=m_ffn2_w13, m_ffn2_w2=m_ffn2_w2, m_final_norm_g=m_final_norm_g, v_c_ctx=v_c_ctx, v_ada_w=v_ada_w, v_ada_b=v_ada_b, v_norm_g=v_norm_g, v_ffn1_w13=v_ffn1_w13, v_ffn1_w2=v_ffn1_w2, v_w_in=v_w_in, v_b_merge=v_b_merge, v_rnn_conv_w=v_rnn_conv_w, v_rnn_conv_b=v_rnn_conv_b, v_lru_w_a=v_lru_w_a, v_lru_b_a=v_lru_b_a, v_lru_w_x=v_lru_w_x, v_lru_b_x=v_lru_b_x, v_lru_lambda=v_lru_lambda, v_sc_conv_w=v_sc_conv_w, v_attn_sink=v_attn_sink, v_w_branch=v_w_branch, v_w_out=v_w_out, v_ffn2_w13=v_ffn2_w13, v_ffn2_w2=v_ffn2_w2, v_final_norm_g=v_final_norm_g)
    weights = {n: given[n] for n in TWIN_WEIGHTS}
    shared = {n: given[n] for n in SHARED_INPUTS}
    per_example = {n: given[n] for n in ['x', 'c', 'ctx']}
    grad_fn = _jax.value_and_grad(_loss, argnums=(0, 1))

    def one_microbatch(ex, loss_target):
        ex = dict(ex)
        diff = ex.pop(TWIN_DIFF_INPUT)
        return grad_fn(weights, diff, {**shared, **ex}, loss_target)

    if N_MICROBATCH == 1:
        loss, (grad_w, grad_x) = one_microbatch(per_example, given["loss_target"])
    else:
        def body(carry, xs):
            loss_sum, grad_sum = carry
            l_k, (gw_k, gx_k) = one_microbatch(xs[0], xs[1])
            with _jax.named_scope("update"):
                return (loss_sum + l_k, _jax.tree.map(_jnp.add, grad_sum, gw_k)), gx_k

        init = (_jnp.zeros((), _jnp.float32), _jax.tree.map(_jnp.zeros_like, weights))
        (loss, grad_w), grad_x = _jax.lax.scan(body, init, (per_example, given["loss_target"]))
    with _jax.named_scope("update"):
        delta_w, new_m, new_v = {}, {}, {}
        for n in TWIN_WEIGHTS:
            delta_w[n], new_m[n], new_v[n] = _adamw(weights[n], grad_w[n], given["m_" + n], given["v_" + n])
    return (loss, grad_x, *[grad_w[n] for n in TWIN_WEIGHTS], *[delta_w[n] for n in TWIN_WEIGHTS],
            *[new_m[n] for n in TWIN_WEIGHTS], *[new_v[n] for n in TWIN_WEIGHTS])
```

```python
import functools
import math

import jax
import jax.numpy as jnp
from jax import lax
from jax.experimental import pallas as pl
from jax.experimental.pallas import tpu as pltpu

F32 = jnp.float32
BF16 = jnp.bfloat16
MESH = pl.DeviceIdType.MESH

HEAD_DIM = 128
GRID_W = 64
WINDOW = 128
Q_BLOCK = 128
ROPE_BASE = 10000.0
LRU_C = 8.0
EPS = 1e-6
NEG_INF = -1e30
N_MOD = 9
N_BRANCH = 3
RNN_BLOCK = 128
HALO = 8
LSE_W = 128

ADAM_LR = 0.001
ADAM_B1 = 0.9
ADAM_B2 = 0.999
ADAM_EPS = 1e-08
ADAM_WD = 0.01
ADAM_STEP = 10

VMEM_LIMIT_BYTES = 48 * 1024 * 1024
N_DEV = 8
N_CHIP = 4


def _pick(n, cands):
    for c in cands:
        if c <= n and n % c == 0:
            return c
    return n


def _cparams(sem):
    return pltpu.CompilerParams(dimension_semantics=sem, vmem_limit_bytes=VMEM_LIMIT_BYTES)


def _silu(x):
    return x * jax.nn.sigmoid(x)


def _dsilu(x):
    s = jax.nn.sigmoid(x)
    return s * (1.0 + x * (1.0 - s))


_GELU_K = math.sqrt(2.0 / math.pi)


def _gelu(x):
    return 0.5 * x * (1.0 + jnp.tanh(_GELU_K * (x + 0.044715 * x * x * x)))


def _dgelu(x):
    t = jnp.tanh(_GELU_K * (x + 0.044715 * x * x * x))
    return 0.5 * (1.0 + t) + 0.5 * x * (1.0 - t * t) * _GELU_K * (1.0 + 3.0 * 0.044715 * x * x)


def _expm1(x):
    series = x * (1.0 + x * (0.5 + x * (1.0 / 6.0 + x * (1.0 / 24.0 + x * (1.0 / 120.0)))))
    return jnp.where(jnp.abs(x) < 0.1, series, jnp.exp(x) - 1.0)


def _my_pos():
    return lax.axis_index("x"), lax.axis_index("y"), lax.axis_index("c")


_DIMS = {"nn": (((1,), (0,)), ((), ())), "nt": (((1,), (1,)), ((), ())), "tn": (((0,), (0,)), ((), ()))}


def _mm(name, a, b, *, mode, grid, a_blk, a_map, b_blk, b_map, o_blk, o_map, out_shape, out_dtype):
    nk = grid[-1]
    nax = len(grid)
    acc_shape = tuple(d for d in o_blk if d is not None)

    def body(a_ref, b_ref, o_ref, acc_ref):
        k = pl.program_id(nax - 1)

        @pl.when(k == 0)
        def _():
            acc_ref[...] = jnp.zeros_like(acc_ref)

        acc_ref[...] += lax.dot_general(a_ref[...].astype(BF16), b_ref[...].astype(BF16), _DIMS[mode],
                                        preferred_element_type=F32)

        @pl.when(k == nk - 1)
        def _():
            o_ref[...] = acc_ref[...].astype(o_ref.dtype)

    return pl.pallas_call(
        body, name=name, grid=grid,
        in_specs=[pl.BlockSpec(a_blk, a_map), pl.BlockSpec(b_blk, b_map)],
        out_specs=pl.BlockSpec(o_blk, o_map),
        out_shape=jax.ShapeDtypeStruct(out_shape, out_dtype),
        scratch_shapes=[pltpu.VMEM(acc_shape, F32)],
        compiler_params=_cparams(("parallel",) * (nax - 1) + ("arbitrary",)),
    )(a, b)


def _tiles(n):
    return _pick(n, (768, 512, 384, 256, 128, 64, 32, 16))


def _ktile(n):
    return _pick(n, (512, 256, 128))


def mm_cols(name, a, wg, l, out_dtype, flat):
    T, K = a.shape
    Nq = wg.shape[-1]
    tm, tk = _tiles(T), _ktile(K)
    if flat:
        o_blk, o_map, o_shape = (tm, Nq), (lambda i, j, k: (i, j)), (T, N_CHIP * Nq)
    else:
        o_blk, o_map, o_shape = (None, tm, Nq), (lambda i, j, k: (j, i, 0)), (N_CHIP, T, Nq)
    return _mm(name, a, wg, mode="nn", grid=(T // tm, N_CHIP, K // tk),
               a_blk=(tm, tk), a_map=lambda i, j, k: (i, k),
               b_blk=(None, None, tk, Nq), b_map=lambda i, j, k: (l, j, k, 0),
               o_blk=o_blk, o_map=o_map, out_shape=o_shape, out_dtype=out_dtype)


def mm_cols_t(name, d, wg, l, flat):
    K, Nq = wg.shape[-2:]
    T = d.shape[-2]
    tm, tn = _tiles(T), _ktile(K)
    if flat:
        a_blk, a_map = (tm, Nq), (lambda i, j, k: (i, k))
    else:
        a_blk, a_map = (None, tm, Nq), (lambda i, j, k: (k, i, 0))
    return _mm(name, d, wg, mode="nt", grid=(T // tm, K // tn, N_CHIP),
               a_blk=a_blk, a_map=a_map,
               b_blk=(None, None, tn, Nq), b_map=lambda i, j, k: (l, k, j, 0),
               o_blk=(tm, tn), o_map=lambda i, j, k: (i, j), out_shape=(T, K), out_dtype=F32)


def mm_cols_grad(name, a, d, flat):
    T, K = a.shape
    Nq = d.shape[-1] // N_CHIP if flat else d.shape[-1]
    tt, br = _tiles(T), _ktile(K)
    if flat:
        b_blk, b_map = (tt, Nq), (lambda j, r, t: (t, j))
    else:
        b_blk, b_map = (None, tt, Nq), (lambda j, r, t: (j, t, 0))
    return _mm(name, a, d, mode="tn", grid=(N_CHIP, K // br, T // tt),
               a_blk=(tt, br), a_map=lambda j, r, t: (t, r),
               b_blk=b_blk, b_map=b_map,
               o_blk=(None, br, Nq), o_map=lambda j, r, t: (j, r, 0),
               out_shape=(N_CHIP, K, Nq), out_dtype=BF16)


def mm_rows(name, a, wg, l):
    G, T, Kg = a.shape
    N = wg.shape[-1]
    tm, tn = _tiles(T), _pick(N, (1024, 512, 256, 128))
    return _mm(name, a, wg, mode="nn", grid=(T // tm, N // tn, G),
               a_blk=(None, tm, Kg), a_map=lambda i, j, k: (k, i, 0),
               b_blk=(None, None, Kg, tn), b_map=lambda i, j, k: (l, k, 0, j),
               o_blk=(tm, tn), o_map=lambda i, j, k: (i, j), out_shape=(T, N), out_dtype=F32)


def mm_rows_t(name, d, wg, l):
    T, N = d.shape
    G, Kg = wg.shape[1], wg.shape[2]
    tm, tk = _tiles(T), _ktile(N)
    return _mm(name, d, wg, mode="nt", grid=(T // tm, G, N // tk),
               a_blk=(tm, tk), a_map=lambda i, j, k: (i, k),
               b_blk=(None, None, Kg, tk), b_map=lambda i, j, k: (l, j, 0, k),
               o_blk=(None, tm, Kg), o_map=lambda i, j, k: (j, i, 0), out_shape=(G, T, Kg), out_dtype=BF16)


def mm_rows_grad(name, a, d):
    G, T, Kg = a.shape
    N = d.shape[-1]
    tt, tn = _tiles(T), _ktile(N)
    return _mm(name, a, d, mode="tn", grid=(G, N // tn, T // tt),
               a_blk=(None, tt, Kg), a_map=lambda g, j, t: (g, t, 0),
               b_blk=(tt, tn), b_map=lambda g, j, t: (t, j),
               o_blk=(None, Kg, tn), o_map=lambda g, j, t: (g, 0, j), out_shape=(G, Kg, N), out_dtype=BF16)


def mm_plain(name, a, w, l, mode, out_dtype):
    T = a.shape[0]
    K, N = w.shape[-2:]
    tm = _tiles(T)
    if mode == "nn":
        tn, tk = _ktile(N), _ktile(K)
        return _mm(name, a, w, mode="nn", grid=(T // tm, N // tn, K // tk),
                   a_blk=(tm, tk), a_map=lambda i, j, k: (i, k),
                   b_blk=(None, tk, tn), b_map=lambda i, j, k: (l, k, j),
                   o_blk=(tm, tn), o_map=lambda i, j, k: (i, j), out_shape=(T, N), out_dtype=out_dtype)
    tn, tk = _ktile(K), _ktile(N)
    return _mm(name, a, w, mode="nt", grid=(T // tm, K // tn, N // tk),
               a_blk=(tm, tk), a_map=lambda i, j, k: (i, k),
               b_blk=(None, tn, tk), b_map=lambda i, j, k: (l, j, k),
               o_blk=(tm, tn), o_map=lambda i, j, k: (i, j), out_shape=(T, K), out_dtype=out_dtype)


def mm_plain_grad(name, a, d):
    T, K = a.shape
    N = d.shape[-1]
    tt, br, tn = _tiles(T), _ktile(K), _ktile(N)
    return _mm(name, a, d, mode="tn", grid=(K // br, N // tn, T // tt),
               a_blk=(tt, br), a_map=lambda r, j, t: (t, r),
               b_blk=(tt, tn), b_map=lambda r, j, t: (t, j),
               o_blk=(br, tn), o_map=lambda r, j, t: (r, j), out_shape=(K, N), out_dtype=BF16)


def mm_branch(name, y, wbg, l, br):
    T, RW = y.shape
    Dq = wbg.shape[-1]
    tm = _tiles(T)
    return _mm(name, y, wbg, mode="nn", grid=(T // tm, N_CHIP, 1),
               a_blk=(tm, RW), a_map=lambda i, j, k: (i, 0),
               b_blk=(None, None, None, RW, Dq), b_map=lambda i, j, k: (l, j, br, 0, 0),
               o_blk=(tm, Dq), o_map=lambda i, j, k: (i, j), out_shape=(T, N_CHIP * Dq), out_dtype=F32)


def mm_branch_t(name, d, wbg, l, br):
    T = d.shape[0]
    RW, Dq = wbg.shape[-2:]
    tm = _tiles(T)
    return _mm(name, d, wbg, mode="nt", grid=(T // tm, 1, N_CHIP),
               a_blk=(tm, Dq), a_map=lambda i, j, k: (i, k),
               b_blk=(None, None, None, RW, Dq), b_map=lambda i, j, k: (l, k, br, 0, 0),
               o_blk=(tm, RW), o_map=lambda i, j, k: (i, 0), out_shape=(T, RW), out_dtype=F32)


def mm_branch_grad(name, y, d):
    T, RW = y.shape
    Dq = d.shape[-1] // N_CHIP
    tt = _tiles(T)
    return _mm(name, y, d, mode="tn", grid=(N_CHIP, 1, T // tt),
               a_blk=(tt, RW), a_map=lambda j, r, t: (t, 0),
               b_blk=(tt, Dq), b_map=lambda j, r, t: (t, j),
               o_blk=(None, RW, Dq), o_map=lambda j, r, t: (j, 0, 0), out_shape=(N_CHIP, RW, Dq), out_dtype=BF16)


def allgather8(name, x_shard):
    m_per, n = x_shard.shape

    def body(x_ref, out_ref, send_sems, recv_sems, local_sem):
        x, y, c = _my_pos()
        me, sibling = (x, y, c), (x, y, 1 - c)
        chips = [(1 - x, y), (x, 1 - y), (1 - x, 1 - y)]

        def rows(px, py, pc):
            return out_ref.at[pl.ds((4 * px + 2 * py + pc) * m_per, m_per), :]

        def copy(k, block, to, src=None):
            return pltpu.make_async_remote_copy(
                src_ref=rows(*block) if src is None else src, dst_ref=rows(*block),
                send_sem=send_sems.at[k], recv_sem=recv_sems.at[k], device_id=to, device_id_type=MESH)

        mine = pltpu.make_async_copy(x_ref, rows(*me), local_sem)
        mine.start()
        first = [copy(0, me, sibling, src=x_ref)]
        first += [copy(1 + j, me, (*chip, c), src=x_ref) for j, chip in enumerate(chips)]
        for cp in first:
            cp.start()
        passed = [copy(4 + j, (*chip, c), sibling) for j, chip in enumerate(chips)]
        for j, chip in enumerate(chips):
            copy(1 + j, (*chip, c), me).wait_recv()
            passed[j].start()
        copy(0, sibling, me).wait_recv()
        for j, chip in enumerate(chips):
            copy(4 + j, (*chip, 1 - c), me).wait_recv()
        for cp in first + passed:
            cp.wait_send()
        mine.wait()

    return pl.pallas_call(
        body, name=name,
        out_shape=jax.ShapeDtypeStruct((N_DEV * m_per, n), x_shard.dtype),
        in_specs=[pl.BlockSpec(memory_space=pltpu.VMEM)],
        out_specs=pl.BlockSpec(memory_space=pltpu.VMEM),
        scratch_shapes=[pltpu.SemaphoreType.DMA((7,)), pltpu.SemaphoreType.DMA((7,)), pltpu.SemaphoreType.DMA],
        compiler_params=pltpu.CompilerParams(vmem_limit_bytes=VMEM_LIMIT_BYTES),
    )(x_shard)


def _other_chips(x, y):
    return [(1 - x, y), (x, 1 - y), (1 - x, 1 - y)]


def gather_weights(shards):
    n = len(shards)

    def body(*refs):
        src, dst = refs[:n], refs[n:2 * n]
        send_sems, recv_sems, local_sems = refs[2 * n:]
        x, y, c = _my_pos()
        j_me = 2 * x + y
        copies, locals_ = [], []
        for t in range(n):
            for l in range(2):
                lc = pltpu.make_async_copy(src[t].at[l], dst[t].at[l, j_me], local_sems.at[2 * t + l])
                lc.start()
                locals_.append(lc)
                for k, (px, py) in enumerate(_other_chips(x, y)):
                    s = (2 * t + l) * 3 + k
                    cp = pltpu.make_async_remote_copy(
                        src_ref=src[t].at[l], dst_ref=dst[t].at[l, j_me],
                        send_sem=send_sems.at[s], recv_sem=recv_sems.at[s], device_id=(px, py, c), device_id_type=MESH)
                    cp.start()
                    copies.append((cp, t, l, k, px, py))
        for cp, t, l, k, px, py in copies:
            s = (2 * t + l) * 3 + k
            pltpu.make_async_remote_copy(
                src_ref=src[t].at[l], dst_ref=dst[t].at[l, 2 * px + py],
                send_sem=send_sems.at[s], recv_sem=recv_sems.at[s], device_id=(px, py, c), device_id_type=MESH).wait_recv()
        for cp, *_ in copies:
            cp.wait_send()
        for lc in locals_:
            lc.wait()

    out_shape = [jax.ShapeDtypeStruct((2, N_CHIP) + s.shape[1:], s.dtype) for s in shards]
    return pl.pallas_call(
        body, name="gather_weights",
        out_shape=out_shape,
        in_specs=[pl.BlockSpec(memory_space=pl.ANY)] * n,
        out_specs=[pl.BlockSpec(memory_space=pl.ANY)] * n,
        scratch_shapes=[pltpu.SemaphoreType.DMA((6 * n,)), pltpu.SemaphoreType.DMA((6 * n,)),
                        pltpu.SemaphoreType.DMA((2 * n,))],
    )(*shards)


def scatter_grads(pieces, shapes):
    flat = [(t, arr, pre) for t, ps in enumerate(pieces) for arr, pre in ps]
    n_in, n_out = len(flat), len(shapes)

    def body(*refs):
        src, dst = refs[:n_in], refs[n_in:n_in + n_out]
        send_sems, recv_sems, local_sems = refs[n_in + n_out:]
        x, y, c = _my_pos()
        j_me = 2 * x + y
        copies, locals_ = [], []
        for p, (t, _, pre) in enumerate(flat):
            lc = pltpu.make_async_copy(src[p].at[j_me], dst[t].at[(3, *pre)], local_sems.at[p])
            lc.start()
            locals_.append(lc)
            for k, (px, py) in enumerate(_other_chips(x, y)):
                cp = pltpu.make_async_remote_copy(
                    src_ref=src[p].at[2 * px + py], dst_ref=dst[t].at[(k, *pre)],
                    send_sem=send_sems.at[3 * p + k], recv_sem=recv_sems.at[3 * p + k],
                    device_id=(px, py, c), device_id_type=MESH)
                cp.start()
                copies.append(cp)
        for cp in copies:
            cp.wait_recv()
        for cp in copies:
            cp.wait_send()
        for lc in locals_:
            lc.wait()

    out_shape = [jax.ShapeDtypeStruct((4,) + tuple(s), BF16) for s in shapes]
    return pl.pallas_call(
        body, name="scatter_grads",
        out_shape=out_shape,
        in_specs=[pl.BlockSpec(memory_space=pl.ANY)] * n_in,
        out_specs=[pl.BlockSpec(memory_space=pl.ANY)] * n_out,
        scratch_shapes=[pltpu.SemaphoreType.DMA((3 * n_in,)), pltpu.SemaphoreType.DMA((3 * n_in,)),
                        pltpu.SemaphoreType.DMA((n_in,))],
    )(*[arr for _, arr, _ in flat])


def swap_sibling(parts):
    n = len(parts)

    def body(*refs):
        src, dst = refs[:n], refs[n:2 * n]
        send_sems, recv_sems = refs[2 * n:]
        x, y, c = _my_pos()
        copies = [pltpu.make_async_remote_copy(src_ref=src[t], dst_ref=dst[t], send_sem=send_sems.at[t],
                                               recv_sem=recv_sems.at[t], device_id=(x, y, 1 - c), device_id_type=MESH)
                  for t in range(n)]
        for cp in copies:
            cp.start()
        for cp in copies:
            cp.wait_recv()
        for cp in copies:
            cp.wait_send()

    return pl.pallas_call(
        body, name="swap_sibling",
        out_shape=[jax.ShapeDtypeStruct(p.shape, p.dtype) for p in parts],
        in_specs=[pl.BlockSpec(memory_space=pl.ANY)] * n,
        out_specs=[pl.BlockSpec(memory_space=pl.ANY)] * n,
        scratch_shapes=[pltpu.SemaphoreType.DMA((n,)), pltpu.SemaphoreType.DMA((n,))],
    )(*parts)


def _view2d(a):
    return a.reshape(-1, a.shape[-1])


def _row_tile(rows, width, itemsize=4, budget=1 << 20):
    t = 8
    for cand in (1024, 512, 256, 128, 64, 32, 16, 8):
        if rows % cand == 0 and cand * width * itemsize <= budget:
            t = cand
            break
    return t if rows % t == 0 else rows


def cast_bf16(name, a):
    a2 = _view2d(a)
    R, W = a2.shape
    tr = _row_tile(R, W)

    def body(a_ref, o_ref):
        o_ref[...] = a_ref[...].astype(BF16)

    out = pl.pallas_call(
        body, name=name, grid=(R // tr,),
        in_specs=[pl.BlockSpec((tr, W), lambda i: (i, 0))], out_specs=pl.BlockSpec((tr, W), lambda i: (i, 0)),
        out_shape=jax.ShapeDtypeStruct((R, W), BF16), compiler_params=_cparams(("parallel",)),
    )(a2)
    return out.reshape(a.shape)


def sum4(name, buf):
    b3 = buf.reshape(4, -1, buf.shape[-1])
    _, R, W = b3.shape
    tr = _row_tile(R, W)

    def body(b_ref, o_ref):
        o_ref[...] = ((b_ref[3].astype(F32) + b_ref[0].astype(F32)) + b_ref[1].astype(F32)) + b_ref[2].astype(F32)

    out = pl.pallas_call(
        body, name=name, grid=(R // tr,),
        in_specs=[pl.BlockSpec((4, tr, W), lambda i: (0, i, 0))], out_specs=pl.BlockSpec((tr, W), lambda i: (i, 0)),
        out_shape=jax.ShapeDtypeStruct((R, W), F32), compiler_params=_cparams(("parallel",)),
    )(b3)
    return out.reshape(buf.shape[1:])


def adamw(name, w, m, v, g_parts):
    shape = w.shape
    w2, m2, v2 = _view2d(w), _view2d(m), _view2d(v)
    gs = [_view2d(g) for g in g_parts]
    R, W = w2.shape
    tr = _row_tile(R, W, budget=1 << 19)
    ng = len(gs)
    bc1 = 1.0 - ADAM_B1 ** ADAM_STEP
    bc2 = 1.0 - ADAM_B2 ** ADAM_STEP

    def body(*refs):
        w_ref, m_ref, v_ref = refs[:3]
        g_refs = refs[3:3 + ng]
        go_ref, d_ref, mo_ref, vo_ref = refs[3 + ng:]
        g = g_refs[0][...]
        for r in g_refs[1:]:
            g = g + r[...]
        mn = ADAM_B1 * m_ref[...] + (1.0 - ADAM_B1) * g
        vn = ADAM_B2 * v_ref[...] + (1.0 - ADAM_B2) * (g * g)
        m_hat = mn / bc1
        v_hat = vn / bc2
        go_ref[...] = g
        d_ref[...] = -ADAM_LR * (m_hat / (jnp.sqrt(v_hat) + ADAM_EPS) + ADAM_WD * w_ref[...])
        mo_ref[...] = mn
        vo_ref[...] = vn

    spec = pl.BlockSpec((tr, W), lambda i: (i, 0))
    outs = pl.pallas_call(
        body, name=name, grid=(R // tr,),
        in_specs=[spec] * (3 + ng), out_specs=[spec] * 4,
        out_shape=[jax.ShapeDtypeStruct((R, W), F32)] * 4, compiler_params=_cparams(("parallel",)),
    )(w2, m2, v2, *gs)
    return tuple(o.reshape(shape) for o in outs)


class Dims:
    pass


def _sel(dm):
    return (pl.program_id(0) >= dm.nctx).astype(jnp.int32)


def norm_mod(name, h, gn, modtab, s, dm):
    T, D = h.shape
    tm = dm.tme

    def body(h_ref, g_ref, m_ref, u_ref):
        sel = _sel(dm)
        x = h_ref[...]
        r = lax.rsqrt(jnp.mean(x * x, axis=-1, keepdims=True) + EPS)
        ng = x * r * g_ref[...]
        u_ref[...] = (ng * (1.0 + m_ref[sel, 3 * s + 1]) + m_ref[sel, 3 * s]).astype(u_ref.dtype)

    return pl.pallas_call(
        body, name=name, grid=(T // tm,),
        in_specs=[pl.BlockSpec((tm, D), lambda i: (i, 0)), pl.BlockSpec((1, D), lambda i: (0, 0)),
                  pl.BlockSpec((2, N_MOD, 1, D), lambda i: (0, 0, 0, 0))],
        out_specs=pl.BlockSpec((tm, D), lambda i: (i, 0)),
        out_shape=jax.ShapeDtypeStruct((T, D), BF16), compiler_params=_cparams(("parallel",)),
    )(h, gn, modtab)


def norm_mod_bwd(name, du, h, gn, modtab, s, dh_in, dm):
    T, D = h.shape
    tm = dm.tme

    def body(du_ref, h_ref, g_ref, m_ref, dhi_ref, dh_ref, dmod_ref, dg_ref):
        i = pl.program_id(0)
        sel = _sel(dm)

        @pl.when(i == 0)
        def _():
            dmod_ref[...] = jnp.zeros_like(dmod_ref)
            dg_ref[...] = jnp.zeros_like(dg_ref)

        x = h_ref[...]
        r = lax.rsqrt(jnp.mean(x * x, axis=-1, keepdims=True) + EPS)
        n = x * r
        g = g_ref[...]
        du = du_ref[...]
        dmod_ref[sel, 0] += jnp.sum(du, axis=0, keepdims=True)
        dmod_ref[sel, 1] += jnp.sum(du * (n * g), axis=0, keepdims=True)
        dng = du * (1.0 + m_ref[sel, 3 * s + 1])
        dg_ref[...] += jnp.sum(dng * n, axis=0, keepdims=True)
        dn = dng * g
        dh_ref[...] = dhi_ref[...] + r * (dn - n * jnp.mean(dn * n, axis=-1, keepdims=True))

    row = pl.BlockSpec((tm, D), lambda i: (i, 0))
    return pl.pallas_call(
        body, name=name, grid=(T // tm,),
        in_specs=[row, row, pl.BlockSpec((1, D), lambda i: (0, 0)),
                  pl.BlockSpec((2, N_MOD, 1, D), lambda i: (0, 0, 0, 0)), row],
        out_specs=[row, pl.BlockSpec((2, 2, 1, D), lambda i: (0, 0, 0, 0)), pl.BlockSpec((1, D), lambda i: (0, 0))],
        out_shape=[jax.ShapeDtypeStruct((T, D), F32), jax.ShapeDtypeStruct((2, 2, 1, D), F32),
                   jax.ShapeDtypeStruct((1, D), F32)],
        compiler_params=_cparams(("arbitrary",)),
    )(du, h, gn, modtab, dh_in)


def resid(name, h, f, modtab, s, coef, dm):
    T, D = h.shape
    tm = dm.tme

    def body(h_ref, f_ref, m_ref, o_ref):
        o_ref[...] = h_ref[...] + (coef * m_ref[_sel(dm), 3 * s + 2]) * f_ref[...]

    row = pl.BlockSpec((tm, D), lambda i: (i, 0))
    return pl.pallas_call(
        body, name=name, grid=(T // tm,),
        in_specs=[row, row, pl.BlockSpec((2, N_MOD, 1, D), lambda i: (0, 0, 0, 0))], out_specs=row,
        out_shape=jax.ShapeDtypeStruct((T, D), F32), compiler_params=_cparams(("parallel",)),
    )(h, f, modtab)


def resid_bwd(name, dh, f, modtab, s, coef, dm):
    T, D = dh.shape
    tm = dm.tme

    def body(dh_ref, f_ref, m_ref, df_ref, dg_ref):
        sel = _sel(dm)

        @pl.when(pl.program_id(0) == 0)
        def _():
            dg_ref[...] = jnp.zeros_like(dg_ref)

        d = coef * dh_ref[...]
        df_ref[...] = (d * m_ref[sel, 3 * s + 2]).astype(df_ref.dtype)
        dg_ref[sel, 0] += jnp.sum(d * f_ref[...], axis=0, keepdims=True)

    row = pl.BlockSpec((tm, D), lambda i: (i, 0))
    return pl.pallas_call(
        body, name=name, grid=(T // tm,),
        in_specs=[row, row, pl.BlockSpec((2, N_MOD, 1, D), lambda i: (0, 0, 0, 0))],
        out_specs=[row, pl.BlockSpec((2, 1, 1, D), lambda i: (0, 0, 0, 0))],
        out_shape=[jax.ShapeDtypeStruct((T, D), BF16), jax.ShapeDtypeStruct((2, 1, 1, D), F32)],
        compiler_params=_cparams(("arbitrary",)),
    )(dh, f, modtab)


def swiglu(name, gu, dm):
    _, T, Nq = gu.shape
    tm = dm.tme
    gu4 = gu.reshape(2, 2, T, Nq)

    def body(gu_ref, o_ref):
        g = gu_ref[0].astype(F32)
        o_ref[...] = (_silu(g) * gu_ref[1].astype(F32)).astype(o_ref.dtype)

    return pl.pallas_call(
        body, name=name, grid=(2, T // tm),
        in_specs=[pl.BlockSpec((2, None, tm, Nq), lambda k, i: (0, k, i, 0))],
        out_specs=pl.BlockSpec((None, tm, Nq), lambda k, i: (k, i, 0)),
        out_shape=jax.ShapeDtypeStruct((2, T, Nq), BF16), compiler_params=_cparams(("parallel", "parallel")),
    )(gu4)


def swiglu_bwd(name, dact, gu, dm):
    _, T, Nq = gu.shape
    tm = dm.tme
    gu4 = gu.reshape(2, 2, T, Nq)

    def body(da_ref, gu_ref, o_ref):
        g = gu_ref[0].astype(F32)
        da = da_ref[...].astype(F32)
        o_ref[0] = (da * gu_ref[1].astype(F32) * _dsilu(g)).astype(o_ref.dtype)
        o_ref[1] = (da * _silu(g)).astype(o_ref.dtype)

    out = pl.pallas_call(
        body, name=name, grid=(2, T // tm),
        in_specs=[pl.BlockSpec((None, tm, Nq), lambda k, i: (k, i, 0)),
                  pl.BlockSpec((2, None, tm, Nq), lambda k, i: (0, k, i, 0))],
        out_specs=pl.BlockSpec((2, None, tm, Nq), lambda k, i: (0, k, i, 0)),
        out_shape=jax.ShapeDtypeStruct((2, 2, T, Nq), BF16), compiler_params=_cparams(("parallel", "parallel")),
    )(dact, gu4)
    return out.reshape(4, T, Nq)


def _halo_specs(dm, width, col):
    tm = dm.tme
    per = tm // HALO
    last = dm.T // HALO - 1
    return [pl.BlockSpec((tm, width), lambda i: (i, col)),
            pl.BlockSpec((HALO, width), lambda i: (jnp.maximum(i * per - 1, 0), col)),
            pl.BlockSpec((HALO, width), lambda i: (jnp.minimum((i + 1) * per, last), col))]


def _segment_edges(dm, i):
    first = jnp.logical_or(i == 0, i == dm.nctx)
    last = jnp.logical_or(i == dm.nctx - 1, i == dm.nt - 1)
    return first, last


def _extend(main, prev, nxt, first, last):
    return jnp.concatenate([jnp.where(first, 0.0, prev), main, jnp.where(last, 0.0, nxt)], axis=0)


def _shift(ext, o, tm):
    n = ext.shape[0]
    rolled = ext if o == 0 else pltpu.roll(ext, (-o) % n, 0)
    return rolled[HALO:HALO + tm]


def _load_ext(refs, first, last):
    main, prev, nxt = refs
    return _extend(main[...].astype(F32), prev[...].astype(F32), nxt[...].astype(F32), first, last)


def rnn_conv(name, z, w, b, dm):
    T, RW, tm = dm.T, dm.RW, dm.tme

    def body(main, prev, nxt, w_ref, b_ref, o_ref):
        first, last = _segment_edges(dm, pl.program_id(0))
        ext = _load_ext((main, prev, nxt), first, last)
        acc = jnp.zeros((tm, RW), F32) + b_ref[...]
        for k in range(4):
            acc = acc + w_ref[k] * _shift(ext, k - 2, tm)
        o_ref[...] = acc

    return pl.pallas_call(
        body, name=name, grid=(dm.nt,),
        in_specs=_halo_specs(dm, RW, 0) + [pl.BlockSpec((4, 1, RW), lambda i: (0, 0, 0)),
                                           pl.BlockSpec((1, RW), lambda i: (0, 0))],
        out_specs=pl.BlockSpec((tm, RW), lambda i: (i, 0)),
        out_shape=jax.ShapeDtypeStruct((T, RW), F32), compiler_params=_cparams(("parallel",)),
    )(z, z, z, w, b)


def _blockdiag(x, w_ref):
    nb = w_ref.shape[0]
    outs = [jnp.dot(x[:, n * RNN_BLOCK:(n + 1) * RNN_BLOCK], w_ref[n].astype(BF16), preferred_element_type=F32)
            for n in range(nb)]
    return jnp.concatenate(outs, axis=-1)


def _lru_gates(xa, wa_ref, ba_ref, wx_ref, bx_ref, lam_ref):
    xb = xa.astype(BF16)
    r = jax.nn.sigmoid(_blockdiag(xb, wa_ref) + ba_ref[...])
    ig = jax.nn.sigmoid(_blockdiag(xb, wx_ref) + bx_ref[...])
    nl = -lam_ref[...]
    sp = jnp.maximum(nl, 0.0) + jnp.log(1.0 + jnp.exp(-jnp.abs(nl)))
    log_a = -LRU_C * r * sp
    a = jnp.exp(log_a)
    m = jnp.sqrt(-_expm1(2.0 * log_a))
    return r, ig, sp, a, m


def _lru_specs(l, d, nb, RW):
    wspec = pl.BlockSpec((None, None, nb, RNN_BLOCK, RNN_BLOCK), lambda i: (l, d, 0, 0, 0))
    vspec = pl.BlockSpec((None, None, 1, RW), lambda i: (l, d, 0, 0))
    return [wspec, vspec, wspec, vspec, vspec]


def lru_gates(name, xa, lw, l, d, dm):
    T, RW, tm = dm.T, dm.RW, dm.tme

    def body(xa_ref, wa_ref, ba_ref, wx_ref, bx_ref, lam_ref, a_ref, u_ref):
        xa_v = xa_ref[...]
        r, ig, sp, a, m = _lru_gates(xa_v, wa_ref, ba_ref, wx_ref, bx_ref, lam_ref)
        a_ref[...] = a
        u_ref[...] = m * (ig * xa_v)

    row = pl.BlockSpec((tm, RW), lambda i: (i, 0))
    return pl.pallas_call(
        body, name=name, grid=(dm.nt,),
        in_specs=[row] + _lru_specs(l, d, dm.NB, RW), out_specs=[row, row],
        out_shape=[jax.ShapeDtypeStruct((T, RW), F32)] * 2, compiler_params=_cparams(("parallel",)),
    )(xa, lw["w_a"], lw["b_a"], lw["w_x"], lw["b_x"], lw["lam"])


def lru_gates_bwd(name, xa, lw, l, d, du, dloga, dm):
    T, RW, tm, NB = dm.T, dm.RW, dm.tme, dm.NB

    def body(xa_ref, wa_ref, ba_ref, wx_ref, bx_ref, lam_ref, du_ref, dla_ref,
             dxa_ref, dwa_ref, dba_ref, dwx_ref, dbx_ref, dlam_ref):
        @pl.when(pl.program_id(0) == 0)
        def _():
            for ref in (dwa_ref, dba_ref, dwx_ref, dbx_ref, dlam_ref):
                ref[...] = jnp.zeros_like(ref)

        xa_v = xa_ref[...]
        r, ig, sp, a, m = _lru_gates(xa_v, wa_ref, ba_ref, wx_ref, bx_ref, lam_ref)
        duu = du_ref[...]
        dm_ = duu * (ig * xa_v)
        dig = duu * m * xa_v
        dxa = duu * m * ig
        dla = dla_ref[...] - dm_ * (a * a) / m
        dr = dla * (-LRU_C * sp)
        dsp = jnp.sum(dla * (-LRU_C * r), axis=0, keepdims=True)
        dlam_ref[...] += dsp * (-jax.nn.sigmoid(-lam_ref[...]))
        dpa = dr * r * (1.0 - r)
        dpx = dig * ig * (1.0 - ig)
        dba_ref[...] += jnp.sum(dpa, axis=0, keepdims=True)
        dbx_ref[...] += jnp.sum(dpx, axis=0, keepdims=True)
        xb, dpab, dpxb = xa_v.astype(BF16), dpa.astype(BF16), dpx.astype(BF16)
        back = []
        for n in range(NB):
            sl = slice(n * RNN_BLOCK, (n + 1) * RNN_BLOCK)
            dwa_ref[n] += lax.dot_general(xb[:, sl], dpab[:, sl], _DIMS["tn"], preferred_element_type=F32)
            dwx_ref[n] += lax.dot_general(xb[:, sl], dpxb[:, sl], _DIMS["tn"], preferred_element_type=F32)
            back.append(lax.dot_general(dpab[:, sl], wa_ref[n].astype(BF16), _DIMS["nt"], preferred_element_type=F32)
                        + lax.dot_general(dpxb[:, sl], wx_ref[n].astype(BF16), _DIMS["nt"], preferred_element_type=F32))
        dxa_ref[...] = dxa + jnp.concatenate(back, axis=-1)

    row = pl.BlockSpec((tm, RW), lambda i: (i, 0))
    wacc = pl.BlockSpec((NB, RNN_BLOCK, RNN_BLOCK), lambda i: (0, 0, 0))
    vacc = pl.BlockSpec((1, RW), lambda i: (0, 0))
    wshape = jax.ShapeDtypeStruct((NB, RNN_BLOCK, RNN_BLOCK), F32)
    vshape = jax.ShapeDtypeStruct((1, RW), F32)
    return pl.pallas_call(
        body, name=name, grid=(dm.nt,),
        in_specs=[row] + _lru_specs(l, d, NB, RW) + [row, row],
        out_specs=[row, wacc, vacc, wacc, vacc, vacc],
        out_shape=[jax.ShapeDtypeStruct((T, RW), F32), wshape, vshape, wshape, vshape, vshape],
        compiler_params=_cparams(("arbitrary",)),
    )(xa, lw["w_a"], lw["b_a"], lw["w_x"], lw["b_x"], lw["lam"], du, dloga)


def _chunk_order(dm, ctx_first, descending):
    nch, nctx = dm.nt, dm.nctx
    nlat = nch - nctx

    def order(s):
        if ctx_first and not descending:
            return s
        if not ctx_first and descending:
            return nch - 1 - s
        if ctx_first:
            return jnp.where(s < nctx, nctx - 1 - s, nch - 1 - (s - nctx))
        return jnp.where(s < nlat, nctx + s, s - nlat)

    return order


def _tile_scan(a, b, carry, descending):
    row = lax.broadcasted_iota(jnp.int32, a.shape, 0)
    for s in (1, 2, 4):
        sh = (HALO - s) if descending else s
        keep = (row < HALO - s) if descending else (row >= s)
        ap = pltpu.roll(a, sh, 0)
        bp = pltpu.roll(b, sh, 0)
        b = jnp.where(keep, b + a * bp, b)
        a = jnp.where(keep, a * ap, a)
    h = b + a * carry
    edge = 0 if descending else HALO - 1
    new_carry = jnp.sum(jnp.where(row == edge, h, 0.0), axis=0, keepdims=True)
    return h, new_carry


def lru_scan(name, a, u, ctx_first, descending, dm):
    T, RW, ch = dm.T, dm.RW, dm.tme
    order = _chunk_order(dm, ctx_first, descending)
    ngrp = ch // HALO

    def body(a_ref, u_ref, h_ref, carry_ref):
        @pl.when(pl.program_id(0) == 0)
        def _():
            carry_ref[...] = jnp.zeros_like(carry_ref)

        def step(g, carry):
            g = (ngrp - 1 - g) if descending else g
            rows = pl.ds(pl.multiple_of(g * HALO, HALO), HALO)
            h, carry = _tile_scan(a_ref[rows, :], u_ref[rows, :], carry, descending)
            h_ref[rows, :] = h
            return carry

        carry_ref[...] = lax.fori_loop(0, ngrp, step, carry_ref[...])

    row = pl.BlockSpec((ch, RW), lambda s: (order(s), 0))
    return pl.pallas_call(
        body, name=name, grid=(dm.nt,),
        in_specs=[row, row], out_specs=row,
        out_shape=jax.ShapeDtypeStruct((T, RW), F32),
        scratch_shapes=[pltpu.VMEM((1, RW), F32)], compiler_params=_cparams(("arbitrary",)),
    )(a, u)


def lru_scan_bwd(name, a, u, h, dh, ctx_first, descending, dm):
    T, RW, ch = dm.T, dm.RW, dm.tme
    order = _chunk_order(dm, ctx_first, descending)
    ngrp = ch // HALO

    def body(a_ref, u_ref, h_ref, dh_ref, lam_ref, dla_ref, carry_ref):
        @pl.when(pl.program_id(0) == 0)
        def _():
            carry_ref[...] = jnp.zeros_like(carry_ref)

        def step(g, carry):
            g = (ngrp - 1 - g) if descending else g
            rows = pl.ds(pl.multiple_of(g * HALO, HALO), HALO)
            a_v, dh_v = a_ref[rows, :], dh_ref[rows, :]
            mu, new_carry = _tile_scan(a_v, a_v * dh_v, carry, descending)
            row = lax.broadcasted_iota(jnp.int32, mu.shape, 0)
            if descending:
                nxt = jnp.where(row == HALO - 1, carry, pltpu.roll(mu, HALO - 1, 0))
            else:
                nxt = jnp.where(row == 0, carry, pltpu.roll(mu, 1, 0))
            lam = dh_v + nxt
            lam_ref[rows, :] = lam
            dla_ref[rows, :] = lam * (h_ref[rows, :] - u_ref[rows, :])
            return new_carry

        carry_ref[...] = lax.fori_loop(0, ngrp, step, carry_ref[...])

    row = pl.BlockSpec((ch, RW), lambda s: (order(s), 0))
    return pl.pallas_call(
        body, name=name, grid=(dm.nt,),
        in_specs=[row] * 4, out_specs=[row, row],
        out_shape=[jax.ShapeDtypeStruct((T, RW), F32)] * 2,
        scratch_shapes=[pltpu.VMEM((1, RW), F32)], compiler_params=_cparams(("arbitrary",)),
    )(a, u, h, dh)


def rnn_out(name, hf, hb, z, dm):
    T, RW, tm = dm.T, dm.RW, dm.tme

    def body(hf_ref, hb_ref, rg_ref, o_ref):
        o_ref[...] = ((hf_ref[...] + hb_ref[...]) * _gelu(rg_ref[...])).astype(o_ref.dtype)

    row = pl.BlockSpec((tm, RW), lambda i: (i, 0))
    return pl.pallas_call(
        body, name=name, grid=(dm.nt,),
        in_specs=[row, row, pl.BlockSpec((tm, RW), lambda i: (i, 1))], out_specs=row,
        out_shape=jax.ShapeDtypeStruct((T, RW), BF16), compiler_params=_cparams(("parallel",)),
    )(hf, hb, z)


def rnn_out_bwd(name, dya, hf, hb, z, dm):
    T, RW, tm = dm.T, dm.RW, dm.tme

    def body(d_ref, hf_ref, hb_ref, rg_ref, dh_ref, drg_ref):
        d, rg = d_ref[...], rg_ref[...]
        dh_ref[...] = d * _gelu(rg)
        drg_ref[...] = d * (hf_ref[...] + hb_ref[...]) * _dgelu(rg)

    row = pl.BlockSpec((tm, RW), lambda i: (i, 0))
    return pl.pallas_call(
        body, name=name, grid=(dm.nt,),
        in_specs=[row, row, row, pl.BlockSpec((tm, RW), lambda i: (i, 1))], out_specs=[row, row],
        out_shape=[jax.ShapeDtypeStruct((T, RW), F32)] * 2, compiler_params=_cparams(("parallel",)),
    )(dya, hf, hb, z)


def rnn_conv_bwd(name, dxa_f, dxa_b, drg, z, w, dz, dm):
    T, RW, tm = dm.T, dm.RW, dm.tme

    def body(f0, f1, f2, b0, b1, b2, x0, x1, x2, drg_ref, w_ref, dz_in, dz_ref, dw_ref, db_ref):
        i = pl.program_id(0)

        @pl.when(i == 0)
        def _():
            dw_ref[...] = jnp.zeros_like(dw_ref)
            db_ref[...] = jnp.zeros_like(db_ref)

        first, last = _segment_edges(dm, i)
        dext = _load_ext((f0, f1, f2), first, last) + _load_ext((b0, b1, b2), first, last)
        xext = _load_ext((x0, x1, x2), first, last)
        dmain = dext[HALO:HALO + tm]
        drx = jnp.zeros((tm, RW), F32)
        for k in range(4):
            drx = drx + w_ref[k] * _shift(dext, -(k - 2), tm)
            dw_ref[k] += jnp.sum(dmain * _shift(xext, k - 2, tm), axis=0, keepdims=True)
        db_ref[...] += jnp.sum(dmain, axis=0, keepdims=True)
        dz_ref[:, :RW] = drx.astype(dz_ref.dtype)
        dz_ref[:, RW:] = drg_ref[...].astype(dz_ref.dtype)

    return pl.pallas_call(
        body, name=name, grid=(dm.nt,),
        in_specs=_halo_specs(dm, RW, 0) * 3 + [pl.BlockSpec((tm, RW), lambda i: (i, 0)),
                                               pl.BlockSpec((4, 1, RW), lambda i: (0, 0, 0)),
                                               pl.BlockSpec(memory_space=pl.ANY)],
        out_specs=[pl.BlockSpec((tm, 2 * RW), lambda i: (i, 0)), pl.BlockSpec((4, 1, RW), lambda i: (0, 0, 0)),
                   pl.BlockSpec((1, RW), lambda i: (0, 0))],
        out_shape=[jax.ShapeDtypeStruct(dz.shape, dz.dtype), jax.ShapeDtypeStruct((4, 1, RW), F32),
                   jax.ShapeDtypeStruct((1, RW), F32)],
        input_output_aliases={11: 0}, compiler_params=_cparams(("arbitrary",)),
    )(dxa_f, dxa_f, dxa_f, dxa_b, dxa_b, dxa_b, z, z, z, drg, w, dz)


def short_conv(name, z, w, dm):
    T, RW, tm = dm.T, dm.RW, dm.tme

    def body(sb_ref, g0, g1, g2, x0, x1, x2, w_ref, o_ref):
        first, last = _segment_edges(dm, pl.program_id(0))
        pext = _load_ext((g0, g1, g2), first, last) * _load_ext((x0, x1, x2), first, last)
        cp = jnp.zeros((tm, RW), F32)
        for k in range(3):
            cp = cp + w_ref[k] * _shift(pext, k - 1, tm)
        o_ref[...] = (sb_ref[...] * cp).astype(o_ref.dtype)

    return pl.pallas_call(
        body, name=name, grid=(dm.nt,),
        in_specs=[pl.BlockSpec((tm, RW), lambda i: (i, 2))] + _halo_specs(dm, RW, 3) + _halo_specs(dm, RW, 4)
        + [pl.BlockSpec((3, 1, RW), lambda i: (0, 0, 0))],
        out_specs=pl.BlockSpec((tm, RW), lambda i: (i, 0)),
        out_shape=jax.ShapeDtypeStruct((T, RW), BF16), compiler_params=_cparams(("parallel",)),
    )(z, z, z, z, z, z, z, w)


def short_conv_bwd(name, dyb, z, w, dz, dm):
    T, RW, tm = dm.T, dm.RW, dm.tme

    def spec3(col):
        per = tm // HALO
        last = T // HALO - 1
        return [pl.BlockSpec((tm, RW), lambda i, p: (i, col)),
                pl.BlockSpec((HALO, RW), lambda i, p: (jnp.maximum(i * per - 1, 0), col)),
                pl.BlockSpec((HALO, RW), lambda i, p: (jnp.minimum((i + 1) * per, last), col))]

    def body(d0, d1, d2, s0, s1, s2, g0, g1, g2, x0, x1, x2, w_ref, dz_in, dz_ref, dw_ref):
        i, p = pl.program_id(0), pl.program_id(1)

        @pl.when(jnp.logical_and(i == 0, p == 0))
        def _():
            dw_ref[...] = jnp.zeros_like(dw_ref)

        first, last = _segment_edges(dm, i)
        gext = _load_ext((g0, g1, g2), first, last)
        xext = _load_ext((x0, x1, x2), first, last)
        pext = gext * xext
        dyext = _load_ext((d0, d1, d2), first, last)
        dcext = dyext * _load_ext((s0, s1, s2), first, last)
        dcmain = dcext[HALO:HALO + tm]
        cp = jnp.zeros((tm, RW), F32)
        dp = jnp.zeros((tm, RW), F32)
        for k in range(3):
            pk = _shift(pext, k - 1, tm)
            cp = cp + w_ref[k] * pk
            dp = dp + w_ref[k] * _shift(dcext, -(k - 1), tm)

            @pl.when(p == 0)
            def _(k=k, pk=pk):
                dw_ref[k] += jnp.sum(dcmain * pk, axis=0, keepdims=True)

        dsb = dyext[HALO:HALO + tm] * cp
        dscg = dp * xext[HALO:HALO + tm]
        dsx = dp * gext[HALO:HALO + tm]
        dz_ref[...] = jnp.where(p == 0, dsb, jnp.where(p == 1, dscg, dsx)).astype(dz_ref.dtype)

    return pl.pallas_call(
        body, name=name, grid=(dm.nt, 3),
        in_specs=spec3(0) + spec3(2) + spec3(3) + spec3(4)
        + [pl.BlockSpec((3, 1, RW), lambda i, p: (0, 0, 0)), pl.BlockSpec(memory_space=pl.ANY)],
        out_specs=[pl.BlockSpec((tm, RW), lambda i, p: (i, 2 + p)), pl.BlockSpec((3, 1, RW), lambda i, p: (0, 0, 0))],
        out_shape=[jax.ShapeDtypeStruct(dz.shape, dz.dtype), jax.ShapeDtypeStruct((3, 1, RW), F32)],
        input_output_aliases={13: 0}, compiler_params=_cparams(("arbitrary", "arbitrary")),
    )(dyb, dyb, dyb, z, z, z, z, z, z, z, z, z, w, dz)


def _rope_tables(dm):
    L, C = dm.L, dm.C
    half = HEAD_DIM // 2
    pos = jnp.arange(L)
    row = (pos // GRID_W).astype(F32)
    col = (pos % GRID_W).astype(F32)
    inv = ROPE_BASE ** (-jnp.arange(0, half, 2, dtype=F32) / half)
    ar, ac = row[:, None] * inv, col[:, None] * inv
    cos = jnp.concatenate([jnp.cos(ar), jnp.cos(ar), jnp.cos(ac), jnp.cos(ac)], axis=-1)
    sin = jnp.concatenate([-jnp.sin(ar), jnp.sin(ar), -jnp.sin(ac), jnp.sin(ac)], axis=-1)
    cos = jnp.concatenate([jnp.ones((C, HEAD_DIM), F32), cos], axis=0)
    sin = jnp.concatenate([jnp.zeros((C, HEAD_DIM), F32), sin], axis=0)
    return cos, sin


def _swap_pairs(x):
    quarter = HEAD_DIM // 4
    lane = lax.broadcasted_iota(jnp.int32, x.shape, 1)
    return jnp.where(lane % (2 * quarter) < quarter, pltpu.roll(x, HEAD_DIM - quarter, 1), pltpu.roll(x, quarter, 1))


def _rope(x, cos, sin):
    return x * cos + _swap_pairs(x) * sin


def _unrope(d, cos, sin):
    return d * cos + _swap_pairs(d * sin)


def qkv_prep(name, z, cos, sin, dm):
    T, tm, HQ, KW = dm.T, dm.tme, dm.HQ, dm.KW
    qcol, kcol = dm.off_q // HQ, dm.off_k // KW

    def body(q_ref, k_ref, v_ref, c_ref, s_ref, qo, ko, vo):
        cos_v, sin_v = c_ref[...], s_ref[...]
        for hd in range(HQ // HEAD_DIM):
            sl = slice(hd * HEAD_DIM, (hd + 1) * HEAD_DIM)
            qo[:, sl] = _rope(q_ref[:, sl], cos_v, sin_v).astype(qo.dtype)
        for hd in range(KW // HEAD_DIM):
            sl = slice(hd * HEAD_DIM, (hd + 1) * HEAD_DIM)
            ko[:, sl] = _rope(k_ref[:, sl], cos_v, sin_v).astype(ko.dtype)
        vo[...] = v_ref[...].astype(vo.dtype)

    tab = pl.BlockSpec((tm, HEAD_DIM), lambda i: (i, 0))
    return pl.pallas_call(
        body, name=name, grid=(dm.nt,),
        in_specs=[pl.BlockSpec((tm, HQ), lambda i: (i, qcol)), pl.BlockSpec((tm, KW), lambda i: (i, kcol)),
                  pl.BlockSpec((tm, KW), lambda i: (i, kcol + 1)), tab, tab],
        out_specs=[pl.BlockSpec((tm, HQ), lambda i: (i, 0)), pl.BlockSpec((tm, KW), lambda i: (i, 0)),
                   pl.BlockSpec((tm, KW), lambda i: (i, 0))],
        out_shape=[jax.ShapeDtypeStruct((T, HQ), BF16), jax.ShapeDtypeStruct((T, KW), BF16),
                   jax.ShapeDtypeStruct((T, KW), BF16)],
        compiler_params=_cparams(("parallel",)),
    )(z, z, z, cos, sin)


def qkv_bwd(name, dq, dk, dv, cos, sin, dz, dm):
    T, tm, HQ, KW = dm.T, dm.tme, dm.HQ, dm.KW
    nq = HQ // KW
    base = dm.off_q // KW

    def body(dq_ref, dk_ref, dv_ref, c_ref, s_ref, dz_in, dz_ref):
        p = pl.program_id(1)
        src = jnp.where(p < nq, dq_ref[...], jnp.where(p == nq, dk_ref[...], dv_ref[...]))
        cos_v, sin_v = c_ref[...], s_ref[...]
        is_v = p == nq + 1
        for hd in range(KW // HEAD_DIM):
            sl = slice(hd * HEAD_DIM, (hd + 1) * HEAD_DIM)
            dz_ref[:, sl] = jnp.where(is_v, src[:, sl], _unrope(src[:, sl], cos_v, sin_v)).astype(dz_ref.dtype)

    tab = pl.BlockSpec((tm, HEAD_DIM), lambda i, p: (i, 0))
    blk = pl.BlockSpec((tm, KW), lambda i, p: (i, 0))
    return pl.pallas_call(
        body, name=name, grid=(dm.nt, nq + 2),
        in_specs=[pl.BlockSpec((tm, KW), lambda i, p: (i, jnp.minimum(p, nq - 1))), blk, blk, tab, tab,
                  pl.BlockSpec(memory_space=pl.ANY)],
        out_specs=pl.BlockSpec((tm, KW), lambda i, p: (i, base + p)),
        out_shape=jax.ShapeDtypeStruct(dz.shape, dz.dtype),
        input_output_aliases={5: 0}, compiler_params=_cparams(("parallel", "arbitrary")),
    )(dq, dk, dv, cos, sin, dz)


def _attn_specs(dm):
    nC, nB, C, KW = dm.C // Q_BLOCK, dm.T // Q_BLOCK, dm.C, dm.KW

    def near(o):
        return lambda b: (jnp.clip(b + o, nC, nB - 1), 0)

    kv = [pl.BlockSpec((Q_BLOCK, KW), near(o)) for o in (-1, 0, 1)] + [pl.BlockSpec((C, KW), lambda b: (0, 0))]
    return kv


def _attn_mask(dm, b):
    nC, C, L = dm.C // Q_BLOCK, dm.C, dm.L
    span = 3 * Q_BLOCK
    n = b - nC
    iq = lax.broadcasted_iota(jnp.int32, (Q_BLOCK, span + C), 0)
    ik = lax.broadcasted_iota(jnp.int32, (Q_BLOCK, span + C), 1)
    kpos = n * Q_BLOCK + ik - Q_BLOCK
    qpos = n * Q_BLOCK + iq
    local = (b >= nC) & (jnp.abs(qpos - kpos) <= WINDOW) & (kpos >= 0) & (kpos < L)
    return jnp.logical_or(ik >= span, local)


def attention(name, q, k, v, sink, dm):
    T, HQ, KW = dm.T, dm.HQ, dm.KW
    H, KV = HQ // HEAD_DIM, KW // HEAD_DIM
    G = H // KV
    scale = HEAD_DIM ** -0.5

    def body(q_ref, kp, kc, kn, kx, vp, vc, vn, vx, sink_ref, o_ref, lse_ref):
        valid = _attn_mask(dm, pl.program_id(0))
        lane = lax.broadcasted_iota(jnp.int32, (Q_BLOCK, LSE_W), 1)
        lse_all = jnp.zeros((Q_BLOCK, LSE_W), F32)
        for kh in range(KV):
            ks = slice(kh * HEAD_DIM, (kh + 1) * HEAD_DIM)
            k_all = jnp.concatenate([kp[:, ks], kc[:, ks], kn[:, ks], kx[:, ks]], axis=0)
            v_all = jnp.concatenate([vp[:, ks], vc[:, ks], vn[:, ks], vx[:, ks]], axis=0)
            for g in range(G):
                hd = kh * G + g
                hs = slice(hd * HEAD_DIM, (hd + 1) * HEAD_DIM)
                s = lax.dot_general(q_ref[:, hs], k_all, _DIMS["nt"], preferred_element_type=F32) * scale
                s = jnp.where(valid, s, NEG_INF)
                snk = sink_ref[0, hd]
                mx = jnp.maximum(jnp.max(s, axis=-1, keepdims=True), snk)
                p = jnp.exp(s - mx)
                den = jnp.sum(p, axis=-1, keepdims=True) + jnp.exp(snk - mx)
                o = jnp.dot(p.astype(BF16), v_all, preferred_element_type=F32) / den
                o_ref[:, hs] = o.astype(o_ref.dtype)
                lse_all = jnp.where(lane == hd, mx + jnp.log(den), lse_all)
        lse_ref[...] = lse_all

    kv = _attn_specs(dm)
    return pl.pallas_call(
        body, name=name, grid=(T // Q_BLOCK,),
        in_specs=[pl.BlockSpec((Q_BLOCK, HQ), lambda b: (b, 0))] + kv + kv + [pl.BlockSpec(memory_space=pltpu.SMEM)],
        out_specs=[pl.BlockSpec((Q_BLOCK, HQ), lambda b: (b, 0)), pl.BlockSpec((Q_BLOCK, LSE_W), lambda b: (b, 0))],
        out_shape=[jax.ShapeDtypeStruct((T, HQ), BF16), jax.ShapeDtypeStruct((T, LSE_W), F32)],
        compiler_params=_cparams(("parallel",)),
    )(q, k, k, k, k, v, v, v, v, sink)


def attention_bwd(name, q, k, v, sink, o, lse, do, dm):
    T, HQ, KW, C = dm.T, dm.HQ, dm.KW, dm.C
    H, KV = HQ // HEAD_DIM, KW // HEAD_DIM
    G = H // KV
    nC, nB = C // Q_BLOCK, T // Q_BLOCK
    scale = HEAD_DIM ** -0.5
    span = 3 * Q_BLOCK

    def body(q_ref, kp, kc, kn, kx, vp, vc, vn, vx, sink_ref, o_ref, lse_ref, do_ref,
             dq_ref, dk_ref, dv_ref, ds_ref):
        b = pl.program_id(0)

        @pl.when(b == 0)
        def _():
            dk_ref[...] = jnp.zeros_like(dk_ref)
            dv_ref[...] = jnp.zeros_like(dv_ref)
            ds_ref[...] = jnp.zeros_like(ds_ref)

        valid = _attn_mask(dm, b)
        starts = [pl.multiple_of(jnp.clip(b + off, nC, nB - 1) * Q_BLOCK, Q_BLOCK) for off in (-1, 0, 1)]
        lane = lax.broadcasted_iota(jnp.int32, (Q_BLOCK, LSE_W), 1)
        lse_all = lse_ref[...]
        dsink = jnp.zeros((1, LSE_W), F32)
        for kh in range(KV):
            ks = slice(kh * HEAD_DIM, (kh + 1) * HEAD_DIM)
            k_all = jnp.concatenate([kp[:, ks], kc[:, ks], kn[:, ks], kx[:, ks]], axis=0)
            v_all = jnp.concatenate([vp[:, ks], vc[:, ks], vn[:, ks], vx[:, ks]], axis=0)
            dk_all = jnp.zeros((span + C, HEAD_DIM), F32)
            dv_all = jnp.zeros((span + C, HEAD_DIM), F32)
            for g in range(G):
                hd = kh * G + g
                hs = slice(hd * HEAD_DIM, (hd + 1) * HEAD_DIM)
                qh = q_ref[:, hs]
                doh = do_ref[:, hs]
                s = lax.dot_general(qh, k_all, _DIMS["nt"], preferred_element_type=F32) * scale
                s = jnp.where(valid, s, NEG_INF)
                lse_h = jnp.sum(jnp.where(lane == hd, lse_all, 0.0), axis=-1, keepdims=True)
                p = jnp.exp(s - lse_h)
                delta = jnp.sum(doh * o_ref[:, hs].astype(F32), axis=-1, keepdims=True)
                dob = doh.astype(BF16)
                dp = lax.dot_general(dob, v_all, _DIMS["nt"], preferred_element_type=F32)
                dsc = (p * (dp - delta) * scale).astype(BF16)
                dq_ref[:, hs] = jnp.dot(dsc, k_all, preferred_element_type=F32)
                dk_all = dk_all + lax.dot_general(dsc, qh, _DIMS["tn"], preferred_element_type=F32)
                dv_all = dv_all + lax.dot_general(p.astype(BF16), dob, _DIMS["tn"], preferred_element_type=F32)
                p_sink = jnp.exp(sink_ref[0, hd] - lse_h)
                dsink = dsink + jnp.where(lane[0:1] == hd, -jnp.sum(p_sink * delta), 0.0)
            for j, st in enumerate(starts):
                rows = pl.ds(st, Q_BLOCK)
                dk_ref[rows, ks] += dk_all[j * Q_BLOCK:(j + 1) * Q_BLOCK]
                dv_ref[rows, ks] += dv_all[j * Q_BLOCK:(j + 1) * Q_BLOCK]
            dk_ref[0:C, ks] += dk_all[span:]
            dv_ref[0:C, ks] += dv_all[span:]
        ds_ref[...] += dsink

    kv = _attn_specs(dm)
    qspec = pl.BlockSpec((Q_BLOCK, HQ), lambda b: (b, 0))
    full = pl.BlockSpec((T, KW), lambda b: (0, 0))
    return pl.pallas_call(
        body, name=name, grid=(nB,),
        in_specs=[qspec] + kv + kv + [pl.BlockSpec(memory_space=pltpu.SMEM), qspec,
                                      pl.BlockSpec((Q_BLOCK, LSE_W), lambda b: (b, 0)), qspec],
        out_specs=[qspec, full, full, pl.BlockSpec((1, LSE_W), lambda b: (0, 0))],
        out_shape=[jax.ShapeDtypeStruct((T, HQ), F32), jax.ShapeDtypeStruct((T, KW), F32),
                   jax.ShapeDtypeStruct((T, KW), F32), jax.ShapeDtypeStruct((1, LSE_W), F32)],
        compiler_params=_cparams(("arbitrary",)),
    )(q, k, k, k, k, v, v, v, v, sink, o, lse, do)


def merge(name, z, lifted, b_merge, dm):
    T, D, tm, cw = dm.T, dm.D, dm.tme, dm.cw
    gcol = dm.off_g // cw
    per = D // cw

    def body(g0, g1, g2, l0, l1, l2, b_ref, o_ref):
        acc = jnp.zeros((tm, cw), F32)
        for i, (g, lf) in enumerate(((g0, l0), (g1, l1), (g2, l2))):
            acc = acc + jax.nn.sigmoid(g[...] + b_ref[i]) * lf[...]
        o_ref[...] = acc.astype(o_ref.dtype)

    gspecs = [pl.BlockSpec((tm, cw), lambda i, j, br=br: (i, gcol + br * per + j)) for br in range(N_BRANCH)]
    blk = pl.BlockSpec((tm, cw), lambda i, j: (i, j))
    return pl.pallas_call(
        body, name=name, grid=(T // tm, per),
        in_specs=gspecs + [blk] * 3 + [pl.BlockSpec((N_BRANCH, 1, cw), lambda i, j: (0, 0, j))], out_specs=blk,
        out_shape=jax.ShapeDtypeStruct((T, D), BF16), compiler_params=_cparams(("parallel", "parallel")),
    )(z, z, z, *lifted, b_merge)


def merge_bwd(name, dmerged, z, lifted_br, b_merge, br, dz, dm):
    T, D, tm, cw = dm.T, dm.D, dm.tme, dm.cw
    gcol = dm.off_g // cw + br * (D // cw)
    per = D // cw

    def body(d_ref, g_ref, l_ref, b_ref, dz_in, dl_ref, dz_ref, db_ref):
        @pl.when(pl.program_id(1) == 0)
        def _():
            db_ref[...] = jnp.zeros_like(db_ref)

        d = d_ref[...]
        gate = jax.nn.sigmoid(g_ref[...] + b_ref[br])
        dl_ref[...] = (d * gate).astype(dl_ref.dtype)
        dg = d * l_ref[...] * gate * (1.0 - gate)
        dz_ref[...] = dg.astype(dz_ref.dtype)
        db_ref[...] += jnp.sum(dg, axis=0, keepdims=True)

    blk = pl.BlockSpec((tm, cw), lambda j, i: (i, j))
    zblk = pl.BlockSpec((tm, cw), lambda j, i: (i, gcol + j))
    return pl.pallas_call(
        body, name=name, grid=(per, T // tm),
        in_specs=[blk, zblk, blk, pl.BlockSpec((N_BRANCH, 1, cw), lambda j, i: (0, 0, j)),
                  pl.BlockSpec(memory_space=pl.ANY)],
        out_specs=[blk, zblk, pl.BlockSpec((1, cw), lambda j, i: (0, j))],
        out_shape=[jax.ShapeDtypeStruct((T, D), BF16), jax.ShapeDtypeStruct(dz.shape, dz.dtype),
                   jax.ShapeDtypeStruct((1, D), F32)],
        input_output_aliases={4: 1}, compiler_params=_cparams(("parallel", "arbitrary")),
    )(dmerged, z, lifted_br, b_merge, dz)


def loss_head(name, h, gf, target, dm):
    T, D, tm, nctx = dm.T, dm.D, dm.tme, dm.nctx

    def body(h_ref, g_ref, t_ref, dh_ref, loss_ref, dg_ref):
        i = pl.program_id(0)

        @pl.when(i == 0)
        def _():
            loss_ref[...] = jnp.zeros_like(loss_ref)
            dg_ref[...] = jnp.zeros_like(dg_ref)

        @pl.when(i < nctx)
        def _():
            dh_ref[...] = jnp.zeros_like(dh_ref)

        @pl.when(i >= nctx)
        def _():
            x = h_ref[...]
            r = lax.rsqrt(jnp.mean(x * x, axis=-1, keepdims=True) + EPS)
            n = x * r
            g = g_ref[...]
            err = n * g - t_ref[...]
            loss_ref[...] += jnp.sum(err * err) * (0.5 / D)
            dy = err * (1.0 / D)
            dg_ref[...] += jnp.sum(dy * n, axis=0, keepdims=True)
            dn = dy * g
            dh_ref[...] = r * (dn - n * jnp.mean(dn * n, axis=-1, keepdims=True))

    row = pl.BlockSpec((tm, D), lambda i: (i, 0))
    return pl.pallas_call(
        body, name=name, grid=(T // tm,),
        in_specs=[row, pl.BlockSpec((1, D), lambda i: (0, 0)),
                  pl.BlockSpec((tm, D), lambda i: (jnp.maximum(i - nctx, 0), 0))],
        out_specs=[row, pl.BlockSpec((1, 128), lambda i: (0, 0)), pl.BlockSpec((1, D), lambda i: (0, 0))],
        out_shape=[jax.ShapeDtypeStruct((T, D), F32), jax.ShapeDtypeStruct((1, 128), F32),
                   jax.ShapeDtypeStruct((1, D), F32)],
        compiler_params=_cparams(("arbitrary",)),
    )(h, gf, target)


_HI = lax.Precision.HIGHEST
ADA_ROWS = 16


def ada_forward(name, cond, ada_w, bias):
    _, D, Aq = ada_w.shape
    tc = _pick(Aq, (1536, 1152, 768, 512, 384, 256, 128))
    tk = _ktile(D)
    nk = D // tk

    def body(c_ref, w_ref, b_ref, o_ref):
        k = pl.program_id(2)

        @pl.when(k == 0)
        def _():
            o_ref[...] = jnp.zeros_like(o_ref) + b_ref[...]

        o_ref[...] += jnp.dot(_silu(c_ref[...]), w_ref[...], precision=_HI, preferred_element_type=F32)

    return pl.pallas_call(
        body, name=name, grid=(2, Aq // tc, nk),
        in_specs=[pl.BlockSpec((ADA_ROWS, tk), lambda l, j, k: (0, k)),
                  pl.BlockSpec((None, tk, tc), lambda l, j, k: (l, k, j)),
                  pl.BlockSpec((None, 1, tc), lambda l, j, k: (l, 0, j))],
        out_specs=pl.BlockSpec((None, ADA_ROWS, tc), lambda l, j, k: (l, 0, j)),
        out_shape=jax.ShapeDtypeStruct((2, ADA_ROWS, Aq), F32),
        compiler_params=_cparams(("parallel", "parallel", "arbitrary")),
    )(cond, ada_w, bias)


def ada_cond_grad(name, dmod, ada_w):
    _, D, Aq = ada_w.shape
    tc = _pick(Aq, (1536, 1152, 768, 512, 384, 256, 128))
    tn = _ktile(D)
    nc = Aq // tc

    def body(d_ref, w_ref, o_ref):
        @pl.when(jnp.logical_and(pl.program_id(1) == 0, pl.program_id(2) == 0))
        def _():
            o_ref[...] = jnp.zeros_like(o_ref)

        o_ref[...] += lax.dot_general(d_ref[...], w_ref[...], _DIMS["nt"], precision=_HI, preferred_element_type=F32)

    return pl.pallas_call(
        body, name=name, grid=(D // tn, 2, nc),
        in_specs=[pl.BlockSpec((None, ADA_ROWS, tc), lambda j, l, c: (l, 0, c)),
                  pl.BlockSpec((None, tn, tc), lambda j, l, c: (l, j, c))],
        out_specs=pl.BlockSpec((ADA_ROWS, tn), lambda j, l, c: (0, j)),
        out_shape=jax.ShapeDtypeStruct((ADA_ROWS, D), F32),
        compiler_params=_cparams(("parallel", "arbitrary", "arbitrary")),
    )(dmod, ada_w)


def ada_update(name, cond, dmod, w, m, v):
    _, D, Aq = w.shape
    tc = _pick(Aq, (1536, 1152, 768, 512, 384, 256, 128))
    tr = 128 if D % 128 == 0 else D
    bc1 = 1.0 - ADAM_B1 ** ADAM_STEP
    bc2 = 1.0 - ADAM_B2 ** ADAM_STEP

    def body(c_ref, d_ref, w_ref, m_ref, v_ref, go_ref, dl_ref, mo_ref, vo_ref):
        g = lax.dot_general(_silu(c_ref[...]), d_ref[...], _DIMS["tn"], precision=_HI, preferred_element_type=F32)
        mn = ADAM_B1 * m_ref[...] + (1.0 - ADAM_B1) * g
        vn = ADAM_B2 * v_ref[...] + (1.0 - ADAM_B2) * (g * g)
        go_ref[...] = g
        dl_ref[...] = -ADAM_LR * ((mn / bc1) / (jnp.sqrt(vn / bc2) + ADAM_EPS) + ADAM_WD * w_ref[...])
        mo_ref[...] = mn
        vo_ref[...] = vn

    blk = pl.BlockSpec((None, tr, tc), lambda l, i, j: (l, i, j))
    return pl.pallas_call(
        body, name=name, grid=(2, D // tr, Aq // tc),
        in_specs=[pl.BlockSpec((ADA_ROWS, tr), lambda l, i, j: (0, i)),
                  pl.BlockSpec((None, ADA_ROWS, tc), lambda l, i, j: (l, 0, j)), blk, blk, blk],
        out_specs=[blk] * 4, out_shape=[jax.ShapeDtypeStruct(w.shape, F32)] * 4,
        compiler_params=_cparams(("parallel", "parallel", "parallel")),
    )(cond, dmod, w, m, v)


def dmod_assemble(name, gathered):
    A = gathered.shape[-1]
    tc = _pick(A, (2048, 1024, 512, 256, 128))

    def body(g_ref, o_ref, b_ref):
        ctx = g_ref[0, 1]
        for dev in range(1, N_DEV):
            ctx = ctx + g_ref[dev, 1]
        tot = ctx
        for dev in range(N_DEV):
            o_ref[dev:dev + 1, :] = g_ref[dev, 0]
            tot = tot + g_ref[dev, 0]
        o_ref[N_DEV:N_DEV + 1, :] = ctx
        o_ref[N_DEV + 1:, :] = jnp.zeros((ADA_ROWS - N_DEV - 1, tc), F32)
        b_ref[...] = tot

    return pl.pallas_call(
        body, name=name, grid=(2, A // tc),
        in_specs=[pl.BlockSpec((N_DEV, None, 2, 1, tc), lambda l, j: (0, l, 0, 0, j))],
        out_specs=[pl.BlockSpec((None, ADA_ROWS, tc), lambda l, j: (l, 0, j)),
                   pl.BlockSpec((None, 1, tc), lambda l, j: (l, 0, j))],
        out_shape=[jax.ShapeDtypeStruct((2, ADA_ROWS, A), F32), jax.ShapeDtypeStruct((2, 1, A), F32)],
        compiler_params=_cparams(("parallel", "parallel")),
    )(gathered)


def sum_devices(name, gathered):
    _, R, W = gathered.shape
    tr = _row_tile(R, W, budget=1 << 18)

    def body(g_ref, all_ref, chip_ref):
        even = g_ref[0]
        odd = g_ref[1]
        for dev in range(2, N_DEV, 2):
            even = even + g_ref[dev]
            odd = odd + g_ref[dev + 1]
        all_ref[...] = even + odd
        chip_ref[...] = even

    blk = pl.BlockSpec((tr, W), lambda i: (i, 0))
    return pl.pallas_call(
        body, name=name, grid=(R // tr,),
        in_specs=[pl.BlockSpec((N_DEV, tr, W), lambda i: (0, i, 0))], out_specs=[blk, blk],
        out_shape=[jax.ShapeDtypeStruct((R, W), F32)] * 2, compiler_params=_cparams(("parallel",)),
    )(gathered)


def _pack(arrays):
    flat = jnp.concatenate([a.reshape(-1).astype(F32) for a in arrays])
    pad = (-flat.shape[0]) % (8 * 128)
    return jnp.pad(flat, (0, pad)).reshape(-1, 128)


def _unpack(buf, shapes, lead=()):
    flat = buf.reshape(lead + (-1,))
    out, start = [], 0
    for s in shapes:
        n = math.prod(s)
        out.append(flat[..., start:start + n].reshape(lead + tuple(s)))
        start += n
    return out


def _unshard_last(g):
    g = jnp.moveaxis(g, 0, -2)
    return g.reshape(g.shape[:-2] + (g.shape[-2] * g.shape[-1],))


def _ffn_forward(tag, h, gn, modtab, s, w13g, w2g, l, dm):
    u = norm_mod(f"{tag}_norm", h, gn, modtab, s, dm)
    gu = mm_cols(f"{tag}_w13", u, w13g, l, BF16, flat=False)
    act = swiglu(f"{tag}_act", gu, dm)
    f = mm_rows(f"{tag}_w2", act, w2g, l)
    h_out = resid(f"{tag}_res", h, f, modtab, s, 0.5, dm)
    return h_out, (h, u, gu, act, f)


def _ffn_backward(tag, dh, saved, gn, modtab, s, w13g, w2g, l, dm):
    h, u, gu, act, f = saved
    df, dgate = resid_bwd(f"{tag}_res_bwd", dh, f, modtab, s, 0.5, dm)
    dact = mm_rows_t(f"{tag}_dact", df, w2g, l)
    dw2 = mm_rows_grad(f"{tag}_dw2", act, df).reshape(N_CHIP, -1, df.shape[-1])
    dgu = swiglu_bwd(f"{tag}_act_bwd", dact, gu, dm)
    du = mm_cols_t(f"{tag}_du", dgu, w13g, l, flat=False)
    dw13 = mm_cols_grad(f"{tag}_dw13", u, dgu, flat=False)
    dh_in, dss, dgn = norm_mod_bwd(f"{tag}_norm_bwd", du, h, gn, modtab, s, dh, dm)
    return dh_in, dw13, dw2, dss, dgate, dgn


def kernel(x, c, ctx, c_ctx, ada_w, ada_b, norm_g, ffn1_w13, ffn1_w2, w_in, b_merge, rnn_conv_w, rnn_conv_b, lru_w_a, lru_b_a, lru_w_x, lru_b_x, lru_lambda, sc_conv_w, attn_sink, w_branch, w_out, ffn2_w13, ffn2_w2, final_norm_g, loss_target, m_c_ctx, m_ada_w, m_ada_b, m_norm_g, m_ffn1_w13, m_ffn1_w2, m_w_in, m_b_merge, m_rnn_conv_w, m_rnn_conv_b, m_lru_w_a, m_lru_b_a, m_lru_w_x, m_lru_b_x, m_lru_lambda, m_sc_conv_w, m_attn_sink, m_w_branch, m_w_out, m_ffn2_w13, m_ffn2_w2, m_final_norm_g, v_c_ctx, v_ada_w, v_ada_b, v_norm_g, v_ffn1_w13, v_ffn1_w2, v_w_in, v_b_merge, v_rnn_conv_w, v_rnn_conv_b, v_lru_w_a, v_lru_b_a, v_lru_w_x, v_lru_b_x, v_lru_lambda, v_sc_conv_w, v_attn_sink, v_w_branch, v_w_out, v_ffn2_w13, v_ffn2_w2, v_final_norm_g):
    weights = dict(c_ctx=c_ctx, ada_w=ada_w, ada_b=ada_b, norm_g=norm_g, ffn1_w13=ffn1_w13, ffn1_w2=ffn1_w2, w_in=w_in,
                   b_merge=b_merge, rnn_conv_w=rnn_conv_w, rnn_conv_b=rnn_conv_b, lru_w_a=lru_w_a, lru_b_a=lru_b_a,
                   lru_w_x=lru_w_x, lru_b_x=lru_b_x, lru_lambda=lru_lambda, sc_conv_w=sc_conv_w, attn_sink=attn_sink,
                   w_branch=w_branch, w_out=w_out, ffn2_w13=ffn2_w13, ffn2_w2=ffn2_w2, final_norm_g=final_norm_g)
    mom_m = dict(c_ctx=m_c_ctx, ada_w=m_ada_w, ada_b=m_ada_b, norm_g=m_norm_g, ffn1_w13=m_ffn1_w13, ffn1_w2=m_ffn1_w2,
                 w_in=m_w_in, b_merge=m_b_merge, rnn_conv_w=m_rnn_conv_w, rnn_conv_b=m_rnn_conv_b, lru_w_a=m_lru_w_a,
                 lru_b_a=m_lru_b_a, lru_w_x=m_lru_w_x, lru_b_x=m_lru_b_x, lru_lambda=m_lru_lambda,
                 sc_conv_w=m_sc_conv_w, attn_sink=m_attn_sink, w_branch=m_w_branch, w_out=m_w_out,
                 ffn2_w13=m_ffn2_w13, ffn2_w2=m_ffn2_w2, final_norm_g=m_final_norm_g)
    mom_v = dict(c_ctx=v_c_ctx, ada_w=v_ada_w, ada_b=v_ada_b, norm_g=v_norm_g, ffn1_w13=v_ffn1_w13, ffn1_w2=v_ffn1_w2,
                 w_in=v_w_in, b_merge=v_b_merge, rnn_conv_w=v_rnn_conv_w, rnn_conv_b=v_rnn_conv_b, lru_w_a=v_lru_w_a,
                 lru_b_a=v_lru_b_a, lru_w_x=v_lru_w_x, lru_b_x=v_lru_b_x, lru_lambda=v_lru_lambda,
                 sc_conv_w=v_sc_conv_w, attn_sink=v_attn_sink, w_branch=v_w_branch, w_out=v_w_out,
                 ffn2_w13=v_ffn2_w13, ffn2_w2=v_ffn2_w2, final_norm_g=v_final_norm_g)
    order = list(weights)

    dm = Dims()
    dm.D = D = x.shape[-1]
    dm.L = L = x.shape[1]
    dm.C = C = ctx.shape[1]
    dm.T = T = L + C
    dm.RW = RW = rnn_conv_b.shape[-1]
    dm.NB = lru_w_a.shape[2]
    H = attn_sink.shape[-1]
    dm.HQ = HQ = H * HEAD_DIM
    NZ = w_in.shape[-1] * N_CHIP
    dm.KW = KW = (NZ - 5 * RW - HQ - N_BRANCH * D) // 2
    dm.off_q = 5 * RW
    dm.off_k = dm.off_q + HQ
    dm.off_g = dm.off_k + 2 * KW
    dm.tme = _pick(C, (256, 128))
    dm.nt = T // dm.tme
    dm.nctx = C // dm.tme
    dm.cw = next(w for w in (512, 256, 128) if dm.off_g % w == 0 and D % w == 0)
    A = ada_b.shape[-1]
    Aq = ada_w.shape[-1]
    assert dm.off_q % HQ == 0 and dm.off_k % KW == 0 and HQ % KW == 0 and L % dm.tme == 0 and RW == HQ
    assert C % Q_BLOCK == 0 and L % Q_BLOCK == 0 and D % N_CHIP == 0 and A == N_MOD * D

    mx, my, mc = _my_pos()
    j_me = 2 * mx + my
    b_me = 4 * mx + 2 * my + mc

    small_sharded = ["norm_g", "b_merge", "rnn_conv_w", "lru_b_a", "lru_b_x", "lru_lambda", "sc_conv_w"]
    pack1 = _pack([c] + [weights[n] for n in small_sharded])
    g1 = allgather8("gather_small_params", pack1).reshape(N_DEV, -1, 128)
    parts = _unpack(g1, [c.shape] + [weights[n].shape for n in small_sharded], lead=(N_DEV,))
    c_all = parts[0].reshape(N_DEV, D)
    full = {n: _unshard_last(p[0::2]) for n, p in zip(small_sharded, parts[1:])}
    cond = jnp.concatenate([c_all, c_ctx[None, :], jnp.zeros((ADA_ROWS - N_DEV - 1, D), F32)], axis=0)

    bias_q = lax.dynamic_slice_in_dim(ada_b, j_me * Aq, Aq, axis=1)[:, None, :]
    mod_q = ada_forward("ada_forward", cond, ada_w, bias_q)
    g2 = allgather8("gather_mod", mod_q.reshape(-1, 128)).reshape(N_DEV, 2, ADA_ROWS, Aq)
    mod_full = _unshard_last(g2[0::2])
    mod_lat = lax.dynamic_index_in_dim(mod_full, b_me, axis=1, keepdims=False)
    mod_ctx = mod_full[:, N_DEV]
    modtabs = [jnp.stack([mod_ctx[l], mod_lat[l]]).reshape(2, N_MOD, 1, D) for l in range(2)]

    big = ["ffn1_w13", "ffn1_w2", "w_in", "w_branch", "w_out", "ffn2_w13", "ffn2_w2"]
    gathered = dict(zip(big, gather_weights([cast_bf16(f"cast_{n}", weights[n]) for n in big])))
    FFq = ffn1_w2.shape[1]
    w13g = {1: gathered["ffn1_w13"], 2: gathered["ffn2_w13"]}
    w2g = {1: gathered["ffn1_w2"].reshape(2, 2, 2 * FFq, D), 2: gathered["ffn2_w2"].reshape(2, 2, 2 * FFq, D)}
    wing = gathered["w_in"]
    wbg = gathered["w_branch"]
    woutg = gathered["w_out"].reshape(2, D, D)

    cos, sin = _rope_tables(dm)
    sink = attn_sink.reshape(2, 1, H)
    lw = dict(w_a=lru_w_a, w_x=lru_w_x,
              b_a=full["lru_b_a"][:, :, None, :], b_x=full["lru_b_x"][:, :, None, :],
              lam=full["lru_lambda"][:, :, None, :])
    gn = full["norm_g"]
    bm = full["b_merge"][:, :, None, :]
    rcw = full["rnn_conv_w"][:, :, None, :]
    scw = full["sc_conv_w"][:, :, None, :]

    h = jnp.concatenate([ctx[0], x[0]], axis=0)
    saved = []
    for l in range(2):
        mt = modtabs[l]
        sv = {}
        h, sv["ffn1"] = _ffn_forward(f"l{l}_ffn1", h, gn[l, 0:1], mt, 0, w13g[1], w2g[1], l, dm)
        sv["h_mix"] = h
        u = norm_mod(f"l{l}_mix_norm", h, gn[l, 1:2], mt, 1, dm)
        z = mm_cols(f"l{l}_w_in", u, wing, l, F32, flat=True)
        xa = rnn_conv(f"l{l}_rnn_conv", z, rcw[l], rnn_conv_b[l][None, :], dm)
        scans = []
        for d in range(2):
            a_d, u_d = lru_gates(f"l{l}_lru_gates{d}", xa, lw, l, d, dm)
            h_d = lru_scan(f"l{l}_lru_scan{d}", a_d, u_d, True, d == 1, dm)
            scans.append((a_d, u_d, h_d))
        ya = rnn_out(f"l{l}_rnn_out", scans[0][2], scans[1][2], z, dm)
        yb = short_conv(f"l{l}_short_conv", z, scw[l], dm)
        qr, kr, vv = qkv_prep(f"l{l}_qkv", z, cos, sin, dm)
        yatt, lse = attention(f"l{l}_attn", qr, kr, vv, sink[l], dm)
        ys = (ya, yb, yatt)
        lifted = [mm_branch(f"l{l}_lift{br}", ys[br], wbg, l, br) for br in range(N_BRANCH)]
        merged = merge(f"l{l}_merge", z, lifted, bm[l], dm)
        y = mm_plain(f"l{l}_w_out", merged, woutg, l, "nn", F32)
        h = resid(f"l{l}_mix_res", h, y, mt, 1, 1.0, dm)
        sv.update(u=u, z=z, xa=xa, scans=scans, ys=ys, qkv=(qr, kr, vv), lse=lse, lifted=lifted, merged=merged, y=y)
        h, sv["ffn2"] = _ffn_forward(f"l{l}_ffn2", h, gn[l, 2:3], mt, 2, w13g[2], w2g[2], l, dm)
        saved.append(sv)

    dh, loss_vec, d_final_g = loss_head("loss_head", h, final_norm_g[None, :], loss_target[0], dm)
    loss = lax.psum(loss_vec[0, 0], ("x", "y", "c"))

    big_grads = {n: [None, None] for n in big}
    small = {n: [None, None] for n in ["norm_g", "b_merge", "rnn_conv_w", "rnn_conv_b", "lru_w_a", "lru_b_a", "lru_w_x",
                                       "lru_b_x", "lru_lambda", "sc_conv_w", "attn_sink"]}
    dmods = [None, None]
    for l in (1, 0):
        mt = modtabs[l]
        sv = saved[l]
        dh, dw13, dw2, dss2, dgate2, dgn2 = _ffn_backward(f"l{l}_ffn2", dh, sv["ffn2"], gn[l, 2:3], mt, 2,
                                                         w13g[2], w2g[2], l, dm)
        big_grads["ffn2_w13"][l], big_grads["ffn2_w2"][l] = dw13, dw2

        dyg, dgate1 = resid_bwd(f"l{l}_mix_res_bwd", dh, sv["y"], mt, 1, 1.0, dm)
        dmerged = mm_plain(f"l{l}_dmerged", dyg, woutg, l, "nt", F32)
        big_grads["w_out"][l] = mm_plain_grad(f"l{l}_dw_out", sv["merged"], dyg).reshape(N_CHIP, D // N_CHIP, D)
        dz = jnp.zeros((T, NZ), BF16)
        dys, dwb, dbm = [], [], []
        for br in range(N_BRANCH):
            dl, dz, db = merge_bwd(f"l{l}_merge_bwd{br}", dmerged, sv["z"], sv["lifted"][br], bm[l], br, dz, dm)
            dys.append(mm_branch_t(f"l{l}_dy{br}", dl, wbg, l, br))
            dwb.append(mm_branch_grad(f"l{l}_dwb{br}", sv["ys"][br], dl))
            dbm.append(db)
        big_grads["w_branch"][l] = dwb
        small["b_merge"][l] = jnp.concatenate(dbm, axis=0)

        qr, kr, vv = sv["qkv"]
        dq, dk, dv, dsink = attention_bwd(f"l{l}_attn_bwd", qr, kr, vv, sink[l], sv["ys"][2], sv["lse"], dys[2], dm)
        dz = qkv_bwd(f"l{l}_qkv_bwd", dq, dk, dv, cos, sin, dz, dm)
        small["attn_sink"][l] = dsink[0, :H]

        dz, dscw = short_conv_bwd(f"l{l}_short_conv_bwd", dys[1], sv["z"], scw[l], dz, dm)
        small["sc_conv_w"][l] = dscw[:, 0]

        (a0, u0, h0), (a1, u1, h1) = sv["scans"]
        dhs, drg = rnn_out_bwd(f"l{l}_rnn_out_bwd", dys[0], h0, h1, sv["z"], dm)
        dxa, lru_sums = [], []
        for d, (a_d, u_d, h_d) in enumerate(sv["scans"]):
            lam_d, dla_d = lru_scan_bwd(f"l{l}_lru_scan_bwd{d}", a_d, u_d, h_d, dhs, False, d == 0, dm)
            outs = lru_gates_bwd(f"l{l}_lru_gates_bwd{d}", sv["xa"], lw, l, d, lam_d, dla_d, dm)
            dxa.append(outs[0])
            lru_sums.append(outs[1:])
        dz, drcw, drcb = rnn_conv_bwd(f"l{l}_rnn_conv_bwd", dxa[0], dxa[1], drg, sv["z"], rcw[l], dz, dm)
        small["rnn_conv_w"][l] = drcw[:, 0]
        small["rnn_conv_b"][l] = drcb[0]
        for i, n in enumerate(["lru_w_a", "lru_b_a", "lru_w_x", "lru_b_x", "lru_lambda"]):
            small[n][l] = jnp.stack([lru_sums[0][i], lru_sums[1][i]]).reshape((2,) + weights[n].shape[2:-1] + (-1,))

        du = mm_cols_t(f"l{l}_du_mix", dz, wing, l, flat=True)
        big_grads["w_in"][l] = mm_cols_grad(f"l{l}_dw_in", sv["u"], dz, flat=True)
        dh, dss1, dgn1 = norm_mod_bwd(f"l{l}_mix_norm_bwd", du, sv["h_mix"], gn[l, 1:2], mt, 1, dh, dm)

        dh, dw13, dw2, dss0, dgate0, dgn0 = _ffn_backward(f"l{l}_ffn1", dh, sv["ffn1"], gn[l, 0:1], mt, 0,
                                                         w13g[1], w2g[1], l, dm)
        big_grads["ffn1_w13"][l], big_grads["ffn1_w2"][l] = dw13, dw2
        small["norm_g"][l] = jnp.concatenate([dgn0, dgn1, dgn2], axis=0)
        dmods[l] = jnp.concatenate([dss0, dgate0, dss1, dgate1, dss2, dgate2], axis=1).reshape(2, A)

    grad_x = dh[C:][None]

    pack_mod = jnp.stack([jnp.stack([dmods[l][1], dmods[l][0]]) for l in range(2)])
    g3 = allgather8("gather_dmod", pack_mod.reshape(-1, 128)).reshape(N_DEV, 2, 2, 1, A)
    dmod_full, d_ada_b = dmod_assemble("dmod_assemble", g3)
    dmod_q = lax.dynamic_slice_in_dim(dmod_full, j_me * Aq, Aq, axis=2)
    dcond_q = ada_cond_grad("ada_cond_grad", dmod_q, ada_w)

    small_names = list(small)
    small_parts = [jnp.stack(small[n]) for n in small_names] + [d_final_g, dcond_q[N_DEV]]
    small_shapes = [p.shape for p in small_parts]
    lru_big = [small_names.index("lru_w_a"), small_names.index("lru_w_x")]
    rest_idx = [i for i in range(len(small_parts)) if i not in lru_big]
    summed = [None] * len(small_parts)
    for i in lru_big:
        buf = _pack([small_parts[i]])
        tot, _ = sum_devices(f"sum_{small_names[i]}", allgather8(f"gather_{small_names[i]}", buf).reshape(N_DEV, -1, 128))
        summed[i] = _unpack(tot, [small_shapes[i]])[0]
    buf = _pack([small_parts[i] for i in rest_idx])
    tot, chip_tot = sum_devices("sum_small_grads", allgather8("gather_small_grads", buf).reshape(N_DEV, -1, 128))
    for i, val in zip(rest_idx, _unpack(tot, [small_shapes[i] for i in rest_idx])):
        summed[i] = val
    dcond_ctx = _unpack(chip_tot, [small_shapes[i] for i in rest_idx])[-1]
    sg = jax.nn.sigmoid(c_ctx)
    grads = dict(zip(small_names, summed[:len(small_names)]))
    grads["final_norm_g"] = summed[len(small_names)][0]
    grads["c_ctx"] = dcond_ctx * (sg * (1.0 + c_ctx * (1.0 - sg)))
    grads["ada_b"] = d_ada_b[:, 0]
    for n in small_sharded:
        g = grads[n]
        q = g.shape[-1] // N_CHIP
        grads[n] = lax.dynamic_slice_in_dim(g, j_me * q, q, axis=g.ndim - 1)

    pieces, shapes = [], []
    for n in big:
        if n == "w_branch":
            pieces.append([(big_grads[n][l][br], (l, br)) for l in range(2) for br in range(N_BRANCH)])
        else:
            pieces.append([(big_grads[n][l], (l,)) for l in range(2)])
        shapes.append(weights[n].shape)
    bufs = scatter_grads(pieces, shapes)
    mine = [sum4(f"sum4_{n}", b) for n, b in zip(big, bufs)]
    theirs = swap_sibling(mine)

    results = {}
    for n, p_own, p_sib in zip(big, mine, theirs):
        results[n] = adamw(f"adamw_{n}", weights[n], mom_m[n], mom_v[n], [p_own, p_sib])
    results["ada_w"] = ada_update("ada_update", cond, dmod_q, ada_w, m_ada_w, v_ada_w)
    small_all = [n for n in order if n not in results]
    pk = lambda d: _pack([d[n] for n in small_all])
    outs = adamw("adamw_small", pk(weights), pk(mom_m), pk(mom_v), [pk(grads)])
    shapes_small = [weights[n].shape for n in small_all]
    unpacked = [_unpack(o, shapes_small) for o in outs]
    for i, n in enumerate(small_all):
        results[n] = tuple(unpacked[k][i] for k in range(4))

    return (loss, grad_x, *[results[n][0] for n in order], *[results[n][1] for n in order],
            *[results[n][2] for n in order], *[results[n][3] for n in order])
```

```python
import functools
import math

import jax
import jax.numpy as jnp
from jax import lax
from jax.experimental import pallas as pl
from jax.experimental.pallas import tpu as pltpu

F32 = jnp.float32
BF16 = jnp.bfloat16
MESH = pl.DeviceIdType.MESH

HEAD_DIM = 128
GRID_W = 64
WINDOW = 128
Q_BLOCK = 128
ROPE_BASE = 10000.0
LRU_C = 8.0
EPS = 1e-6
NEG_INF = -1e30
N_MOD = 9
N_BRANCH = 3
RNN_BLOCK = 128
HALO = 8
LSE_W = 128

ADAM_LR = 0.001
ADAM_B1 = 0.9
ADAM_B2 = 0.999
ADAM_EPS = 1e-08
ADAM_WD = 0.01
ADAM_STEP = 10

VMEM_LIMIT_BYTES = 48 * 1024 * 1024
N_DEV = 8
N_CHIP = 4


def _pick(n, cands):
    for c in cands:
        if c <= n and n % c == 0:
            return c
    return n


def _cparams(sem):
    return pltpu.CompilerParams(dimension_semantics=sem, vmem_limit_bytes=VMEM_LIMIT_BYTES)


def _silu(x):
    return x * jax.nn.sigmoid(x)


def _dsilu(x):
    s = jax.nn.sigmoid(x)
    return s * (1.0 + x * (1.0 - s))


_GELU_K = math.sqrt(2.0 / math.pi)


def _gelu(x):
    return 0.5 * x * (1.0 + jnp.tanh(_GELU_K * (x + 0.044715 * x * x * x)))


def _dgelu(x):
    t = jnp.tanh(_GELU_K * (x + 0.044715 * x * x * x))
    return 0.5 * (1.0 + t) + 0.5 * x * (1.0 - t * t) * _GELU_K * (1.0 + 3.0 * 0.044715 * x * x)


def _expm1(x):
    series = x * (1.0 + x * (0.5 + x * (1.0 / 6.0 + x * (1.0 / 24.0 + x * (1.0 / 120.0)))))
    return jnp.where(jnp.abs(x) < 0.1, series, jnp.exp(x) - 1.0)


def _my_pos():
    return lax.axis_index("x"), lax.axis_index("y"), lax.axis_index("c")


_DIMS = {"nn": (((1,), (0,)), ((), ())), "nt": (((1,), (1,)), ((), ())), "tn": (((0,), (0,)), ((), ()))}


def _mm(name, a, b, *, mode, grid, a_blk, a_map, b_blk, b_map, o_blk, o_map, out_shape, out_dtype):
    nk = grid[-1]
    nax = len(grid)
    acc_shape = tuple(d for d in o_blk if d is not None)

    def body(a_ref, b_ref, o_ref, acc_ref):
        k = pl.program_id(nax - 1)

        @pl.when(k == 0)
        def _():
            acc_ref[...] = jnp.zeros_like(acc_ref)

        acc_ref[...] += lax.dot_general(a_ref[...].astype(BF16), b_ref[...].astype(BF16), _DIMS[mode],
                                        preferred_element_type=F32)

        @pl.when(k == nk - 1)
        def _():
            o_ref[...] = acc_ref[...].astype(o_ref.dtype)

    return pl.pallas_call(
        body, name=name, grid=grid,
        in_specs=[pl.BlockSpec(a_blk, a_map), pl.BlockSpec(b_blk, b_map)],
        out_specs=pl.BlockSpec(o_blk, o_map),
        out_shape=jax.ShapeDtypeStruct(out_shape, out_dtype),
        scratch_shapes=[pltpu.VMEM(acc_shape, F32)],
        compiler_params=_cparams(("parallel",) * (nax - 1) + ("arbitrary",)),
    )(a, b)


def _tiles(n):
    return _pick(n, (768, 512, 384, 256, 128, 64, 32, 16))


def _ktile(n):
    return _pick(n, (512, 256, 128))


def mm_cols(name, a, wg, out_dtype, flat):
    T, K = a.shape
    Nq = wg.shape[-1]
    tm, tk = _tiles(T), _ktile(K)
    if flat:
        o_blk, o_map, o_shape = (tm, Nq), (lambda i, j, k: (i, j)), (T, N_CHIP * Nq)
    else:
        o_blk, o_map, o_shape = (None, tm, Nq), (lambda i, j, k: (j, i, 0)), (N_CHIP, T, Nq)
    return _mm(name, a, wg, mode="nn", grid=(T // tm, N_CHIP, K // tk),
               a_blk=(tm, tk), a_map=lambda i, j, k: (i, k),
               b_blk=(None, tk, Nq), b_map=lambda i, j, k: (j, k, 0),
               o_blk=o_blk, o_map=o_map, out_shape=o_shape, out_dtype=out_dtype)


def mm_cols_t(name, d, wg, flat):
    K, Nq = wg.shape[-2:]
    T = d.shape[-2]
    tm, tn = _tiles(T), _ktile(K)
    if flat:
        a_blk, a_map = (tm, Nq), (lambda i, j, k: (i, k))
    else:
        a_blk, a_map = (None, tm, Nq), (lambda i, j, k: (k, i, 0))
    return _mm(name, d, wg, mode="nt", grid=(T // tm, K // tn, N_CHIP),
               a_blk=a_blk, a_map=a_map,
               b_blk=(None, tn, Nq), b_map=lambda i, j, k: (k, j, 0),
               o_blk=(tm, tn), o_map=lambda i, j, k: (i, j), out_shape=(T, K), out_dtype=F32)


def mm_cols_grad(name, a, d, flat):
    T, K = a.shape
    Nq = d.shape[-1] // N_CHIP if flat else d.shape[-1]
    tt, br = _tiles(T), _ktile(K)
    if flat:
        b_blk, b_map = (tt, Nq), (lambda j, r, t: (t, j))
    else:
        b_blk, b_map = (None, tt, Nq), (lambda j, r, t: (j, t, 0))
    return _mm(name, a, d, mode="tn", grid=(N_CHIP, K // br, T // tt),
               a_blk=(tt, br), a_map=lambda j, r, t: (t, r),
               b_blk=b_blk, b_map=b_map,
               o_blk=(None, br, Nq), o_map=lambda j, r, t: (j, r, 0),
               out_shape=(N_CHIP, K, Nq), out_dtype=BF16)


def mm_rows(name, a, wg):
    G, T, Kg = a.shape
    N = wg.shape[-1]
    tm, tn = _tiles(T), _pick(N, (1024, 512, 256, 128))
    return _mm(name, a, wg, mode="nn", grid=(T // tm, N // tn, G),
               a_blk=(None, tm, Kg), a_map=lambda i, j, k: (k, i, 0),
               b_blk=(None, Kg, tn), b_map=lambda i, j, k: (k, 0, j),
               o_blk=(tm, tn), o_map=lambda i, j, k: (i, j), out_shape=(T, N), out_dtype=F32)


def mm_rows_t(name, d, wg):
    T, N = d.shape
    G, Kg = wg.shape[0], wg.shape[1]
    tm, tk = _tiles(T), _ktile(N)
    return _mm(name, d, wg, mode="nt", grid=(T // tm, G, N // tk),
               a_blk=(tm, tk), a_map=lambda i, j, k: (i, k),
               b_blk=(None, Kg, tk), b_map=lambda i, j, k: (j, 0, k),
               o_blk=(None, tm, Kg), o_map=lambda i, j, k: (j, i, 0), out_shape=(G, T, Kg), out_dtype=BF16)


def mm_rows_grad(name, a, d):
    G, T, Kg = a.shape
    N = d.shape[-1]
    tt, tn = _tiles(T), _ktile(N)
    return _mm(name, a, d, mode="tn", grid=(G, N // tn, T // tt),
               a_blk=(None, tt, Kg), a_map=lambda g, j, t: (g, t, 0),
               b_blk=(tt, tn), b_map=lambda g, j, t: (t, j),
               o_blk=(None, Kg, tn), o_map=lambda g, j, t: (g, 0, j), out_shape=(G, Kg, N), out_dtype=BF16)


def mm_plain(name, a, w, mode, out_dtype):
    T = a.shape[0]
    K, N = w.shape[-2:]
    tm = _tiles(T)
    if mode == "nn":
        tn, tk = _ktile(N), _ktile(K)
        return _mm(name, a, w, mode="nn", grid=(T // tm, N // tn, K // tk),
                   a_blk=(tm, tk), a_map=lambda i, j, k: (i, k),
                   b_blk=(tk, tn), b_map=lambda i, j, k: (k, j),
                   o_blk=(tm, tn), o_map=lambda i, j, k: (i, j), out_shape=(T, N), out_dtype=out_dtype)
    tn, tk = _ktile(K), _ktile(N)
    return _mm(name, a, w, mode="nt", grid=(T // tm, K // tn, N // tk),
               a_blk=(tm, tk), a_map=lambda i, j, k: (i, k),
               b_blk=(tn, tk), b_map=lambda i, j, k: (j, k),
               o_blk=(tm, tn), o_map=lambda i, j, k: (i, j), out_shape=(T, K), out_dtype=out_dtype)


def mm_plain_grad(name, a, d):
    T, K = a.shape
    N = d.shape[-1]
    tt, br, tn = _tiles(T), _ktile(K), _ktile(N)
    return _mm(name, a, d, mode="tn", grid=(K // br, N // tn, T // tt),
               a_blk=(tt, br), a_map=lambda r, j, t: (t, r),
               b_blk=(tt, tn), b_map=lambda r, j, t: (t, j),
               o_blk=(br, tn), o_map=lambda r, j, t: (r, j), out_shape=(K, N), out_dtype=BF16)


def mm_branch(name, y, wbg, br):
    T, RW = y.shape
    Dq = wbg.shape[-1]
    tm = _tiles(T)
    return _mm(name, y, wbg, mode="nn", grid=(T // tm, N_CHIP, 1),
               a_blk=(tm, RW), a_map=lambda i, j, k: (i, 0),
               b_blk=(None, None, RW, Dq), b_map=lambda i, j, k: (j, br, 0, 0),
               o_blk=(tm, Dq), o_map=lambda i, j, k: (i, j), out_shape=(T, N_CHIP * Dq), out_dtype=F32)


def mm_branch_t(name, d, wbg, br):
    T = d.shape[0]
    RW, Dq = wbg.shape[-2:]
    tm = _tiles(T)
    return _mm(name, d, wbg, mode="nt", grid=(T // tm, 1, N_CHIP),
               a_blk=(tm, Dq), a_map=lambda i, j, k: (i, k),
               b_blk=(None, None, RW, Dq), b_map=lambda i, j, k: (k, br, 0, 0),
               o_blk=(tm, RW), o_map=lambda i, j, k: (i, 0), out_shape=(T, RW), out_dtype=F32)


def mm_branch_grad(name, y, d):
    T, RW = y.shape
    Dq = d.shape[-1] // N_CHIP
    tt = _tiles(T)
    return _mm(name, y, d, mode="tn", grid=(N_CHIP, 1, T // tt),
               a_blk=(tt, RW), a_map=lambda j, r, t: (t, 0),
               b_blk=(tt, Dq), b_map=lambda j, r, t: (t, j),
               o_blk=(None, RW, Dq), o_map=lambda j, r, t: (j, 0, 0), out_shape=(N_CHIP, RW, Dq), out_dtype=BF16)


def allgather8(name, x_shard):
    m_per, n = x_shard.shape

    def body(x_ref, out_ref, send_sems, recv_sems, local_sem):
        x, y, c = _my_pos()
        me, sibling = (x, y, c), (x, y, 1 - c)
        chips = [(1 - x, y), (x, 1 - y), (1 - x, 1 - y)]

        def rows(px, py, pc):
            return out_ref.at[pl.ds((4 * px + 2 * py + pc) * m_per, m_per), :]

        def copy(k, block, to, src=None):
            return pltpu.make_async_remote_copy(
                src_ref=rows(*block) if src is None else src, dst_ref=rows(*block),
                send_sem=send_sems.at[k], recv_sem=recv_sems.at[k], device_id=to, device_id_type=MESH)

        mine = pltpu.make_async_copy(x_ref, rows(*me), local_sem)
        mine.start()
        first = [copy(0, me, sibling, src=x_ref)]
        first += [copy(1 + j, me, (*chip, c), src=x_ref) for j, chip in enumerate(chips)]
        for cp in first:
            cp.start()
        passed = [copy(4 + j, (*chip, c), sibling) for j, chip in enumerate(chips)]
        for j, chip in enumerate(chips):
            copy(1 + j, (*chip, c), me).wait_recv()
            passed[j].start()
        copy(0, sibling, me).wait_recv()
        for j, chip in enumerate(chips):
            copy(4 + j, (*chip, 1 - c), me).wait_recv()
        for cp in first + passed:
            cp.wait_send()
        mine.wait()

    return pl.pallas_call(
        body, name=name,
        out_shape=jax.ShapeDtypeStruct((N_DEV * m_per, n), x_shard.dtype),
        in_specs=[pl.BlockSpec(memory_space=pltpu.VMEM)],
        out_specs=pl.BlockSpec(memory_space=pltpu.VMEM),
        scratch_shapes=[pltpu.SemaphoreType.DMA((7,)), pltpu.SemaphoreType.DMA((7,)), pltpu.SemaphoreType.DMA],
        compiler_params=pltpu.CompilerParams(vmem_limit_bytes=VMEM_LIMIT_BYTES),
    )(x_shard)


def _other_chips(x, y):
    return [(1 - x, y), (x, 1 - y), (1 - x, 1 - y)]


_HBM = pl.BlockSpec(memory_space=pltpu.HBM)
_SEM = pl.BlockSpec(memory_space=pltpu.SEMAPHORE)
_ANY = pl.BlockSpec(memory_space=pl.ANY)
_EFFECT = pltpu.SideEffectType.DATAFLOW_SIDE_EFFECTING
TOKEN_SHAPE = (8, 128)


def _in_hbm(a):
    return pltpu.with_memory_space_constraint(a, pltpu.HBM)


def exchange_start(name, bufs, plan):
    n = len(bufs)
    n_copies = len(plan([None] * n, 0, 0, 0, dry=True))

    def body(*refs):
        ins = refs[:n]
        send_sems, recv_sems = refs[n], refs[n + 1]
        token = refs[-1]
        x, y, c = _my_pos()
        for i, (src, dst, to) in enumerate(plan(ins, x, y, c)):
            pltpu.make_async_remote_copy(src_ref=src, dst_ref=dst, send_sem=send_sems.at[i], recv_sem=recv_sems.at[i],
                                         device_id=to, device_id_type=MESH).start()
        token[...] = jnp.zeros_like(token)

    outs = pl.pallas_call(
        body, name=name,
        out_shape=(pltpu.SemaphoreType.DMA((n_copies,)), pltpu.SemaphoreType.DMA((n_copies,)),
                   *[pltpu.HBM(b.shape, b.dtype) for b in bufs], jax.ShapeDtypeStruct(TOKEN_SHAPE, F32)),
        in_specs=[_HBM] * n,
        out_specs=(_SEM, _SEM, *[_HBM] * n, pl.BlockSpec(memory_space=pltpu.VMEM)),
        input_output_aliases={i: 2 + i for i in range(n)},
        compiler_params=pltpu.CompilerParams(has_side_effects=_EFFECT),
    )(*[_in_hbm(b) for b in bufs])
    return outs[0], outs[1], list(outs[2:2 + n]), outs[-1]


def exchange_wait(name, send_sems, recv_sems, bufs, plan, after):
    n = len(bufs)

    def body(*refs):
        ins = refs[:n]
        send_sems, recv_sems = refs[n], refs[n + 1]
        x, y, c = _my_pos()
        for i, (src, dst, to) in enumerate(plan(ins, x, y, c, arriving=True)):
            cp = pltpu.make_async_remote_copy(src_ref=src, dst_ref=dst, send_sem=send_sems.at[i],
                                              recv_sem=recv_sems.at[i], device_id=to, device_id_type=MESH)
            cp.wait_send()
            cp.wait_recv()

    outs = pl.pallas_call(
        body, name=name,
        out_shape=tuple(pltpu.HBM(b.shape, b.dtype) for b in bufs),
        in_specs=[_HBM] * n + [_SEM, _SEM, _ANY],
        out_specs=tuple([_HBM] * n),
        input_output_aliases={i: i for i in range(n)},
        compiler_params=pltpu.CompilerParams(has_side_effects=_EFFECT),
    )(*bufs, send_sems, recv_sems, after)
    return list(outs)


def _gather_plan(refs, x, y, c, dry=False, arriving=False):
    if dry:
        return [None] * 3
    (land,) = refs
    j_me = 2 * x + y
    return [(land.at[j_me], land.at[(2 * px + py) if arriving else j_me], (px, py, c)) for px, py in _other_chips(x, y)]


def _scatter_plan(n_pieces):
    def plan(refs, x, y, c, dry=False, arriving=False):
        if dry:
            return [None] * (3 * n_pieces)
        grads, lands = refs[:n_pieces], refs[n_pieces:]
        return [(grads[p].at[2 * px + py], lands[p].at[k], (px, py, c))
                for p in range(n_pieces) for k, (px, py) in enumerate(_other_chips(x, y))]
    return plan


def swap_sibling(parts):
    n = len(parts)

    def body(*refs):
        src, dst = refs[:n], refs[n:2 * n]
        send_sems, recv_sems = refs[2 * n:]
        x, y, c = _my_pos()
        copies = [pltpu.make_async_remote_copy(src_ref=src[t], dst_ref=dst[t], send_sem=send_sems.at[t],
                                               recv_sem=recv_sems.at[t], device_id=(x, y, 1 - c), device_id_type=MESH)
                  for t in range(n)]
        for cp in copies:
            cp.start()
        for cp in copies:
            cp.wait_recv()
        for cp in copies:
            cp.wait_send()

    return pl.pallas_call(
        body, name="swap_sibling",
        out_shape=[jax.ShapeDtypeStruct(p.shape, p.dtype) for p in parts],
        in_specs=[pl.BlockSpec(memory_space=pl.ANY)] * n,
        out_specs=[pl.BlockSpec(memory_space=pl.ANY)] * n,
        scratch_shapes=[pltpu.SemaphoreType.DMA((n,)), pltpu.SemaphoreType.DMA((n,))],
    )(*parts)


def _view2d(a):
    return a.reshape(-1, a.shape[-1])


def _row_tile(rows, width, itemsize=4, budget=1 << 20):
    t = 8
    for cand in (1024, 512, 256, 128, 64, 32, 16, 8):
        if rows % cand == 0 and cand * width * itemsize <= budget:
            t = cand
            break
    return t if rows % t == 0 else rows


def cast_into_slot(name, w, l, j_idx):
    w3 = w.reshape(w.shape[0], -1, w.shape[-1])
    _, R, W = w3.shape
    tr = _row_tile(R, W)

    def body(j_ref, a_ref, o_ref):
        o_ref[...] = a_ref[...].astype(BF16)

    return pl.pallas_call(
        body, name=name,
        grid_spec=pltpu.PrefetchScalarGridSpec(
            num_scalar_prefetch=1, grid=(R // tr,),
            in_specs=[pl.BlockSpec((None, tr, W), lambda i, j: (l, i, 0))],
            out_specs=pl.BlockSpec((None, tr, W), lambda i, j: (j[0], i, 0))),
        out_shape=jax.ShapeDtypeStruct((N_CHIP, R, W), BF16), compiler_params=_cparams(("parallel",)),
    )(j_idx, w3)


def sum_parts(name, groups, j_idx):
    n = len(groups)
    _, R, W = groups[0][0].shape
    tr = _row_tile(R, W)

    def body(j_ref, *refs):
        o_ref = refs[-1]
        g = pl.program_id(0)
        for q in range(n):
            @pl.when(g == q)
            def _(q=q):
                own, got = refs[2 * q], refs[2 * q + 1]
                o_ref[...] = ((own[...].astype(F32) + got[0].astype(F32)) + got[1].astype(F32)) + got[2].astype(F32)

    in_specs = []
    for q in range(n):
        in_specs.append(pl.BlockSpec((None, tr, W), lambda g, i, j, q=q: (j[0], jnp.where(g == q, i, 0), 0)))
        in_specs.append(pl.BlockSpec((3, tr, W), lambda g, i, j, q=q: (0, jnp.where(g == q, i, 0), 0)))
    return pl.pallas_call(
        body, name=name,
        grid_spec=pltpu.PrefetchScalarGridSpec(
            num_scalar_prefetch=1, grid=(n, R // tr), in_specs=in_specs,
            out_specs=pl.BlockSpec((None, tr, W), lambda g, i, j: (g, i, 0))),
        out_shape=jax.ShapeDtypeStruct((n, R, W), F32), compiler_params=_cparams(("arbitrary", "arbitrary")),
    )(j_idx, *[a for pair in groups for a in pair])


def adamw(name, w, m, v, g_parts):
    shape = w.shape
    w2, m2, v2 = _view2d(w), _view2d(m), _view2d(v)
    gs = [_view2d(g) for g in g_parts]
    R, W = w2.shape
    tr = _row_tile(R, W, budget=1 << 19)
    ng = len(gs)
    bc1 = 1.0 - ADAM_B1 ** ADAM_STEP
    bc2 = 1.0 - ADAM_B2 ** ADAM_STEP

    def body(*refs):
        w_ref, m_ref, v_ref = refs[:3]
        g_refs = refs[3:3 + ng]
        go_ref, d_ref, mo_ref, vo_ref = refs[3 + ng:]
        g = g_refs[0][...]
        for r in g_refs[1:]:
            g = g + r[...]
        mn = ADAM_B1 * m_ref[...] + (1.0 - ADAM_B1) * g
        vn = ADAM_B2 * v_ref[...] + (1.0 - ADAM_B2) * (g * g)
        m_hat = mn / bc1
        v_hat = vn / bc2
        go_ref[...] = g
        d_ref[...] = -ADAM_LR * (m_hat / (jnp.sqrt(v_hat) + ADAM_EPS) + ADAM_WD * w_ref[...])
        mo_ref[...] = mn
        vo_ref[...] = vn

    spec = pl.BlockSpec((tr, W), lambda i: (i, 0))
    outs = pl.pallas_call(
        body, name=name, grid=(R // tr,),
        in_specs=[spec] * (3 + ng), out_specs=[spec] * 4,
        out_shape=[jax.ShapeDtypeStruct((R, W), F32)] * 4, compiler_params=_cparams(("parallel",)),
    )(w2, m2, v2, *gs)
    return tuple(o.reshape(shape) for o in outs)


class Dims:
    pass


def _sel(dm):
    return (pl.program_id(0) >= dm.nctx).astype(jnp.int32)


def norm_mod(name, h, gn, modtab, s, dm, deps=()):
    T, D = h.shape
    tm = dm.tme

    def body(h_ref, g_ref, m_ref, *rest):
        u_ref = rest[-1]
        sel = _sel(dm)
        x = h_ref[...]
        r = lax.rsqrt(jnp.mean(x * x, axis=-1, keepdims=True) + EPS)
        ng = x * r * g_ref[...]
        u_ref[...] = (ng * (1.0 + m_ref[sel, 3 * s + 1]) + m_ref[sel, 3 * s]).astype(u_ref.dtype)

    return pl.pallas_call(
        body, name=name, grid=(T // tm,),
        in_specs=[pl.BlockSpec((tm, D), lambda i: (i, 0)), pl.BlockSpec((1, D), lambda i: (0, 0)),
                  pl.BlockSpec((2, N_MOD, 1, D), lambda i: (0, 0, 0, 0))] + [_ANY] * len(deps),
        out_specs=pl.BlockSpec((tm, D), lambda i: (i, 0)),
        out_shape=jax.ShapeDtypeStruct((T, D), BF16), compiler_params=_cparams(("parallel",)),
    )(h, gn, modtab, *deps)


def norm_mod_bwd(name, du, h, gn, modtab, s, dh_in, dm):
    T, D = h.shape
    tm = dm.tme

    def body(du_ref, h_ref, g_ref, m_ref, dhi_ref, dh_ref, dmod_ref, dg_ref):
        i = pl.program_id(0)
        sel = _sel(dm)

        @pl.when(i == 0)
        def _():
            dmod_ref[...] = jnp.zeros_like(dmod_ref)
            dg_ref[...] = jnp.zeros_like(dg_ref)

        x = h_ref[...]
        r = lax.rsqrt(jnp.mean(x * x, axis=-1, keepdims=True) + EPS)
        n = x * r
        g = g_ref[...]
        du = du_ref[...]
        dmod_ref[sel, 0] += jnp.sum(du, axis=0, keepdims=True)
        dmod_ref[sel, 1] += jnp.sum(du * (n * g), axis=0, keepdims=True)
        dng = du * (1.0 + m_ref[sel, 3 * s + 1])
        dg_ref[...] += jnp.sum(dng * n, axis=0, keepdims=True)
        dn = dng * g
        dh_ref[...] = dhi_ref[...] + r * (dn - n * jnp.mean(dn * n, axis=-1, keepdims=True))

    row = pl.BlockSpec((tm, D), lambda i: (i, 0))
    return pl.pallas_call(
        body, name=name, grid=(T // tm,),
        in_specs=[row, row, pl.BlockSpec((1, D), lambda i: (0, 0)),
                  pl.BlockSpec((2, N_MOD, 1, D), lambda i: (0, 0, 0, 0)), row],
        out_specs=[row, pl.BlockSpec((2, 2, 1, D), lambda i: (0, 0, 0, 0)), pl.BlockSpec((1, D), lambda i: (0, 0))],
        out_shape=[jax.ShapeDtypeStruct((T, D), F32), jax.ShapeDtypeStruct((2, 2, 1, D), F32),
                   jax.ShapeDtypeStruct((1, D), F32)],
        compiler_params=_cparams(("arbitrary",)),
    )(du, h, gn, modtab, dh_in)


def resid(name, h, f, modtab, s, coef, dm):
    T, D = h.shape
    tm = dm.tme

    def body(h_ref, f_ref, m_ref, o_ref):
        o_ref[...] = h_ref[...] + (coef * m_ref[_sel(dm), 3 * s + 2]) * f_ref[...]

    row = pl.BlockSpec((tm, D), lambda i: (i, 0))
    return pl.pallas_call(
        body, name=name, grid=(T // tm,),
        in_specs=[row, row, pl.BlockSpec((2, N_MOD, 1, D), lambda i: (0, 0, 0, 0))], out_specs=row,
        out_shape=jax.ShapeDtypeStruct((T, D), F32), compiler_params=_cparams(("parallel",)),
    )(h, f, modtab)


def resid_bwd(name, dh, f, modtab, s, coef, dm, deps=()):
    T, D = dh.shape
    tm = dm.tme

    def body(dh_ref, f_ref, m_ref, *rest):
        df_ref, dg_ref = rest[-2:]
        sel = _sel(dm)

        @pl.when(pl.program_id(0) == 0)
        def _():
            dg_ref[...] = jnp.zeros_like(dg_ref)

        d = coef * dh_ref[...]
        df_ref[...] = (d * m_ref[sel, 3 * s + 2]).astype(df_ref.dtype)
        dg_ref[sel, 0] += jnp.sum(d * f_ref[...], axis=0, keepdims=True)

    row = pl.BlockSpec((tm, D), lambda i: (i, 0))
    return pl.pallas_call(
        body, name=name, grid=(T // tm,),
        in_specs=[row, row, pl.BlockSpec((2, N_MOD, 1, D), lambda i: (0, 0, 0, 0))] + [_ANY] * len(deps),
        out_specs=[row, pl.BlockSpec((2, 1, 1, D), lambda i: (0, 0, 0, 0))],
        out_shape=[jax.ShapeDtypeStruct((T, D), BF16), jax.ShapeDtypeStruct((2, 1, 1, D), F32)],
        compiler_params=_cparams(("arbitrary",)),
    )(dh, f, modtab, *deps)


def swiglu(name, gu, dm):
    _, T, Nq = gu.shape
    tm = dm.tme
    gu4 = gu.reshape(2, 2, T, Nq)

    def body(gu_ref, o_ref):
        g = gu_ref[0].astype(F32)
        o_ref[...] = (_silu(g) * gu_ref[1].astype(F32)).astype(o_ref.dtype)

    return pl.pallas_call(
        body, name=name, grid=(2, T // tm),
        in_specs=[pl.BlockSpec((2, None, tm, Nq), lambda k, i: (0, k, i, 0))],
        out_specs=pl.BlockSpec((None, tm, Nq), lambda k, i: (k, i, 0)),
        out_shape=jax.ShapeDtypeStruct((2, T, Nq), BF16), compiler_params=_cparams(("parallel", "parallel")),
    )(gu4)


def swiglu_bwd(name, dact, gu, dm):
    _, T, Nq = gu.shape
    tm = dm.tme
    gu4 = gu.reshape(2, 2, T, Nq)

    def body(da_ref, gu_ref, o_ref):
        g = gu_ref[0].astype(F32)
        da = da_ref[...].astype(F32)
        o_ref[0] = (da * gu_ref[1].astype(F32) * _dsilu(g)).astype(o_ref.dtype)
        o_ref[1] = (da * _silu(g)).astype(o_ref.dtype)

    out = pl.pallas_call(
        body, name=name, grid=(2, T // tm),
        in_specs=[pl.BlockSpec((None, tm, Nq), lambda k, i: (k, i, 0)),
                  pl.BlockSpec((2, None, tm, Nq), lambda k, i: (0, k, i, 0))],
        out_specs=pl.BlockSpec((2, None, tm, Nq), lambda k, i: (0, k, i, 0)),
        out_shape=jax.ShapeDtypeStruct((2, 2, T, Nq), BF16), compiler_params=_cparams(("parallel", "parallel")),
    )(dact, gu4)
    return out.reshape(4, T, Nq)


def _halo_specs(dm, width, col):
    tm = dm.tme
    per = tm // HALO
    last = dm.T // HALO - 1
    return [pl.BlockSpec((tm, width), lambda i: (i, col)),
            pl.BlockSpec((HALO, width), lambda i: (jnp.maximum(i * per - 1, 0), col)),
            pl.BlockSpec((HALO, width), lambda i: (jnp.minimum((i + 1) * per, last), col))]


def _segment_edges(dm, i):
    first = jnp.logical_or(i == 0, i == dm.nctx)
    last = jnp.logical_or(i == dm.nctx - 1, i == dm.nt - 1)
    return first, last


def _extend(main, prev, nxt, first, last):
    return jnp.concatenate([jnp.where(first, 0.0, prev), main, jnp.where(last, 0.0, nxt)], axis=0)


def _shift(ext, o, tm):
    n = ext.shape[0]
    rolled = ext if o == 0 else pltpu.roll(ext, (-o) % n, 0)
    return rolled[HALO:HALO + tm]


def _load_ext(refs, first, last):
    main, prev, nxt = refs
    return _extend(main[...].astype(F32), prev[...].astype(F32), nxt[...].astype(F32), first, last)


def rnn_conv(name, z, w, b, dm):
    T, RW, tm = dm.T, dm.RW, dm.tme

    def body(main, prev, nxt, w_ref, b_ref, o_ref):
        first, last = _segment_edges(dm, pl.program_id(0))
        ext = _load_ext((main, prev, nxt), first, last)
        acc = jnp.zeros((tm, RW), F32) + b_ref[...]
        for k in range(4):
            acc = acc + w_ref[k] * _shift(ext, k - 2, tm)
        o_ref[...] = acc

    return pl.pallas_call(
        body, name=name, grid=(dm.nt,),
        in_specs=_halo_specs(dm, RW, 0) + [pl.BlockSpec((4, 1, RW), lambda i: (0, 0, 0)),
                                           pl.BlockSpec((1, RW), lambda i: (0, 0))],
        out_specs=pl.BlockSpec((tm, RW), lambda i: (i, 0)),
        out_shape=jax.ShapeDtypeStruct((T, RW), F32), compiler_params=_cparams(("parallel",)),
    )(z, z, z, w, b)


def _blockdiag(x, w_ref):
    nb = w_ref.shape[0]
    outs = [jnp.dot(x[:, n * RNN_BLOCK:(n + 1) * RNN_BLOCK], w_ref[n].astype(BF16), preferred_element_type=F32)
            for n in range(nb)]
    return jnp.concatenate(outs, axis=-1)


def _lru_gates(xa, wa_ref, ba_ref, wx_ref, bx_ref, lam_ref):
    xb = xa.astype(BF16)
    r = jax.nn.sigmoid(_blockdiag(xb, wa_ref) + ba_ref[...])
    ig = jax.nn.sigmoid(_blockdiag(xb, wx_ref) + bx_ref[...])
    nl = -lam_ref[...]
    sp = jnp.maximum(nl, 0.0) + jnp.log(1.0 + jnp.exp(-jnp.abs(nl)))
    log_a = -LRU_C * r * sp
    a = jnp.exp(log_a)
    m = jnp.sqrt(-_expm1(2.0 * log_a))
    return r, ig, sp, a, m


def _lru_specs(l, d, nb, RW):
    wspec = pl.BlockSpec((None, None, nb, RNN_BLOCK, RNN_BLOCK), lambda i: (l, d, 0, 0, 0))
    vspec = pl.BlockSpec((None, None, 1, RW), lambda i: (l, d, 0, 0))
    return [wspec, vspec, wspec, vspec, vspec]


def lru_gates(name, xa, lw, l, d, dm):
    T, RW, tm = dm.T, dm.RW, dm.tme

    def body(xa_ref, wa_ref, ba_ref, wx_ref, bx_ref, lam_ref, a_ref, u_ref):
        xa_v = xa_ref[...]
        r, ig, sp, a, m = _lru_gates(xa_v, wa_ref, ba_ref, wx_ref, bx_ref, lam_ref)
        a_ref[...] = a
        u_ref[...] = m * (ig * xa_v)

    row = pl.BlockSpec((tm, RW), lambda i: (i, 0))
    return pl.pallas_call(
        body, name=name, grid=(dm.nt,),
        in_specs=[row] + _lru_specs(l, d, dm.NB, RW), out_specs=[row, row],
        out_shape=[jax.ShapeDtypeStruct((T, RW), F32)] * 2, compiler_params=_cparams(("parallel",)),
    )(xa, lw["w_a"], lw["b_a"], lw["w_x"], lw["b_x"], lw["lam"])


def lru_gates_bwd(name, xa, lw, l, d, du, dloga, dm):
    T, RW, tm, NB = dm.T, dm.RW, dm.tme, dm.NB

    def body(xa_ref, wa_ref, ba_ref, wx_ref, bx_ref, lam_ref, du_ref, dla_ref,
             dxa_ref, dwa_ref, dba_ref, dwx_ref, dbx_ref, dlam_ref):
        @pl.when(pl.program_id(0) == 0)
        def _():
            for ref in (dwa_ref, dba_ref, dwx_ref, dbx_ref, dlam_ref):
                ref[...] = jnp.zeros_like(ref)

        xa_v = xa_ref[...]
        r, ig, sp, a, m = _lru_gates(xa_v, wa_ref, ba_ref, wx_ref, bx_ref, lam_ref)
        duu = du_ref[...]
        dm_ = duu * (ig * xa_v)
        dig = duu * m * xa_v
        dxa = duu * m * ig
        dla = dla_ref[...] - dm_ * (a * a) / m
        dr = dla * (-LRU_C * sp)
        dsp = jnp.sum(dla * (-LRU_C * r), axis=0, keepdims=True)
        dlam_ref[...] += dsp * (-jax.nn.sigmoid(-lam_ref[...]))
        dpa = dr * r * (1.0 - r)
        dpx = dig * ig * (1.0 - ig)
        dba_ref[...] += jnp.sum(dpa, axis=0, keepdims=True)
        dbx_ref[...] += jnp.sum(dpx, axis=0, keepdims=True)
        xb, dpab, dpxb = xa_v.astype(BF16), dpa.astype(BF16), dpx.astype(BF16)
        back = []
        for n in range(NB):
            sl = slice(n * RNN_BLOCK, (n + 1) * RNN_BLOCK)
            dwa_ref[n] += lax.dot_general(xb[:, sl], dpab[:, sl], _DIMS["tn"], preferred_element_type=F32)
            dwx_ref[n] += lax.dot_general(xb[:, sl], dpxb[:, sl], _DIMS["tn"], preferred_element_type=F32)
            back.append(lax.dot_general(dpab[:, sl], wa_ref[n].astype(BF16), _DIMS["nt"], preferred_element_type=F32)
                        + lax.dot_general(dpxb[:, sl], wx_ref[n].astype(BF16), _DIMS["nt"], preferred_element_type=F32))
        dxa_ref[...] = dxa + jnp.concatenate(back, axis=-1)

    row = pl.BlockSpec((tm, RW), lambda i: (i, 0))
    wacc = pl.BlockSpec((NB, RNN_BLOCK, RNN_BLOCK), lambda i: (0, 0, 0))
    vacc = pl.BlockSpec((1, RW), lambda i: (0, 0))
    wshape = jax.ShapeDtypeStruct((NB, RNN_BLOCK, RNN_BLOCK), F32)
    vshape = jax.ShapeDtypeStruct((1, RW), F32)
    return pl.pallas_call(
        body, name=name, grid=(dm.nt,),
        in_specs=[row] + _lru_specs(l, d, NB, RW) + [row, row],
        out_specs=[row, wacc, vacc, wacc, vacc, vacc],
        out_shape=[jax.ShapeDtypeStruct((T, RW), F32), wshape, vshape, wshape, vshape, vshape],
        compiler_params=_cparams(("arbitrary",)),
    )(xa, lw["w_a"], lw["b_a"], lw["w_x"], lw["b_x"], lw["lam"], du, dloga)


def _chunk_order(dm, ctx_first, descending):
    nch, nctx = dm.nt, dm.nctx
    nlat = nch - nctx

    def order(s):
        if ctx_first and not descending:
            return s
        if not ctx_first and descending:
            return nch - 1 - s
        if ctx_first:
            return jnp.where(s < nctx, nctx - 1 - s, nch - 1 - (s - nctx))
        return jnp.where(s < nlat, nctx + s, s - nlat)

    return order


def _tile_scan(a, b, carry, descending):
    row = lax.broadcasted_iota(jnp.int32, a.shape, 0)
    for s in (1, 2, 4):
        sh = (HALO - s) if descending else s
        keep = (row < HALO - s) if descending else (row >= s)
        ap = pltpu.roll(a, sh, 0)
        bp = pltpu.roll(b, sh, 0)
        b = jnp.where(keep, b + a * bp, b)
        a = jnp.where(keep, a * ap, a)
    h = b + a * carry
    edge = 0 if descending else HALO - 1
    new_carry = jnp.sum(jnp.where(row == edge, h, 0.0), axis=0, keepdims=True)
    return h, new_carry


def lru_scan(name, a, u, ctx_first, descending, dm):
    T, RW, ch = dm.T, dm.RW, dm.tme
    order = _chunk_order(dm, ctx_first, descending)
    ngrp = ch // HALO

    def body(a_ref, u_ref, h_ref, carry_ref):
        @pl.when(pl.program_id(0) == 0)
        def _():
            carry_ref[...] = jnp.zeros_like(carry_ref)

        def step(g, carry):
            g = (ngrp - 1 - g) if descending else g
            rows = pl.ds(pl.multiple_of(g * HALO, HALO), HALO)
            h, carry = _tile_scan(a_ref[rows, :], u_ref[rows, :], carry, descending)
            h_ref[rows, :] = h
            return carry

        carry_ref[...] = lax.fori_loop(0, ngrp, step, carry_ref[...])

    row = pl.BlockSpec((ch, RW), lambda s: (order(s), 0))
    return pl.pallas_call(
        body, name=name, grid=(dm.nt,),
        in_specs=[row, row], out_specs=row,
        out_shape=jax.ShapeDtypeStruct((T, RW), F32),
        scratch_shapes=[pltpu.VMEM((1, RW), F32)], compiler_params=_cparams(("arbitrary",)),
    )(a, u)


def lru_scan_bwd(name, a, u, h, dh, ctx_first, descending, dm):
    T, RW, ch = dm.T, dm.RW, dm.tme
    order = _chunk_order(dm, ctx_first, descending)
    ngrp = ch // HALO

    def body(a_ref, u_ref, h_ref, dh_ref, lam_ref, dla_ref, carry_ref):
        @pl.when(pl.program_id(0) == 0)
        def _():
            carry_ref[...] = jnp.zeros_like(carry_ref)

        def step(g, carry):
            g = (ngrp - 1 - g) if descending else g
            rows = pl.ds(pl.multiple_of(g * HALO, HALO), HALO)
            a_v, dh_v = a_ref[rows, :], dh_ref[rows, :]
            mu, new_carry = _tile_scan(a_v, a_v * dh_v, carry, descending)
            row = lax.broadcasted_iota(jnp.int32, mu.shape, 0)
            if descending:
                nxt = jnp.where(row == HALO - 1, carry, pltpu.roll(mu, HALO - 1, 0))
            else:
                nxt = jnp.where(row == 0, carry, pltpu.roll(mu, 1, 0))
            lam = dh_v + nxt
            lam_ref[rows, :] = lam
            dla_ref[rows, :] = lam * (h_ref[rows, :] - u_ref[rows, :])
            return new_carry

        carry_ref[...] = lax.fori_loop(0, ngrp, step, carry_ref[...])

    row = pl.BlockSpec((ch, RW), lambda s: (order(s), 0))
    return pl.pallas_call(
        body, name=name, grid=(dm.nt,),
        in_specs=[row] * 4, out_specs=[row, row],
        out_shape=[jax.ShapeDtypeStruct((T, RW), F32)] * 2,
        scratch_shapes=[pltpu.VMEM((1, RW), F32)], compiler_params=_cparams(("arbitrary",)),
    )(a, u, h, dh)


def rnn_out(name, hf, hb, z, dm):
    T, RW, tm = dm.T, dm.RW, dm.tme

    def body(hf_ref, hb_ref, rg_ref, o_ref):
        o_ref[...] = ((hf_ref[...] + hb_ref[...]) * _gelu(rg_ref[...])).astype(o_ref.dtype)

    row = pl.BlockSpec((tm, RW), lambda i: (i, 0))
    return pl.pallas_call(
        body, name=name, grid=(dm.nt,),
        in_specs=[row, row, pl.BlockSpec((tm, RW), lambda i: (i, 1))], out_specs=row,
        out_shape=jax.ShapeDtypeStruct((T, RW), BF16), compiler_params=_cparams(("parallel",)),
    )(hf, hb, z)


def rnn_out_bwd(name, dya, hf, hb, z, dm):
    T, RW, tm = dm.T, dm.RW, dm.tme

    def body(d_ref, hf_ref, hb_ref, rg_ref, dh_ref, drg_ref):
        d, rg = d_ref[...], rg_ref[...]
        dh_ref[...] = d * _gelu(rg)
        drg_ref[...] = d * (hf_ref[...] + hb_ref[...]) * _dgelu(rg)

    row = pl.BlockSpec((tm, RW), lambda i: (i, 0))
    return pl.pallas_call(
        body, name=name, grid=(dm.nt,),
        in_specs=[row, row, row, pl.BlockSpec((tm, RW), lambda i: (i, 1))], out_specs=[row, row],
        out_shape=[jax.ShapeDtypeStruct((T, RW), F32)] * 2, compiler_params=_cparams(("parallel",)),
    )(dya, hf, hb, z)


def rnn_conv_bwd(name, dxa_f, dxa_b, drg, z, w, dz, dm):
    T, RW, tm = dm.T, dm.RW, dm.tme

    def body(f0, f1, f2, b0, b1, b2, x0, x1, x2, drg_ref, w_ref, dz_in, dz_ref, dw_ref, db_ref):
        i = pl.program_id(0)

        @pl.when(i == 0)
        def _():
            dw_ref[...] = jnp.zeros_like(dw_ref)
            db_ref[...] = jnp.zeros_like(db_ref)

        first, last = _segment_edges(dm, i)
        dext = _load_ext((f0, f1, f2), first, last) + _load_ext((b0, b1, b2), first, last)
        xext = _load_ext((x0, x1, x2), first, last)
        dmain = dext[HALO:HALO + tm]
        drx = jnp.zeros((tm, RW), F32)
        for k in range(4):
            drx = drx + w_ref[k] * _shift(dext, -(k - 2), tm)
            dw_ref[k] += jnp.sum(dmain * _shift(xext, k - 2, tm), axis=0, keepdims=True)
        db_ref[...] += jnp.sum(dmain, axis=0, keepdims=True)
        dz_ref[:, :RW] = drx.astype(dz_ref.dtype)
        dz_ref[:, RW:] = drg_ref[...].astype(dz_ref.dtype)

    return pl.pallas_call(
        body, name=name, grid=(dm.nt,),
        in_specs=_halo_specs(dm, RW, 0) * 3 + [pl.BlockSpec((tm, RW), lambda i: (i, 0)),
                                               pl.BlockSpec((4, 1, RW), lambda i: (0, 0, 0)),
                                               pl.BlockSpec(memory_space=pl.ANY)],
        out_specs=[pl.BlockSpec((tm, 2 * RW), lambda i: (i, 0)), pl.BlockSpec((4, 1, RW), lambda i: (0, 0, 0)),
                   pl.BlockSpec((1, RW), lambda i: (0, 0))],
        out_shape=[jax.ShapeDtypeStruct(dz.shape, dz.dtype), jax.ShapeDtypeStruct((4, 1, RW), F32),
                   jax.ShapeDtypeStruct((1, RW), F32)],
        input_output_aliases={11: 0}, compiler_params=_cparams(("arbitrary",)),
    )(dxa_f, dxa_f, dxa_f, dxa_b, dxa_b, dxa_b, z, z, z, drg, w, dz)


def short_conv(name, z, w, dm):
    T, RW, tm = dm.T, dm.RW, dm.tme

    def body(sb_ref, g0, g1, g2, x0, x1, x2, w_ref, o_ref):
        first, last = _segment_edges(dm, pl.program_id(0))
        pext = _load_ext((g0, g1, g2), first, last) * _load_ext((x0, x1, x2), first, last)
        cp = jnp.zeros((tm, RW), F32)
        for k in range(3):
            cp = cp + w_ref[k] * _shift(pext, k - 1, tm)
        o_ref[...] = (sb_ref[...] * cp).astype(o_ref.dtype)

    return pl.pallas_call(
        body, name=name, grid=(dm.nt,),
        in_specs=[pl.BlockSpec((tm, RW), lambda i: (i, 2))] + _halo_specs(dm, RW, 3) + _halo_specs(dm, RW, 4)
        + [pl.BlockSpec((3, 1, RW), lambda i: (0, 0, 0))],
        out_specs=pl.BlockSpec((tm, RW), lambda i: (i, 0)),
        out_shape=jax.ShapeDtypeStruct((T, RW), BF16), compiler_params=_cparams(("parallel",)),
    )(z, z, z, z, z, z, z, w)


def short_conv_bwd(name, dyb, z, w, dz, dm):
    T, RW, tm = dm.T, dm.RW, dm.tme

    def spec3(col):
        per = tm // HALO
        last = T // HALO - 1
        return [pl.BlockSpec((tm, RW), lambda i, p: (i, col)),
                pl.BlockSpec((HALO, RW), lambda i, p: (jnp.maximum(i * per - 1, 0), col)),
                pl.BlockSpec((HALO, RW), lambda i, p: (jnp.minimum((i + 1) * per, last), col))]

    def body(d0, d1, d2, s0, s1, s2, g0, g1, g2, x0, x1, x2, w_ref, dz_in, dz_ref, dw_ref):
        i, p = pl.program_id(0), pl.program_id(1)

        @pl.when(jnp.logical_and(i == 0, p == 0))
        def _():
            dw_ref[...] = jnp.zeros_like(dw_ref)

        first, last = _segment_edges(dm, i)
        gext = _load_ext((g0, g1, g2), first, last)
        xext = _load_ext((x0, x1, x2), first, last)
        pext = gext * xext
        dyext = _load_ext((d0, d1, d2), first, last)
        dcext = dyext * _load_ext((s0, s1, s2), first, last)
        dcmain = dcext[HALO:HALO + tm]
        cp = jnp.zeros((tm, RW), F32)
        dp = jnp.zeros((tm, RW), F32)
        for k in range(3):
            pk = _shift(pext, k - 1, tm)
            cp = cp + w_ref[k] * pk
            dp = dp + w_ref[k] * _shift(dcext, -(k - 1), tm)

            @pl.when(p == 0)
            def _(k=k, pk=pk):
                dw_ref[k] += jnp.sum(dcmain * pk, axis=0, keepdims=True)

        dsb = dyext[HALO:HALO + tm] * cp
        dscg = dp * xext[HALO:HALO + tm]
        dsx = dp * gext[HALO:HALO + tm]
        dz_ref[...] = jnp.where(p == 0, dsb, jnp.where(p == 1, dscg, dsx)).astype(dz_ref.dtype)

    return pl.pallas_call(
        body, name=name, grid=(dm.nt, 3),
        in_specs=spec3(0) + spec3(2) + spec3(3) + spec3(4)
        + [pl.BlockSpec((3, 1, RW), lambda i, p: (0, 0, 0)), pl.BlockSpec(memory_space=pl.ANY)],
        out_specs=[pl.BlockSpec((tm, RW), lambda i, p: (i, 2 + p)), pl.BlockSpec((3, 1, RW), lambda i, p: (0, 0, 0))],
        out_shape=[jax.ShapeDtypeStruct(dz.shape, dz.dtype), jax.ShapeDtypeStruct((3, 1, RW), F32)],
        input_output_aliases={13: 0}, compiler_params=_cparams(("arbitrary", "arbitrary")),
    )(dyb, dyb, dyb, z, z, z, z, z, z, z, z, z, w, dz)


def _rope_tables(dm):
    L, C = dm.L, dm.C
    half = HEAD_DIM // 2
    pos = jnp.arange(L)
    row = (pos // GRID_W).astype(F32)
    col = (pos % GRID_W).astype(F32)
    inv = ROPE_BASE ** (-jnp.arange(0, half, 2, dtype=F32) / half)
    ar, ac = row[:, None] * inv, col[:, None] * inv
    cos = jnp.concatenate([jnp.cos(ar), jnp.cos(ar), jnp.cos(ac), jnp.cos(ac)], axis=-1)
    sin = jnp.concatenate([-jnp.sin(ar), jnp.sin(ar), -jnp.sin(ac), jnp.sin(ac)], axis=-1)
    cos = jnp.concatenate([jnp.ones((C, HEAD_DIM), F32), cos], axis=0)
    sin = jnp.concatenate([jnp.zeros((C, HEAD_DIM), F32), sin], axis=0)
    return cos, sin


def _swap_pairs(x):
    quarter = HEAD_DIM // 4
    lane = lax.broadcasted_iota(jnp.int32, x.shape, 1)
    return jnp.where(lane % (2 * quarter) < quarter, pltpu.roll(x, HEAD_DIM - quarter, 1), pltpu.roll(x, quarter, 1))


def _rope(x, cos, sin):
    return x * cos + _swap_pairs(x) * sin


def _unrope(d, cos, sin):
    return d * cos + _swap_pairs(d * sin)


def qkv_prep(name, z, cos, sin, dm):
    T, tm, HQ, KW = dm.T, dm.tme, dm.HQ, dm.KW
    qcol, kcol = dm.off_q // HQ, dm.off_k // KW

    def body(q_ref, k_ref, v_ref, c_ref, s_ref, qo, ko, vo):
        cos_v, sin_v = c_ref[...], s_ref[...]
        for hd in range(HQ // HEAD_DIM):
            sl = slice(hd * HEAD_DIM, (hd + 1) * HEAD_DIM)
            qo[:, sl] = _rope(q_ref[:, sl], cos_v, sin_v).astype(qo.dtype)
        for hd in range(KW // HEAD_DIM):
            sl = slice(hd * HEAD_DIM, (hd + 1) * HEAD_DIM)
            ko[:, sl] = _rope(k_ref[:, sl], cos_v, sin_v).astype(ko.dtype)
        vo[...] = v_ref[...].astype(vo.dtype)

    tab = pl.BlockSpec((tm, HEAD_DIM), lambda i: (i, 0))
    return pl.pallas_call(
        body, name=name, grid=(dm.nt,),
        in_specs=[pl.BlockSpec((tm, HQ), lambda i: (i, qcol)), pl.BlockSpec((tm, KW), lambda i: (i, kcol)),
                  pl.BlockSpec((tm, KW), lambda i: (i, kcol + 1)), tab, tab],
        out_specs=[pl.BlockSpec((tm, HQ), lambda i: (i, 0)), pl.BlockSpec((tm, KW), lambda i: (i, 0)),
                   pl.BlockSpec((tm, KW), lambda i: (i, 0))],
        out_shape=[jax.ShapeDtypeStruct((T, HQ), BF16), jax.ShapeDtypeStruct((T, KW), BF16),
                   jax.ShapeDtypeStruct((T, KW), BF16)],
        compiler_params=_cparams(("parallel",)),
    )(z, z, z, cos, sin)


def qkv_bwd(name, dq, dk, dv, cos, sin, dz, dm):
    T, tm, HQ, KW = dm.T, dm.tme, dm.HQ, dm.KW
    nq = HQ // KW
    base = dm.off_q // KW

    def body(dq_ref, dk_ref, dv_ref, c_ref, s_ref, dz_in, dz_ref):
        p = pl.program_id(1)
        src = jnp.where(p < nq, dq_ref[...], jnp.where(p == nq, dk_ref[...], dv_ref[...]))
        cos_v, sin_v = c_ref[...], s_ref[...]
        is_v = p == nq + 1
        for hd in range(KW // HEAD_DIM):
            sl = slice(hd * HEAD_DIM, (hd + 1) * HEAD_DIM)
            dz_ref[:, sl] = jnp.where(is_v, src[:, sl], _unrope(src[:, sl], cos_v, sin_v)).astype(dz_ref.dtype)

    tab = pl.BlockSpec((tm, HEAD_DIM), lambda i, p: (i, 0))
    blk = pl.BlockSpec((tm, KW), lambda i, p: (i, 0))
    return pl.pallas_call(
        body, name=name, grid=(dm.nt, nq + 2),
        in_specs=[pl.BlockSpec((tm, KW), lambda i, p: (i, jnp.minimum(p, nq - 1))), blk, blk, tab, tab,
                  pl.BlockSpec(memory_space=pl.ANY)],
        out_specs=pl.BlockSpec((tm, KW), lambda i, p: (i, base + p)),
        out_shape=jax.ShapeDtypeStruct(dz.shape, dz.dtype),
        input_output_aliases={5: 0}, compiler_params=_cparams(("parallel", "arbitrary")),
    )(dq, dk, dv, cos, sin, dz)


def _attn_specs(dm):
    nC, nB, C, KW = dm.C // Q_BLOCK, dm.T // Q_BLOCK, dm.C, dm.KW

    def near(o):
        return lambda b: (jnp.clip(b + o, nC, nB - 1), 0)

    kv = [pl.BlockSpec((Q_BLOCK, KW), near(o)) for o in (-1, 0, 1)] + [pl.BlockSpec((C, KW), lambda b: (0, 0))]
    return kv


def _attn_mask(dm, b):
    nC, C, L = dm.C // Q_BLOCK, dm.C, dm.L
    span = 3 * Q_BLOCK
    n = b - nC
    iq = lax.broadcasted_iota(jnp.int32, (Q_BLOCK, span + C), 0)
    ik = lax.broadcasted_iota(jnp.int32, (Q_BLOCK, span + C), 1)
    kpos = n * Q_BLOCK + ik - Q_BLOCK
    qpos = n * Q_BLOCK + iq
    local = (b >= nC) & (jnp.abs(qpos - kpos) <= WINDOW) & (kpos >= 0) & (kpos < L)
    return jnp.logical_or(ik >= span, local)


def attention(name, q, k, v, sink, dm):
    T, HQ, KW = dm.T, dm.HQ, dm.KW
    H, KV = HQ // HEAD_DIM, KW // HEAD_DIM
    G = H // KV
    scale = HEAD_DIM ** -0.5

    def body(q_ref, kp, kc, kn, kx, vp, vc, vn, vx, sink_ref, o_ref, lse_ref):
        valid = _attn_mask(dm, pl.program_id(0))
        lane = lax.broadcasted_iota(jnp.int32, (Q_BLOCK, LSE_W), 1)
        lse_all = jnp.zeros((Q_BLOCK, LSE_W), F32)
        for kh in range(KV):
            ks = slice(kh * HEAD_DIM, (kh + 1) * HEAD_DIM)
            k_all = jnp.concatenate([kp[:, ks], kc[:, ks], kn[:, ks], kx[:, ks]], axis=0)
            v_all = jnp.concatenate([vp[:, ks], vc[:, ks], vn[:, ks], vx[:, ks]], axis=0)
            for g in range(G):
                hd = kh * G + g
                hs = slice(hd * HEAD_DIM, (hd + 1) * HEAD_DIM)
                s = lax.dot_general(q_ref[:, hs], k_all, _DIMS["nt"], preferred_element_type=F32) * scale
                s = jnp.where(valid, s, NEG_INF)
                snk = sink_ref[0, hd]
                mx = jnp.maximum(jnp.max(s, axis=-1, keepdims=True), snk)
                p = jnp.exp(s - mx)
                den = jnp.sum(p, axis=-1, keepdims=True) + jnp.exp(snk - mx)
                o = jnp.dot(p.astype(BF16), v_all, preferred_element_type=F32) / den
                o_ref[:, hs] = o.astype(o_ref.dtype)
                lse_all = jnp.where(lane == hd, mx + jnp.log(den), lse_all)
        lse_ref[...] = lse_all

    kv = _attn_specs(dm)
    return pl.pallas_call(
        body, name=name, grid=(T // Q_BLOCK,),
        in_specs=[pl.BlockSpec((Q_BLOCK, HQ), lambda b: (b, 0))] + kv + kv + [pl.BlockSpec(memory_space=pltpu.SMEM)],
        out_specs=[pl.BlockSpec((Q_BLOCK, HQ), lambda b: (b, 0)), pl.BlockSpec((Q_BLOCK, LSE_W), lambda b: (b, 0))],
        out_shape=[jax.ShapeDtypeStruct((T, HQ), BF16), jax.ShapeDtypeStruct((T, LSE_W), F32)],
        compiler_params=_cparams(("parallel",)),
    )(q, k, k, k, k, v, v, v, v, sink)


def attention_bwd(name, q, k, v, sink, o, lse, do, dm):
    T, HQ, KW, C = dm.T, dm.HQ, dm.KW, dm.C
    H, KV = HQ // HEAD_DIM, KW // HEAD_DIM
    G = H // KV
    nC, nB = C // Q_BLOCK, T // Q_BLOCK
    scale = HEAD_DIM ** -0.5
    span = 3 * Q_BLOCK

    def body(q_ref, kp, kc, kn, kx, vp, vc, vn, vx, sink_ref, o_ref, lse_ref, do_ref,
             dq_ref, dk_ref, dv_ref, ds_ref):
        b = pl.program_id(0)

        @pl.when(b == 0)
        def _():
            dk_ref[...] = jnp.zeros_like(dk_ref)
            dv_ref[...] = jnp.zeros_like(dv_ref)
            ds_ref[...] = jnp.zeros_like(ds_ref)

        valid = _attn_mask(dm, b)
        starts = [pl.multiple_of(jnp.clip(b + off, nC, nB - 1) * Q_BLOCK, Q_BLOCK) for off in (-1, 0, 1)]
        lane = lax.broadcasted_iota(jnp.int32, (Q_BLOCK, LSE_W), 1)
        lse_all = lse_ref[...]
        dsink = jnp.zeros((1, LSE_W), F32)
        for kh in range(KV):
            ks = slice(kh * HEAD_DIM, (kh + 1) * HEAD_DIM)
            k_all = jnp.concatenate([kp[:, ks], kc[:, ks], kn[:, ks], kx[:, ks]], axis=0)
            v_all = jnp.concatenate([vp[:, ks], vc[:, ks], vn[:, ks], vx[:, ks]], axis=0)
            dk_all = jnp.zeros((span + C, HEAD_DIM), F32)
            dv_all = jnp.zeros((span + C, HEAD_DIM), F32)
            for g in range(G):
                hd = kh * G + g
                hs = slice(hd * HEAD_DIM, (hd + 1) * HEAD_DIM)
                qh = q_ref[:, hs]
                doh = do_ref[:, hs]
                s = lax.dot_general(qh, k_all, _DIMS["nt"], preferred_element_type=F32) * scale
                s = jnp.where(valid, s, NEG_INF)
                lse_h = jnp.sum(jnp.where(lane == hd, lse_all, 0.0), axis=-1, keepdims=True)
                p = jnp.exp(s - lse_h)
                delta = jnp.sum(doh * o_ref[:, hs].astype(F32), axis=-1, keepdims=True)
                dob = doh.astype(BF16)
                dp = lax.dot_general(dob, v_all, _DIMS["nt"], preferred_element_type=F32)
                dsc = (p * (dp - delta) * scale).astype(BF16)
                dq_ref[:, hs] = jnp.dot(dsc, k_all, preferred_element_type=F32)
                dk_all = dk_all + lax.dot_general(dsc, qh, _DIMS["tn"], preferred_element_type=F32)
                dv_all = dv_all + lax.dot_general(p.astype(BF16), dob, _DIMS["tn"], preferred_element_type=F32)
                p_sink = jnp.exp(sink_ref[0, hd] - lse_h)
                dsink = dsink + jnp.where(lane[0:1] == hd, -jnp.sum(p_sink * delta), 0.0)
            for j, st in enumerate(starts):
                rows = pl.ds(st, Q_BLOCK)
                dk_ref[rows, ks] += dk_all[j * Q_BLOCK:(j + 1) * Q_BLOCK]
                dv_ref[rows, ks] += dv_all[j * Q_BLOCK:(j + 1) * Q_BLOCK]
            dk_ref[0:C, ks] += dk_all[span:]
            dv_ref[0:C, ks] += dv_all[span:]
        ds_ref[...] += dsink

    kv = _attn_specs(dm)
    qspec = pl.BlockSpec((Q_BLOCK, HQ), lambda b: (b, 0))
    full = pl.BlockSpec((T, KW), lambda b: (0, 0))
    return pl.pallas_call(
        body, name=name, grid=(nB,),
        in_specs=[qspec] + kv + kv + [pl.BlockSpec(memory_space=pltpu.SMEM), qspec,
                                      pl.BlockSpec((Q_BLOCK, LSE_W), lambda b: (b, 0)), qspec],
        out_specs=[qspec, full, full, pl.BlockSpec((1, LSE_W), lambda b: (0, 0))],
        out_shape=[jax.ShapeDtypeStruct((T, HQ), F32), jax.ShapeDtypeStruct((T, KW), F32),
                   jax.ShapeDtypeStruct((T, KW), F32), jax.ShapeDtypeStruct((1, LSE_W), F32)],
        compiler_params=_cparams(("arbitrary",)),
    )(q, k, k, k, k, v, v, v, v, sink, o, lse, do)


def merge(name, z, lifted, b_merge, dm):
    T, D, tm, cw = dm.T, dm.D, dm.tme, dm.cw
    gcol = dm.off_g // cw
    per = D // cw

    def body(g0, g1, g2, l0, l1, l2, b_ref, o_ref):
        acc = jnp.zeros((tm, cw), F32)
        for i, (g, lf) in enumerate(((g0, l0), (g1, l1), (g2, l2))):
            acc = acc + jax.nn.sigmoid(g[...] + b_ref[i]) * lf[...]
        o_ref[...] = acc.astype(o_ref.dtype)

    gspecs = [pl.BlockSpec((tm, cw), lambda i, j, br=br: (i, gcol + br * per + j)) for br in range(N_BRANCH)]
    blk = pl.BlockSpec((tm, cw), lambda i, j: (i, j))
    return pl.pallas_call(
        body, name=name, grid=(T // tm, per),
        in_specs=gspecs + [blk] * 3 + [pl.BlockSpec((N_BRANCH, 1, cw), lambda i, j: (0, 0, j))], out_specs=blk,
        out_shape=jax.ShapeDtypeStruct((T, D), BF16), compiler_params=_cparams(("parallel", "parallel")),
    )(z, z, z, *lifted, b_merge)


def merge_bwd(name, dmerged, z, lifted_br, b_merge, br, dz, dm):
    T, D, tm, cw = dm.T, dm.D, dm.tme, dm.cw
    gcol = dm.off_g // cw + br * (D // cw)
    per = D // cw

    def body(d_ref, g_ref, l_ref, b_ref, dz_in, dl_ref, dz_ref, db_ref):
        @pl.when(pl.program_id(1) == 0)
        def _():
            db_ref[...] = jnp.zeros_like(db_ref)

        d = d_ref[...]
        gate = jax.nn.sigmoid(g_ref[...] + b_ref[br])
        dl_ref[...] = (d * gate).astype(dl_ref.dtype)
        dg = d * l_ref[...] * gate * (1.0 - gate)
        dz_ref[...] = dg.astype(dz_ref.dtype)
        db_ref[...] += jnp.sum(dg, axis=0, keepdims=True)

    blk = pl.BlockSpec((tm, cw), lambda j, i: (i, j))
    zblk = pl.BlockSpec((tm, cw), lambda j, i: (i, gcol + j))
    return pl.pallas_call(
        body, name=name, grid=(per, T // tm),
        in_specs=[blk, zblk, blk, pl.BlockSpec((N_BRANCH, 1, cw), lambda j, i: (0, 0, j)),
                  pl.BlockSpec(memory_space=pl.ANY)],
        out_specs=[blk, zblk, pl.BlockSpec((1, cw), lambda j, i: (0, j))],
        out_shape=[jax.ShapeDtypeStruct((T, D), BF16), jax.ShapeDtypeStruct(dz.shape, dz.dtype),
                   jax.ShapeDtypeStruct((1, D), F32)],
        input_output_aliases={4: 1}, compiler_params=_cparams(("parallel", "arbitrary")),
    )(dmerged, z, lifted_br, b_merge, dz)


def loss_head(name, h, gf, target, dm):
    T, D, tm, nctx = dm.T, dm.D, dm.tme, dm.nctx

    def body(h_ref, g_ref, t_ref, dh_ref, loss_ref, dg_ref):
        i = pl.program_id(0)

        @pl.when(i == 0)
        def _():
            loss_ref[...] = jnp.zeros_like(loss_ref)
            dg_ref[...] = jnp.zeros_like(dg_ref)

        @pl.when(i < nctx)
        def _():
            dh_ref[...] = jnp.zeros_like(dh_ref)

        @pl.when(i >= nctx)
        def _():
            x = h_ref[...]
            r = lax.rsqrt(jnp.mean(x * x, axis=-1, keepdims=True) + EPS)
            n = x * r
            g = g_ref[...]
            err = n * g - t_ref[...]
            loss_ref[...] += jnp.sum(err * err) * (0.5 / D)
            dy = err * (1.0 / D)
            dg_ref[...] += jnp.sum(dy * n, axis=0, keepdims=True)
            dn = dy * g
            dh_ref[...] = r * (dn - n * jnp.mean(dn * n, axis=-1, keepdims=True))

    row = pl.BlockSpec((tm, D), lambda i: (i, 0))
    return pl.pallas_call(
        body, name=name, grid=(T // tm,),
        in_specs=[row, pl.BlockSpec((1, D), lambda i: (0, 0)),
                  pl.BlockSpec((tm, D), lambda i: (jnp.maximum(i - nctx, 0), 0))],
        out_specs=[row, pl.BlockSpec((1, 128), lambda i: (0, 0)), pl.BlockSpec((1, D), lambda i: (0, 0))],
        out_shape=[jax.ShapeDtypeStruct((T, D), F32), jax.ShapeDtypeStruct((1, 128), F32),
                   jax.ShapeDtypeStruct((1, D), F32)],
        compiler_params=_cparams(("arbitrary",)),
    )(h, gf, target)


_HI = lax.Precision.HIGHEST
ADA_ROWS = 16


def ada_forward(name, cond, ada_w, bias):
    _, D, Aq = ada_w.shape
    tc = _pick(Aq, (1536, 1152, 768, 512, 384, 256, 128))
    tk = _ktile(D)
    nk = D // tk

    def body(c_ref, w_ref, b_ref, o_ref):
        k = pl.program_id(2)

        @pl.when(k == 0)
        def _():
            o_ref[...] = jnp.zeros_like(o_ref) + b_ref[...]

        o_ref[...] += jnp.dot(_silu(c_ref[...]), w_ref[...], precision=_HI, preferred_element_type=F32)

    return pl.pallas_call(
        body, name=name, grid=(2, Aq // tc, nk),
        in_specs=[pl.BlockSpec((ADA_ROWS, tk), lambda l, j, k: (0, k)),
                  pl.BlockSpec((None, tk, tc), lambda l, j, k: (l, k, j)),
                  pl.BlockSpec((None, 1, tc), lambda l, j, k: (l, 0, j))],
        out_specs=pl.BlockSpec((None, ADA_ROWS, tc), lambda l, j, k: (l, 0, j)),
        out_shape=jax.ShapeDtypeStruct((2, ADA_ROWS, Aq), F32),
        compiler_params=_cparams(("parallel", "parallel", "arbitrary")),
    )(cond, ada_w, bias)


def ada_cond_grad(name, dmod, ada_w):
    _, D, Aq = ada_w.shape
    tc = _pick(Aq, (1536, 1152, 768, 512, 384, 256, 128))
    tn = _ktile(D)
    nc = Aq // tc

    def body(d_ref, w_ref, o_ref):
        @pl.when(jnp.logical_and(pl.program_id(1) == 0, pl.program_id(2) == 0))
        def _():
            o_ref[...] = jnp.zeros_like(o_ref)

        o_ref[...] += lax.dot_general(d_ref[...], w_ref[...], _DIMS["nt"], precision=_HI, preferred_element_type=F32)

    return pl.pallas_call(
        body, name=name, grid=(D // tn, 2, nc),
        in_specs=[pl.BlockSpec((None, ADA_ROWS, tc), lambda j, l, c: (l, 0, c)),
                  pl.BlockSpec((None, tn, tc), lambda j, l, c: (l, j, c))],
        out_specs=pl.BlockSpec((ADA_ROWS, tn), lambda j, l, c: (0, j)),
        out_shape=jax.ShapeDtypeStruct((ADA_ROWS, D), F32),
        compiler_params=_cparams(("parallel", "arbitrary", "arbitrary")),
    )(dmod, ada_w)


def ada_update(name, cond, dmod, w, m, v):
    _, D, Aq = w.shape
    tc = _pick(Aq, (1536, 1152, 768, 512, 384, 256, 128))
    tr = 128 if D % 128 == 0 else D
    bc1 = 1.0 - ADAM_B1 ** ADAM_STEP
    bc2 = 1.0 - ADAM_B2 ** ADAM_STEP

    def body(c_ref, d_ref, w_ref, m_ref, v_ref, go_ref, dl_ref, mo_ref, vo_ref):
        g = lax.dot_general(_silu(c_ref[...]), d_ref[...], _DIMS["tn"], precision=_HI, preferred_element_type=F32)
        mn = ADAM_B1 * m_ref[...] + (1.0 - ADAM_B1) * g
        vn = ADAM_B2 * v_ref[...] + (1.0 - ADAM_B2) * (g * g)
        go_ref[...] = g
        dl_ref[...] = -ADAM_LR * ((mn / bc1) / (jnp.sqrt(vn / bc2) + ADAM_EPS) + ADAM_WD * w_ref[...])
        mo_ref[...] = mn
        vo_ref[...] = vn

    blk = pl.BlockSpec((None, tr, tc), lambda l, i, j: (l, i, j))
    return pl.pallas_call(
        body, name=name, grid=(2, D // tr, Aq // tc),
        in_specs=[pl.BlockSpec((ADA_ROWS, tr), lambda l, i, j: (0, i)),
                  pl.BlockSpec((None, ADA_ROWS, tc), lambda l, i, j: (l, 0, j)), blk, blk, blk],
        out_specs=[blk] * 4, out_shape=[jax.ShapeDtypeStruct(w.shape, F32)] * 4,
        compiler_params=_cparams(("parallel", "parallel", "parallel")),
    )(cond, dmod, w, m, v)


def dmod_assemble(name, gathered):
    A = gathered.shape[-1]
    tc = _pick(A, (2048, 1024, 512, 256, 128))

    def body(g_ref, o_ref, b_ref):
        ctx = g_ref[0, 1]
        for dev in range(1, N_DEV):
            ctx = ctx + g_ref[dev, 1]
        tot = ctx
        for dev in range(N_DEV):
            o_ref[dev:dev + 1, :] = g_ref[dev, 0]
            tot = tot + g_ref[dev, 0]
        o_ref[N_DEV:N_DEV + 1, :] = ctx
        o_ref[N_DEV + 1:, :] = jnp.zeros((ADA_ROWS - N_DEV - 1, tc), F32)
        b_ref[...] = tot

    return pl.pallas_call(
        body, name=name, grid=(2, A // tc),
        in_specs=[pl.BlockSpec((N_DEV, None, 2, 1, tc), lambda l, j: (0, l, 0, 0, j))],
        out_specs=[pl.BlockSpec((None, ADA_ROWS, tc), lambda l, j: (l, 0, j)),
                   pl.BlockSpec((None, 1, tc), lambda l, j: (l, 0, j))],
        out_shape=[jax.ShapeDtypeStruct((2, ADA_ROWS, A), F32), jax.ShapeDtypeStruct((2, 1, A), F32)],
        compiler_params=_cparams(("parallel", "parallel")),
    )(gathered)


def sum_devices(name, gathered):
    _, R, W = gathered.shape
    tr = _row_tile(R, W, budget=1 << 18)

    def body(g_ref, all_ref, chip_ref):
        even = g_ref[0]
        odd = g_ref[1]
        for dev in range(2, N_DEV, 2):
            even = even + g_ref[dev]
            odd = odd + g_ref[dev + 1]
        all_ref[...] = even + odd
        chip_ref[...] = even

    blk = pl.BlockSpec((tr, W), lambda i: (i, 0))
    return pl.pallas_call(
        body, name=name, grid=(R // tr,),
        in_specs=[pl.BlockSpec((N_DEV, tr, W), lambda i: (0, i, 0))], out_specs=[blk, blk],
        out_shape=[jax.ShapeDtypeStruct((R, W), F32)] * 2, compiler_params=_cparams(("parallel",)),
    )(gathered)


PACK_ROWS = 1024


def _pack(arrays):
    flat = jnp.concatenate([a.reshape(-1).astype(F32) for a in arrays])
    pad = (-flat.shape[0]) % (PACK_ROWS * 128)
    return jnp.pad(flat, (0, pad)).reshape(-1, 128)


def _unpack(buf, shapes, lead=()):
    flat = buf.reshape(lead + (-1,))
    out, start = [], 0
    for s in shapes:
        n = math.prod(s)
        out.append(flat[..., start:start + n].reshape(lead + tuple(s)))
        start += n
    return out


def _unshard_last(g):
    g = jnp.moveaxis(g, 0, -2)
    return g.reshape(g.shape[:-2] + (g.shape[-2] * g.shape[-1],))


class Gathered:
    def __init__(self, name, land, view=None):
        self.name, self.view = name, view
        self.send_sems, self.recv_sems, (self.land,), self.token = exchange_start(f"{name}_start", [land], _gather_plan)
        self.value = None

    def get(self, after):
        if self.value is None:
            (full,) = exchange_wait(f"{self.name}_wait", self.send_sems, self.recv_sems, [self.land], _gather_plan, after)
            self.value = full if self.view is None else full.reshape(self.view)
        return self.value


def _ffn_forward(tag, h, gn, modtab, s, w13, w2, dm, deps=()):
    u = norm_mod(f"{tag}_norm", h, gn, modtab, s, dm, deps)
    gu = mm_cols(f"{tag}_w13", u, w13.get(u), BF16, flat=False)
    act = swiglu(f"{tag}_act", gu, dm)
    f = mm_rows(f"{tag}_w2", act, w2.get(act))
    h_out = resid(f"{tag}_res", h, f, modtab, s, 0.5, dm)
    return h_out, (h, u, gu, act, f)


def _ffn_backward(tag, dh, saved, gn, modtab, s, w13, w2, dm, deps=()):
    h, u, gu, act, f = saved
    w13g, w2g = w13.get(None), w2.get(None)
    df, dgate = resid_bwd(f"{tag}_res_bwd", dh, f, modtab, s, 0.5, dm, deps)
    dact = mm_rows_t(f"{tag}_dact", df, w2g)
    dw2 = mm_rows_grad(f"{tag}_dw2", act, df).reshape(N_CHIP, -1, df.shape[-1])
    dgu = swiglu_bwd(f"{tag}_act_bwd", dact, gu, dm)
    du = mm_cols_t(f"{tag}_du", dgu, w13g, flat=False)
    dw13 = mm_cols_grad(f"{tag}_dw13", u, dgu, flat=False)
    dh_in, dss, dgn = norm_mod_bwd(f"{tag}_norm_bwd", du, h, gn, modtab, s, dh, dm)
    return dh_in, dw13, dw2, dss, dgate, dgn


def kernel(x, c, ctx, c_ctx, ada_w, ada_b, norm_g, ffn1_w13, ffn1_w2, w_in, b_merge, rnn_conv_w, rnn_conv_b, lru_w_a, lru_b_a, lru_w_x, lru_b_x, lru_lambda, sc_conv_w, attn_sink, w_branch, w_out, ffn2_w13, ffn2_w2, final_norm_g, loss_target, m_c_ctx, m_ada_w, m_ada_b, m_norm_g, m_ffn1_w13, m_ffn1_w2, m_w_in, m_b_merge, m_rnn_conv_w, m_rnn_conv_b, m_lru_w_a, m_lru_b_a, m_lru_w_x, m_lru_b_x, m_lru_lambda, m_sc_conv_w, m_attn_sink, m_w_branch, m_w_out, m_ffn2_w13, m_ffn2_w2, m_final_norm_g, v_c_ctx, v_ada_w, v_ada_b, v_norm_g, v_ffn1_w13, v_ffn1_w2, v_w_in, v_b_merge, v_rnn_conv_w, v_rnn_conv_b, v_lru_w_a, v_lru_b_a, v_lru_w_x, v_lru_b_x, v_lru_lambda, v_sc_conv_w, v_attn_sink, v_w_branch, v_w_out, v_ffn2_w13, v_ffn2_w2, v_final_norm_g):
    weights = dict(c_ctx=c_ctx, ada_w=ada_w, ada_b=ada_b, norm_g=norm_g, ffn1_w13=ffn1_w13, ffn1_w2=ffn1_w2, w_in=w_in,
                   b_merge=b_merge, rnn_conv_w=rnn_conv_w, rnn_conv_b=rnn_conv_b, lru_w_a=lru_w_a, lru_b_a=lru_b_a,
                   lru_w_x=lru_w_x, lru_b_x=lru_b_x, lru_lambda=lru_lambda, sc_conv_w=sc_conv_w, attn_sink=attn_sink,
                   w_branch=w_branch, w_out=w_out, ffn2_w13=ffn2_w13, ffn2_w2=ffn2_w2, final_norm_g=final_norm_g)
    mom_m = dict(c_ctx=m_c_ctx, ada_w=m_ada_w, ada_b=m_ada_b, norm_g=m_norm_g, ffn1_w13=m_ffn1_w13, ffn1_w2=m_ffn1_w2,
                 w_in=m_w_in, b_merge=m_b_merge, rnn_conv_w=m_rnn_conv_w, rnn_conv_b=m_rnn_conv_b, lru_w_a=m_lru_w_a,
                 lru_b_a=m_lru_b_a, lru_w_x=m_lru_w_x, lru_b_x=m_lru_b_x, lru_lambda=m_lru_lambda,
                 sc_conv_w=m_sc_conv_w, attn_sink=m_attn_sink, w_branch=m_w_branch, w_out=m_w_out,
                 ffn2_w13=m_ffn2_w13, ffn2_w2=m_ffn2_w2, final_norm_g=m_final_norm_g)
    mom_v = dict(c_ctx=v_c_ctx, ada_w=v_ada_w, ada_b=v_ada_b, norm_g=v_norm_g, ffn1_w13=v_ffn1_w13, ffn1_w2=v_ffn1_w2,
                 w_in=v_w_in, b_merge=v_b_merge, rnn_conv_w=v_rnn_conv_w, rnn_conv_b=v_rnn_conv_b, lru_w_a=v_lru_w_a,
                 lru_b_a=v_lru_b_a, lru_w_x=v_lru_w_x, lru_b_x=v_lru_b_x, lru_lambda=v_lru_lambda,
                 sc_conv_w=v_sc_conv_w, attn_sink=v_attn_sink, w_branch=v_w_branch, w_out=v_w_out,
                 ffn2_w13=v_ffn2_w13, ffn2_w2=v_ffn2_w2, final_norm_g=v_final_norm_g)
    order = list(weights)

    dm = Dims()
    dm.D = D = x.shape[-1]
    dm.L = L = x.shape[1]
    dm.C = C = ctx.shape[1]
    dm.T = T = L + C
    dm.RW = RW = rnn_conv_b.shape[-1]
    dm.NB = lru_w_a.shape[2]
    H = attn_sink.shape[-1]
    dm.HQ = HQ = H * HEAD_DIM
    NZ = w_in.shape[-1] * N_CHIP
    dm.KW = KW = (NZ - 5 * RW - HQ - N_BRANCH * D) // 2
    dm.off_q = 5 * RW
    dm.off_k = dm.off_q + HQ
    dm.off_g = dm.off_k + 2 * KW
    dm.tme = _pick(C, (256, 128))
    dm.nt = T // dm.tme
    dm.nctx = C // dm.tme
    dm.cw = next(w for w in (512, 256, 128) if dm.off_g % w == 0 and D % w == 0)
    A = ada_b.shape[-1]
    Aq = ada_w.shape[-1]
    assert dm.off_q % HQ == 0 and dm.off_k % KW == 0 and HQ % KW == 0 and L % dm.tme == 0 and RW == HQ
    assert C % Q_BLOCK == 0 and L % Q_BLOCK == 0 and D % N_CHIP == 0 and A == N_MOD * D

    mx, my, mc = _my_pos()
    j_me = 2 * mx + my
    b_me = 4 * mx + 2 * my + mc

    small_sharded = ["norm_g", "b_merge", "rnn_conv_w", "lru_b_a", "lru_b_x", "lru_lambda", "sc_conv_w"]
    pack1 = _pack([c] + [weights[n] for n in small_sharded])
    g1 = allgather8("gather_small_params", pack1).reshape(N_DEV, -1, 128)
    parts = _unpack(g1, [c.shape] + [weights[n].shape for n in small_sharded], lead=(N_DEV,))
    c_all = parts[0].reshape(N_DEV, D)
    full = {n: _unshard_last(p[0::2]) for n, p in zip(small_sharded, parts[1:])}
    cond = jnp.concatenate([c_all, c_ctx[None, :], jnp.zeros((ADA_ROWS - N_DEV - 1, D), F32)], axis=0)

    bias_q = lax.dynamic_slice_in_dim(ada_b, j_me * Aq, Aq, axis=1)[:, None, :]
    mod_q = ada_forward("ada_forward", cond, ada_w, bias_q)
    g2 = allgather8("gather_mod", mod_q.reshape(-1, 128)).reshape(N_DEV, 2, ADA_ROWS, Aq)
    mod_full = _unshard_last(g2[0::2])
    mod_lat = lax.dynamic_index_in_dim(mod_full, b_me, axis=1, keepdims=False)
    mod_ctx = mod_full[:, N_DEV]
    modtabs = [jnp.stack([mod_ctx[l], mod_lat[l]]).reshape(2, N_MOD, 1, D) for l in range(2)]

    big = ["ffn1_w13", "ffn1_w2", "w_in", "w_branch", "w_out", "ffn2_w13", "ffn2_w2"]
    j_idx = jnp.reshape(j_me, (1,)).astype(jnp.int32)
    FFq = ffn1_w2.shape[1]
    views = {"ffn1_w2": (2, 2 * FFq, D), "ffn2_w2": (2, 2 * FFq, D), "w_out": (D, D),
             "w_branch": (N_CHIP, N_BRANCH, RW, D // N_CHIP)}
    wts = [{n: Gathered(f"l{l}_{n}", cast_into_slot(f"l{l}_cast_{n}", weights[n], l, j_idx), views.get(n))
            for n in big} for l in range(2)]
    gather_tokens = [wts[l][n].token for l in range(2) for n in big]

    cos, sin = _rope_tables(dm)
    sink = attn_sink.reshape(2, 1, H)
    lw = dict(w_a=lru_w_a, w_x=lru_w_x,
              b_a=full["lru_b_a"][:, :, None, :], b_x=full["lru_b_x"][:, :, None, :],
              lam=full["lru_lambda"][:, :, None, :])
    gn = full["norm_g"]
    bm = full["b_merge"][:, :, None, :]
    rcw = full["rnn_conv_w"][:, :, None, :]
    scw = full["sc_conv_w"][:, :, None, :]

    h = jnp.concatenate([ctx[0], x[0]], axis=0)
    saved = []
    for l in range(2):
        mt = modtabs[l]
        wl = wts[l]
        sv = {}
        h, sv["ffn1"] = _ffn_forward(f"l{l}_ffn1", h, gn[l, 0:1], mt, 0, wl["ffn1_w13"], wl["ffn1_w2"], dm,
                                     deps=gather_tokens if l == 0 else ())
        sv["h_mix"] = h
        u = norm_mod(f"l{l}_mix_norm", h, gn[l, 1:2], mt, 1, dm)
        z = mm_cols(f"l{l}_w_in", u, wl["w_in"].get(u), F32, flat=True)
        xa = rnn_conv(f"l{l}_rnn_conv", z, rcw[l], rnn_conv_b[l][None, :], dm)
        scans = []
        for d in range(2):
            a_d, u_d = lru_gates(f"l{l}_lru_gates{d}", xa, lw, l, d, dm)
            h_d = lru_scan(f"l{l}_lru_scan{d}", a_d, u_d, True, d == 1, dm)
            scans.append((a_d, u_d, h_d))
        ya = rnn_out(f"l{l}_rnn_out", scans[0][2], scans[1][2], z, dm)
        yb = short_conv(f"l{l}_short_conv", z, scw[l], dm)
        qr, kr, vv = qkv_prep(f"l{l}_qkv", z, cos, sin, dm)
        yatt, lse = attention(f"l{l}_attn", qr, kr, vv, sink[l], dm)
        ys = (ya, yb, yatt)
        wbg = wl["w_branch"].get(yatt)
        lifted = [mm_branch(f"l{l}_lift{br}", ys[br], wbg, br) for br in range(N_BRANCH)]
        merged = merge(f"l{l}_merge", z, lifted, bm[l], dm)
        y = mm_plain(f"l{l}_w_out", merged, wl["w_out"].get(merged), "nn", F32)
        h = resid(f"l{l}_mix_res", h, y, mt, 1, 1.0, dm)
        sv.update(u=u, z=z, xa=xa, scans=scans, ys=ys, qkv=(qr, kr, vv), lse=lse, lifted=lifted, merged=merged, y=y)
        h, sv["ffn2"] = _ffn_forward(f"l{l}_ffn2", h, gn[l, 2:3], mt, 2, wl["ffn2_w13"], wl["ffn2_w2"], dm)
        saved.append(sv)

    dh, loss_vec, d_final_g = loss_head("loss_head", h, final_norm_g[None, :], loss_target[0], dm)
    loss = lax.psum(loss_vec[0, 0], ("x", "y", "c"))

    small = {n: [None, None] for n in ["norm_g", "b_merge", "rnn_conv_w", "rnn_conv_b", "lru_w_a", "lru_b_a", "lru_w_x",
                                       "lru_b_x", "lru_lambda", "sc_conv_w", "attn_sink"]}
    dmods = [None, None]
    scatters = []

    def scatter(name, keyed):
        grads3 = [g.reshape(N_CHIP, -1, g.shape[-1]) for g in keyed.values()]
        lands = [lax.empty((3,) + g.shape[1:], BF16) for g in grads3]
        ss, rs, bufs, token = exchange_start(f"{name}_start", grads3 + lands, _scatter_plan(len(grads3)))
        scatters.append((name, ss, rs, bufs, list(keyed)))
        return token

    tok = ()
    for l in (1, 0):
        mt = modtabs[l]
        sv = saved[l]
        wl = wts[l]
        dh, dw13, dw2, dss2, dgate2, dgn2 = _ffn_backward(f"l{l}_ffn2", dh, sv["ffn2"], gn[l, 2:3], mt, 2,
                                                         wl["ffn2_w13"], wl["ffn2_w2"], dm, deps=tok)
        tok = (scatter(f"l{l}_scatter_ffn2", {("ffn2_w13", l): dw13, ("ffn2_w2", l): dw2}),)

        dyg, dgate1 = resid_bwd(f"l{l}_mix_res_bwd", dh, sv["y"], mt, 1, 1.0, dm, deps=tok)
        woutg, wbg, wing = wl["w_out"].get(None), wl["w_branch"].get(None), wl["w_in"].get(None)
        dmerged = mm_plain(f"l{l}_dmerged", dyg, woutg, "nt", F32)
        mix_grads = {("w_out", l): mm_plain_grad(f"l{l}_dw_out", sv["merged"], dyg)}
        dz = jnp.zeros((T, NZ), BF16)
        dys, dbm = [], []
        for br in range(N_BRANCH):
            dl, dz, db = merge_bwd(f"l{l}_merge_bwd{br}", dmerged, sv["z"], sv["lifted"][br], bm[l], br, dz, dm)
            dys.append(mm_branch_t(f"l{l}_dy{br}", dl, wbg, br))
            mix_grads[("w_branch", l, br)] = mm_branch_grad(f"l{l}_dwb{br}", sv["ys"][br], dl)
            dbm.append(db)
        small["b_merge"][l] = jnp.concatenate(dbm, axis=0)

        qr, kr, vv = sv["qkv"]
        dq, dk, dv, dsink = attention_bwd(f"l{l}_attn_bwd", qr, kr, vv, sink[l], sv["ys"][2], sv["lse"], dys[2], dm)
        dz = qkv_bwd(f"l{l}_qkv_bwd", dq, dk, dv, cos, sin, dz, dm)
        small["attn_sink"][l] = dsink[0, :H]

        dz, dscw = short_conv_bwd(f"l{l}_short_conv_bwd", dys[1], sv["z"], scw[l], dz, dm)
        small["sc_conv_w"][l] = dscw[:, 0]

        (a0, u0, h0), (a1, u1, h1) = sv["scans"]
        dhs, drg = rnn_out_bwd(f"l{l}_rnn_out_bwd", dys[0], h0, h1, sv["z"], dm)
        dxa, lru_sums = [], []
        for d, (a_d, u_d, h_d) in enumerate(sv["scans"]):
            lam_d, dla_d = lru_scan_bwd(f"l{l}_lru_scan_bwd{d}", a_d, u_d, h_d, dhs, False, d == 0, dm)
            outs = lru_gates_bwd(f"l{l}_lru_gates_bwd{d}", sv["xa"], lw, l, d, lam_d, dla_d, dm)
            dxa.append(outs[0])
            lru_sums.append(outs[1:])
        dz, drcw, drcb = rnn_conv_bwd(f"l{l}_rnn_conv_bwd", dxa[0], dxa[1], drg, sv["z"], rcw[l], dz, dm)
        small["rnn_conv_w"][l] = drcw[:, 0]
        small["rnn_conv_b"][l] = drcb[0]
        for i, n in enumerate(["lru_w_a", "lru_b_a", "lru_w_x", "lru_b_x", "lru_lambda"]):
            small[n][l] = jnp.stack([lru_sums[0][i], lru_sums[1][i]]).reshape((2,) + weights[n].shape[2:-1] + (-1,))

        du = mm_cols_t(f"l{l}_du_mix", dz, wing, flat=True)
        mix_grads[("w_in", l)] = mm_cols_grad(f"l{l}_dw_in", sv["u"], dz, flat=True)
        dh, dss1, dgn1 = norm_mod_bwd(f"l{l}_mix_norm_bwd", du, sv["h_mix"], gn[l, 1:2], mt, 1, dh, dm)
        tok = (scatter(f"l{l}_scatter_mix", mix_grads),)

        dh, dw13, dw2, dss0, dgate0, dgn0 = _ffn_backward(f"l{l}_ffn1", dh, sv["ffn1"], gn[l, 0:1], mt, 0,
                                                         wl["ffn1_w13"], wl["ffn1_w2"], dm, deps=tok)
        tok = (scatter(f"l{l}_scatter_ffn1", {("ffn1_w13", l): dw13, ("ffn1_w2", l): dw2}),)
        small["norm_g"][l] = jnp.concatenate([dgn0, dgn1, dgn2], axis=0)
        dmods[l] = jnp.concatenate([dss0, dgate0, dss1, dgate1, dss2, dgate2], axis=1).reshape(2, A)

    grad_x = dh[C:][None]

    pack_mod = jnp.stack([jnp.stack([dmods[l][1], dmods[l][0]]) for l in range(2)])
    g3 = allgather8("gather_dmod", pack_mod.reshape(-1, 128)).reshape(N_DEV, 2, 2, 1, A)
    dmod_full, d_ada_b = dmod_assemble("dmod_assemble", g3)
    dmod_q = lax.dynamic_slice_in_dim(dmod_full, j_me * Aq, Aq, axis=2)
    dcond_q = ada_cond_grad("ada_cond_grad", dmod_q, ada_w)

    small_names = list(small)
    small_parts = [jnp.stack(small[n]) for n in small_names] + [d_final_g, dcond_q[N_DEV]]
    small_shapes = [p.shape for p in small_parts]
    lru_big = [small_names.index("lru_w_a"), small_names.index("lru_w_x")]
    rest_idx = [i for i in range(len(small_parts)) if i not in lru_big]
    summed = [None] * len(small_parts)
    for i in lru_big:
        buf = _pack([small_parts[i]])
        tot, _ = sum_devices(f"sum_{small_names[i]}", allgather8(f"gather_{small_names[i]}", buf).reshape(N_DEV, -1, 128))
        summed[i] = _unpack(tot, [small_shapes[i]])[0]
    buf = _pack([small_parts[i] for i in rest_idx])
    tot, chip_tot = sum_devices("sum_small_grads", allgather8("gather_small_grads", buf).reshape(N_DEV, -1, 128))
    for i, val in zip(rest_idx, _unpack(tot, [small_shapes[i] for i in rest_idx])):
        summed[i] = val
    dcond_ctx = _unpack(chip_tot, [small_shapes[i] for i in rest_idx])[-1]
    sg = jax.nn.sigmoid(c_ctx)
    grads = dict(zip(small_names, summed[:len(small_names)]))
    grads["final_norm_g"] = summed[len(small_names)][0]
    grads["c_ctx"] = dcond_ctx * (sg * (1.0 + c_ctx * (1.0 - sg)))
    grads["ada_b"] = d_ada_b[:, 0]
    for n in small_sharded:
        g = grads[n]
        q = g.shape[-1] // N_CHIP
        grads[n] = lax.dynamic_slice_in_dim(g, j_me * q, q, axis=g.ndim - 1)

    arrived = {}
    for name, ss, rs, bufs, keys in scatters:
        done = exchange_wait(f"{name}_wait", ss, rs, bufs, _scatter_plan(len(keys)), dh)
        for i, key in enumerate(keys):
            arrived[key] = (done[i], done[len(keys) + i])
    mine = []
    for n in big:
        keys = [k for l in range(2) for k in arrived if k[0] == n and k[1] == l]
        keys.sort(key=lambda k: k[1:])
        mine.append(sum_parts(f"sum_{n}", [arrived[k] for k in keys], j_idx).reshape(weights[n].shape))
    theirs = swap_sibling(mine)

    results = {}
    for n, p_own, p_sib in zip(big, mine, theirs):
        results[n] = adamw(f"adamw_{n}", weights[n], mom_m[n], mom_v[n], [p_own, p_sib])
    results["ada_w"] = ada_update("ada_update", cond, dmod_q, ada_w, m_ada_w, v_ada_w)
    small_all = [n for n in order if n not in results]
    pk = lambda d: _pack([d[n] for n in small_all])
    outs = adamw("adamw_small", pk(weights), pk(mom_m), pk(mom_v), [pk(grads)])
    shapes_small = [weights[n].shape for n in small_all]
    unpacked = [_unpack(o, shapes_small) for o in outs]
    for i, n in enumerate(small_all):
        results[n] = tuple(unpacked[k][i] for k in range(4))

    return (loss, grad_x, *[results[n][0] for n in order], *[results[n][1] for n in order],
            *[results[n][2] for n in order], *[results[n][3] for n in order])
```

```python
import functools
import math

import jax
import jax.numpy as jnp
from jax import lax
from jax.experimental import pallas as pl
from jax.experimental.pallas import tpu as pltpu

F32 = jnp.float32
BF16 = jnp.bfloat16
MESH = pl.DeviceIdType.MESH

HEAD_DIM = 128
GRID_W = 64
WINDOW = 128
Q_BLOCK = 128
ROPE_BASE = 10000.0
LRU_C = 8.0
EPS = 1e-6
NEG_INF = -1e30
N_MOD = 9
N_BRANCH = 3
RNN_BLOCK = 128
HALO = 8
LSE_W = 128

ADAM_LR = 0.001
ADAM_B1 = 0.9
ADAM_B2 = 0.999
ADAM_EPS = 1e-08
ADAM_WD = 0.01
ADAM_STEP = 10

VMEM_LIMIT_BYTES = 48 * 1024 * 1024
N_DEV = 8
N_CHIP = 4


def _pick(n, cands):
    for c in cands:
        if c <= n and n % c == 0:
            return c
    return n


def _cparams(sem):
    return pltpu.CompilerParams(dimension_semantics=sem, vmem_limit_bytes=VMEM_LIMIT_BYTES)


def _silu(x):
    return x * jax.nn.sigmoid(x)


def _dsilu(x):
    s = jax.nn.sigmoid(x)
    return s * (1.0 + x * (1.0 - s))


_GELU_K = math.sqrt(2.0 / math.pi)


def _gelu(x):
    return 0.5 * x * (1.0 + jnp.tanh(_GELU_K * (x + 0.044715 * x * x * x)))


def _dgelu(x):
    t = jnp.tanh(_GELU_K * (x + 0.044715 * x * x * x))
    return 0.5 * (1.0 + t) + 0.5 * x * (1.0 - t * t) * _GELU_K * (1.0 + 3.0 * 0.044715 * x * x)


def _expm1(x):
    series = x * (1.0 + x * (0.5 + x * (1.0 / 6.0 + x * (1.0 / 24.0 + x * (1.0 / 120.0)))))
    return jnp.where(jnp.abs(x) < 0.1, series, jnp.exp(x) - 1.0)


def _my_pos():
    return lax.axis_index("x"), lax.axis_index("y"), lax.axis_index("c")


_DIMS = {"nn": (((1,), (0,)), ((), ())), "nt": (((1,), (1,)), ((), ())), "tn": (((0,), (0,)), ((), ()))}


def _mm(name, a, b, *, mode, grid, a_blk, a_map, b_blk, b_map, o_blk, o_map, out_shape, out_dtype, deps=()):
    nk = grid[-1]
    nax = len(grid)
    acc_shape = tuple(d for d in o_blk if d is not None)

    def body(a_ref, b_ref, *rest):
        o_ref, acc_ref = rest[-2:]
        k = pl.program_id(nax - 1)

        @pl.when(k == 0)
        def _():
            acc_ref[...] = jnp.zeros_like(acc_ref)

        acc_ref[...] += lax.dot_general(a_ref[...].astype(BF16), b_ref[...].astype(BF16), _DIMS[mode],
                                        preferred_element_type=F32)

        @pl.when(k == nk - 1)
        def _():
            o_ref[...] = acc_ref[...].astype(o_ref.dtype)

    return pl.pallas_call(
        body, name=name, grid=grid,
        in_specs=[pl.BlockSpec(a_blk, a_map), pl.BlockSpec(b_blk, b_map)]
        + [pl.BlockSpec(memory_space=pl.ANY)] * len(deps),
        out_specs=pl.BlockSpec(o_blk, o_map),
        out_shape=jax.ShapeDtypeStruct(out_shape, out_dtype),
        scratch_shapes=[pltpu.VMEM(acc_shape, F32)],
        compiler_params=_cparams(("parallel",) * (nax - 1) + ("arbitrary",)),
    )(a, b, *deps)


def _tiles(n):
    return _pick(n, (768, 512, 384, 256, 128, 64, 32, 16))


def _ktile(n):
    return _pick(n, (512, 256, 128))


def mm_cols(name, a, wg, out_dtype, flat, deps=()):
    T, K = a.shape
    Nq = wg.shape[-1]
    tm, tk = _tiles(T), _ktile(K)
    if flat:
        o_blk, o_map, o_shape = (tm, Nq), (lambda i, j, k: (i, j)), (T, N_CHIP * Nq)
    else:
        o_blk, o_map, o_shape = (None, tm, Nq), (lambda i, j, k: (j, i, 0)), (N_CHIP, T, Nq)
    return _mm(name, a, wg, mode="nn", grid=(T // tm, N_CHIP, K // tk),
               a_blk=(tm, tk), a_map=lambda i, j, k: (i, k),
               b_blk=(None, tk, Nq), b_map=lambda i, j, k: (j, k, 0),
               o_blk=o_blk, o_map=o_map, out_shape=o_shape, out_dtype=out_dtype, deps=deps)


def mm_cols_t(name, d, wg, flat):
    K, Nq = wg.shape[-2:]
    T = d.shape[-2]
    tm, tn = _tiles(T), _ktile(K)
    if flat:
        a_blk, a_map = (tm, Nq), (lambda i, j, k: (i, k))
    else:
        a_blk, a_map = (None, tm, Nq), (lambda i, j, k: (k, i, 0))
    return _mm(name, d, wg, mode="nt", grid=(T // tm, K // tn, N_CHIP),
               a_blk=a_blk, a_map=a_map,
               b_blk=(None, tn, Nq), b_map=lambda i, j, k: (k, j, 0),
               o_blk=(tm, tn), o_map=lambda i, j, k: (i, j), out_shape=(T, K), out_dtype=F32)


def mm_cols_grad(name, a, d, flat):
    T, K = a.shape
    Nq = d.shape[-1] // N_CHIP if flat else d.shape[-1]
    tt, br = _tiles(T), _ktile(K)
    if flat:
        b_blk, b_map = (tt, Nq), (lambda j, r, t: (t, j))
    else:
        b_blk, b_map = (None, tt, Nq), (lambda j, r, t: (j, t, 0))
    return _mm(name, a, d, mode="tn", grid=(N_CHIP, K // br, T // tt),
               a_blk=(tt, br), a_map=lambda j, r, t: (t, r),
               b_blk=b_blk, b_map=b_map,
               o_blk=(None, br, Nq), o_map=lambda j, r, t: (j, r, 0),
               out_shape=(N_CHIP, K, Nq), out_dtype=BF16)


def mm_rows(name, a, wg, deps=()):
    G, T, Kg = a.shape
    N = wg.shape[-1]
    tm, tn = _tiles(T), _pick(N, (1024, 512, 256, 128))
    return _mm(name, a, wg, mode="nn", grid=(T // tm, N // tn, G),
               a_blk=(None, tm, Kg), a_map=lambda i, j, k: (k, i, 0),
               b_blk=(None, Kg, tn), b_map=lambda i, j, k: (k, 0, j),
               o_blk=(tm, tn), o_map=lambda i, j, k: (i, j), out_shape=(T, N), out_dtype=F32, deps=deps)


def mm_rows_t(name, d, wg):
    T, N = d.shape
    G, Kg = wg.shape[0], wg.shape[1]
    tm, tk = _tiles(T), _ktile(N)
    return _mm(name, d, wg, mode="nt", grid=(T // tm, G, N // tk),
               a_blk=(tm, tk), a_map=lambda i, j, k: (i, k),
               b_blk=(None, Kg, tk), b_map=lambda i, j, k: (j, 0, k),
               o_blk=(None, tm, Kg), o_map=lambda i, j, k: (j, i, 0), out_shape=(G, T, Kg), out_dtype=BF16)


def mm_rows_grad(name, a, d):
    G, T, Kg = a.shape
    N = d.shape[-1]
    tt, tn = _tiles(T), _ktile(N)
    return _mm(name, a, d, mode="tn", grid=(G, N // tn, T // tt),
               a_blk=(None, tt, Kg), a_map=lambda g, j, t: (g, t, 0),
               b_blk=(tt, tn), b_map=lambda g, j, t: (t, j),
               o_blk=(None, Kg, tn), o_map=lambda g, j, t: (g, 0, j), out_shape=(G, Kg, N), out_dtype=BF16)


def mm_plain(name, a, w, mode, out_dtype, deps=()):
    T = a.shape[0]
    K, N = w.shape[-2:]
    tm = _tiles(T)
    if mode == "nn":
        tn, tk = _ktile(N), _ktile(K)
        return _mm(name, a, w, mode="nn", grid=(T // tm, N // tn, K // tk),
                   a_blk=(tm, tk), a_map=lambda i, j, k: (i, k),
                   b_blk=(tk, tn), b_map=lambda i, j, k: (k, j),
                   o_blk=(tm, tn), o_map=lambda i, j, k: (i, j), out_shape=(T, N), out_dtype=out_dtype, deps=deps)
    tn, tk = _ktile(K), _ktile(N)
    return _mm(name, a, w, mode="nt", grid=(T // tm, K // tn, N // tk),
               a_blk=(tm, tk), a_map=lambda i, j, k: (i, k),
               b_blk=(tn, tk), b_map=lambda i, j, k: (j, k),
               o_blk=(tm, tn), o_map=lambda i, j, k: (i, j), out_shape=(T, K), out_dtype=out_dtype)


def mm_plain_grad(name, a, d):
    T, K = a.shape
    N = d.shape[-1]
    tt, br, tn = _tiles(T), _ktile(K), _ktile(N)
    return _mm(name, a, d, mode="tn", grid=(K // br, N // tn, T // tt),
               a_blk=(tt, br), a_map=lambda r, j, t: (t, r),
               b_blk=(tt, tn), b_map=lambda r, j, t: (t, j),
               o_blk=(br, tn), o_map=lambda r, j, t: (r, j), out_shape=(K, N), out_dtype=BF16)


def mm_branch(name, y, wbg, br, deps=()):
    T, RW = y.shape
    Dq = wbg.shape[-1]
    tm = _tiles(T)
    return _mm(name, y, wbg, mode="nn", grid=(T // tm, N_CHIP, 1),
               a_blk=(tm, RW), a_map=lambda i, j, k: (i, 0),
               b_blk=(None, None, RW, Dq), b_map=lambda i, j, k: (j, br, 0, 0),
               o_blk=(tm, Dq), o_map=lambda i, j, k: (i, j), out_shape=(T, N_CHIP * Dq), out_dtype=BF16, deps=deps)


def mm_branch_t(name, d, wbg, br):
    T = d.shape[0]
    RW, Dq = wbg.shape[-2:]
    tm = _tiles(T)
    return _mm(name, d, wbg, mode="nt", grid=(T // tm, 1, N_CHIP),
               a_blk=(tm, Dq), a_map=lambda i, j, k: (i, k),
               b_blk=(None, None, RW, Dq), b_map=lambda i, j, k: (k, br, 0, 0),
               o_blk=(tm, RW), o_map=lambda i, j, k: (i, 0), out_shape=(T, RW), out_dtype=F32)


def mm_branch_grad(name, y, d):
    T, RW = y.shape
    Dq = d.shape[-1] // N_CHIP
    tt = _tiles(T)
    return _mm(name, y, d, mode="tn", grid=(N_CHIP, 1, T // tt),
               a_blk=(tt, RW), a_map=lambda j, r, t: (t, 0),
               b_blk=(tt, Dq), b_map=lambda j, r, t: (t, j),
               o_blk=(None, RW, Dq), o_map=lambda j, r, t: (j, 0, 0), out_shape=(N_CHIP, RW, Dq), out_dtype=BF16)


def allgather8(name, x_shard):
    m_per, n = x_shard.shape

    def body(x_ref, out_ref, send_sems, recv_sems, local_sem):
        x, y, c = _my_pos()
        me, sibling = (x, y, c), (x, y, 1 - c)
        chips = [(1 - x, y), (x, 1 - y), (1 - x, 1 - y)]

        def rows(px, py, pc):
            return out_ref.at[pl.ds((4 * px + 2 * py + pc) * m_per, m_per), :]

        def copy(k, block, to, src=None):
            return pltpu.make_async_remote_copy(
                src_ref=rows(*block) if src is None else src, dst_ref=rows(*block),
                send_sem=send_sems.at[k], recv_sem=recv_sems.at[k], device_id=to, device_id_type=MESH)

        mine = pltpu.make_async_copy(x_ref, rows(*me), local_sem)
        mine.start()
        first = [copy(0, me, sibling, src=x_ref)]
        first += [copy(1 + j, me, (*chip, c), src=x_ref) for j, chip in enumerate(chips)]
        for cp in first:
            cp.start()
        passed = [copy(4 + j, (*chip, c), sibling) for j, chip in enumerate(chips)]
        for j, chip in enumerate(chips):
            copy(1 + j, (*chip, c), me).wait_recv()
            passed[j].start()
        copy(0, sibling, me).wait_recv()
        for j, chip in enumerate(chips):
            copy(4 + j, (*chip, 1 - c), me).wait_recv()
        for cp in first + passed:
            cp.wait_send()
        mine.wait()

    return pl.pallas_call(
        body, name=name,
        out_shape=jax.ShapeDtypeStruct((N_DEV * m_per, n), x_shard.dtype),
        in_specs=[pl.BlockSpec(memory_space=pltpu.VMEM)],
        out_specs=pl.BlockSpec(memory_space=pltpu.VMEM),
        scratch_shapes=[pltpu.SemaphoreType.DMA((7,)), pltpu.SemaphoreType.DMA((7,)), pltpu.SemaphoreType.DMA],
        compiler_params=pltpu.CompilerParams(vmem_limit_bytes=VMEM_LIMIT_BYTES),
    )(x_shard)


def _other_chips(x, y):
    return [(1 - x, y), (x, 1 - y), (1 - x, 1 - y)]


_HBM = pl.BlockSpec(memory_space=pltpu.HBM)
_SEM = pl.BlockSpec(memory_space=pltpu.SEMAPHORE)
_ANY = pl.BlockSpec(memory_space=pl.ANY)
_EFFECT = pltpu.SideEffectType.DATAFLOW_SIDE_EFFECTING
TOKEN_SHAPE = (8, 128)


def _in_hbm(a):
    return pltpu.with_memory_space_constraint(a, pltpu.HBM)


def exchange_start(name, bufs, plan):
    n = len(bufs)
    n_copies = len(plan([None] * n, 0, 0, 0, dry=True))

    def body(*refs):
        ins = refs[:n]
        send_sems, recv_sems = refs[n], refs[n + 1]
        token = refs[-1]
        x, y, c = _my_pos()
        for i, (src, dst, to) in enumerate(plan(ins, x, y, c)):
            pltpu.make_async_remote_copy(src_ref=src, dst_ref=dst, send_sem=send_sems.at[i], recv_sem=recv_sems.at[i],
                                         device_id=to, device_id_type=MESH).start()
        token[...] = jnp.zeros_like(token)

    outs = pl.pallas_call(
        body, name=name,
        out_shape=(pltpu.SemaphoreType.DMA((n_copies,)), pltpu.SemaphoreType.DMA((n_copies,)),
                   *[pltpu.HBM(b.shape, b.dtype) for b in bufs], jax.ShapeDtypeStruct(TOKEN_SHAPE, F32)),
        in_specs=[_HBM] * n,
        out_specs=(_SEM, _SEM, *[_HBM] * n, pl.BlockSpec(memory_space=pltpu.VMEM)),
        input_output_aliases={i: 2 + i for i in range(n)},
        compiler_params=pltpu.CompilerParams(has_side_effects=_EFFECT),
    )(*[_in_hbm(b) for b in bufs])
    return outs[0], outs[1], list(outs[2:2 + n]), outs[-1]


def exchange_wait(name, send_sems, recv_sems, bufs, plan, after):
    n = len(bufs)

    def body(*refs):
        ins = refs[:n]
        send_sems, recv_sems = refs[n], refs[n + 1]
        x, y, c = _my_pos()
        for i, (src, dst, to) in enumerate(plan(ins, x, y, c, arriving=True)):
            cp = pltpu.make_async_remote_copy(src_ref=src, dst_ref=dst, send_sem=send_sems.at[i],
                                              recv_sem=recv_sems.at[i], device_id=to, device_id_type=MESH)
            cp.wait_send()
            cp.wait_recv()

    outs = pl.pallas_call(
        body, name=name,
        out_shape=tuple(pltpu.HBM(b.shape, b.dtype) for b in bufs),
        in_specs=[_HBM] * n + [_SEM, _SEM, _ANY],
        out_specs=tuple([_HBM] * n),
        input_output_aliases={i: i for i in range(n)},
        compiler_params=pltpu.CompilerParams(has_side_effects=_EFFECT),
    )(*bufs, send_sems, recv_sems, after)
    return list(outs)


def _gather_plan(refs, x, y, c, dry=False, arriving=False):
    if dry:
        return [None] * 3
    (land,) = refs
    j_me = 2 * x + y
    return [(land.at[j_me], land.at[(2 * px + py) if arriving else j_me], (px, py, c)) for px, py in _other_chips(x, y)]


def _sibling_plan(refs, x, y, c, dry=False, arriving=False):
    if dry:
        return [None]
    src, land = refs
    return [(src, land, (x, y, 1 - c))]


def _scatter_plan(n_pieces):
    def plan(refs, x, y, c, dry=False, arriving=False):
        if dry:
            return [None] * (3 * n_pieces)
        grads, lands = refs[:n_pieces], refs[n_pieces:]
        return [(grads[p].at[2 * px + py], lands[p].at[k], (px, py, c))
                for p in range(n_pieces) for k, (px, py) in enumerate(_other_chips(x, y))]
    return plan


def _view2d(a):
    return a.reshape(-1, a.shape[-1])


def _row_tile(rows, width, itemsize=4, budget=1 << 20):
    t = 8
    for cand in (1024, 512, 256, 128, 64, 32, 16, 8):
        if rows % cand == 0 and cand * width * itemsize <= budget:
            t = cand
            break
    return t if rows % t == 0 else rows


def cast_into_slot(name, w, l, j_idx, deps=()):
    w3 = w.reshape(w.shape[0], -1, w.shape[-1])
    _, R, W = w3.shape
    tr = _row_tile(R, W)

    def body(j_ref, a_ref, *rest):
        o_ref = rest[-1]
        o_ref[...] = a_ref[...].astype(BF16)

    return pl.pallas_call(
        body, name=name,
        grid_spec=pltpu.PrefetchScalarGridSpec(
            num_scalar_prefetch=1, grid=(R // tr,),
            in_specs=[pl.BlockSpec((None, tr, W), lambda i, j: (l, i, 0))] + [pl.BlockSpec(memory_space=pl.ANY)] * len(deps),
            out_specs=pl.BlockSpec((None, tr, W), lambda i, j: (j[0], i, 0))),
        out_shape=jax.ShapeDtypeStruct((N_CHIP, R, W), BF16), compiler_params=_cparams(("parallel",)),
    )(j_idx, w3, *deps)


def sum_parts(name, groups, j_idx):
    n = len(groups)
    _, R, W = groups[0][0].shape
    tr = _row_tile(R, W)

    def body(j_ref, *refs):
        o_ref = refs[-1]
        g = pl.program_id(0)
        for q in range(n):
            @pl.when(g == q)
            def _(q=q):
                own, got = refs[2 * q], refs[2 * q + 1]
                o_ref[...] = ((own[...].astype(F32) + got[0].astype(F32)) + got[1].astype(F32)) + got[2].astype(F32)

    in_specs = []
    for q in range(n):
        in_specs.append(pl.BlockSpec((None, tr, W), lambda g, i, j, q=q: (j[0], jnp.where(g == q, i, 0), 0)))
        in_specs.append(pl.BlockSpec((3, tr, W), lambda g, i, j, q=q: (0, jnp.where(g == q, i, 0), 0)))
    return pl.pallas_call(
        body, name=name,
        grid_spec=pltpu.PrefetchScalarGridSpec(
            num_scalar_prefetch=1, grid=(n, R // tr), in_specs=in_specs,
            out_specs=pl.BlockSpec((None, tr, W), lambda g, i, j: (g, i, 0))),
        out_shape=jax.ShapeDtypeStruct((n, R, W), F32), compiler_params=_cparams(("arbitrary", "arbitrary")),
    )(j_idx, *[a for pair in groups for a in pair])


def adamw(name, w, m, v, g_parts):
    shape = w.shape
    w2, m2, v2 = _view2d(w), _view2d(m), _view2d(v)
    gs = [_view2d(g) for g in g_parts]
    R, W = w2.shape
    tr = _row_tile(R, W, budget=1 << 19)
    ng = len(gs)
    bc1 = 1.0 - ADAM_B1 ** ADAM_STEP
    bc2 = 1.0 - ADAM_B2 ** ADAM_STEP

    def body(*refs):
        w_ref, m_ref, v_ref = refs[:3]
        g_refs = refs[3:3 + ng]
        go_ref, d_ref, mo_ref, vo_ref = refs[3 + ng:]
        g = g_refs[0][...]
        for r in g_refs[1:]:
            g = g + r[...]
        mn = ADAM_B1 * m_ref[...] + (1.0 - ADAM_B1) * g
        vn = ADAM_B2 * v_ref[...] + (1.0 - ADAM_B2) * (g * g)
        m_hat = mn / bc1
        v_hat = vn / bc2
        go_ref[...] = g
        d_ref[...] = -ADAM_LR * (m_hat / (jnp.sqrt(v_hat) + ADAM_EPS) + ADAM_WD * w_ref[...])
        mo_ref[...] = mn
        vo_ref[...] = vn

    spec = pl.BlockSpec((tr, W), lambda i: (i, 0))
    outs = pl.pallas_call(
        body, name=name, grid=(R // tr,),
        in_specs=[spec] * (3 + ng), out_specs=[spec] * 4,
        out_shape=[jax.ShapeDtypeStruct((R, W), F32)] * 4, compiler_params=_cparams(("parallel",)),
    )(w2, m2, v2, *gs)
    return tuple(o.reshape(shape) for o in outs)


class Dims:
    pass


def _sel(dm):
    return (pl.program_id(0) >= dm.nctx).astype(jnp.int32)


def norm_mod(name, h, gn, modtab, s, dm, deps=()):
    T, D = h.shape
    tm = dm.tme

    def body(h_ref, g_ref, m_ref, *rest):
        u_ref = rest[-1]
        sel = _sel(dm)
        x = h_ref[...]
        r = lax.rsqrt(jnp.mean(x * x, axis=-1, keepdims=True) + EPS)
        ng = x * r * g_ref[...]
        u_ref[...] = (ng * (1.0 + m_ref[sel, 3 * s + 1]) + m_ref[sel, 3 * s]).astype(u_ref.dtype)

    return pl.pallas_call(
        body, name=name, grid=(T // tm,),
        in_specs=[pl.BlockSpec((tm, D), lambda i: (i, 0)), pl.BlockSpec((1, D), lambda i: (0, 0)),
                  pl.BlockSpec((2, N_MOD, 1, D), lambda i: (0, 0, 0, 0))] + [_ANY] * len(deps),
        out_specs=pl.BlockSpec((tm, D), lambda i: (i, 0)),
        out_shape=jax.ShapeDtypeStruct((T, D), BF16), compiler_params=_cparams(("parallel",)),
    )(h, gn, modtab, *deps)


def norm_mod_bwd(name, du, h, gn, modtab, s, dh_in, dm):
    T, D = h.shape
    tm = dm.tme

    def body(du_ref, h_ref, g_ref, m_ref, dhi_ref, dh_ref, dmod_ref, dg_ref):
        i = pl.program_id(0)
        sel = _sel(dm)

        @pl.when(i == 0)
        def _():
            dmod_ref[...] = jnp.zeros_like(dmod_ref)
            dg_ref[...] = jnp.zeros_like(dg_ref)

        x = h_ref[...]
        r = lax.rsqrt(jnp.mean(x * x, axis=-1, keepdims=True) + EPS)
        n = x * r
        g = g_ref[...]
        du = du_ref[...]
        dmod_ref[sel, 0] += jnp.sum(du, axis=0, keepdims=True)
        dmod_ref[sel, 1] += jnp.sum(du * (n * g), axis=0, keepdims=True)
        dng = du * (1.0 + m_ref[sel, 3 * s + 1])
        dg_ref[...] += jnp.sum(dng * n, axis=0, keepdims=True)
        dn = dng * g
        dh_ref[...] = dhi_ref[...] + r * (dn - n * jnp.mean(dn * n, axis=-1, keepdims=True))

    row = pl.BlockSpec((tm, D), lambda i: (i, 0))
    return pl.pallas_call(
        body, name=name, grid=(T // tm,),
        in_specs=[row, row, pl.BlockSpec((1, D), lambda i: (0, 0)),
                  pl.BlockSpec((2, N_MOD, 1, D), lambda i: (0, 0, 0, 0)), row],
        out_specs=[row, pl.BlockSpec((2, 2, 1, D), lambda i: (0, 0, 0, 0)), pl.BlockSpec((1, D), lambda i: (0, 0))],
        out_shape=[jax.ShapeDtypeStruct((T, D), F32), jax.ShapeDtypeStruct((2, 2, 1, D), F32),
                   jax.ShapeDtypeStruct((1, D), F32)],
        compiler_params=_cparams(("arbitrary",)),
    )(du, h, gn, modtab, dh_in)


def resid(name, h, f, modtab, s, coef, dm):
    T, D = h.shape
    tm = dm.tme

    def body(h_ref, f_ref, m_ref, o_ref):
        o_ref[...] = h_ref[...] + (coef * m_ref[_sel(dm), 3 * s + 2]) * f_ref[...]

    row = pl.BlockSpec((tm, D), lambda i: (i, 0))
    return pl.pallas_call(
        body, name=name, grid=(T // tm,),
        in_specs=[row, row, pl.BlockSpec((2, N_MOD, 1, D), lambda i: (0, 0, 0, 0))], out_specs=row,
        out_shape=jax.ShapeDtypeStruct((T, D), F32), compiler_params=_cparams(("parallel",)),
    )(h, f, modtab)


def resid_bwd(name, dh, f, modtab, s, coef, dm, deps=()):
    T, D = dh.shape
    tm = dm.tme

    def body(dh_ref, f_ref, m_ref, *rest):
        df_ref, dg_ref = rest[-2:]
        sel = _sel(dm)

        @pl.when(pl.program_id(0) == 0)
        def _():
            dg_ref[...] = jnp.zeros_like(dg_ref)

        d = coef * dh_ref[...]
        df_ref[...] = (d * m_ref[sel, 3 * s + 2]).astype(df_ref.dtype)
        dg_ref[sel, 0] += jnp.sum(d * f_ref[...], axis=0, keepdims=True)

    row = pl.BlockSpec((tm, D), lambda i: (i, 0))
    return pl.pallas_call(
        body, name=name, grid=(T // tm,),
        in_specs=[row, row, pl.BlockSpec((2, N_MOD, 1, D), lambda i: (0, 0, 0, 0))] + [_ANY] * len(deps),
        out_specs=[row, pl.BlockSpec((2, 1, 1, D), lambda i: (0, 0, 0, 0))],
        out_shape=[jax.ShapeDtypeStruct((T, D), BF16), jax.ShapeDtypeStruct((2, 1, 1, D), F32)],
        compiler_params=_cparams(("arbitrary",)),
    )(dh, f, modtab, *deps)


def swiglu(name, gu, dm):
    _, T, Nq = gu.shape
    tm = dm.tme
    gu4 = gu.reshape(2, 2, T, Nq)

    def body(gu_ref, o_ref):
        g = gu_ref[0].astype(F32)
        o_ref[...] = (_silu(g) * gu_ref[1].astype(F32)).astype(o_ref.dtype)

    return pl.pallas_call(
        body, name=name, grid=(2, T // tm),
        in_specs=[pl.BlockSpec((2, None, tm, Nq), lambda k, i: (0, k, i, 0))],
        out_specs=pl.BlockSpec((None, tm, Nq), lambda k, i: (k, i, 0)),
        out_shape=jax.ShapeDtypeStruct((2, T, Nq), BF16), compiler_params=_cparams(("parallel", "parallel")),
    )(gu4)


def swiglu_bwd(name, dact, gu, dm):
    _, T, Nq = gu.shape
    tm = dm.tme
    gu4 = gu.reshape(2, 2, T, Nq)

    def body(da_ref, gu_ref, o_ref):
        g = gu_ref[0].astype(F32)
        da = da_ref[...].astype(F32)
        o_ref[0] = (da * gu_ref[1].astype(F32) * _dsilu(g)).astype(o_ref.dtype)
        o_ref[1] = (da * _silu(g)).astype(o_ref.dtype)

    out = pl.pallas_call(
        body, name=name, grid=(2, T // tm),
        in_specs=[pl.BlockSpec((None, tm, Nq), lambda k, i: (k, i, 0)),
                  pl.BlockSpec((2, None, tm, Nq), lambda k, i: (0, k, i, 0))],
        out_specs=pl.BlockSpec((2, None, tm, Nq), lambda k, i: (0, k, i, 0)),
        out_shape=jax.ShapeDtypeStruct((2, 2, T, Nq), BF16), compiler_params=_cparams(("parallel", "parallel")),
    )(dact, gu4)
    return out.reshape(4, T, Nq)


def _halo_specs(dm, width, col):
    tm = dm.tme
    per = tm // HALO
    last = dm.T // HALO - 1
    return [pl.BlockSpec((tm, width), lambda i: (i, col)),
            pl.BlockSpec((HALO, width), lambda i: (jnp.maximum(i * per - 1, 0), col)),
            pl.BlockSpec((HALO, width), lambda i: (jnp.minimum((i + 1) * per, last), col))]


def _segment_edges(dm, i):
    first = jnp.logical_or(i == 0, i == dm.nctx)
    last = jnp.logical_or(i == dm.nctx - 1, i == dm.nt - 1)
    return first, last


def _extend(main, prev, nxt, first, last):
    return jnp.concatenate([jnp.where(first, 0.0, prev), main, jnp.where(last, 0.0, nxt)], axis=0)


def _shift(ext, o, tm):
    n = ext.shape[0]
    rolled = ext if o == 0 else pltpu.roll(ext, (-o) % n, 0)
    return rolled[HALO:HALO + tm]


def _load_ext(refs, first, last):
    main, prev, nxt = refs
    return _extend(main[...].astype(F32), prev[...].astype(F32), nxt[...].astype(F32), first, last)


def rnn_conv(name, z, w, b, dm):
    T, RW, tm = dm.T, dm.RW, dm.tme

    def body(main, prev, nxt, w_ref, b_ref, o_ref):
        first, last = _segment_edges(dm, pl.program_id(0))
        ext = _load_ext((main, prev, nxt), first, last)
        acc = jnp.zeros((tm, RW), F32) + b_ref[...]
        for k in range(4):
            acc = acc + w_ref[k] * _shift(ext, k - 2, tm)
        o_ref[...] = acc

    return pl.pallas_call(
        body, name=name, grid=(dm.nt,),
        in_specs=_halo_specs(dm, RW, 0) + [pl.BlockSpec((4, 1, RW), lambda i: (0, 0, 0)),
                                           pl.BlockSpec((1, RW), lambda i: (0, 0))],
        out_specs=pl.BlockSpec((tm, RW), lambda i: (i, 0)),
        out_shape=jax.ShapeDtypeStruct((T, RW), F32), compiler_params=_cparams(("parallel",)),
    )(z, z, z, w, b)


def _blockdiag(x, w_ref):
    nb = w_ref.shape[0]
    outs = [jnp.dot(x[:, n * RNN_BLOCK:(n + 1) * RNN_BLOCK], w_ref[n].astype(BF16), preferred_element_type=F32)
            for n in range(nb)]
    return jnp.concatenate(outs, axis=-1)


def _lru_gates(xa, wa_ref, ba_ref, wx_ref, bx_ref, lam_ref):
    xb = xa.astype(BF16)
    r = jax.nn.sigmoid(_blockdiag(xb, wa_ref) + ba_ref[...])
    ig = jax.nn.sigmoid(_blockdiag(xb, wx_ref) + bx_ref[...])
    nl = -lam_ref[...]
    sp = jnp.maximum(nl, 0.0) + jnp.log(1.0 + jnp.exp(-jnp.abs(nl)))
    log_a = -LRU_C * r * sp
    a = jnp.exp(log_a)
    m = jnp.sqrt(-_expm1(2.0 * log_a))
    return r, ig, sp, a, m


def _lru_specs(l, d, nb, RW):
    wspec = pl.BlockSpec((None, None, nb, RNN_BLOCK, RNN_BLOCK), lambda i: (l, d, 0, 0, 0))
    vspec = pl.BlockSpec((None, None, 1, RW), lambda i: (l, d, 0, 0))
    return [wspec, vspec, wspec, vspec, vspec]


def lru_gates(name, xa, lw, l, d, dm):
    T, RW, tm = dm.T, dm.RW, dm.tme

    def body(xa_ref, wa_ref, ba_ref, wx_ref, bx_ref, lam_ref, a_ref, u_ref):
        xa_v = xa_ref[...]
        r, ig, sp, a, m = _lru_gates(xa_v, wa_ref, ba_ref, wx_ref, bx_ref, lam_ref)
        a_ref[...] = a
        u_ref[...] = m * (ig * xa_v)

    row = pl.BlockSpec((tm, RW), lambda i: (i, 0))
    return pl.pallas_call(
        body, name=name, grid=(dm.nt,),
        in_specs=[row] + _lru_specs(l, d, dm.NB, RW), out_specs=[row, row],
        out_shape=[jax.ShapeDtypeStruct((T, RW), F32)] * 2, compiler_params=_cparams(("parallel",)),
    )(xa, lw["w_a"], lw["b_a"], lw["w_x"], lw["b_x"], lw["lam"])


def lru_gates_bwd(name, xa, lw, l, d, du, dloga, dm):
    T, RW, tm, NB = dm.T, dm.RW, dm.tme, dm.NB

    def body(xa_ref, wa_ref, ba_ref, wx_ref, bx_ref, lam_ref, du_ref, dla_ref,
             dxa_ref, dwa_ref, dba_ref, dwx_ref, dbx_ref, dlam_ref):
        @pl.when(pl.program_id(0) == 0)
        def _():
            for ref in (dwa_ref, dba_ref, dwx_ref, dbx_ref, dlam_ref):
                ref[...] = jnp.zeros_like(ref)

        xa_v = xa_ref[...]
        r, ig, sp, a, m = _lru_gates(xa_v, wa_ref, ba_ref, wx_ref, bx_ref, lam_ref)
        duu = du_ref[...]
        dm_ = duu * (ig * xa_v)
        dig = duu * m * xa_v
        dxa = duu * m * ig
        dla = dla_ref[...] - dm_ * (a * a) / m
        dr = dla * (-LRU_C * sp)
        dsp = jnp.sum(dla * (-LRU_C * r), axis=0, keepdims=True)
        dlam_ref[...] += dsp * (-jax.nn.sigmoid(-lam_ref[...]))
        dpa = dr * r * (1.0 - r)
        dpx = dig * ig * (1.0 - ig)
        dba_ref[...] += jnp.sum(dpa, axis=0, keepdims=True)
        dbx_ref[...] += jnp.sum(dpx, axis=0, keepdims=True)
        xb, dpab, dpxb = xa_v.astype(BF16), dpa.astype(BF16), dpx.astype(BF16)
        back = []
        for n in range(NB):
            sl = slice(n * RNN_BLOCK, (n + 1) * RNN_BLOCK)
            dwa_ref[n] += lax.dot_general(xb[:, sl], dpab[:, sl], _DIMS["tn"], preferred_element_type=F32)
            dwx_ref[n] += lax.dot_general(xb[:, sl], dpxb[:, sl], _DIMS["tn"], preferred_element_type=F32)
            back.append(lax.dot_general(dpab[:, sl], wa_ref[n].astype(BF16), _DIMS["nt"], preferred_element_type=F32)
                        + lax.dot_general(dpxb[:, sl], wx_ref[n].astype(BF16), _DIMS["nt"], preferred_element_type=F32))
        dxa_ref[...] = dxa + jnp.concatenate(back, axis=-1)

    row = pl.BlockSpec((tm, RW), lambda i: (i, 0))
    wacc = pl.BlockSpec((NB, RNN_BLOCK, RNN_BLOCK), lambda i: (0, 0, 0))
    vacc = pl.BlockSpec((1, RW), lambda i: (0, 0))
    wshape = jax.ShapeDtypeStruct((NB, RNN_BLOCK, RNN_BLOCK), F32)
    vshape = jax.ShapeDtypeStruct((1, RW), F32)
    return pl.pallas_call(
        body, name=name, grid=(dm.nt,),
        in_specs=[row] + _lru_specs(l, d, NB, RW) + [row, row],
        out_specs=[row, wacc, vacc, wacc, vacc, vacc],
        out_shape=[jax.ShapeDtypeStruct((T, RW), F32), wshape, vshape, wshape, vshape, vshape],
        compiler_params=_cparams(("arbitrary",)),
    )(xa, lw["w_a"], lw["b_a"], lw["w_x"], lw["b_x"], lw["lam"], du, dloga)


def _chunk_order(dm, ctx_first, descending):
    nch, nctx = dm.nt, dm.nctx
    nlat = nch - nctx

    def order(s):
        if ctx_first and not descending:
            return s
        if not ctx_first and descending:
            return nch - 1 - s
        if ctx_first:
            return jnp.where(s < nctx, nctx - 1 - s, nch - 1 - (s - nctx))
        return jnp.where(s < nlat, nctx + s, s - nlat)

    return order


def _tile_scan(a, b, carry, descending):
    row = lax.broadcasted_iota(jnp.int32, a.shape, 0)
    for s in (1, 2, 4):
        sh = (HALO - s) if descending else s
        keep = (row < HALO - s) if descending else (row >= s)
        ap = pltpu.roll(a, sh, 0)
        bp = pltpu.roll(b, sh, 0)
        b = jnp.where(keep, b + a * bp, b)
        a = jnp.where(keep, a * ap, a)
    h = b + a * carry
    edge = 0 if descending else HALO - 1
    new_carry = jnp.sum(jnp.where(row == edge, h, 0.0), axis=0, keepdims=True)
    return h, new_carry


def lru_scan(name, a, u, ctx_first, descending, dm):
    T, RW, ch = dm.T, dm.RW, dm.tme
    order = _chunk_order(dm, ctx_first, descending)
    ngrp = ch // HALO

    def body(a_ref, u_ref, h_ref, carry_ref):
        @pl.when(pl.program_id(0) == 0)
        def _():
            carry_ref[...] = jnp.zeros_like(carry_ref)

        def step(g, carry):
            g = (ngrp - 1 - g) if descending else g
            rows = pl.ds(pl.multiple_of(g * HALO, HALO), HALO)
            h, carry = _tile_scan(a_ref[rows, :], u_ref[rows, :], carry, descending)
            h_ref[rows, :] = h
            return carry

        carry_ref[...] = lax.fori_loop(0, ngrp, step, carry_ref[...])

    row = pl.BlockSpec((ch, RW), lambda s: (order(s), 0))
    return pl.pallas_call(
        body, name=name, grid=(dm.nt,),
        in_specs=[row, row], out_specs=row,
        out_shape=jax.ShapeDtypeStruct((T, RW), F32),
        scratch_shapes=[pltpu.VMEM((1, RW), F32)], compiler_params=_cparams(("arbitrary",)),
    )(a, u)


def lru_scan_bwd(name, a, u, h, dh, ctx_first, descending, dm):
    T, RW, ch = dm.T, dm.RW, dm.tme
    order = _chunk_order(dm, ctx_first, descending)
    ngrp = ch // HALO

    def body(a_ref, u_ref, h_ref, dh_ref, lam_ref, dla_ref, carry_ref):
        @pl.when(pl.program_id(0) == 0)
        def _():
            carry_ref[...] = jnp.zeros_like(carry_ref)

        def step(g, carry):
            g = (ngrp - 1 - g) if descending else g
            rows = pl.ds(pl.multiple_of(g * HALO, HALO), HALO)
            a_v, dh_v = a_ref[rows, :], dh_ref[rows, :]
            mu, new_carry = _tile_scan(a_v, a_v * dh_v, carry, descending)
            row = lax.broadcasted_iota(jnp.int32, mu.shape, 0)
            if descending:
                nxt = jnp.where(row == HALO - 1, carry, pltpu.roll(mu, HALO - 1, 0))
            else:
                nxt = jnp.where(row == 0, carry, pltpu.roll(mu, 1, 0))
            lam = dh_v + nxt
            lam_ref[rows, :] = lam
            dla_ref[rows, :] = lam * (h_ref[rows, :] - u_ref[rows, :])
            return new_carry

        carry_ref[...] = lax.fori_loop(0, ngrp, step, carry_ref[...])

    row = pl.BlockSpec((ch, RW), lambda s: (order(s), 0))
    return pl.pallas_call(
        body, name=name, grid=(dm.nt,),
        in_specs=[row] * 4, out_specs=[row, row],
        out_shape=[jax.ShapeDtypeStruct((T, RW), F32)] * 2,
        scratch_shapes=[pltpu.VMEM((1, RW), F32)], compiler_params=_cparams(("arbitrary",)),
    )(a, u, h, dh)


def rnn_out(name, hf, hb, z, dm):
    T, RW, tm = dm.T, dm.RW, dm.tme

    def body(hf_ref, hb_ref, rg_ref, o_ref):
        o_ref[...] = ((hf_ref[...] + hb_ref[...]) * _gelu(rg_ref[...])).astype(o_ref.dtype)

    row = pl.BlockSpec((tm, RW), lambda i: (i, 0))
    return pl.pallas_call(
        body, name=name, grid=(dm.nt,),
        in_specs=[row, row, pl.BlockSpec((tm, RW), lambda i: (i, 1))], out_specs=row,
        out_shape=jax.ShapeDtypeStruct((T, RW), BF16), compiler_params=_cparams(("parallel",)),
    )(hf, hb, z)


def rnn_out_bwd(name, dya, hf, hb, z, dm):
    T, RW, tm = dm.T, dm.RW, dm.tme

    def body(d_ref, hf_ref, hb_ref, rg_ref, dh_ref, drg_ref):
        d, rg = d_ref[...], rg_ref[...]
        dh_ref[...] = d * _gelu(rg)
        drg_ref[...] = d * (hf_ref[...] + hb_ref[...]) * _dgelu(rg)

    row = pl.BlockSpec((tm, RW), lambda i: (i, 0))
    return pl.pallas_call(
        body, name=name, grid=(dm.nt,),
        in_specs=[row, row, row, pl.BlockSpec((tm, RW), lambda i: (i, 1))], out_specs=[row, row],
        out_shape=[jax.ShapeDtypeStruct((T, RW), F32)] * 2, compiler_params=_cparams(("parallel",)),
    )(dya, hf, hb, z)


def rnn_conv_bwd(name, dxa_f, dxa_b, drg, z, w, dz, dm):
    T, RW, tm = dm.T, dm.RW, dm.tme

    def body(f0, f1, f2, b0, b1, b2, x0, x1, x2, drg_ref, w_ref, dz_in, dz_ref, dw_ref, db_ref):
        i = pl.program_id(0)

        @pl.when(i == 0)
        def _():
            dw_ref[...] = jnp.zeros_like(dw_ref)
            db_ref[...] = jnp.zeros_like(db_ref)

        first, last = _segment_edges(dm, i)
        dext = _load_ext((f0, f1, f2), first, last) + _load_ext((b0, b1, b2), first, last)
        xext = _load_ext((x0, x1, x2), first, last)
        dmain = dext[HALO:HALO + tm]
        drx = jnp.zeros((tm, RW), F32)
        for k in range(4):
            drx = drx + w_ref[k] * _shift(dext, -(k - 2), tm)
            dw_ref[k] += jnp.sum(dmain * _shift(xext, k - 2, tm), axis=0, keepdims=True)
        db_ref[...] += jnp.sum(dmain, axis=0, keepdims=True)
        dz_ref[:, :RW] = drx.astype(dz_ref.dtype)
        dz_ref[:, RW:] = drg_ref[...].astype(dz_ref.dtype)

    return pl.pallas_call(
        body, name=name, grid=(dm.nt,),
        in_specs=_halo_specs(dm, RW, 0) * 3 + [pl.BlockSpec((tm, RW), lambda i: (i, 0)),
                                               pl.BlockSpec((4, 1, RW), lambda i: (0, 0, 0)),
                                               pl.BlockSpec(memory_space=pl.ANY)],
        out_specs=[pl.BlockSpec((tm, 2 * RW), lambda i: (i, 0)), pl.BlockSpec((4, 1, RW), lambda i: (0, 0, 0)),
                   pl.BlockSpec((1, RW), lambda i: (0, 0))],
        out_shape=[jax.ShapeDtypeStruct(dz.shape, dz.dtype), jax.ShapeDtypeStruct((4, 1, RW), F32),
                   jax.ShapeDtypeStruct((1, RW), F32)],
        input_output_aliases={11: 0}, compiler_params=_cparams(("arbitrary",)),
    )(dxa_f, dxa_f, dxa_f, dxa_b, dxa_b, dxa_b, z, z, z, drg, w, dz)


def short_conv(name, z, w, dm):
    T, RW, tm = dm.T, dm.RW, dm.tme

    def body(sb_ref, g0, g1, g2, x0, x1, x2, w_ref, o_ref):
        first, last = _segment_edges(dm, pl.program_id(0))
        pext = _load_ext((g0, g1, g2), first, last) * _load_ext((x0, x1, x2), first, last)
        cp = jnp.zeros((tm, RW), F32)
        for k in range(3):
            cp = cp + w_ref[k] * _shift(pext, k - 1, tm)
        o_ref[...] = (sb_ref[...] * cp).astype(o_ref.dtype)

    return pl.pallas_call(
        body, name=name, grid=(dm.nt,),
        in_specs=[pl.BlockSpec((tm, RW), lambda i: (i, 2))] + _halo_specs(dm, RW, 3) + _halo_specs(dm, RW, 4)
        + [pl.BlockSpec((3, 1, RW), lambda i: (0, 0, 0))],
        out_specs=pl.BlockSpec((tm, RW), lambda i: (i, 0)),
        out_shape=jax.ShapeDtypeStruct((T, RW), BF16), compiler_params=_cparams(("parallel",)),
    )(z, z, z, z, z, z, z, w)


def short_conv_bwd(name, dyb, z, w, dz, dm):
    T, RW, tm = dm.T, dm.RW, dm.tme

    def spec3(col):
        per = tm // HALO
        last = T // HALO - 1
        return [pl.BlockSpec((tm, RW), lambda i, p: (i, col)),
                pl.BlockSpec((HALO, RW), lambda i, p: (jnp.maximum(i * per - 1, 0), col)),
                pl.BlockSpec((HALO, RW), lambda i, p: (jnp.minimum((i + 1) * per, last), col))]

    def body(d0, d1, d2, s0, s1, s2, g0, g1, g2, x0, x1, x2, w_ref, dz_in, dz_ref, dw_ref):
        i, p = pl.program_id(0), pl.program_id(1)

        @pl.when(jnp.logical_and(i == 0, p == 0))
        def _():
            dw_ref[...] = jnp.zeros_like(dw_ref)

        first, last = _segment_edges(dm, i)
        gext = _load_ext((g0, g1, g2), first, last)
        xext = _load_ext((x0, x1, x2), first, last)
        pext = gext * xext
        dyext = _load_ext((d0, d1, d2), first, last)
        dcext = dyext * _load_ext((s0, s1, s2), first, last)
        dcmain = dcext[HALO:HALO + tm]
        cp = jnp.zeros((tm, RW), F32)
        dp = jnp.zeros((tm, RW), F32)
        for k in range(3):
            pk = _shift(pext, k - 1, tm)
            cp = cp + w_ref[k] * pk
            dp = dp + w_ref[k] * _shift(dcext, -(k - 1), tm)

            @pl.when(p == 0)
            def _(k=k, pk=pk):
                dw_ref[k] += jnp.sum(dcmain * pk, axis=0, keepdims=True)

        dsb = dyext[HALO:HALO + tm] * cp
        dscg = dp * xext[HALO:HALO + tm]
        dsx = dp * gext[HALO:HALO + tm]
        dz_ref[...] = jnp.where(p == 0, dsb, jnp.where(p == 1, dscg, dsx)).astype(dz_ref.dtype)

    return pl.pallas_call(
        body, name=name, grid=(dm.nt, 3),
        in_specs=spec3(0) + spec3(2) + spec3(3) + spec3(4)
        + [pl.BlockSpec((3, 1, RW), lambda i, p: (0, 0, 0)), pl.BlockSpec(memory_space=pl.ANY)],
        out_specs=[pl.BlockSpec((tm, RW), lambda i, p: (i, 2 + p)), pl.BlockSpec((3, 1, RW), lambda i, p: (0, 0, 0))],
        out_shape=[jax.ShapeDtypeStruct(dz.shape, dz.dtype), jax.ShapeDtypeStruct((3, 1, RW), F32)],
        input_output_aliases={13: 0}, compiler_params=_cparams(("arbitrary", "arbitrary")),
    )(dyb, dyb, dyb, z, z, z, z, z, z, z, z, z, w, dz)


def _rope_tables(dm):
    L, C = dm.L, dm.C
    half = HEAD_DIM // 2
    pos = jnp.arange(L)
    row = (pos // GRID_W).astype(F32)
    col = (pos % GRID_W).astype(F32)
    inv = ROPE_BASE ** (-jnp.arange(0, half, 2, dtype=F32) / half)
    ar, ac = row[:, None] * inv, col[:, None] * inv
    cos = jnp.concatenate([jnp.cos(ar), jnp.cos(ar), jnp.cos(ac), jnp.cos(ac)], axis=-1)
    sin = jnp.concatenate([-jnp.sin(ar), jnp.sin(ar), -jnp.sin(ac), jnp.sin(ac)], axis=-1)
    cos = jnp.concatenate([jnp.ones((C, HEAD_DIM), F32), cos], axis=0)
    sin = jnp.concatenate([jnp.zeros((C, HEAD_DIM), F32), sin], axis=0)
    return cos, sin


def _swap_pairs(x):
    quarter = HEAD_DIM // 4
    lane = lax.broadcasted_iota(jnp.int32, x.shape, 1)
    return jnp.where(lane % (2 * quarter) < quarter, pltpu.roll(x, HEAD_DIM - quarter, 1), pltpu.roll(x, quarter, 1))


def _rope(x, cos, sin):
    return x * cos + _swap_pairs(x) * sin


def _unrope(d, cos, sin):
    return d * cos + _swap_pairs(d * sin)


def qkv_prep(name, z, cos, sin, dm):
    T, tm, HQ, KW = dm.T, dm.tme, dm.HQ, dm.KW
    qcol, kcol = dm.off_q // HQ, dm.off_k // KW

    def body(q_ref, k_ref, v_ref, c_ref, s_ref, qo, ko, vo):
        cos_v, sin_v = c_ref[...], s_ref[...]
        for hd in range(HQ // HEAD_DIM):
            sl = slice(hd * HEAD_DIM, (hd + 1) * HEAD_DIM)
            qo[:, sl] = _rope(q_ref[:, sl], cos_v, sin_v).astype(qo.dtype)
        for hd in range(KW // HEAD_DIM):
            sl = slice(hd * HEAD_DIM, (hd + 1) * HEAD_DIM)
            ko[:, sl] = _rope(k_ref[:, sl], cos_v, sin_v).astype(ko.dtype)
        vo[...] = v_ref[...].astype(vo.dtype)

    tab = pl.BlockSpec((tm, HEAD_DIM), lambda i: (i, 0))
    return pl.pallas_call(
        body, name=name, grid=(dm.nt,),
        in_specs=[pl.BlockSpec((tm, HQ), lambda i: (i, qcol)), pl.BlockSpec((tm, KW), lambda i: (i, kcol)),
                  pl.BlockSpec((tm, KW), lambda i: (i, kcol + 1)), tab, tab],
        out_specs=[pl.BlockSpec((tm, HQ), lambda i: (i, 0)), pl.BlockSpec((tm, KW), lambda i: (i, 0)),
                   pl.BlockSpec((tm, KW), lambda i: (i, 0))],
        out_shape=[jax.ShapeDtypeStruct((T, HQ), BF16), jax.ShapeDtypeStruct((T, KW), BF16),
                   jax.ShapeDtypeStruct((T, KW), BF16)],
        compiler_params=_cparams(("parallel",)),
    )(z, z, z, cos, sin)


def qkv_bwd(name, dq, dk, dv, cos, sin, dz, dm):
    T, tm, HQ, KW = dm.T, dm.tme, dm.HQ, dm.KW
    nq = HQ // KW
    base = dm.off_q // KW

    def body(dq_ref, dk_ref, dv_ref, c_ref, s_ref, dz_in, dz_ref):
        p = pl.program_id(1)
        src = jnp.where(p < nq, dq_ref[...], jnp.where(p == nq, dk_ref[...], dv_ref[...]))
        cos_v, sin_v = c_ref[...], s_ref[...]
        is_v = p == nq + 1
        for hd in range(KW // HEAD_DIM):
            sl = slice(hd * HEAD_DIM, (hd + 1) * HEAD_DIM)
            dz_ref[:, sl] = jnp.where(is_v, src[:, sl], _unrope(src[:, sl], cos_v, sin_v)).astype(dz_ref.dtype)

    tab = pl.BlockSpec((tm, HEAD_DIM), lambda i, p: (i, 0))
    blk = pl.BlockSpec((tm, KW), lambda i, p: (i, 0))
    return pl.pallas_call(
        body, name=name, grid=(dm.nt, nq + 2),
        in_specs=[pl.BlockSpec((tm, KW), lambda i, p: (i, jnp.minimum(p, nq - 1))), blk, blk, tab, tab,
                  pl.BlockSpec(memory_space=pl.ANY)],
        out_specs=pl.BlockSpec((tm, KW), lambda i, p: (i, base + p)),
        out_shape=jax.ShapeDtypeStruct(dz.shape, dz.dtype),
        input_output_aliases={5: 0}, compiler_params=_cparams(("parallel", "arbitrary")),
    )(dq, dk, dv, cos, sin, dz)


def _attn_specs(dm):
    nC, nB, C, KW = dm.C // Q_BLOCK, dm.T // Q_BLOCK, dm.C, dm.KW

    def near(o):
        return lambda b: (jnp.clip(b + o, nC, nB - 1), 0)

    kv = [pl.BlockSpec((Q_BLOCK, KW), near(o)) for o in (-1, 0, 1)] + [pl.BlockSpec((C, KW), lambda b: (0, 0))]
    return kv


def _attn_mask(dm, b):
    nC, C, L = dm.C // Q_BLOCK, dm.C, dm.L
    span = 3 * Q_BLOCK
    n = b - nC
    iq = lax.broadcasted_iota(jnp.int32, (Q_BLOCK, span + C), 0)
    ik = lax.broadcasted_iota(jnp.int32, (Q_BLOCK, span + C), 1)
    kpos = n * Q_BLOCK + ik - Q_BLOCK
    qpos = n * Q_BLOCK + iq
    local = (b >= nC) & (jnp.abs(qpos - kpos) <= WINDOW) & (kpos >= 0) & (kpos < L)
    return jnp.logical_or(ik >= span, local)


def attention(name, q, k, v, sink, dm):
    T, HQ, KW = dm.T, dm.HQ, dm.KW
    H, KV = HQ // HEAD_DIM, KW // HEAD_DIM
    G = H // KV
    scale = HEAD_DIM ** -0.5

    def body(q_ref, kp, kc, kn, kx, vp, vc, vn, vx, sink_ref, o_ref, lse_ref):
        valid = _attn_mask(dm, pl.program_id(0))
        lane = lax.broadcasted_iota(jnp.int32, (Q_BLOCK, LSE_W), 1)
        lse_all = jnp.zeros((Q_BLOCK, LSE_W), F32)
        for kh in range(KV):
            ks = slice(kh * HEAD_DIM, (kh + 1) * HEAD_DIM)
            k_all = jnp.concatenate([kp[:, ks], kc[:, ks], kn[:, ks], kx[:, ks]], axis=0)
            v_all = jnp.concatenate([vp[:, ks], vc[:, ks], vn[:, ks], vx[:, ks]], axis=0)
            for g in range(G):
                hd = kh * G + g
                hs = slice(hd * HEAD_DIM, (hd + 1) * HEAD_DIM)
                s = lax.dot_general(q_ref[:, hs], k_all, _DIMS["nt"], preferred_element_type=F32) * scale
                s = jnp.where(valid, s, NEG_INF)
                snk = sink_ref[0, hd]
                mx = jnp.maximum(jnp.max(s, axis=-1, keepdims=True), snk)
                p = jnp.exp(s - mx)
                den = jnp.sum(p, axis=-1, keepdims=True) + jnp.exp(snk - mx)
                o = jnp.dot(p.astype(BF16), v_all, preferred_element_type=F32) / den
                o_ref[:, hs] = o.astype(o_ref.dtype)
                lse_all = jnp.where(lane == hd, mx + jnp.log(den), lse_all)
        lse_ref[...] = lse_all

    kv = _attn_specs(dm)
    return pl.pallas_call(
        body, name=name, grid=(T // Q_BLOCK,),
        in_specs=[pl.BlockSpec((Q_BLOCK, HQ), lambda b: (b, 0))] + kv + kv + [pl.BlockSpec(memory_space=pltpu.SMEM)],
        out_specs=[pl.BlockSpec((Q_BLOCK, HQ), lambda b: (b, 0)), pl.BlockSpec((Q_BLOCK, LSE_W), lambda b: (b, 0))],
        out_shape=[jax.ShapeDtypeStruct((T, HQ), BF16), jax.ShapeDtypeStruct((T, LSE_W), F32)],
        compiler_params=_cparams(("parallel",)),
    )(q, k, k, k, k, v, v, v, v, sink)


def attention_bwd(name, q, k, v, sink, o, lse, do, dm):
    T, HQ, KW, C = dm.T, dm.HQ, dm.KW, dm.C
    H, KV = HQ // HEAD_DIM, KW // HEAD_DIM
    G = H // KV
    nC, nB = C // Q_BLOCK, T // Q_BLOCK
    scale = HEAD_DIM ** -0.5
    span = 3 * Q_BLOCK

    def body(q_ref, kp, kc, kn, kx, vp, vc, vn, vx, sink_ref, o_ref, lse_ref, do_ref,
             dq_ref, dk_ref, dv_ref, ds_ref):
        b = pl.program_id(0)

        @pl.when(b == 0)
        def _():
            dk_ref[...] = jnp.zeros_like(dk_ref)
            dv_ref[...] = jnp.zeros_like(dv_ref)
            ds_ref[...] = jnp.zeros_like(ds_ref)

        valid = _attn_mask(dm, b)
        starts = [pl.multiple_of(jnp.clip(b + off, nC, nB - 1) * Q_BLOCK, Q_BLOCK) for off in (-1, 0, 1)]
        lane = lax.broadcasted_iota(jnp.int32, (Q_BLOCK, LSE_W), 1)
        lse_all = lse_ref[...]
        dsink = jnp.zeros((1, LSE_W), F32)
        for kh in range(KV):
            ks = slice(kh * HEAD_DIM, (kh + 1) * HEAD_DIM)
            k_all = jnp.concatenate([kp[:, ks], kc[:, ks], kn[:, ks], kx[:, ks]], axis=0)
            v_all = jnp.concatenate([vp[:, ks], vc[:, ks], vn[:, ks], vx[:, ks]], axis=0)
            dk_all = jnp.zeros((span + C, HEAD_DIM), F32)
            dv_all = jnp.zeros((span + C, HEAD_DIM), F32)
            for g in range(G):
                hd = kh * G + g
                hs = slice(hd * HEAD_DIM, (hd + 1) * HEAD_DIM)
                qh = q_ref[:, hs]
                doh = do_ref[:, hs]
                s = lax.dot_general(qh, k_all, _DIMS["nt"], preferred_element_type=F32) * scale
                s = jnp.where(valid, s, NEG_INF)
                lse_h = jnp.sum(jnp.where(lane == hd, lse_all, 0.0), axis=-1, keepdims=True)
                p = jnp.exp(s - lse_h)
                delta = jnp.sum(doh * o_ref[:, hs].astype(F32), axis=-1, keepdims=True)
                dob = doh.astype(BF16)
                dp = lax.dot_general(dob, v_all, _DIMS["nt"], preferred_element_type=F32)
                dsc = (p * (dp - delta) * scale).astype(BF16)
                dq_ref[:, hs] = jnp.dot(dsc, k_all, preferred_element_type=F32)
                dk_all = dk_all + lax.dot_general(dsc, qh, _DIMS["tn"], preferred_element_type=F32)
                dv_all = dv_all + lax.dot_general(p.astype(BF16), dob, _DIMS["tn"], preferred_element_type=F32)
                p_sink = jnp.exp(sink_ref[0, hd] - lse_h)
                dsink = dsink + jnp.where(lane[0:1] == hd, -jnp.sum(p_sink * delta), 0.0)
            for j, st in enumerate(starts):
                rows = pl.ds(st, Q_BLOCK)
                dk_ref[rows, ks] += dk_all[j * Q_BLOCK:(j + 1) * Q_BLOCK]
                dv_ref[rows, ks] += dv_all[j * Q_BLOCK:(j + 1) * Q_BLOCK]
            dk_ref[0:C, ks] += dk_all[span:]
            dv_ref[0:C, ks] += dv_all[span:]
        ds_ref[...] += dsink

    kv = _attn_specs(dm)
    qspec = pl.BlockSpec((Q_BLOCK, HQ), lambda b: (b, 0))
    full = pl.BlockSpec((T, KW), lambda b: (0, 0))
    return pl.pallas_call(
        body, name=name, grid=(nB,),
        in_specs=[qspec] + kv + kv + [pl.BlockSpec(memory_space=pltpu.SMEM), qspec,
                                      pl.BlockSpec((Q_BLOCK, LSE_W), lambda b: (b, 0)), qspec],
        out_specs=[qspec, full, full, pl.BlockSpec((1, LSE_W), lambda b: (0, 0))],
        out_shape=[jax.ShapeDtypeStruct((T, HQ), F32), jax.ShapeDtypeStruct((T, KW), F32),
                   jax.ShapeDtypeStruct((T, KW), F32), jax.ShapeDtypeStruct((1, LSE_W), F32)],
        compiler_params=_cparams(("arbitrary",)),
    )(q, k, k, k, k, v, v, v, v, sink, o, lse, do)


def merge(name, z, lifted, b_merge, dm):
    T, D, tm, cw = dm.T, dm.D, dm.tme, dm.cw
    gcol = dm.off_g // cw
    per = D // cw

    def body(g0, g1, g2, l0, l1, l2, b_ref, o_ref):
        acc = jnp.zeros((tm, cw), F32)
        for i, (g, lf) in enumerate(((g0, l0), (g1, l1), (g2, l2))):
            acc = acc + jax.nn.sigmoid(g[...] + b_ref[i]) * lf[...]
        o_ref[...] = acc.astype(o_ref.dtype)

    gspecs = [pl.BlockSpec((tm, cw), lambda i, j, br=br: (i, gcol + br * per + j)) for br in range(N_BRANCH)]
    blk = pl.BlockSpec((tm, cw), lambda i, j: (i, j))
    return pl.pallas_call(
        body, name=name, grid=(T // tm, per),
        in_specs=gspecs + [blk] * 3 + [pl.BlockSpec((N_BRANCH, 1, cw), lambda i, j: (0, 0, j))], out_specs=blk,
        out_shape=jax.ShapeDtypeStruct((T, D), BF16), compiler_params=_cparams(("parallel", "parallel")),
    )(z, z, z, *lifted, b_merge)


def merge_bwd(name, dmerged, z, lifted_br, b_merge, br, dz, dm):
    T, D, tm, cw = dm.T, dm.D, dm.tme, dm.cw
    gcol = dm.off_g // cw + br * (D // cw)
    per = D // cw

    def body(d_ref, g_ref, l_ref, b_ref, dz_in, dl_ref, dz_ref, db_ref):
        @pl.when(pl.program_id(1) == 0)
        def _():
            db_ref[...] = jnp.zeros_like(db_ref)

        d = d_ref[...]
        gate = jax.nn.sigmoid(g_ref[...] + b_ref[br])
        dl_ref[...] = (d * gate).astype(dl_ref.dtype)
        dg = d * l_ref[...] * gate * (1.0 - gate)
        dz_ref[...] = dg.astype(dz_ref.dtype)
        db_ref[...] += jnp.sum(dg, axis=0, keepdims=True)

    blk = pl.BlockSpec((tm, cw), lambda j, i: (i, j))
    zblk = pl.BlockSpec((tm, cw), lambda j, i: (i, gcol + j))
    return pl.pallas_call(
        body, name=name, grid=(per, T // tm),
        in_specs=[blk, zblk, blk, pl.BlockSpec((N_BRANCH, 1, cw), lambda j, i: (0, 0, j)),
                  pl.BlockSpec(memory_space=pl.ANY)],
        out_specs=[blk, zblk, pl.BlockSpec((1, cw), lambda j, i: (0, j))],
        out_shape=[jax.ShapeDtypeStruct((T, D), BF16), jax.ShapeDtypeStruct(dz.shape, dz.dtype),
                   jax.ShapeDtypeStruct((1, D), F32)],
        input_output_aliases={4: 1}, compiler_params=_cparams(("parallel", "arbitrary")),
    )(dmerged, z, lifted_br, b_merge, dz)


def loss_head(name, h, gf, target, dm):
    T, D, tm, nctx = dm.T, dm.D, dm.tme, dm.nctx

    def body(h_ref, g_ref, t_ref, dh_ref, loss_ref, dg_ref):
        i = pl.program_id(0)

        @pl.when(i == 0)
        def _():
            loss_ref[...] = jnp.zeros_like(loss_ref)
            dg_ref[...] = jnp.zeros_like(dg_ref)

        @pl.when(i < nctx)
        def _():
            dh_ref[...] = jnp.zeros_like(dh_ref)

        @pl.when(i >= nctx)
        def _():
            x = h_ref[...]
            r = lax.rsqrt(jnp.mean(x * x, axis=-1, keepdims=True) + EPS)
            n = x * r
            g = g_ref[...]
            err = n * g - t_ref[...]
            loss_ref[...] += jnp.sum(err * err) * (0.5 / D)
            dy = err * (1.0 / D)
            dg_ref[...] += jnp.sum(dy * n, axis=0, keepdims=True)
            dn = dy * g
            dh_ref[...] = r * (dn - n * jnp.mean(dn * n, axis=-1, keepdims=True))

    row = pl.BlockSpec((tm, D), lambda i: (i, 0))
    return pl.pallas_call(
        body, name=name, grid=(T // tm,),
        in_specs=[row, pl.BlockSpec((1, D), lambda i: (0, 0)),
                  pl.BlockSpec((tm, D), lambda i: (jnp.maximum(i - nctx, 0), 0))],
        out_specs=[row, pl.BlockSpec((1, 128), lambda i: (0, 0)), pl.BlockSpec((1, D), lambda i: (0, 0))],
        out_shape=[jax.ShapeDtypeStruct((T, D), F32), jax.ShapeDtypeStruct((1, 128), F32),
                   jax.ShapeDtypeStruct((1, D), F32)],
        compiler_params=_cparams(("arbitrary",)),
    )(h, gf, target)


_HI = lax.Precision.HIGHEST
ADA_ROWS = 16


def ada_forward(name, cond, ada_w, bias):
    _, D, Aq = ada_w.shape
    tc = _pick(Aq, (1536, 1152, 768, 512, 384, 256, 128))
    tk = _ktile(D)
    nk = D // tk

    def body(c_ref, w_ref, b_ref, o_ref):
        k = pl.program_id(2)

        @pl.when(k == 0)
        def _():
            o_ref[...] = jnp.zeros_like(o_ref) + b_ref[...]

        o_ref[...] += jnp.dot(_silu(c_ref[...]), w_ref[...], precision=_HI, preferred_element_type=F32)

    return pl.pallas_call(
        body, name=name, grid=(2, Aq // tc, nk),
        in_specs=[pl.BlockSpec((ADA_ROWS, tk), lambda l, j, k: (0, k)),
                  pl.BlockSpec((None, tk, tc), lambda l, j, k: (l, k, j)),
                  pl.BlockSpec((None, 1, tc), lambda l, j, k: (l, 0, j))],
        out_specs=pl.BlockSpec((None, ADA_ROWS, tc), lambda l, j, k: (l, 0, j)),
        out_shape=jax.ShapeDtypeStruct((2, ADA_ROWS, Aq), F32),
        compiler_params=_cparams(("parallel", "parallel", "arbitrary")),
    )(cond, ada_w, bias)


def ada_cond_grad(name, dmod, ada_w):
    _, D, Aq = ada_w.shape
    tc = _pick(Aq, (1536, 1152, 768, 512, 384, 256, 128))
    tn = _ktile(D)
    nc = Aq // tc

    def body(d_ref, w_ref, o_ref):
        @pl.when(jnp.logical_and(pl.program_id(1) == 0, pl.program_id(2) == 0))
        def _():
            o_ref[...] = jnp.zeros_like(o_ref)

        o_ref[...] += lax.dot_general(d_ref[...], w_ref[...], _DIMS["nt"], precision=_HI, preferred_element_type=F32)

    return pl.pallas_call(
        body, name=name, grid=(D // tn, 2, nc),
        in_specs=[pl.BlockSpec((None, ADA_ROWS, tc), lambda j, l, c: (l, 0, c)),
                  pl.BlockSpec((None, tn, tc), lambda j, l, c: (l, j, c))],
        out_specs=pl.BlockSpec((ADA_ROWS, tn), lambda j, l, c: (0, j)),
        out_shape=jax.ShapeDtypeStruct((ADA_ROWS, D), F32),
        compiler_params=_cparams(("parallel", "arbitrary", "arbitrary")),
    )(dmod, ada_w)


def ada_update(name, cond, dmod, w, m, v):
    _, D, Aq = w.shape
    tc = _pick(Aq, (1536, 1152, 768, 512, 384, 256, 128))
    tr = 128 if D % 128 == 0 else D
    bc1 = 1.0 - ADAM_B1 ** ADAM_STEP
    bc2 = 1.0 - ADAM_B2 ** ADAM_STEP

    def body(c_ref, d_ref, w_ref, m_ref, v_ref, go_ref, dl_ref, mo_ref, vo_ref):
        g = lax.dot_general(_silu(c_ref[...]), d_ref[...], _DIMS["tn"], precision=_HI, preferred_element_type=F32)
        mn = ADAM_B1 * m_ref[...] + (1.0 - ADAM_B1) * g
        vn = ADAM_B2 * v_ref[...] + (1.0 - ADAM_B2) * (g * g)
        go_ref[...] = g
        dl_ref[...] = -ADAM_LR * ((mn / bc1) / (jnp.sqrt(vn / bc2) + ADAM_EPS) + ADAM_WD * w_ref[...])
        mo_ref[...] = mn
        vo_ref[...] = vn

    blk = pl.BlockSpec((None, tr, tc), lambda l, i, j: (l, i, j))
    return pl.pallas_call(
        body, name=name, grid=(2, D // tr, Aq // tc),
        in_specs=[pl.BlockSpec((ADA_ROWS, tr), lambda l, i, j: (0, i)),
                  pl.BlockSpec((None, ADA_ROWS, tc), lambda l, i, j: (l, 0, j)), blk, blk, blk],
        out_specs=[blk] * 4, out_shape=[jax.ShapeDtypeStruct(w.shape, F32)] * 4,
        compiler_params=_cparams(("parallel", "parallel", "parallel")),
    )(cond, dmod, w, m, v)


def dmod_assemble(name, gathered):
    A = gathered.shape[-1]
    tc = _pick(A, (2048, 1024, 512, 256, 128))

    def body(g_ref, o_ref, b_ref):
        ctx = g_ref[0, 1]
        for dev in range(1, N_DEV):
            ctx = ctx + g_ref[dev, 1]
        tot = ctx
        for dev in range(N_DEV):
            o_ref[dev:dev + 1, :] = g_ref[dev, 0]
            tot = tot + g_ref[dev, 0]
        o_ref[N_DEV:N_DEV + 1, :] = ctx
        o_ref[N_DEV + 1:, :] = jnp.zeros((ADA_ROWS - N_DEV - 1, tc), F32)
        b_ref[...] = tot

    return pl.pallas_call(
        body, name=name, grid=(2, A // tc),
        in_specs=[pl.BlockSpec((N_DEV, None, 2, 1, tc), lambda l, j: (0, l, 0, 0, j))],
        out_specs=[pl.BlockSpec((None, ADA_ROWS, tc), lambda l, j: (l, 0, j)),
                   pl.BlockSpec((None, 1, tc), lambda l, j: (l, 0, j))],
        out_shape=[jax.ShapeDtypeStruct((2, ADA_ROWS, A), F32), jax.ShapeDtypeStruct((2, 1, A), F32)],
        compiler_params=_cparams(("parallel", "parallel")),
    )(gathered)


def sum_devices(name, gathered):
    _, R, W = gathered.shape
    tr = _row_tile(R, W, budget=1 << 18)

    def body(g_ref, all_ref, chip_ref):
        even = g_ref[0]
        odd = g_ref[1]
        for dev in range(2, N_DEV, 2):
            even = even + g_ref[dev]
            odd = odd + g_ref[dev + 1]
        all_ref[...] = even + odd
        chip_ref[...] = even

    blk = pl.BlockSpec((tr, W), lambda i: (i, 0))
    return pl.pallas_call(
        body, name=name, grid=(R // tr,),
        in_specs=[pl.BlockSpec((N_DEV, tr, W), lambda i: (0, i, 0))], out_specs=[blk, blk],
        out_shape=[jax.ShapeDtypeStruct((R, W), F32)] * 2, compiler_params=_cparams(("parallel",)),
    )(gathered)


PACK_ROWS = 1024


def _pack(arrays):
    flat = jnp.concatenate([a.reshape(-1).astype(F32) for a in arrays])
    pad = (-flat.shape[0]) % (PACK_ROWS * 128)
    return jnp.pad(flat, (0, pad)).reshape(-1, 128)


def _unpack(buf, shapes, lead=()):
    flat = buf.reshape(lead + (-1,))
    out, start = [], 0
    for s in shapes:
        n = math.prod(s)
        out.append(flat[..., start:start + n].reshape(lead + tuple(s)))
        start += n
    return out


def _unshard_last(g):
    g = jnp.moveaxis(g, 0, -2)
    return g.reshape(g.shape[:-2] + (g.shape[-2] * g.shape[-1],))


class WeightStream:
    AHEAD = 2

    def __init__(self, keys, make_land):
        self.keys, self.make_land = list(keys), make_land
        self.pending, self.values, self.tokens, self.started = {}, {}, [], 0
        for _ in range(self.AHEAD):
            self._start_next(())

    def _start_next(self, deps):
        if self.started < len(self.keys):
            key = self.keys[self.started]
            self.started += 1
            land, view = self.make_land(key, deps)
            ss, rs, (land,), token = exchange_start(f"{key}_start", [land], _gather_plan)
            self.pending[key] = (ss, rs, land, view)
            self.tokens.append(token)

    def get(self, key, after=None):
        if key not in self.values:
            ss, rs, land, view = self.pending.pop(key)
            (full,) = exchange_wait(f"{key}_wait", ss, rs, [land], _gather_plan, after)
            self.values[key] = full if view is None else full.reshape(view)
            self._start_next((full,))
        return self.values[key]

    def take_tokens(self):
        out, self.tokens = tuple(self.tokens), []
        return out


def _ffn_forward(tag, h, gn, modtab, s, ws, k13, k2, dm):
    u = norm_mod(f"{tag}_norm", h, gn, modtab, s, dm)
    w13g = ws.get(k13, u)
    gu = mm_cols(f"{tag}_w13", u, w13g, BF16, flat=False, deps=ws.take_tokens())
    act = swiglu(f"{tag}_act", gu, dm)
    w2g = ws.get(k2, act)
    f = mm_rows(f"{tag}_w2", act, w2g, deps=ws.take_tokens())
    h_out = resid(f"{tag}_res", h, f, modtab, s, 0.5, dm)
    return h_out, (h, u, gu, act, f)


def _ffn_backward(tag, dh, saved, gn, modtab, s, ws, k13, k2, dm, deps=()):
    h, u, gu, act, f = saved
    w13g, w2g = ws.get(k13), ws.get(k2)
    df, dgate = resid_bwd(f"{tag}_res_bwd", dh, f, modtab, s, 0.5, dm, deps)
    dact = mm_rows_t(f"{tag}_dact", df, w2g)
    dw2 = mm_rows_grad(f"{tag}_dw2", act, df).reshape(N_CHIP, -1, df.shape[-1])
    dgu = swiglu_bwd(f"{tag}_act_bwd", dact, gu, dm)
    du = mm_cols_t(f"{tag}_du", dgu, w13g, flat=False)
    dw13 = mm_cols_grad(f"{tag}_dw13", u, dgu, flat=False)
    dh_in, dss, dgn = norm_mod_bwd(f"{tag}_norm_bwd", du, h, gn, modtab, s, dh, dm)
    return dh_in, dw13, dw2, dss, dgate, dgn


def kernel(x, c, ctx, c_ctx, ada_w, ada_b, norm_g, ffn1_w13, ffn1_w2, w_in, b_merge, rnn_conv_w, rnn_conv_b, lru_w_a, lru_b_a, lru_w_x, lru_b_x, lru_lambda, sc_conv_w, attn_sink, w_branch, w_out, ffn2_w13, ffn2_w2, final_norm_g, loss_target, m_c_ctx, m_ada_w, m_ada_b, m_norm_g, m_ffn1_w13, m_ffn1_w2, m_w_in, m_b_merge, m_rnn_conv_w, m_rnn_conv_b, m_lru_w_a, m_lru_b_a, m_lru_w_x, m_lru_b_x, m_lru_lambda, m_sc_conv_w, m_attn_sink, m_w_branch, m_w_out, m_ffn2_w13, m_ffn2_w2, m_final_norm_g, v_c_ctx, v_ada_w, v_ada_b, v_norm_g, v_ffn1_w13, v_ffn1_w2, v_w_in, v_b_merge, v_rnn_conv_w, v_rnn_conv_b, v_lru_w_a, v_lru_b_a, v_lru_w_x, v_lru_b_x, v_lru_lambda, v_sc_conv_w, v_attn_sink, v_w_branch, v_w_out, v_ffn2_w13, v_ffn2_w2, v_final_norm_g):
    weights = dict(c_ctx=c_ctx, ada_w=ada_w, ada_b=ada_b, norm_g=norm_g, ffn1_w13=ffn1_w13, ffn1_w2=ffn1_w2, w_in=w_in,
                   b_merge=b_merge, rnn_conv_w=rnn_conv_w, rnn_conv_b=rnn_conv_b, lru_w_a=lru_w_a, lru_b_a=lru_b_a,
                   lru_w_x=lru_w_x, lru_b_x=lru_b_x, lru_lambda=lru_lambda, sc_conv_w=sc_conv_w, attn_sink=attn_sink,
                   w_branch=w_branch, w_out=w_out, ffn2_w13=ffn2_w13, ffn2_w2=ffn2_w2, final_norm_g=final_norm_g)
    mom_m = dict(c_ctx=m_c_ctx, ada_w=m_ada_w, ada_b=m_ada_b, norm_g=m_norm_g, ffn1_w13=m_ffn1_w13, ffn1_w2=m_ffn1_w2,
                 w_in=m_w_in, b_merge=m_b_merge, rnn_conv_w=m_rnn_conv_w, rnn_conv_b=m_rnn_conv_b, lru_w_a=m_lru_w_a,
                 lru_b_a=m_lru_b_a, lru_w_x=m_lru_w_x, lru_b_x=m_lru_b_x, lru_lambda=m_lru_lambda,
                 sc_conv_w=m_sc_conv_w, attn_sink=m_attn_sink, w_branch=m_w_branch, w_out=m_w_out,
                 ffn2_w13=m_ffn2_w13, ffn2_w2=m_ffn2_w2, final_norm_g=m_final_norm_g)
    mom_v = dict(c_ctx=v_c_ctx, ada_w=v_ada_w, ada_b=v_ada_b, norm_g=v_norm_g, ffn1_w13=v_ffn1_w13, ffn1_w2=v_ffn1_w2,
                 w_in=v_w_in, b_merge=v_b_merge, rnn_conv_w=v_rnn_conv_w, rnn_conv_b=v_rnn_conv_b, lru_w_a=v_lru_w_a,
                 lru_b_a=v_lru_b_a, lru_w_x=v_lru_w_x, lru_b_x=v_lru_b_x, lru_lambda=v_lru_lambda,
                 sc_conv_w=v_sc_conv_w, attn_sink=v_attn_sink, w_branch=v_w_branch, w_out=v_w_out,
                 ffn2_w13=v_ffn2_w13, ffn2_w2=v_ffn2_w2, final_norm_g=v_final_norm_g)
    order = list(weights)

    dm = Dims()
    dm.D = D = x.shape[-1]
    dm.L = L = x.shape[1]
    dm.C = C = ctx.shape[1]
    dm.T = T = L + C
    dm.RW = RW = rnn_conv_b.shape[-1]
    dm.NB = lru_w_a.shape[2]
    H = attn_sink.shape[-1]
    dm.HQ = HQ = H * HEAD_DIM
    NZ = w_in.shape[-1] * N_CHIP
    dm.KW = KW = (NZ - 5 * RW - HQ - N_BRANCH * D) // 2
    dm.off_q = 5 * RW
    dm.off_k = dm.off_q + HQ
    dm.off_g = dm.off_k + 2 * KW
    dm.tme = _pick(C, (256, 128))
    dm.nt = T // dm.tme
    dm.nctx = C // dm.tme
    dm.cw = next(w for w in (512, 256, 128) if dm.off_g % w == 0 and D % w == 0)
    A = ada_b.shape[-1]
    Aq = ada_w.shape[-1]
    assert dm.off_q % HQ == 0 and dm.off_k % KW == 0 and HQ % KW == 0 and L % dm.tme == 0 and RW == HQ
    assert C % Q_BLOCK == 0 and L % Q_BLOCK == 0 and D % N_CHIP == 0 and A == N_MOD * D

    mx, my, mc = _my_pos()
    j_me = 2 * mx + my
    b_me = 4 * mx + 2 * my + mc

    big = ["ffn1_w13", "ffn1_w2", "w_in", "w_branch", "w_out", "ffn2_w13", "ffn2_w2"]
    j_idx = jnp.reshape(j_me, (1,)).astype(jnp.int32)
    FFq = ffn1_w2.shape[1]
    views = {"ffn1_w2": (2, 2 * FFq, D), "ffn2_w2": (2, 2 * FFq, D), "w_out": (D, D),
             "w_branch": (N_CHIP, N_BRANCH, RW, D // N_CHIP)}

    def make_land(key, deps):
        l, n = int(key[1]), key[3:]
        return cast_into_slot(f"{key}_cast", weights[n], l, j_idx, deps), views.get(n)

    ws = WeightStream([f"l{l}_{n}" for l in range(2) for n in big], make_land)
    first_tokens = ws.take_tokens()

    small_sharded = ["norm_g", "b_merge", "rnn_conv_w", "lru_b_a", "lru_b_x", "lru_lambda", "sc_conv_w"]
    pack1 = _pack([c] + [weights[n] for n in small_sharded])
    pack1 = pack1 + sum(t[0, 0] for t in first_tokens)
    g1 = allgather8("gather_small_params", pack1).reshape(N_DEV, -1, 128)
    parts = _unpack(g1, [c.shape] + [weights[n].shape for n in small_sharded], lead=(N_DEV,))
    c_all = parts[0].reshape(N_DEV, D)
    full = {n: _unshard_last(p[0::2]) for n, p in zip(small_sharded, parts[1:])}
    cond = jnp.concatenate([c_all, c_ctx[None, :], jnp.zeros((ADA_ROWS - N_DEV - 1, D), F32)], axis=0)

    bias_q = lax.dynamic_slice_in_dim(ada_b, j_me * Aq, Aq, axis=1)[:, None, :]
    mod_q = ada_forward("ada_forward", cond, ada_w, bias_q)
    g2 = allgather8("gather_mod", mod_q.reshape(-1, 128)).reshape(N_DEV, 2, ADA_ROWS, Aq)
    mod_full = _unshard_last(g2[0::2])
    mod_lat = lax.dynamic_index_in_dim(mod_full, b_me, axis=1, keepdims=False)
    mod_ctx = mod_full[:, N_DEV]
    modtabs = [jnp.stack([mod_ctx[l], mod_lat[l]]).reshape(2, N_MOD, 1, D) for l in range(2)]

    cos, sin = _rope_tables(dm)
    sink = attn_sink.reshape(2, 1, H)
    lw = dict(w_a=lru_w_a, w_x=lru_w_x,
              b_a=full["lru_b_a"][:, :, None, :], b_x=full["lru_b_x"][:, :, None, :],
              lam=full["lru_lambda"][:, :, None, :])
    gn = full["norm_g"]
    bm = full["b_merge"][:, :, None, :]
    rcw = full["rnn_conv_w"][:, :, None, :]
    scw = full["sc_conv_w"][:, :, None, :]

    h = jnp.concatenate([ctx[0], x[0]], axis=0)
    saved = []
    for l in range(2):
        mt = modtabs[l]
        sv = {}
        h, sv["ffn1"] = _ffn_forward(f"l{l}_ffn1", h, gn[l, 0:1], mt, 0, ws, f"l{l}_ffn1_w13", f"l{l}_ffn1_w2", dm)
        sv["h_mix"] = h
        u = norm_mod(f"l{l}_mix_norm", h, gn[l, 1:2], mt, 1, dm)
        wing = ws.get(f"l{l}_w_in", u)
        z = mm_cols(f"l{l}_w_in", u, wing, F32, flat=True, deps=ws.take_tokens())
        xa = rnn_conv(f"l{l}_rnn_conv", z, rcw[l], rnn_conv_b[l][None, :], dm)
        scans = []
        for d in range(2):
            a_d, u_d = lru_gates(f"l{l}_lru_gates{d}", xa, lw, l, d, dm)
            h_d = lru_scan(f"l{l}_lru_scan{d}", a_d, u_d, True, d == 1, dm)
            scans.append((a_d, u_d, h_d))
        ya = rnn_out(f"l{l}_rnn_out", scans[0][2], scans[1][2], z, dm)
        yb = short_conv(f"l{l}_short_conv", z, scw[l], dm)
        qr, kr, vv = qkv_prep(f"l{l}_qkv", z, cos, sin, dm)
        yatt, lse = attention(f"l{l}_attn", qr, kr, vv, sink[l], dm)
        ys = (ya, yb, yatt)
        wbg = ws.get(f"l{l}_w_branch", yatt)
        lifted = [mm_branch(f"l{l}_lift{br}", ys[br], wbg, br, deps=ws.take_tokens()) for br in range(N_BRANCH)]
        merged = merge(f"l{l}_merge", z, lifted, bm[l], dm)
        woutg = ws.get(f"l{l}_w_out", merged)
        y = mm_plain(f"l{l}_w_out", merged, woutg, "nn", F32, deps=ws.take_tokens())
        h = resid(f"l{l}_mix_res", h, y, mt, 1, 1.0, dm)
        sv.update(u=u, z=z, xa=xa, scans=scans, ys=ys, qkv=(qr, kr, vv), lse=lse, lifted=lifted, merged=merged, y=y)
        h, sv["ffn2"] = _ffn_forward(f"l{l}_ffn2", h, gn[l, 2:3], mt, 2, ws, f"l{l}_ffn2_w13", f"l{l}_ffn2_w2", dm)
        saved.append(sv)

    dh, loss_vec, d_final_g = loss_head("loss_head", h, final_norm_g[None, :], loss_target[0], dm)
    loss = lax.psum(loss_vec[0, 0], ("x", "y", "c"))

    small = {n: [None, None] for n in ["norm_g", "b_merge", "rnn_conv_w", "rnn_conv_b", "lru_w_a", "lru_b_a", "lru_w_x",
                                       "lru_b_x", "lru_lambda", "sc_conv_w", "attn_sink"]}
    dmods = [None, None]
    scatters = []

    def scatter(name, keyed):
        grads3 = [g.reshape(N_CHIP, -1, g.shape[-1]) for g in keyed.values()]
        lands = [lax.empty((3,) + g.shape[1:], BF16) for g in grads3]
        ss, rs, bufs, token = exchange_start(f"{name}_start", grads3 + lands, _scatter_plan(len(grads3)))
        scatters.append((name, ss, rs, bufs, list(keyed)))
        return token

    tok = ()
    for l in (1, 0):
        mt = modtabs[l]
        sv = saved[l]
        dh, dw13, dw2, dss2, dgate2, dgn2 = _ffn_backward(f"l{l}_ffn2", dh, sv["ffn2"], gn[l, 2:3], mt, 2,
                                                         ws, f"l{l}_ffn2_w13", f"l{l}_ffn2_w2", dm, deps=tok)
        tok = (scatter(f"l{l}_scatter_ffn2", {("ffn2_w13", l): dw13, ("ffn2_w2", l): dw2}),)

        dyg, dgate1 = resid_bwd(f"l{l}_mix_res_bwd", dh, sv["y"], mt, 1, 1.0, dm, deps=tok)
        woutg, wbg, wing = ws.get(f"l{l}_w_out"), ws.get(f"l{l}_w_branch"), ws.get(f"l{l}_w_in")
        dmerged = mm_plain(f"l{l}_dmerged", dyg, woutg, "nt", F32)
        mix_grads = {("w_out", l): mm_plain_grad(f"l{l}_dw_out", sv["merged"], dyg)}
        dz = lax.empty((T, NZ), BF16)
        dys, dbm = [], []
        for br in range(N_BRANCH):
            dl, dz, db = merge_bwd(f"l{l}_merge_bwd{br}", dmerged, sv["z"], sv["lifted"][br], bm[l], br, dz, dm)
            dys.append(mm_branch_t(f"l{l}_dy{br}", dl, wbg, br))
            mix_grads[("w_branch", l, br)] = mm_branch_grad(f"l{l}_dwb{br}", sv["ys"][br], dl)
            dbm.append(db)
        small["b_merge"][l] = jnp.concatenate(dbm, axis=0)

        qr, kr, vv = sv["qkv"]
        dq, dk, dv, dsink = attention_bwd(f"l{l}_attn_bwd", qr, kr, vv, sink[l], sv["ys"][2], sv["lse"], dys[2], dm)
        dz = qkv_bwd(f"l{l}_qkv_bwd", dq, dk, dv, cos, sin, dz, dm)
        small["attn_sink"][l] = dsink[0, :H]

        dz, dscw = short_conv_bwd(f"l{l}_short_conv_bwd", dys[1], sv["z"], scw[l], dz, dm)
        small["sc_conv_w"][l] = dscw[:, 0]

        (a0, u0, h0), (a1, u1, h1) = sv["scans"]
        dhs, drg = rnn_out_bwd(f"l{l}_rnn_out_bwd", dys[0], h0, h1, sv["z"], dm)
        dxa, lru_sums = [], []
        for d, (a_d, u_d, h_d) in enumerate(sv["scans"]):
            lam_d, dla_d = lru_scan_bwd(f"l{l}_lru_scan_bwd{d}", a_d, u_d, h_d, dhs, False, d == 0, dm)
            outs = lru_gates_bwd(f"l{l}_lru_gates_bwd{d}", sv["xa"], lw, l, d, lam_d, dla_d, dm)
            dxa.append(outs[0])
            lru_sums.append(outs[1:])
        dz, drcw, drcb = rnn_conv_bwd(f"l{l}_rnn_conv_bwd", dxa[0], dxa[1], drg, sv["z"], rcw[l], dz, dm)
        small["rnn_conv_w"][l] = drcw[:, 0]
        small["rnn_conv_b"][l] = drcb[0]
        for i, n in enumerate(["lru_w_a", "lru_b_a", "lru_w_x", "lru_b_x", "lru_lambda"]):
            small[n][l] = jnp.stack([lru_sums[0][i], lru_sums[1][i]]).reshape((2,) + weights[n].shape[2:-1] + (-1,))

        du = mm_cols_t(f"l{l}_du_mix", dz, wing, flat=True)
        mix_grads[("w_in", l)] = mm_cols_grad(f"l{l}_dw_in", sv["u"], dz, flat=True)
        dh, dss1, dgn1 = norm_mod_bwd(f"l{l}_mix_norm_bwd", du, sv["h_mix"], gn[l, 1:2], mt, 1, dh, dm)
        tok = (scatter(f"l{l}_scatter_mix", mix_grads),)

        dh, dw13, dw2, dss0, dgate0, dgn0 = _ffn_backward(f"l{l}_ffn1", dh, sv["ffn1"], gn[l, 0:1], mt, 0,
                                                         ws, f"l{l}_ffn1_w13", f"l{l}_ffn1_w2", dm, deps=tok)
        tok = (scatter(f"l{l}_scatter_ffn1", {("ffn1_w13", l): dw13, ("ffn1_w2", l): dw2}),)
        small["norm_g"][l] = jnp.concatenate([dgn0, dgn1, dgn2], axis=0)
        dmods[l] = jnp.concatenate([dss0, dgate0, dss1, dgate1, dss2, dgate2], axis=1).reshape(2, A)

    grad_x = dh[C:][None]

    pack_mod = jnp.stack([jnp.stack([dmods[l][1], dmods[l][0]]) for l in range(2)])
    g3 = allgather8("gather_dmod", pack_mod.reshape(-1, 128)).reshape(N_DEV, 2, 2, 1, A)
    dmod_full, d_ada_b = dmod_assemble("dmod_assemble", g3)
    dmod_q = lax.dynamic_slice_in_dim(dmod_full, j_me * Aq, Aq, axis=2)
    dcond_q = ada_cond_grad("ada_cond_grad", dmod_q, ada_w)

    small_names = list(small)
    small_parts = [jnp.stack(small[n]) for n in small_names] + [d_final_g, dcond_q[N_DEV]]
    small_shapes = [p.shape for p in small_parts]
    lru_big = [small_names.index("lru_w_a"), small_names.index("lru_w_x")]
    rest_idx = [i for i in range(len(small_parts)) if i not in lru_big]
    summed = [None] * len(small_parts)
    for i in lru_big:
        buf = _pack([small_parts[i]])
        tot, _ = sum_devices(f"sum_{small_names[i]}", allgather8(f"gather_{small_names[i]}", buf).reshape(N_DEV, -1, 128))
        summed[i] = _unpack(tot, [small_shapes[i]])[0]
    buf = _pack([small_parts[i] for i in rest_idx])
    tot, chip_tot = sum_devices("sum_small_grads", allgather8("gather_small_grads", buf).reshape(N_DEV, -1, 128))
    for i, val in zip(rest_idx, _unpack(tot, [small_shapes[i] for i in rest_idx])):
        summed[i] = val
    dcond_ctx = _unpack(chip_tot, [small_shapes[i] for i in rest_idx])[-1]
    sg = jax.nn.sigmoid(c_ctx)
    grads = dict(zip(small_names, summed[:len(small_names)]))
    grads["final_norm_g"] = summed[len(small_names)][0]
    grads["c_ctx"] = dcond_ctx * (sg * (1.0 + c_ctx * (1.0 - sg)))
    grads["ada_b"] = d_ada_b[:, 0]
    for n in small_sharded:
        g = grads[n]
        q = g.shape[-1] // N_CHIP
        grads[n] = lax.dynamic_slice_in_dim(g, j_me * q, q, axis=g.ndim - 1)

    arrived = {}

    def collect(idx, after):
        name, ss, rs, bufs, keys = scatters[idx]
        done = exchange_wait(f"{name}_wait", ss, rs, bufs, _scatter_plan(len(keys)), after)
        for i, key in enumerate(keys):
            arrived[key] = (done[i], done[len(keys) + i])

    for idx in range(len(scatters) - 1):
        collect(idx, dh)
    results = {}
    last_done = dh

    def finish(swap, after):
        n, ss, rs, bufs = swap
        own, other = exchange_wait(f"swap_{n}_wait", ss, rs, bufs, _sibling_plan, after)
        results[n] = adamw(f"adamw_{n}", weights[n], mom_m[n], mom_v[n], [own, other])
        return results[n][0]

    in_flight = None
    for n in ["ffn2_w13", "ffn2_w2", "w_out", "w_branch", "w_in", "ffn1_w13", "ffn1_w2"]:
        if not any(k[0] == n and k[1] == 0 for k in arrived):
            collect(len(scatters) - 1, last_done)
        keys = sorted((k for k in arrived if k[0] == n), key=lambda k: k[1:])
        part = sum_parts(f"sum_{n}", [arrived[k] for k in keys], j_idx).reshape(weights[n].shape)
        ss, rs, bufs, token = exchange_start(f"swap_{n}_start", [part, lax.empty(part.shape, F32)], _sibling_plan)
        if in_flight is not None:
            last_done = finish(in_flight, token)
        in_flight = (n, ss, rs, bufs)
    finish(in_flight, last_done)
    results["ada_w"] = ada_update("ada_update", cond, dmod_q, ada_w, m_ada_w, v_ada_w)
    small_all = [n for n in order if n not in results]
    pk = lambda d: _pack([d[n] for n in small_all])
    outs = adamw("adamw_small", pk(weights), pk(mom_m), pk(mom_v), [pk(grads)])
    shapes_small = [weights[n].shape for n in small_all]
    unpacked = [_unpack(o, shapes_small) for o in outs]
    for i, n in enumerate(small_all):
        results[n] = tuple(unpacked[k][i] for k in range(4))

    return (loss, grad_x, *[results[n][0] for n in order], *[results[n][1] for n in order],
            *[results[n][2] for n in order], *[results[n][3] for n in order])
```

```python
import functools
import math

import jax
import jax.numpy as jnp
from jax import lax
from jax.experimental import pallas as pl
from jax.experimental.pallas import tpu as pltpu

F32 = jnp.float32
BF16 = jnp.bfloat16
MESH = pl.DeviceIdType.MESH

HEAD_DIM = 128
GRID_W = 64
WINDOW = 128
Q_BLOCK = 128
ROPE_BASE = 10000.0
LRU_C = 8.0
EPS = 1e-6
NEG_INF = -1e30
N_MOD = 9
N_BRANCH = 3
RNN_BLOCK = 128
HALO = 8
LSE_W = 128

ADAM_LR = 0.001
ADAM_B1 = 0.9
ADAM_B2 = 0.999
ADAM_EPS = 1e-08
ADAM_WD = 0.01
ADAM_STEP = 10

VMEM_LIMIT_BYTES = 48 * 1024 * 1024
N_DEV = 8
N_CHIP = 4


def _pick(n, cands):
    for c in cands:
        if c <= n and n % c == 0:
            return c
    return n


def _cparams(sem):
    return pltpu.CompilerParams(dimension_semantics=sem, vmem_limit_bytes=VMEM_LIMIT_BYTES)


def _silu(x):
    return x * jax.nn.sigmoid(x)


def _dsilu(x):
    s = jax.nn.sigmoid(x)
    return s * (1.0 + x * (1.0 - s))


_GELU_K = math.sqrt(2.0 / math.pi)


def _gelu(x):
    return 0.5 * x * (1.0 + jnp.tanh(_GELU_K * (x + 0.044715 * x * x * x)))


def _dgelu(x):
    t = jnp.tanh(_GELU_K * (x + 0.044715 * x * x * x))
    return 0.5 * (1.0 + t) + 0.5 * x * (1.0 - t * t) * _GELU_K * (1.0 + 3.0 * 0.044715 * x * x)


def _expm1(x):
    series = x * (1.0 + x * (0.5 + x * (1.0 / 6.0 + x * (1.0 / 24.0 + x * (1.0 / 120.0)))))
    return jnp.where(jnp.abs(x) < 0.1, series, jnp.exp(x) - 1.0)


def _my_pos():
    return lax.axis_index("x"), lax.axis_index("y"), lax.axis_index("c")


_DIMS = {"nn": (((1,), (0,)), ((), ())), "nt": (((1,), (1,)), ((), ())), "tn": (((0,), (0,)), ((), ()))}


def _mm(name, a, b, *, mode, grid, a_blk, a_map, b_blk, b_map, o_blk, o_map, out_shape, out_dtype, deps=()):
    nk = grid[-1]
    nax = len(grid)
    acc_shape = tuple(d for d in o_blk if d is not None)

    def product(a_ref, b_ref):
        return lax.dot_general(a_ref[...].astype(BF16), b_ref[...].astype(BF16), _DIMS[mode], preferred_element_type=F32)

    def body_one_step(a_ref, b_ref, *rest):
        rest[-1][...] = product(a_ref, b_ref).astype(rest[-1].dtype)

    def body(a_ref, b_ref, *rest):
        o_ref, acc_ref = rest[-2:]
        k = pl.program_id(nax - 1)

        @pl.when(k == 0)
        def _():
            acc_ref[...] = jnp.zeros_like(acc_ref)

        acc_ref[...] += product(a_ref, b_ref)

        @pl.when(k == nk - 1)
        def _():
            o_ref[...] = acc_ref[...].astype(o_ref.dtype)

    return pl.pallas_call(
        body_one_step if nk == 1 else body, name=name, grid=grid,
        in_specs=[pl.BlockSpec(a_blk, a_map), pl.BlockSpec(b_blk, b_map)]
        + [pl.BlockSpec(memory_space=pl.ANY)] * len(deps),
        out_specs=pl.BlockSpec(o_blk, o_map),
        out_shape=jax.ShapeDtypeStruct(out_shape, out_dtype),
        scratch_shapes=[] if nk == 1 else [pltpu.VMEM(acc_shape, F32)],
        compiler_params=_cparams(("parallel",) * (nax - 1) + ("arbitrary",)),
    )(a, b, *deps)


def _tiles(n):
    return _pick(n, (768, 512, 384, 256, 128, 64, 32, 16))


def _tiles_long(n):
    return _pick(n, (1408, 768, 512, 384, 256, 128, 64, 32, 16))


VMEM_TILE_BUDGET = 40 * 1024 * 1024


def _fit(n, nbytes):
    for c in (768, 512, 384, 256, 128, 64, 32, 16):
        if c <= n and n % c == 0 and nbytes(c) <= VMEM_TILE_BUDGET:
            return c
    return _pick(n, (16, 8))


def _whole(n, cap=2048):
    return n if n <= cap else _ktile(n)


def _ktile(n):
    return _pick(n, (512, 256, 128))


def mm_cols(name, a, wg, out_dtype, flat, deps=()):
    T, K = a.shape
    Nq = wg.shape[-1]
    tk = _whole(K)
    osize = jnp.dtype(out_dtype).itemsize
    tm = _fit(T, lambda t: 2 * t * Nq * osize + 4 * tk * Nq + 4 * t * tk + (4 * t * Nq if tk < K else 0))
    if flat:
        o_blk, o_map, o_shape = (tm, Nq), (lambda j, i, k: (i, j)), (T, N_CHIP * Nq)
    else:
        o_blk, o_map, o_shape = (None, tm, Nq), (lambda j, i, k: (j, i, 0)), (N_CHIP, T, Nq)
    return _mm(name, a, wg, mode="nn", grid=(N_CHIP, T // tm, K // tk),
               a_blk=(tm, tk), a_map=lambda j, i, k: (i, k),
               b_blk=(None, tk, Nq), b_map=lambda j, i, k: (j, k, 0),
               o_blk=o_blk, o_map=o_map, out_shape=o_shape, out_dtype=out_dtype, deps=deps)


def mm_cols_t(name, d, wg, flat):
    K, Nq = wg.shape[-2:]
    T = d.shape[-2]
    tm, tn = _tiles_long(T), _ktile(K)
    if flat:
        a_blk, a_map = (tm, Nq), (lambda i, j, k: (i, k))
    else:
        a_blk, a_map = (None, tm, Nq), (lambda i, j, k: (k, i, 0))
    return _mm(name, d, wg, mode="nt", grid=(T // tm, K // tn, N_CHIP),
               a_blk=a_blk, a_map=a_map,
               b_blk=(None, tn, Nq), b_map=lambda i, j, k: (k, j, 0),
               o_blk=(tm, tn), o_map=lambda i, j, k: (i, j), out_shape=(T, K), out_dtype=F32)


def mm_cols_grad(name, a, d, flat):
    T, K = a.shape
    Nq = d.shape[-1] // N_CHIP if flat else d.shape[-1]
    tt, br = _tiles_long(T), _ktile(K)
    if flat:
        b_blk, b_map = (tt, Nq), (lambda j, r, t: (t, j))
    else:
        b_blk, b_map = (None, tt, Nq), (lambda j, r, t: (j, t, 0))
    return _mm(name, a, d, mode="tn", grid=(N_CHIP, K // br, T // tt),
               a_blk=(tt, br), a_map=lambda j, r, t: (t, r),
               b_blk=b_blk, b_map=b_map,
               o_blk=(None, br, Nq), o_map=lambda j, r, t: (j, r, 0),
               out_shape=(N_CHIP, K, Nq), out_dtype=BF16)


def mm_rows(name, a, wg, deps=()):
    G, T, Kg = a.shape
    N = wg.shape[-1]
    tm, tn = _tiles(T), _pick(N, (1024, 512, 256, 128))
    return _mm(name, a, wg, mode="nn", grid=(T // tm, N // tn, G),
               a_blk=(None, tm, Kg), a_map=lambda i, j, k: (k, i, 0),
               b_blk=(None, Kg, tn), b_map=lambda i, j, k: (k, 0, j),
               o_blk=(tm, tn), o_map=lambda i, j, k: (i, j), out_shape=(T, N), out_dtype=F32, deps=deps)


def mm_rows_t(name, d, wg):
    T, N = d.shape
    G, Kg = wg.shape[0], wg.shape[1]
    tk = _whole(N)
    tm = _fit(T, lambda t: 4 * t * Kg + 4 * Kg * tk + 4 * t * tk + (4 * t * Kg if tk < N else 0))
    return _mm(name, d, wg, mode="nt", grid=(G, T // tm, N // tk),
               a_blk=(tm, tk), a_map=lambda j, i, k: (i, k),
               b_blk=(None, Kg, tk), b_map=lambda j, i, k: (j, 0, k),
               o_blk=(None, tm, Kg), o_map=lambda j, i, k: (j, i, 0), out_shape=(G, T, Kg), out_dtype=BF16)


def mm_rows_grad(name, a, d):
    G, T, Kg = a.shape
    N = d.shape[-1]
    tt, tn = _tiles_long(T), _ktile(N)
    return _mm(name, a, d, mode="tn", grid=(G, N // tn, T // tt),
               a_blk=(None, tt, Kg), a_map=lambda g, j, t: (g, t, 0),
               b_blk=(tt, tn), b_map=lambda g, j, t: (t, j),
               o_blk=(None, Kg, tn), o_map=lambda g, j, t: (g, 0, j), out_shape=(G, Kg, N), out_dtype=BF16)


def mm_plain(name, a, w, mode, out_dtype, deps=()):
    T = a.shape[0]
    K, N = w.shape[-2:]
    tm = _tiles(T)
    if mode == "nn":
        tn, tk = _whole(N), _whole(K)
        return _mm(name, a, w, mode="nn", grid=(T // tm, N // tn, K // tk),
                   a_blk=(tm, tk), a_map=lambda i, j, k: (i, k),
                   b_blk=(tk, tn), b_map=lambda i, j, k: (k, j),
                   o_blk=(tm, tn), o_map=lambda i, j, k: (i, j), out_shape=(T, N), out_dtype=out_dtype, deps=deps)
    tn, tk = _whole(K), _whole(N)
    return _mm(name, a, w, mode="nt", grid=(T // tm, K // tn, N // tk),
               a_blk=(tm, tk), a_map=lambda i, j, k: (i, k),
               b_blk=(tn, tk), b_map=lambda i, j, k: (j, k),
               o_blk=(tm, tn), o_map=lambda i, j, k: (i, j), out_shape=(T, K), out_dtype=out_dtype)


def mm_plain_grad(name, a, d):
    T, K = a.shape
    N = d.shape[-1]
    tt, br, tn = _tiles_long(T), _ktile(K), _whole(N)
    return _mm(name, a, d, mode="tn", grid=(K // br, N // tn, T // tt),
               a_blk=(tt, br), a_map=lambda r, j, t: (t, r),
               b_blk=(tt, tn), b_map=lambda r, j, t: (t, j),
               o_blk=(br, tn), o_map=lambda r, j, t: (r, j), out_shape=(K, N), out_dtype=BF16)


def mm_branch(name, y, wbg, br, deps=()):
    T, RW = y.shape
    Dq = wbg.shape[-1]
    tm = _tiles(T)
    return _mm(name, y, wbg, mode="nn", grid=(T // tm, N_CHIP, 1),
               a_blk=(tm, RW), a_map=lambda i, j, k: (i, 0),
               b_blk=(None, None, RW, Dq), b_map=lambda i, j, k: (j, br, 0, 0),
               o_blk=(tm, Dq), o_map=lambda i, j, k: (i, j), out_shape=(T, N_CHIP * Dq), out_dtype=BF16, deps=deps)


def mm_branch_t(name, d, wbg, br):
    T = d.shape[0]
    RW, Dq = wbg.shape[-2:]
    tm = _tiles(T)
    return _mm(name, d, wbg, mode="nt", grid=(T // tm, 1, N_CHIP),
               a_blk=(tm, Dq), a_map=lambda i, j, k: (i, k),
               b_blk=(None, None, RW, Dq), b_map=lambda i, j, k: (k, br, 0, 0),
               o_blk=(tm, RW), o_map=lambda i, j, k: (i, 0), out_shape=(T, RW), out_dtype=F32)


def mm_branch_grad(name, y, d):
    T, RW = y.shape
    Dq = d.shape[-1] // N_CHIP
    tt = _tiles(T)
    return _mm(name, y, d, mode="tn", grid=(N_CHIP, 1, T // tt),
               a_blk=(tt, RW), a_map=lambda j, r, t: (t, 0),
               b_blk=(tt, Dq), b_map=lambda j, r, t: (t, j),
               o_blk=(None, RW, Dq), o_map=lambda j, r, t: (j, 0, 0), out_shape=(N_CHIP, RW, Dq), out_dtype=BF16)


def allgather8(name, x_shard):
    m_per, n = x_shard.shape

    def body(x_ref, out_ref, send_sems, recv_sems, local_sem):
        x, y, c = _my_pos()
        me, sibling = (x, y, c), (x, y, 1 - c)
        chips = [(1 - x, y), (x, 1 - y), (1 - x, 1 - y)]

        def rows(px, py, pc):
            return out_ref.at[pl.ds((4 * px + 2 * py + pc) * m_per, m_per), :]

        def copy(k, block, to, src=None):
            return pltpu.make_async_remote_copy(
                src_ref=rows(*block) if src is None else src, dst_ref=rows(*block),
                send_sem=send_sems.at[k], recv_sem=recv_sems.at[k], device_id=to, device_id_type=MESH)

        mine = pltpu.make_async_copy(x_ref, rows(*me), local_sem)
        mine.start()
        first = [copy(0, me, sibling, src=x_ref)]
        first += [copy(1 + j, me, (*chip, c), src=x_ref) for j, chip in enumerate(chips)]
        for cp in first:
            cp.start()
        passed = [copy(4 + j, (*chip, c), sibling) for j, chip in enumerate(chips)]
        for j, chip in enumerate(chips):
            copy(1 + j, (*chip, c), me).wait_recv()
            passed[j].start()
        copy(0, sibling, me).wait_recv()
        for j, chip in enumerate(chips):
            copy(4 + j, (*chip, 1 - c), me).wait_recv()
        for cp in first + passed:
            cp.wait_send()
        mine.wait()

    return pl.pallas_call(
        body, name=name,
        out_shape=jax.ShapeDtypeStruct((N_DEV * m_per, n), x_shard.dtype),
        in_specs=[pl.BlockSpec(memory_space=pltpu.VMEM)],
        out_specs=pl.BlockSpec(memory_space=pltpu.VMEM),
        scratch_shapes=[pltpu.SemaphoreType.DMA((7,)), pltpu.SemaphoreType.DMA((7,)), pltpu.SemaphoreType.DMA],
        compiler_params=pltpu.CompilerParams(vmem_limit_bytes=VMEM_LIMIT_BYTES),
    )(x_shard)


def _other_chips(x, y):
    return [(1 - x, y), (x, 1 - y), (1 - x, 1 - y)]


_HBM = pl.BlockSpec(memory_space=pltpu.HBM)
_SEM = pl.BlockSpec(memory_space=pltpu.SEMAPHORE)
_ANY = pl.BlockSpec(memory_space=pl.ANY)
_EFFECT = pltpu.SideEffectType.DATAFLOW_SIDE_EFFECTING
TOKEN_SHAPE = (8, 128)


def _in_hbm(a):
    return pltpu.with_memory_space_constraint(a, pltpu.HBM)


def exchange_start(name, bufs, plan):
    n = len(bufs)
    n_copies = len(plan([None] * n, 0, 0, 0, dry=True))

    def body(*refs):
        ins = refs[:n]
        send_sems, recv_sems = refs[n], refs[n + 1]
        token = refs[-1]
        x, y, c = _my_pos()
        for i, (src, dst, to) in enumerate(plan(ins, x, y, c)):
            pltpu.make_async_remote_copy(src_ref=src, dst_ref=dst, send_sem=send_sems.at[i], recv_sem=recv_sems.at[i],
                                         device_id=to, device_id_type=MESH).start()
        token[...] = jnp.zeros_like(token)

    outs = pl.pallas_call(
        body, name=name,
        out_shape=(pltpu.SemaphoreType.DMA((n_copies,)), pltpu.SemaphoreType.DMA((n_copies,)),
                   *[pltpu.HBM(b.shape, b.dtype) for b in bufs], jax.ShapeDtypeStruct(TOKEN_SHAPE, F32)),
        in_specs=[_HBM] * n,
        out_specs=(_SEM, _SEM, *[_HBM] * n, pl.BlockSpec(memory_space=pltpu.VMEM)),
        input_output_aliases={i: 2 + i for i in range(n)},
        compiler_params=pltpu.CompilerParams(has_side_effects=_EFFECT),
    )(*[_in_hbm(b) for b in bufs])
    return outs[0], outs[1], list(outs[2:2 + n]), outs[-1]


def exchange_wait(name, send_sems, recv_sems, bufs, plan, after):
    n = len(bufs)

    def body(*refs):
        ins = refs[:n]
        send_sems, recv_sems = refs[n], refs[n + 1]
        x, y, c = _my_pos()
        for i, (src, dst, to) in enumerate(plan(ins, x, y, c, arriving=True)):
            cp = pltpu.make_async_remote_copy(src_ref=src, dst_ref=dst, send_sem=send_sems.at[i],
                                              recv_sem=recv_sems.at[i], device_id=to, device_id_type=MESH)
            cp.wait_send()
            cp.wait_recv()

    outs = pl.pallas_call(
        body, name=name,
        out_shape=tuple(pltpu.HBM(b.shape, b.dtype) for b in bufs),
        in_specs=[_HBM] * n + [_SEM, _SEM, _ANY],
        out_specs=tuple([_HBM] * n),
        input_output_aliases={i: i for i in range(n)},
        compiler_params=pltpu.CompilerParams(has_side_effects=_EFFECT),
    )(*bufs, send_sems, recv_sems, after)
    return list(outs)


def _gather_plan(refs, x, y, c, dry=False, arriving=False):
    if dry:
        return [None] * 3
    (land,) = refs
    j_me = 2 * x + y
    return [(land.at[j_me], land.at[(2 * px + py) if arriving else j_me], (px, py, c)) for px, py in _other_chips(x, y)]


def _sibling_plan(refs, x, y, c, dry=False, arriving=False):
    if dry:
        return [None]
    src, land = refs
    return [(src, land, (x, y, 1 - c))]


def _scatter_plan(n_pieces):
    def plan(refs, x, y, c, dry=False, arriving=False):
        if dry:
            return [None] * (3 * n_pieces)
        grads, lands = refs[:n_pieces], refs[n_pieces:]
        return [(grads[p].at[2 * px + py], lands[p].at[k], (px, py, c))
                for p in range(n_pieces) for k, (px, py) in enumerate(_other_chips(x, y))]
    return plan


def _view2d(a):
    return a.reshape(-1, a.shape[-1])


def _row_tile(rows, width, itemsize=4, budget=1 << 20):
    t = 8
    for cand in (1024, 512, 256, 128, 64, 32, 16, 8):
        if rows % cand == 0 and cand * width * itemsize <= budget:
            t = cand
            break
    return t if rows % t == 0 else rows


def cast_into_slot(name, w, l, j_idx, deps=()):
    w3 = w.reshape(w.shape[0], -1, w.shape[-1])
    _, R, W = w3.shape
    tr = _row_tile(R, W)

    def body(j_ref, a_ref, *rest):
        o_ref = rest[-1]
        o_ref[...] = a_ref[...].astype(BF16)

    return pl.pallas_call(
        body, name=name,
        grid_spec=pltpu.PrefetchScalarGridSpec(
            num_scalar_prefetch=1, grid=(R // tr,),
            in_specs=[pl.BlockSpec((None, tr, W), lambda i, j: (l, i, 0))] + [pl.BlockSpec(memory_space=pl.ANY)] * len(deps),
            out_specs=pl.BlockSpec((None, tr, W), lambda i, j: (j[0], i, 0))),
        out_shape=jax.ShapeDtypeStruct((N_CHIP, R, W), BF16), compiler_params=_cparams(("parallel",)),
    )(j_idx, w3, *deps)


def sum_parts(name, groups, j_idx):
    n = len(groups)
    _, R, W = groups[0][0].shape
    tr = _row_tile(R, W)

    def body(j_ref, *refs):
        o_ref = refs[-1]
        g = pl.program_id(0)
        for q in range(n):
            @pl.when(g == q)
            def _(q=q):
                own, got = refs[2 * q], refs[2 * q + 1]
                o_ref[...] = ((own[...].astype(F32) + got[0].astype(F32)) + got[1].astype(F32)) + got[2].astype(F32)

    in_specs = []
    for q in range(n):
        in_specs.append(pl.BlockSpec((None, tr, W), lambda g, i, j, q=q: (j[0], jnp.where(g == q, i, 0), 0)))
        in_specs.append(pl.BlockSpec((3, tr, W), lambda g, i, j, q=q: (0, jnp.where(g == q, i, 0), 0)))
    return pl.pallas_call(
        body, name=name,
        grid_spec=pltpu.PrefetchScalarGridSpec(
            num_scalar_prefetch=1, grid=(n, R // tr), in_specs=in_specs,
            out_specs=pl.BlockSpec((None, tr, W), lambda g, i, j: (g, i, 0))),
        out_shape=jax.ShapeDtypeStruct((n, R, W), F32), compiler_params=_cparams(("arbitrary", "arbitrary")),
    )(j_idx, *[a for pair in groups for a in pair])


def adamw(name, w, m, v, g_parts):
    shape = w.shape
    w2, m2, v2 = _view2d(w), _view2d(m), _view2d(v)
    gs = [_view2d(g) for g in g_parts]
    R, W = w2.shape
    tr = _row_tile(R, W, budget=1 << 19)
    ng = len(gs)
    bc1 = 1.0 - ADAM_B1 ** ADAM_STEP
    bc2 = 1.0 - ADAM_B2 ** ADAM_STEP

    def body(*refs):
        w_ref, m_ref, v_ref = refs[:3]
        g_refs = refs[3:3 + ng]
        go_ref, d_ref, mo_ref, vo_ref = refs[3 + ng:]
        g = g_refs[0][...]
        for r in g_refs[1:]:
            g = g + r[...]
        mn = ADAM_B1 * m_ref[...] + (1.0 - ADAM_B1) * g
        vn = ADAM_B2 * v_ref[...] + (1.0 - ADAM_B2) * (g * g)
        m_hat = mn / bc1
        v_hat = vn / bc2
        go_ref[...] = g
        d_ref[...] = -ADAM_LR * (m_hat / (jnp.sqrt(v_hat) + ADAM_EPS) + ADAM_WD * w_ref[...])
        mo_ref[...] = mn
        vo_ref[...] = vn

    spec = pl.BlockSpec((tr, W), lambda i: (i, 0))
    outs = pl.pallas_call(
        body, name=name, grid=(R // tr,),
        in_specs=[spec] * (3 + ng), out_specs=[spec] * 4,
        out_shape=[jax.ShapeDtypeStruct((R, W), F32)] * 4, compiler_params=_cparams(("parallel",)),
    )(w2, m2, v2, *gs)
    return tuple(o.reshape(shape) for o in outs)


class Dims:
    pass


def _sel(dm):
    return (pl.program_id(0) >= dm.nctx).astype(jnp.int32)


def norm_mod(name, h, gn, modtab, s, dm, deps=()):
    T, D = h.shape
    tm = dm.tme

    def body(h_ref, g_ref, m_ref, *rest):
        u_ref = rest[-1]
        sel = _sel(dm)
        x = h_ref[...]
        r = lax.rsqrt(jnp.mean(x * x, axis=-1, keepdims=True) + EPS)
        ng = x * r * g_ref[...]
        u_ref[...] = (ng * (1.0 + m_ref[sel, 3 * s + 1]) + m_ref[sel, 3 * s]).astype(u_ref.dtype)

    return pl.pallas_call(
        body, name=name, grid=(T // tm,),
        in_specs=[pl.BlockSpec((tm, D), lambda i: (i, 0)), pl.BlockSpec((1, D), lambda i: (0, 0)),
                  pl.BlockSpec((2, N_MOD, 1, D), lambda i: (0, 0, 0, 0))] + [_ANY] * len(deps),
        out_specs=pl.BlockSpec((tm, D), lambda i: (i, 0)),
        out_shape=jax.ShapeDtypeStruct((T, D), BF16), compiler_params=_cparams(("parallel",)),
    )(h, gn, modtab, *deps)


def norm_mod_bwd(name, du, h, gn, modtab, s, dh_in, dm):
    T, D = h.shape
    tm = dm.tme

    def body(du_ref, h_ref, g_ref, m_ref, dhi_ref, dh_ref, dmod_ref, dg_ref):
        i = pl.program_id(0)
        sel = _sel(dm)

        @pl.when(i == 0)
        def _():
            dmod_ref[...] = jnp.zeros_like(dmod_ref)
            dg_ref[...] = jnp.zeros_like(dg_ref)

        x = h_ref[...]
        r = lax.rsqrt(jnp.mean(x * x, axis=-1, keepdims=True) + EPS)
        n = x * r
        g = g_ref[...]
        du = du_ref[...]
        dmod_ref[sel, 0] += jnp.sum(du, axis=0, keepdims=True)
        dmod_ref[sel, 1] += jnp.sum(du * (n * g), axis=0, keepdims=True)
        dng = du * (1.0 + m_ref[sel, 3 * s + 1])
        dg_ref[...] += jnp.sum(dng * n, axis=0, keepdims=True)
        dn = dng * g
        dh_ref[...] = dhi_ref[...] + r * (dn - n * jnp.mean(dn * n, axis=-1, keepdims=True))

    row = pl.BlockSpec((tm, D), lambda i: (i, 0))
    return pl.pallas_call(
        body, name=name, grid=(T // tm,),
        in_specs=[row, row, pl.BlockSpec((1, D), lambda i: (0, 0)),
                  pl.BlockSpec((2, N_MOD, 1, D), lambda i: (0, 0, 0, 0)), row],
        out_specs=[row, pl.BlockSpec((2, 2, 1, D), lambda i: (0, 0, 0, 0)), pl.BlockSpec((1, D), lambda i: (0, 0))],
        out_shape=[jax.ShapeDtypeStruct((T, D), F32), jax.ShapeDtypeStruct((2, 2, 1, D), F32),
                   jax.ShapeDtypeStruct((1, D), F32)],
        compiler_params=_cparams(("arbitrary",)),
    )(du, h, gn, modtab, dh_in)


def resid(name, h, f, modtab, s, coef, dm):
    T, D = h.shape
    tm = dm.tme

    def body(h_ref, f_ref, m_ref, o_ref):
        o_ref[...] = h_ref[...] + (coef * m_ref[_sel(dm), 3 * s + 2]) * f_ref[...]

    row = pl.BlockSpec((tm, D), lambda i: (i, 0))
    return pl.pallas_call(
        body, name=name, grid=(T // tm,),
        in_specs=[row, row, pl.BlockSpec((2, N_MOD, 1, D), lambda i: (0, 0, 0, 0))], out_specs=row,
        out_shape=jax.ShapeDtypeStruct((T, D), F32), compiler_params=_cparams(("parallel",)),
    )(h, f, modtab)


def resid_bwd(name, dh, f, modtab, s, coef, dm, deps=()):
    T, D = dh.shape
    tm = dm.tme

    def body(dh_ref, f_ref, m_ref, *rest):
        df_ref, dg_ref = rest[-2:]
        sel = _sel(dm)

        @pl.when(pl.program_id(0) == 0)
        def _():
            dg_ref[...] = jnp.zeros_like(dg_ref)

        d = coef * dh_ref[...]
        df_ref[...] = (d * m_ref[sel, 3 * s + 2]).astype(df_ref.dtype)
        dg_ref[sel, 0] += jnp.sum(d * f_ref[...], axis=0, keepdims=True)

    row = pl.BlockSpec((tm, D), lambda i: (i, 0))
    return pl.pallas_call(
        body, name=name, grid=(T // tm,),
        in_specs=[row, row, pl.BlockSpec((2, N_MOD, 1, D), lambda i: (0, 0, 0, 0))] + [_ANY] * len(deps),
        out_specs=[row, pl.BlockSpec((2, 1, 1, D), lambda i: (0, 0, 0, 0))],
        out_shape=[jax.ShapeDtypeStruct((T, D), BF16), jax.ShapeDtypeStruct((2, 1, 1, D), F32)],
        compiler_params=_cparams(("arbitrary",)),
    )(dh, f, modtab, *deps)


def swiglu(name, gu, dm):
    _, T, Nq = gu.shape
    tm = dm.tme
    gu4 = gu.reshape(2, 2, T, Nq)

    def body(gu_ref, o_ref):
        g = gu_ref[0].astype(F32)
        o_ref[...] = (_silu(g) * gu_ref[1].astype(F32)).astype(o_ref.dtype)

    return pl.pallas_call(
        body, name=name, grid=(2, T // tm),
        in_specs=[pl.BlockSpec((2, None, tm, Nq), lambda k, i: (0, k, i, 0))],
        out_specs=pl.BlockSpec((None, tm, Nq), lambda k, i: (k, i, 0)),
        out_shape=jax.ShapeDtypeStruct((2, T, Nq), BF16), compiler_params=_cparams(("parallel", "parallel")),
    )(gu4)


def swiglu_bwd(name, dact, gu, dm):
    _, T, Nq = gu.shape
    tm = dm.tme
    gu4 = gu.reshape(2, 2, T, Nq)

    def body(da_ref, gu_ref, o_ref):
        g = gu_ref[0].astype(F32)
        da = da_ref[...].astype(F32)
        o_ref[0] = (da * gu_ref[1].astype(F32) * _dsilu(g)).astype(o_ref.dtype)
        o_ref[1] = (da * _silu(g)).astype(o_ref.dtype)

    out = pl.pallas_call(
        body, name=name, grid=(2, T // tm),
        in_specs=[pl.BlockSpec((None, tm, Nq), lambda k, i: (k, i, 0)),
                  pl.BlockSpec((2, None, tm, Nq), lambda k, i: (0, k, i, 0))],
        out_specs=pl.BlockSpec((2, None, tm, Nq), lambda k, i: (0, k, i, 0)),
        out_shape=jax.ShapeDtypeStruct((2, 2, T, Nq), BF16), compiler_params=_cparams(("parallel", "parallel")),
    )(dact, gu4)
    return out.reshape(4, T, Nq)


def _halo_specs(dm, width, col):
    tm = dm.tme
    per = tm // HALO
    last = dm.T // HALO - 1
    return [pl.BlockSpec((tm, width), lambda i: (i, col)),
            pl.BlockSpec((HALO, width), lambda i: (jnp.maximum(i * per - 1, 0), col)),
            pl.BlockSpec((HALO, width), lambda i: (jnp.minimum((i + 1) * per, last), col))]


def _segment_edges(dm, i):
    first = jnp.logical_or(i == 0, i == dm.nctx)
    last = jnp.logical_or(i == dm.nctx - 1, i == dm.nt - 1)
    return first, last


def _extend(main, prev, nxt, first, last):
    return jnp.concatenate([jnp.where(first, 0.0, prev), main, jnp.where(last, 0.0, nxt)], axis=0)


def _shift(ext, o, tm):
    n = ext.shape[0]
    rolled = ext if o == 0 else pltpu.roll(ext, (-o) % n, 0)
    return rolled[HALO:HALO + tm]


def _load_ext(refs, first, last):
    main, prev, nxt = refs
    return _extend(main[...].astype(F32), prev[...].astype(F32), nxt[...].astype(F32), first, last)


def rnn_conv(name, z, w, b, dm):
    T, RW, tm = dm.T, dm.RW, dm.tme

    def body(main, prev, nxt, w_ref, b_ref, o_ref):
        first, last = _segment_edges(dm, pl.program_id(0))
        ext = _load_ext((main, prev, nxt), first, last)
        acc = jnp.zeros((tm, RW), F32) + b_ref[...]
        for k in range(4):
            acc = acc + w_ref[k] * _shift(ext, k - 2, tm)
        o_ref[...] = acc

    return pl.pallas_call(
        body, name=name, grid=(dm.nt,),
        in_specs=_halo_specs(dm, RW, 0) + [pl.BlockSpec((4, 1, RW), lambda i: (0, 0, 0)),
                                           pl.BlockSpec((1, RW), lambda i: (0, 0))],
        out_specs=pl.BlockSpec((tm, RW), lambda i: (i, 0)),
        out_shape=jax.ShapeDtypeStruct((T, RW), F32), compiler_params=_cparams(("parallel",)),
    )(z, z, z, w, b)


def _blockdiag(x, w_ref):
    nb = w_ref.shape[0]
    outs = [jnp.dot(x[:, n * RNN_BLOCK:(n + 1) * RNN_BLOCK], w_ref[n].astype(BF16), preferred_element_type=F32)
            for n in range(nb)]
    return jnp.concatenate(outs, axis=-1)


def _lru_gates(xa, wa_ref, ba_ref, wx_ref, bx_ref, lam_ref):
    xb = xa.astype(BF16)
    r = jax.nn.sigmoid(_blockdiag(xb, wa_ref) + ba_ref[...])
    ig = jax.nn.sigmoid(_blockdiag(xb, wx_ref) + bx_ref[...])
    nl = -lam_ref[...]
    sp = jnp.maximum(nl, 0.0) + jnp.log(1.0 + jnp.exp(-jnp.abs(nl)))
    log_a = -LRU_C * r * sp
    a = jnp.exp(log_a)
    m = jnp.sqrt(-_expm1(2.0 * log_a))
    return r, ig, sp, a, m


def _lru_specs(l, d, nb, RW):
    wspec = pl.BlockSpec((None, None, nb, RNN_BLOCK, RNN_BLOCK), lambda i: (l, d, 0, 0, 0))
    vspec = pl.BlockSpec((None, None, 1, RW), lambda i: (l, d, 0, 0))
    return [wspec, vspec, wspec, vspec, vspec]


def lru_gates(name, xa, lw, l, d, dm):
    T, RW, tm = dm.T, dm.RW, dm.tme

    def body(xa_ref, wa_ref, ba_ref, wx_ref, bx_ref, lam_ref, a_ref, u_ref):
        xa_v = xa_ref[...]
        r, ig, sp, a, m = _lru_gates(xa_v, wa_ref, ba_ref, wx_ref, bx_ref, lam_ref)
        a_ref[...] = a
        u_ref[...] = m * (ig * xa_v)

    row = pl.BlockSpec((tm, RW), lambda i: (i, 0))
    return pl.pallas_call(
        body, name=name, grid=(dm.nt,),
        in_specs=[row] + _lru_specs(l, d, dm.NB, RW), out_specs=[row, row],
        out_shape=[jax.ShapeDtypeStruct((T, RW), F32)] * 2, compiler_params=_cparams(("parallel",)),
    )(xa, lw["w_a"], lw["b_a"], lw["w_x"], lw["b_x"], lw["lam"])


def lru_gates_bwd(name, xa, lw, l, d, du, dloga, dm):
    T, RW, tm, NB = dm.T, dm.RW, dm.tme, dm.NB

    def body(xa_ref, wa_ref, ba_ref, wx_ref, bx_ref, lam_ref, du_ref, dla_ref,
             dxa_ref, dwa_ref, dba_ref, dwx_ref, dbx_ref, dlam_ref):
        @pl.when(pl.program_id(0) == 0)
        def _():
            for ref in (dwa_ref, dba_ref, dwx_ref, dbx_ref, dlam_ref):
                ref[...] = jnp.zeros_like(ref)

        xa_v = xa_ref[...]
        r, ig, sp, a, m = _lru_gates(xa_v, wa_ref, ba_ref, wx_ref, bx_ref, lam_ref)
        duu = du_ref[...]
        dm_ = duu * (ig * xa_v)
        dig = duu * m * xa_v
        dxa = duu * m * ig
        dla = dla_ref[...] - dm_ * (a * a) / m
        dr = dla * (-LRU_C * sp)
        dsp = jnp.sum(dla * (-LRU_C * r), axis=0, keepdims=True)
        dlam_ref[...] += dsp * (-jax.nn.sigmoid(-lam_ref[...]))
        dpa = dr * r * (1.0 - r)
        dpx = dig * ig * (1.0 - ig)
        dba_ref[...] += jnp.sum(dpa, axis=0, keepdims=True)
        dbx_ref[...] += jnp.sum(dpx, axis=0, keepdims=True)
        xb, dpab, dpxb = xa_v.astype(BF16), dpa.astype(BF16), dpx.astype(BF16)
        back = []
        for n in range(NB):
            sl = slice(n * RNN_BLOCK, (n + 1) * RNN_BLOCK)
            dwa_ref[n] += lax.dot_general(xb[:, sl], dpab[:, sl], _DIMS["tn"], preferred_element_type=F32)
            dwx_ref[n] += lax.dot_general(xb[:, sl], dpxb[:, sl], _DIMS["tn"], preferred_element_type=F32)
            back.append(lax.dot_general(dpab[:, sl], wa_ref[n].astype(BF16), _DIMS["nt"], preferred_element_type=F32)
                        + lax.dot_general(dpxb[:, sl], wx_ref[n].astype(BF16), _DIMS["nt"], preferred_element_type=F32))
        dxa_ref[...] = dxa + jnp.concatenate(back, axis=-1)

    row = pl.BlockSpec((tm, RW), lambda i: (i, 0))
    wacc = pl.BlockSpec((NB, RNN_BLOCK, RNN_BLOCK), lambda i: (0, 0, 0))
    vacc = pl.BlockSpec((1, RW), lambda i: (0, 0))
    wshape = jax.ShapeDtypeStruct((NB, RNN_BLOCK, RNN_BLOCK), F32)
    vshape = jax.ShapeDtypeStruct((1, RW), F32)
    return pl.pallas_call(
        body, name=name, grid=(dm.nt,),
        in_specs=[row] + _lru_specs(l, d, NB, RW) + [row, row],
        out_specs=[row, wacc, vacc, wacc, vacc, vacc],
        out_shape=[jax.ShapeDtypeStruct((T, RW), F32), wshape, vshape, wshape, vshape, vshape],
        compiler_params=_cparams(("arbitrary",)),
    )(xa, lw["w_a"], lw["b_a"], lw["w_x"], lw["b_x"], lw["lam"], du, dloga)


def _chunk_order(dm, ctx_first, descending):
    nch, nctx = dm.nt, dm.nctx
    nlat = nch - nctx

    def order(s):
        if ctx_first and not descending:
            return s
        if not ctx_first and descending:
            return nch - 1 - s
        if ctx_first:
            return jnp.where(s < nctx, nctx - 1 - s, nch - 1 - (s - nctx))
        return jnp.where(s < nlat, nctx + s, s - nlat)

    return order


def _tile_scan(a, b, carry, descending):
    row = lax.broadcasted_iota(jnp.int32, a.shape, 0)
    for s in (1, 2, 4):
        sh = (HALO - s) if descending else s
        keep = (row < HALO - s) if descending else (row >= s)
        ap = pltpu.roll(a, sh, 0)
        bp = pltpu.roll(b, sh, 0)
        b = jnp.where(keep, b + a * bp, b)
        a = jnp.where(keep, a * ap, a)
    h = b + a * carry
    edge = 0 if descending else HALO - 1
    new_carry = jnp.sum(jnp.where(row == edge, h, 0.0), axis=0, keepdims=True)
    return h, new_carry


def lru_scan(name, a, u, ctx_first, descending, dm):
    T, RW, ch = dm.T, dm.RW, dm.tme
    order = _chunk_order(dm, ctx_first, descending)
    ngrp = ch // HALO

    def body(a_ref, u_ref, h_ref, carry_ref):
        @pl.when(pl.program_id(0) == 0)
        def _():
            carry_ref[...] = jnp.zeros_like(carry_ref)

        def step(g, carry):
            g = (ngrp - 1 - g) if descending else g
            rows = pl.ds(pl.multiple_of(g * HALO, HALO), HALO)
            h, carry = _tile_scan(a_ref[rows, :], u_ref[rows, :], carry, descending)
            h_ref[rows, :] = h
            return carry

        carry_ref[...] = lax.fori_loop(0, ngrp, step, carry_ref[...])

    row = pl.BlockSpec((ch, RW), lambda s: (order(s), 0))
    return pl.pallas_call(
        body, name=name, grid=(dm.nt,),
        in_specs=[row, row], out_specs=row,
        out_shape=jax.ShapeDtypeStruct((T, RW), F32),
        scratch_shapes=[pltpu.VMEM((1, RW), F32)], compiler_params=_cparams(("arbitrary",)),
    )(a, u)


def lru_scan_bwd(name, a, u, h, dh, ctx_first, descending, dm):
    T, RW, ch = dm.T, dm.RW, dm.tme
    order = _chunk_order(dm, ctx_first, descending)
    ngrp = ch // HALO

    def body(a_ref, u_ref, h_ref, dh_ref, lam_ref, dla_ref, carry_ref):
        @pl.when(pl.program_id(0) == 0)
        def _():
            carry_ref[...] = jnp.zeros_like(carry_ref)

        def step(g, carry):
            g = (ngrp - 1 - g) if descending else g
            rows = pl.ds(pl.multiple_of(g * HALO, HALO), HALO)
            a_v, dh_v = a_ref[rows, :], dh_ref[rows, :]
            mu, new_carry = _tile_scan(a_v, a_v * dh_v, carry, descending)
            row = lax.broadcasted_iota(jnp.int32, mu.shape, 0)
            if descending:
                nxt = jnp.where(row == HALO - 1, carry, pltpu.roll(mu, HALO - 1, 0))
            else:
                nxt = jnp.where(row == 0, carry, pltpu.roll(mu, 1, 0))
            lam = dh_v + nxt
            lam_ref[rows, :] = lam
            dla_ref[rows, :] = lam * (h_ref[rows, :] - u_ref[rows, :])
            return new_carry

        carry_ref[...] = lax.fori_loop(0, ngrp, step, carry_ref[...])

    row = pl.BlockSpec((ch, RW), lambda s: (order(s), 0))
    return pl.pallas_call(
        body, name=name, grid=(dm.nt,),
        in_specs=[row] * 4, out_specs=[row, row],
        out_shape=[jax.ShapeDtypeStruct((T, RW), F32)] * 2,
        scratch_shapes=[pltpu.VMEM((1, RW), F32)], compiler_params=_cparams(("arbitrary",)),
    )(a, u, h, dh)


def rnn_out(name, hf, hb, z, dm):
    T, RW, tm = dm.T, dm.RW, dm.tme

    def body(hf_ref, hb_ref, rg_ref, o_ref):
        o_ref[...] = ((hf_ref[...] + hb_ref[...]) * _gelu(rg_ref[...])).astype(o_ref.dtype)

    row = pl.BlockSpec((tm, RW), lambda i: (i, 0))
    return pl.pallas_call(
        body, name=name, grid=(dm.nt,),
        in_specs=[row, row, pl.BlockSpec((tm, RW), lambda i: (i, 1))], out_specs=row,
        out_shape=jax.ShapeDtypeStruct((T, RW), BF16), compiler_params=_cparams(("parallel",)),
    )(hf, hb, z)


def rnn_out_bwd(name, dya, hf, hb, z, dm):
    T, RW, tm = dm.T, dm.RW, dm.tme

    def body(d_ref, hf_ref, hb_ref, rg_ref, dh_ref, drg_ref):
        d, rg = d_ref[...], rg_ref[...]
        dh_ref[...] = d * _gelu(rg)
        drg_ref[...] = d * (hf_ref[...] + hb_ref[...]) * _dgelu(rg)

    row = pl.BlockSpec((tm, RW), lambda i: (i, 0))
    return pl.pallas_call(
        body, name=name, grid=(dm.nt,),
        in_specs=[row, row, row, pl.BlockSpec((tm, RW), lambda i: (i, 1))], out_specs=[row, row],
        out_shape=[jax.ShapeDtypeStruct((T, RW), F32)] * 2, compiler_params=_cparams(("parallel",)),
    )(dya, hf, hb, z)


def rnn_conv_bwd(name, dxa_f, dxa_b, drg, z, w, dz, dm):
    T, RW, tm = dm.T, dm.RW, dm.tme

    def body(f0, f1, f2, b0, b1, b2, x0, x1, x2, drg_ref, w_ref, dz_in, dz_ref, dw_ref, db_ref):
        i = pl.program_id(0)

        @pl.when(i == 0)
        def _():
            dw_ref[...] = jnp.zeros_like(dw_ref)
            db_ref[...] = jnp.zeros_like(db_ref)

        first, last = _segment_edges(dm, i)
        dext = _load_ext((f0, f1, f2), first, last) + _load_ext((b0, b1, b2), first, last)
        xext = _load_ext((x0, x1, x2), first, last)
        dmain = dext[HALO:HALO + tm]
        drx = jnp.zeros((tm, RW), F32)
        for k in range(4):
            drx = drx + w_ref[k] * _shift(dext, -(k - 2), tm)
            dw_ref[k] += jnp.sum(dmain * _shift(xext, k - 2, tm), axis=0, keepdims=True)
        db_ref[...] += jnp.sum(dmain, axis=0, keepdims=True)
        dz_ref[:, :RW] = drx.astype(dz_ref.dtype)
        dz_ref[:, RW:] = drg_ref[...].astype(dz_ref.dtype)

    return pl.pallas_call(
        body, name=name, grid=(dm.nt,),
        in_specs=_halo_specs(dm, RW, 0) * 3 + [pl.BlockSpec((tm, RW), lambda i: (i, 0)),
                                               pl.BlockSpec((4, 1, RW), lambda i: (0, 0, 0)),
                                               pl.BlockSpec(memory_space=pl.ANY)],
        out_specs=[pl.BlockSpec((tm, 2 * RW), lambda i: (i, 0)), pl.BlockSpec((4, 1, RW), lambda i: (0, 0, 0)),
                   pl.BlockSpec((1, RW), lambda i: (0, 0))],
        out_shape=[jax.ShapeDtypeStruct(dz.shape, dz.dtype), jax.ShapeDtypeStruct((4, 1, RW), F32),
                   jax.ShapeDtypeStruct((1, RW), F32)],
        input_output_aliases={11: 0}, compiler_params=_cparams(("arbitrary",)),
    )(dxa_f, dxa_f, dxa_f, dxa_b, dxa_b, dxa_b, z, z, z, drg, w, dz)


def short_conv(name, z, w, dm):
    T, RW, tm = dm.T, dm.RW, dm.tme

    def body(sb_ref, g0, g1, g2, x0, x1, x2, w_ref, o_ref):
        first, last = _segment_edges(dm, pl.program_id(0))
        pext = _load_ext((g0, g1, g2), first, last) * _load_ext((x0, x1, x2), first, last)
        cp = jnp.zeros((tm, RW), F32)
        for k in range(3):
            cp = cp + w_ref[k] * _shift(pext, k - 1, tm)
        o_ref[...] = (sb_ref[...] * cp).astype(o_ref.dtype)

    return pl.pallas_call(
        body, name=name, grid=(dm.nt,),
        in_specs=[pl.BlockSpec((tm, RW), lambda i: (i, 2))] + _halo_specs(dm, RW, 3) + _halo_specs(dm, RW, 4)
        + [pl.BlockSpec((3, 1, RW), lambda i: (0, 0, 0))],
        out_specs=pl.BlockSpec((tm, RW), lambda i: (i, 0)),
        out_shape=jax.ShapeDtypeStruct((T, RW), BF16), compiler_params=_cparams(("parallel",)),
    )(z, z, z, z, z, z, z, w)


def short_conv_bwd(name, dyb, z, w, dz, dm):
    T, RW, tm = dm.T, dm.RW, dm.tme

    def spec3(col):
        per = tm // HALO
        last = T // HALO - 1
        return [pl.BlockSpec((tm, RW), lambda i, p: (i, col)),
                pl.BlockSpec((HALO, RW), lambda i, p: (jnp.maximum(i * per - 1, 0), col)),
                pl.BlockSpec((HALO, RW), lambda i, p: (jnp.minimum((i + 1) * per, last), col))]

    def body(d0, d1, d2, s0, s1, s2, g0, g1, g2, x0, x1, x2, w_ref, dz_in, dz_ref, dw_ref):
        i, p = pl.program_id(0), pl.program_id(1)

        @pl.when(jnp.logical_and(i == 0, p == 0))
        def _():
            dw_ref[...] = jnp.zeros_like(dw_ref)

        first, last = _segment_edges(dm, i)
        gext = _load_ext((g0, g1, g2), first, last)
        xext = _load_ext((x0, x1, x2), first, last)
        pext = gext * xext
        dyext = _load_ext((d0, d1, d2), first, last)
        dcext = dyext * _load_ext((s0, s1, s2), first, last)
        dcmain = dcext[HALO:HALO + tm]
        cp = jnp.zeros((tm, RW), F32)
        dp = jnp.zeros((tm, RW), F32)
        for k in range(3):
            pk = _shift(pext, k - 1, tm)
            cp = cp + w_ref[k] * pk
            dp = dp + w_ref[k] * _shift(dcext, -(k - 1), tm)

            @pl.when(p == 0)
            def _(k=k, pk=pk):
                dw_ref[k] += jnp.sum(dcmain * pk, axis=0, keepdims=True)

        dsb = dyext[HALO:HALO + tm] * cp
        dscg = dp * xext[HALO:HALO + tm]
        dsx = dp * gext[HALO:HALO + tm]
        dz_ref[...] = jnp.where(p == 0, dsb, jnp.where(p == 1, dscg, dsx)).astype(dz_ref.dtype)

    return pl.pallas_call(
        body, name=name, grid=(dm.nt, 3),
        in_specs=spec3(0) + spec3(2) + spec3(3) + spec3(4)
        + [pl.BlockSpec((3, 1, RW), lambda i, p: (0, 0, 0)), pl.BlockSpec(memory_space=pl.ANY)],
        out_specs=[pl.BlockSpec((tm, RW), lambda i, p: (i, 2 + p)), pl.BlockSpec((3, 1, RW), lambda i, p: (0, 0, 0))],
        out_shape=[jax.ShapeDtypeStruct(dz.shape, dz.dtype), jax.ShapeDtypeStruct((3, 1, RW), F32)],
        input_output_aliases={13: 0}, compiler_params=_cparams(("arbitrary", "arbitrary")),
    )(dyb, dyb, dyb, z, z, z, z, z, z, z, z, z, w, dz)


def _rope_tables(dm):
    L, C = dm.L, dm.C
    half = HEAD_DIM // 2
    pos = jnp.arange(L)
    row = (pos // GRID_W).astype(F32)
    col = (pos % GRID_W).astype(F32)
    inv = ROPE_BASE ** (-jnp.arange(0, half, 2, dtype=F32) / half)
    ar, ac = row[:, None] * inv, col[:, None] * inv
    cos = jnp.concatenate([jnp.cos(ar), jnp.cos(ar), jnp.cos(ac), jnp.cos(ac)], axis=-1)
    sin = jnp.concatenate([-jnp.sin(ar), jnp.sin(ar), -jnp.sin(ac), jnp.sin(ac)], axis=-1)
    cos = jnp.concatenate([jnp.ones((C, HEAD_DIM), F32), cos], axis=0)
    sin = jnp.concatenate([jnp.zeros((C, HEAD_DIM), F32), sin], axis=0)
    return cos, sin


def _swap_pairs(x):
    quarter = HEAD_DIM // 4
    lane = lax.broadcasted_iota(jnp.int32, x.shape, 1)
    return jnp.where(lane % (2 * quarter) < quarter, pltpu.roll(x, HEAD_DIM - quarter, 1), pltpu.roll(x, quarter, 1))


def _rope(x, cos, sin):
    return x * cos + _swap_pairs(x) * sin


def _unrope(d, cos, sin):
    return d * cos + _swap_pairs(d * sin)


def qkv_prep(name, z, cos, sin, dm):
    T, tm, HQ, KW = dm.T, dm.tme, dm.HQ, dm.KW
    qcol, kcol = dm.off_q // HQ, dm.off_k // KW

    def body(q_ref, k_ref, v_ref, c_ref, s_ref, qo, ko, vo):
        cos_v, sin_v = c_ref[...], s_ref[...]
        for hd in range(HQ // HEAD_DIM):
            sl = slice(hd * HEAD_DIM, (hd + 1) * HEAD_DIM)
            qo[:, sl] = _rope(q_ref[:, sl], cos_v, sin_v).astype(qo.dtype)
        for hd in range(KW // HEAD_DIM):
            sl = slice(hd * HEAD_DIM, (hd + 1) * HEAD_DIM)
            ko[:, sl] = _rope(k_ref[:, sl], cos_v, sin_v).astype(ko.dtype)
        vo[...] = v_ref[...].astype(vo.dtype)

    tab = pl.BlockSpec((tm, HEAD_DIM), lambda i: (i, 0))
    return pl.pallas_call(
        body, name=name, grid=(dm.nt,),
        in_specs=[pl.BlockSpec((tm, HQ), lambda i: (i, qcol)), pl.BlockSpec((tm, KW), lambda i: (i, kcol)),
                  pl.BlockSpec((tm, KW), lambda i: (i, kcol + 1)), tab, tab],
        out_specs=[pl.BlockSpec((tm, HQ), lambda i: (i, 0)), pl.BlockSpec((tm, KW), lambda i: (i, 0)),
                   pl.BlockSpec((tm, KW), lambda i: (i, 0))],
        out_shape=[jax.ShapeDtypeStruct((T, HQ), BF16), jax.ShapeDtypeStruct((T, KW), BF16),
                   jax.ShapeDtypeStruct((T, KW), BF16)],
        compiler_params=_cparams(("parallel",)),
    )(z, z, z, cos, sin)


def qkv_bwd(name, dq, dk, dv, cos, sin, dz, dm):
    T, tm, HQ, KW = dm.T, dm.tme, dm.HQ, dm.KW
    nq = HQ // KW
    base = dm.off_q // KW

    def body(dq_ref, dk_ref, dv_ref, c_ref, s_ref, dz_in, dz_ref):
        p = pl.program_id(1)
        src = jnp.where(p < nq, dq_ref[...], jnp.where(p == nq, dk_ref[...], dv_ref[...]))
        cos_v, sin_v = c_ref[...], s_ref[...]
        is_v = p == nq + 1
        for hd in range(KW // HEAD_DIM):
            sl = slice(hd * HEAD_DIM, (hd + 1) * HEAD_DIM)
            dz_ref[:, sl] = jnp.where(is_v, src[:, sl], _unrope(src[:, sl], cos_v, sin_v)).astype(dz_ref.dtype)

    tab = pl.BlockSpec((tm, HEAD_DIM), lambda i, p: (i, 0))
    blk = pl.BlockSpec((tm, KW), lambda i, p: (i, 0))
    return pl.pallas_call(
        body, name=name, grid=(dm.nt, nq + 2),
        in_specs=[pl.BlockSpec((tm, KW), lambda i, p: (i, jnp.minimum(p, nq - 1))), blk, blk, tab, tab,
                  pl.BlockSpec(memory_space=pl.ANY)],
        out_specs=pl.BlockSpec((tm, KW), lambda i, p: (i, base + p)),
        out_shape=jax.ShapeDtypeStruct(dz.shape, dz.dtype),
        input_output_aliases={5: 0}, compiler_params=_cparams(("parallel", "arbitrary")),
    )(dq, dk, dv, cos, sin, dz)


def _attn_specs(dm):
    nC, nB, C, KW = dm.C // Q_BLOCK, dm.T // Q_BLOCK, dm.C, dm.KW

    def near(o):
        return lambda b: (jnp.clip(b + o, nC, nB - 1), 0)

    kv = [pl.BlockSpec((Q_BLOCK, KW), near(o)) for o in (-1, 0, 1)] + [pl.BlockSpec((C, KW), lambda b: (0, 0))]
    return kv


def _attn_mask(dm, b):
    nC, C, L = dm.C // Q_BLOCK, dm.C, dm.L
    span = 3 * Q_BLOCK
    n = b - nC
    iq = lax.broadcasted_iota(jnp.int32, (Q_BLOCK, span + C), 0)
    ik = lax.broadcasted_iota(jnp.int32, (Q_BLOCK, span + C), 1)
    kpos = n * Q_BLOCK + ik - Q_BLOCK
    qpos = n * Q_BLOCK + iq
    local = (b >= nC) & (jnp.abs(qpos - kpos) <= WINDOW) & (kpos >= 0) & (kpos < L)
    return jnp.logical_or(ik >= span, local)


def attention(name, q, k, v, sink, dm):
    T, HQ, KW = dm.T, dm.HQ, dm.KW
    H, KV = HQ // HEAD_DIM, KW // HEAD_DIM
    G = H // KV
    scale = HEAD_DIM ** -0.5

    def body(q_ref, kp, kc, kn, kx, vp, vc, vn, vx, sink_ref, o_ref, lse_ref):
        valid = _attn_mask(dm, pl.program_id(0))
        lane = lax.broadcasted_iota(jnp.int32, (Q_BLOCK, LSE_W), 1)
        lse_all = jnp.zeros((Q_BLOCK, LSE_W), F32)
        for kh in range(KV):
            ks = slice(kh * HEAD_DIM, (kh + 1) * HEAD_DIM)
            k_all = jnp.concatenate([kp[:, ks], kc[:, ks], kn[:, ks], kx[:, ks]], axis=0)
            v_all = jnp.concatenate([vp[:, ks], vc[:, ks], vn[:, ks], vx[:, ks]], axis=0)
            for g in range(G):
                hd = kh * G + g
                hs = slice(hd * HEAD_DIM, (hd + 1) * HEAD_DIM)
                s = lax.dot_general(q_ref[:, hs], k_all, _DIMS["nt"], preferred_element_type=F32) * scale
                s = jnp.where(valid, s, NEG_INF)
                snk = sink_ref[0, hd]
                mx = jnp.maximum(jnp.max(s, axis=-1, keepdims=True), snk)
                p = jnp.exp(s - mx)
                den = jnp.sum(p, axis=-1, keepdims=True) + jnp.exp(snk - mx)
                o = jnp.dot(p.astype(BF16), v_all, preferred_element_type=F32) / den
                o_ref[:, hs] = o.astype(o_ref.dtype)
                lse_all = jnp.where(lane == hd, mx + jnp.log(den), lse_all)
        lse_ref[...] = lse_all

    kv = _attn_specs(dm)
    return pl.pallas_call(
        body, name=name, grid=(T // Q_BLOCK,),
        in_specs=[pl.BlockSpec((Q_BLOCK, HQ), lambda b: (b, 0))] + kv + kv + [pl.BlockSpec(memory_space=pltpu.SMEM)],
        out_specs=[pl.BlockSpec((Q_BLOCK, HQ), lambda b: (b, 0)), pl.BlockSpec((Q_BLOCK, LSE_W), lambda b: (b, 0))],
        out_shape=[jax.ShapeDtypeStruct((T, HQ), BF16), jax.ShapeDtypeStruct((T, LSE_W), F32)],
        compiler_params=_cparams(("parallel",)),
    )(q, k, k, k, k, v, v, v, v, sink)


def attention_bwd(name, q, k, v, sink, o, lse, do, dm):
    T, HQ, KW, C = dm.T, dm.HQ, dm.KW, dm.C
    H, KV = HQ // HEAD_DIM, KW // HEAD_DIM
    G = H // KV
    nC, nB = C // Q_BLOCK, T // Q_BLOCK
    scale = HEAD_DIM ** -0.5
    span = 3 * Q_BLOCK

    def body(q_ref, kp, kc, kn, kx, vp, vc, vn, vx, sink_ref, o_ref, lse_ref, do_ref,
             dq_ref, dk_ref, dv_ref, ds_ref):
        b = pl.program_id(0)

        @pl.when(b == 0)
        def _():
            dk_ref[...] = jnp.zeros_like(dk_ref)
            dv_ref[...] = jnp.zeros_like(dv_ref)
            ds_ref[...] = jnp.zeros_like(ds_ref)

        valid = _attn_mask(dm, b)
        starts = [pl.multiple_of(jnp.clip(b + off, nC, nB - 1) * Q_BLOCK, Q_BLOCK) for off in (-1, 0, 1)]
        lane = lax.broadcasted_iota(jnp.int32, (Q_BLOCK, LSE_W), 1)
        lse_all = lse_ref[...]
        dsink = jnp.zeros((1, LSE_W), F32)
        for kh in range(KV):
            ks = slice(kh * HEAD_DIM, (kh + 1) * HEAD_DIM)
            k_all = jnp.concatenate([kp[:, ks], kc[:, ks], kn[:, ks], kx[:, ks]], axis=0)
            v_all = jnp.concatenate([vp[:, ks], vc[:, ks], vn[:, ks], vx[:, ks]], axis=0)
            dk_all = jnp.zeros((span + C, HEAD_DIM), F32)
            dv_all = jnp.zeros((span + C, HEAD_DIM), F32)
            for g in range(G):
                hd = kh * G + g
                hs = slice(hd * HEAD_DIM, (hd + 1) * HEAD_DIM)
                qh = q_ref[:, hs]
                doh = do_ref[:, hs]
                s = lax.dot_general(qh, k_all, _DIMS["nt"], preferred_element_type=F32) * scale
                s = jnp.where(valid, s, NEG_INF)
                lse_h = jnp.sum(jnp.where(lane == hd, lse_all, 0.0), axis=-1, keepdims=True)
                p = jnp.exp(s - lse_h)
                delta = jnp.sum(doh * o_ref[:, hs].astype(F32), axis=-1, keepdims=True)
                dob = doh.astype(BF16)
                dp = lax.dot_general(dob, v_all, _DIMS["nt"], preferred_element_type=F32)
                dsc = (p * (dp - delta) * scale).astype(BF16)
                dq_ref[:, hs] = jnp.dot(dsc, k_all, preferred_element_type=F32)
                dk_all = dk_all + lax.dot_general(dsc, qh, _DIMS["tn"], preferred_element_type=F32)
                dv_all = dv_all + lax.dot_general(p.astype(BF16), dob, _DIMS["tn"], preferred_element_type=F32)
                p_sink = jnp.exp(sink_ref[0, hd] - lse_h)
                dsink = dsink + jnp.where(lane[0:1] == hd, -jnp.sum(p_sink * delta), 0.0)
            for j, st in enumerate(starts):
                rows = pl.ds(st, Q_BLOCK)
                dk_ref[rows, ks] += dk_all[j * Q_BLOCK:(j + 1) * Q_BLOCK]
                dv_ref[rows, ks] += dv_all[j * Q_BLOCK:(j + 1) * Q_BLOCK]
            dk_ref[0:C, ks] += dk_all[span:]
            dv_ref[0:C, ks] += dv_all[span:]
        ds_ref[...] += dsink

    kv = _attn_specs(dm)
    qspec = pl.BlockSpec((Q_BLOCK, HQ), lambda b: (b, 0))
    full = pl.BlockSpec((T, KW), lambda b: (0, 0))
    return pl.pallas_call(
        body, name=name, grid=(nB,),
        in_specs=[qspec] + kv + kv + [pl.BlockSpec(memory_space=pltpu.SMEM), qspec,
                                      pl.BlockSpec((Q_BLOCK, LSE_W), lambda b: (b, 0)), qspec],
        out_specs=[qspec, full, full, pl.BlockSpec((1, LSE_W), lambda b: (0, 0))],
        out_shape=[jax.ShapeDtypeStruct((T, HQ), F32), jax.ShapeDtypeStruct((T, KW), F32),
                   jax.ShapeDtypeStruct((T, KW), F32), jax.ShapeDtypeStruct((1, LSE_W), F32)],
        compiler_params=_cparams(("arbitrary",)),
    )(q, k, k, k, k, v, v, v, v, sink, o, lse, do)


def merge(name, z, lifted, b_merge, dm):
    T, D, tm, cw = dm.T, dm.D, dm.tme, dm.cw
    gcol = dm.off_g // cw
    per = D // cw

    def body(g0, g1, g2, l0, l1, l2, b_ref, o_ref):
        acc = jnp.zeros((tm, cw), F32)
        for i, (g, lf) in enumerate(((g0, l0), (g1, l1), (g2, l2))):
            acc = acc + jax.nn.sigmoid(g[...] + b_ref[i]) * lf[...]
        o_ref[...] = acc.astype(o_ref.dtype)

    gspecs = [pl.BlockSpec((tm, cw), lambda i, j, br=br: (i, gcol + br * per + j)) for br in range(N_BRANCH)]
    blk = pl.BlockSpec((tm, cw), lambda i, j: (i, j))
    return pl.pallas_call(
        body, name=name, grid=(T // tm, per),
        in_specs=gspecs + [blk] * 3 + [pl.BlockSpec((N_BRANCH, 1, cw), lambda i, j: (0, 0, j))], out_specs=blk,
        out_shape=jax.ShapeDtypeStruct((T, D), BF16), compiler_params=_cparams(("parallel", "parallel")),
    )(z, z, z, *lifted, b_merge)


def merge_bwd(name, dmerged, z, lifted_br, b_merge, br, dz, dm):
    T, D, tm, cw = dm.T, dm.D, dm.tme, dm.cw
    gcol = dm.off_g // cw + br * (D // cw)
    per = D // cw

    def body(d_ref, g_ref, l_ref, b_ref, dz_in, dl_ref, dz_ref, db_ref):
        @pl.when(pl.program_id(1) == 0)
        def _():
            db_ref[...] = jnp.zeros_like(db_ref)

        d = d_ref[...]
        gate = jax.nn.sigmoid(g_ref[...] + b_ref[br])
        dl_ref[...] = (d * gate).astype(dl_ref.dtype)
        dg = d * l_ref[...] * gate * (1.0 - gate)
        dz_ref[...] = dg.astype(dz_ref.dtype)
        db_ref[...] += jnp.sum(dg, axis=0, keepdims=True)

    blk = pl.BlockSpec((tm, cw), lambda j, i: (i, j))
    zblk = pl.BlockSpec((tm, cw), lambda j, i: (i, gcol + j))
    return pl.pallas_call(
        body, name=name, grid=(per, T // tm),
        in_specs=[blk, zblk, blk, pl.BlockSpec((N_BRANCH, 1, cw), lambda j, i: (0, 0, j)),
                  pl.BlockSpec(memory_space=pl.ANY)],
        out_specs=[blk, zblk, pl.BlockSpec((1, cw), lambda j, i: (0, j))],
        out_shape=[jax.ShapeDtypeStruct((T, D), BF16), jax.ShapeDtypeStruct(dz.shape, dz.dtype),
                   jax.ShapeDtypeStruct((1, D), F32)],
        input_output_aliases={4: 1}, compiler_params=_cparams(("parallel", "arbitrary")),
    )(dmerged, z, lifted_br, b_merge, dz)


def loss_head(name, h, gf, target, dm):
    T, D, tm, nctx = dm.T, dm.D, dm.tme, dm.nctx

    def body(h_ref, g_ref, t_ref, dh_ref, loss_ref, dg_ref):
        i = pl.program_id(0)

        @pl.when(i == 0)
        def _():
            loss_ref[...] = jnp.zeros_like(loss_ref)
            dg_ref[...] = jnp.zeros_like(dg_ref)

        @pl.when(i < nctx)
        def _():
            dh_ref[...] = jnp.zeros_like(dh_ref)

        @pl.when(i >= nctx)
        def _():
            x = h_ref[...]
            r = lax.rsqrt(jnp.mean(x * x, axis=-1, keepdims=True) + EPS)
            n = x * r
            g = g_ref[...]
            err = n * g - t_ref[...]
            loss_ref[...] += jnp.sum(err * err) * (0.5 / D)
            dy = err * (1.0 / D)
            dg_ref[...] += jnp.sum(dy * n, axis=0, keepdims=True)
            dn = dy * g
            dh_ref[...] = r * (dn - n * jnp.mean(dn * n, axis=-1, keepdims=True))

    row = pl.BlockSpec((tm, D), lambda i: (i, 0))
    return pl.pallas_call(
        body, name=name, grid=(T // tm,),
        in_specs=[row, pl.BlockSpec((1, D), lambda i: (0, 0)),
                  pl.BlockSpec((tm, D), lambda i: (jnp.maximum(i - nctx, 0), 0))],
        out_specs=[row, pl.BlockSpec((1, 128), lambda i: (0, 0)), pl.BlockSpec((1, D), lambda i: (0, 0))],
        out_shape=[jax.ShapeDtypeStruct((T, D), F32), jax.ShapeDtypeStruct((1, 128), F32),
                   jax.ShapeDtypeStruct((1, D), F32)],
        compiler_params=_cparams(("arbitrary",)),
    )(h, gf, target)


_HI = lax.Precision.HIGHEST
ADA_ROWS = 16


def ada_forward(name, cond, ada_w, bias):
    _, D, Aq = ada_w.shape
    tc = _pick(Aq, (1536, 1152, 768, 512, 384, 256, 128))
    tk = _ktile(D)
    nk = D // tk

    def body(c_ref, w_ref, b_ref, o_ref):
        k = pl.program_id(2)

        @pl.when(k == 0)
        def _():
            o_ref[...] = jnp.zeros_like(o_ref) + b_ref[...]

        o_ref[...] += jnp.dot(_silu(c_ref[...]), w_ref[...], precision=_HI, preferred_element_type=F32)

    return pl.pallas_call(
        body, name=name, grid=(2, Aq // tc, nk),
        in_specs=[pl.BlockSpec((ADA_ROWS, tk), lambda l, j, k: (0, k)),
                  pl.BlockSpec((None, tk, tc), lambda l, j, k: (l, k, j)),
                  pl.BlockSpec((None, 1, tc), lambda l, j, k: (l, 0, j))],
        out_specs=pl.BlockSpec((None, ADA_ROWS, tc), lambda l, j, k: (l, 0, j)),
        out_shape=jax.ShapeDtypeStruct((2, ADA_ROWS, Aq), F32),
        compiler_params=_cparams(("parallel", "parallel", "arbitrary")),
    )(cond, ada_w, bias)


def ada_cond_grad(name, dmod, ada_w):
    _, D, Aq = ada_w.shape
    tc = _pick(Aq, (1536, 1152, 768, 512, 384, 256, 128))
    tn = _ktile(D)
    nc = Aq // tc

    def body(d_ref, w_ref, o_ref):
        @pl.when(jnp.logical_and(pl.program_id(1) == 0, pl.program_id(2) == 0))
        def _():
            o_ref[...] = jnp.zeros_like(o_ref)

        o_ref[...] += lax.dot_general(d_ref[...], w_ref[...], _DIMS["nt"], precision=_HI, preferred_element_type=F32)

    return pl.pallas_call(
        body, name=name, grid=(D // tn, 2, nc),
        in_specs=[pl.BlockSpec((None, ADA_ROWS, tc), lambda j, l, c: (l, 0, c)),
                  pl.BlockSpec((None, tn, tc), lambda j, l, c: (l, j, c))],
        out_specs=pl.BlockSpec((ADA_ROWS, tn), lambda j, l, c: (0, j)),
        out_shape=jax.ShapeDtypeStruct((ADA_ROWS, D), F32),
        compiler_params=_cparams(("parallel", "arbitrary", "arbitrary")),
    )(dmod, ada_w)


def ada_update(name, cond, dmod, w, m, v):
    _, D, Aq = w.shape
    tc = _pick(Aq, (1536, 1152, 768, 512, 384, 256, 128))
    tr = 128 if D % 128 == 0 else D
    bc1 = 1.0 - ADAM_B1 ** ADAM_STEP
    bc2 = 1.0 - ADAM_B2 ** ADAM_STEP

    def body(c_ref, d_ref, w_ref, m_ref, v_ref, go_ref, dl_ref, mo_ref, vo_ref):
        g = lax.dot_general(_silu(c_ref[...]), d_ref[...], _DIMS["tn"], precision=_HI, preferred_element_type=F32)
        mn = ADAM_B1 * m_ref[...] + (1.0 - ADAM_B1) * g
        vn = ADAM_B2 * v_ref[...] + (1.0 - ADAM_B2) * (g * g)
        go_ref[...] = g
        dl_ref[...] = -ADAM_LR * ((mn / bc1) / (jnp.sqrt(vn / bc2) + ADAM_EPS) + ADAM_WD * w_ref[...])
        mo_ref[...] = mn
        vo_ref[...] = vn

    blk = pl.BlockSpec((None, tr, tc), lambda l, i, j: (l, i, j))
    return pl.pallas_call(
        body, name=name, grid=(2, D // tr, Aq // tc),
        in_specs=[pl.BlockSpec((ADA_ROWS, tr), lambda l, i, j: (0, i)),
                  pl.BlockSpec((None, ADA_ROWS, tc), lambda l, i, j: (l, 0, j)), blk, blk, blk],
        out_specs=[blk] * 4, out_shape=[jax.ShapeDtypeStruct(w.shape, F32)] * 4,
        compiler_params=_cparams(("parallel", "parallel", "parallel")),
    )(cond, dmod, w, m, v)


def dmod_assemble(name, gathered):
    A = gathered.shape[-1]
    tc = _pick(A, (2048, 1024, 512, 256, 128))

    def body(g_ref, o_ref, b_ref):
        ctx = g_ref[0, 1]
        for dev in range(1, N_DEV):
            ctx = ctx + g_ref[dev, 1]
        tot = ctx
        for dev in range(N_DEV):
            o_ref[dev:dev + 1, :] = g_ref[dev, 0]
            tot = tot + g_ref[dev, 0]
        o_ref[N_DEV:N_DEV + 1, :] = ctx
        o_ref[N_DEV + 1:, :] = jnp.zeros((ADA_ROWS - N_DEV - 1, tc), F32)
        b_ref[...] = tot

    return pl.pallas_call(
        body, name=name, grid=(2, A // tc),
        in_specs=[pl.BlockSpec((N_DEV, None, 2, 1, tc), lambda l, j: (0, l, 0, 0, j))],
        out_specs=[pl.BlockSpec((None, ADA_ROWS, tc), lambda l, j: (l, 0, j)),
                   pl.BlockSpec((None, 1, tc), lambda l, j: (l, 0, j))],
        out_shape=[jax.ShapeDtypeStruct((2, ADA_ROWS, A), F32), jax.ShapeDtypeStruct((2, 1, A), F32)],
        compiler_params=_cparams(("parallel", "parallel")),
    )(gathered)


def sum_devices(name, gathered):
    _, R, W = gathered.shape
    tr = _row_tile(R, W, budget=1 << 18)

    def body(g_ref, all_ref, chip_ref):
        even = g_ref[0]
        odd = g_ref[1]
        for dev in range(2, N_DEV, 2):
            even = even + g_ref[dev]
            odd = odd + g_ref[dev + 1]
        all_ref[...] = even + odd
        chip_ref[...] = even

    blk = pl.BlockSpec((tr, W), lambda i: (i, 0))
    return pl.pallas_call(
        body, name=name, grid=(R // tr,),
        in_specs=[pl.BlockSpec((N_DEV, tr, W), lambda i: (0, i, 0))], out_specs=[blk, blk],
        out_shape=[jax.ShapeDtypeStruct((R, W), F32)] * 2, compiler_params=_cparams(("parallel",)),
    )(gathered)


PACK_ROWS = 1024


def _pack(arrays):
    flat = jnp.concatenate([a.reshape(-1).astype(F32) for a in arrays])
    pad = (-flat.shape[0]) % (PACK_ROWS * 128)
    return jnp.pad(flat, (0, pad)).reshape(-1, 128)


def _unpack(buf, shapes, lead=()):
    flat = buf.reshape(lead + (-1,))
    out, start = [], 0
    for s in shapes:
        n = math.prod(s)
        out.append(flat[..., start:start + n].reshape(lead + tuple(s)))
        start += n
    return out


def _unshard_last(g):
    g = jnp.moveaxis(g, 0, -2)
    return g.reshape(g.shape[:-2] + (g.shape[-2] * g.shape[-1],))


class WeightStream:
    AHEAD = 2

    def __init__(self, keys, make_land):
        self.keys, self.make_land = list(keys), make_land
        self.pending, self.values, self.tokens, self.started = {}, {}, [], 0
        for _ in range(self.AHEAD):
            self._start_next(())

    def _start_next(self, deps):
        if self.started < len(self.keys):
            key = self.keys[self.started]
            self.started += 1
            land, view = self.make_land(key, deps)
            ss, rs, (land,), token = exchange_start(f"{key}_start", [land], _gather_plan)
            self.pending[key] = (ss, rs, land, view)
            self.tokens.append(token)

    def get(self, key, after=None):
        if key not in self.values:
            ss, rs, land, view = self.pending.pop(key)
            (full,) = exchange_wait(f"{key}_wait", ss, rs, [land], _gather_plan, after)
            self.values[key] = full if view is None else full.reshape(view)
            self._start_next((full,))
        return self.values[key]

    def take_tokens(self):
        out, self.tokens = tuple(self.tokens), []
        return out


def _ffn_forward(tag, h, gn, modtab, s, ws, k13, k2, dm):
    u = norm_mod(f"{tag}_norm", h, gn, modtab, s, dm)
    w13g = ws.get(k13, u)
    gu = mm_cols(f"{tag}_w13", u, w13g, BF16, flat=False, deps=ws.take_tokens())
    act = swiglu(f"{tag}_act", gu, dm)
    w2g = ws.get(k2, act)
    f = mm_rows(f"{tag}_w2", act, w2g, deps=ws.take_tokens())
    h_out = resid(f"{tag}_res", h, f, modtab, s, 0.5, dm)
    return h_out, (h, u, gu, act, f)


def _ffn_backward(tag, dh, saved, gn, modtab, s, ws, k13, k2, dm, deps=()):
    h, u, gu, act, f = saved
    w13g, w2g = ws.get(k13), ws.get(k2)
    df, dgate = resid_bwd(f"{tag}_res_bwd", dh, f, modtab, s, 0.5, dm, deps)
    dact = mm_rows_t(f"{tag}_dact", df, w2g)
    dw2 = mm_rows_grad(f"{tag}_dw2", act, df).reshape(N_CHIP, -1, df.shape[-1])
    dgu = swiglu_bwd(f"{tag}_act_bwd", dact, gu, dm)
    du = mm_cols_t(f"{tag}_du", dgu, w13g, flat=False)
    dw13 = mm_cols_grad(f"{tag}_dw13", u, dgu, flat=False)
    dh_in, dss, dgn = norm_mod_bwd(f"{tag}_norm_bwd", du, h, gn, modtab, s, dh, dm)
    return dh_in, dw13, dw2, dss, dgate, dgn


def kernel(x, c, ctx, c_ctx, ada_w, ada_b, norm_g, ffn1_w13, ffn1_w2, w_in, b_merge, rnn_conv_w, rnn_conv_b, lru_w_a, lru_b_a, lru_w_x, lru_b_x, lru_lambda, sc_conv_w, attn_sink, w_branch, w_out, ffn2_w13, ffn2_w2, final_norm_g, loss_target, m_c_ctx, m_ada_w, m_ada_b, m_norm_g, m_ffn1_w13, m_ffn1_w2, m_w_in, m_b_merge, m_rnn_conv_w, m_rnn_conv_b, m_lru_w_a, m_lru_b_a, m_lru_w_x, m_lru_b_x, m_lru_lambda, m_sc_conv_w, m_attn_sink, m_w_branch, m_w_out, m_ffn2_w13, m_ffn2_w2, m_final_norm_g, v_c_ctx, v_ada_w, v_ada_b, v_norm_g, v_ffn1_w13, v_ffn1_w2, v_w_in, v_b_merge, v_rnn_conv_w, v_rnn_conv_b, v_lru_w_a, v_lru_b_a, v_lru_w_x, v_lru_b_x, v_lru_lambda, v_sc_conv_w, v_attn_sink, v_w_branch, v_w_out, v_ffn2_w13, v_ffn2_w2, v_final_norm_g):
    weights = dict(c_ctx=c_ctx, ada_w=ada_w, ada_b=ada_b, norm_g=norm_g, ffn1_w13=ffn1_w13, ffn1_w2=ffn1_w2, w_in=w_in,
                   b_merge=b_merge, rnn_conv_w=rnn_conv_w, rnn_conv_b=rnn_conv_b, lru_w_a=lru_w_a, lru_b_a=lru_b_a,
                   lru_w_x=lru_w_x, lru_b_x=lru_b_x, lru_lambda=lru_lambda, sc_conv_w=sc_conv_w, attn_sink=attn_sink,
                   w_branch=w_branch, w_out=w_out, ffn2_w13=ffn2_w13, ffn2_w2=ffn2_w2, final_norm_g=final_norm_g)
    mom_m = dict(c_ctx=m_c_ctx, ada_w=m_ada_w, ada_b=m_ada_b, norm_g=m_norm_g, ffn1_w13=m_ffn1_w13, ffn1_w2=m_ffn1_w2,
                 w_in=m_w_in, b_merge=m_b_merge, rnn_conv_w=m_rnn_conv_w, rnn_conv_b=m_rnn_conv_b, lru_w_a=m_lru_w_a,
                 lru_b_a=m_lru_b_a, lru_w_x=m_lru_w_x, lru_b_x=m_lru_b_x, lru_lambda=m_lru_lambda,
                 sc_conv_w=m_sc_conv_w, attn_sink=m_attn_sink, w_branch=m_w_branch, w_out=m_w_out,
                 ffn2_w13=m_ffn2_w13, ffn2_w2=m_ffn2_w2, final_norm_g=m_final_norm_g)
    mom_v = dict(c_ctx=v_c_ctx, ada_w=v_ada_w, ada_b=v_ada_b, norm_g=v_norm_g, ffn1_w13=v_ffn1_w13, ffn1_w2=v_ffn1_w2,
                 w_in=v_w_in, b_merge=v_b_merge, rnn_conv_w=v_rnn_conv_w, rnn_conv_b=v_rnn_conv_b, lru_w_a=v_lru_w_a,
                 lru_b_a=v_lru_b_a, lru_w_x=v_lru_w_x, lru_b_x=v_lru_b_x, lru_lambda=v_lru_lambda,
                 sc_conv_w=v_sc_conv_w, attn_sink=v_attn_sink, w_branch=v_w_branch, w_out=v_w_out,
                 ffn2_w13=v_ffn2_w13, ffn2_w2=v_ffn2_w2, final_norm_g=v_final_norm_g)
    order = list(weights)

    dm = Dims()
    dm.D = D = x.shape[-1]
    dm.L = L = x.shape[1]
    dm.C = C = ctx.shape[1]
    dm.T = T = L + C
    dm.RW = RW = rnn_conv_b.shape[-1]
    dm.NB = lru_w_a.shape[2]
    H = attn_sink.shape[-1]
    dm.HQ = HQ = H * HEAD_DIM
    NZ = w_in.shape[-1] * N_CHIP
    dm.KW = KW = (NZ - 5 * RW - HQ - N_BRANCH * D) // 2
    dm.off_q = 5 * RW
    dm.off_k = dm.off_q + HQ
    dm.off_g = dm.off_k + 2 * KW
    dm.tme = _pick(C, (256, 128))
    dm.nt = T // dm.tme
    dm.nctx = C // dm.tme
    dm.cw = next(w for w in (512, 256, 128) if dm.off_g % w == 0 and D % w == 0)
    A = ada_b.shape[-1]
    Aq = ada_w.shape[-1]
    assert dm.off_q % HQ == 0 and dm.off_k % KW == 0 and HQ % KW == 0 and L % dm.tme == 0 and RW == HQ
    assert C % Q_BLOCK == 0 and L % Q_BLOCK == 0 and D % N_CHIP == 0 and A == N_MOD * D

    mx, my, mc = _my_pos()
    j_me = 2 * mx + my
    b_me = 4 * mx + 2 * my + mc

    big = ["ffn1_w13", "ffn1_w2", "w_in", "w_branch", "w_out", "ffn2_w13", "ffn2_w2"]
    j_idx = jnp.reshape(j_me, (1,)).astype(jnp.int32)
    FFq = ffn1_w2.shape[1]
    views = {"ffn1_w2": (2, 2 * FFq, D), "ffn2_w2": (2, 2 * FFq, D), "w_out": (D, D),
             "w_branch": (N_CHIP, N_BRANCH, RW, D // N_CHIP)}

    def make_land(key, deps):
        l, n = int(key[1]), key[3:]
        return cast_into_slot(f"{key}_cast", weights[n], l, j_idx, deps), views.get(n)

    ws = WeightStream([f"l{l}_{n}" for l in range(2) for n in big], make_land)
    first_tokens = ws.take_tokens()

    small_sharded = ["norm_g", "b_merge", "rnn_conv_w", "lru_b_a", "lru_b_x", "lru_lambda", "sc_conv_w"]
    pack1 = _pack([c] + [weights[n] for n in small_sharded])
    pack1 = pack1 + sum(t[0, 0] for t in first_tokens)
    g1 = allgather8("gather_small_params", pack1).reshape(N_DEV, -1, 128)
    parts = _unpack(g1, [c.shape] + [weights[n].shape for n in small_sharded], lead=(N_DEV,))
    c_all = parts[0].reshape(N_DEV, D)
    full = {n: _unshard_last(p[0::2]) for n, p in zip(small_sharded, parts[1:])}
    cond = jnp.concatenate([c_all, c_ctx[None, :], jnp.zeros((ADA_ROWS - N_DEV - 1, D), F32)], axis=0)

    bias_q = lax.dynamic_slice_in_dim(ada_b, j_me * Aq, Aq, axis=1)[:, None, :]
    mod_q = ada_forward("ada_forward", cond, ada_w, bias_q)
    g2 = allgather8("gather_mod", mod_q.reshape(-1, 128)).reshape(N_DEV, 2, ADA_ROWS, Aq)
    mod_full = _unshard_last(g2[0::2])
    mod_lat = lax.dynamic_index_in_dim(mod_full, b_me, axis=1, keepdims=False)
    mod_ctx = mod_full[:, N_DEV]
    modtabs = [jnp.stack([mod_ctx[l], mod_lat[l]]).reshape(2, N_MOD, 1, D) for l in range(2)]

    cos, sin = _rope_tables(dm)
    sink = attn_sink.reshape(2, 1, H)
    lw = dict(w_a=lru_w_a, w_x=lru_w_x,
              b_a=full["lru_b_a"][:, :, None, :], b_x=full["lru_b_x"][:, :, None, :],
              lam=full["lru_lambda"][:, :, None, :])
    gn = full["norm_g"]
    bm = full["b_merge"][:, :, None, :]
    rcw = full["rnn_conv_w"][:, :, None, :]
    scw = full["sc_conv_w"][:, :, None, :]

    h = jnp.concatenate([ctx[0], x[0]], axis=0)
    saved = []
    for l in range(2):
        mt = modtabs[l]
        sv = {}
        h, sv["ffn1"] = _ffn_forward(f"l{l}_ffn1", h, gn[l, 0:1], mt, 0, ws, f"l{l}_ffn1_w13", f"l{l}_ffn1_w2", dm)
        sv["h_mix"] = h
        u = norm_mod(f"l{l}_mix_norm", h, gn[l, 1:2], mt, 1, dm)
        wing = ws.get(f"l{l}_w_in", u)
        z = mm_cols(f"l{l}_w_in", u, wing, F32, flat=True, deps=ws.take_tokens())
        xa = rnn_conv(f"l{l}_rnn_conv", z, rcw[l], rnn_conv_b[l][None, :], dm)
        scans = []
        for d in range(2):
            a_d, u_d = lru_gates(f"l{l}_lru_gates{d}", xa, lw, l, d, dm)
            h_d = lru_scan(f"l{l}_lru_scan{d}", a_d, u_d, True, d == 1, dm)
            scans.append((a_d, u_d, h_d))
        ya = rnn_out(f"l{l}_rnn_out", scans[0][2], scans[1][2], z, dm)
        yb = short_conv(f"l{l}_short_conv", z, scw[l], dm)
        qr, kr, vv = qkv_prep(f"l{l}_qkv", z, cos, sin, dm)
        yatt, lse = attention(f"l{l}_attn", qr, kr, vv, sink[l], dm)
        ys = (ya, yb, yatt)
        wbg = ws.get(f"l{l}_w_branch", yatt)
        lifted = [mm_branch(f"l{l}_lift{br}", ys[br], wbg, br, deps=ws.take_tokens()) for br in range(N_BRANCH)]
        merged = merge(f"l{l}_merge", z, lifted, bm[l], dm)
        woutg = ws.get(f"l{l}_w_out", merged)
        y = mm_plain(f"l{l}_w_out", merged, woutg, "nn", F32, deps=ws.take_tokens())
        h = resid(f"l{l}_mix_res", h, y, mt, 1, 1.0, dm)
        sv.update(u=u, z=z, xa=xa, scans=scans, ys=ys, qkv=(qr, kr, vv), lse=lse, lifted=lifted, merged=merged, y=y)
        h, sv["ffn2"] = _ffn_forward(f"l{l}_ffn2", h, gn[l, 2:3], mt, 2, ws, f"l{l}_ffn2_w13", f"l{l}_ffn2_w2", dm)
        saved.append(sv)

    dh, loss_vec, d_final_g = loss_head("loss_head", h, final_norm_g[None, :], loss_target[0], dm)
    loss = lax.psum(loss_vec[0, 0], ("x", "y", "c"))

    small = {n: [None, None] for n in ["norm_g", "b_merge", "rnn_conv_w", "rnn_conv_b", "lru_w_a", "lru_b_a", "lru_w_x",
                                       "lru_b_x", "lru_lambda", "sc_conv_w", "attn_sink"]}
    dmods = [None, None]
    scatters = []

    def scatter(name, keyed):
        grads3 = [g.reshape(N_CHIP, -1, g.shape[-1]) for g in keyed.values()]
        lands = [lax.empty((3,) + g.shape[1:], BF16) for g in grads3]
        ss, rs, bufs, token = exchange_start(f"{name}_start", grads3 + lands, _scatter_plan(len(grads3)))
        scatters.append((name, ss, rs, bufs, list(keyed)))
        return token

    tok = ()
    for l in (1, 0):
        mt = modtabs[l]
        sv = saved[l]
        dh, dw13, dw2, dss2, dgate2, dgn2 = _ffn_backward(f"l{l}_ffn2", dh, sv["ffn2"], gn[l, 2:3], mt, 2,
                                                         ws, f"l{l}_ffn2_w13", f"l{l}_ffn2_w2", dm, deps=tok)
        tok = (scatter(f"l{l}_scatter_ffn2", {("ffn2_w13", l): dw13, ("ffn2_w2", l): dw2}),)

        dyg, dgate1 = resid_bwd(f"l{l}_mix_res_bwd", dh, sv["y"], mt, 1, 1.0, dm, deps=tok)
        woutg, wbg, wing = ws.get(f"l{l}_w_out"), ws.get(f"l{l}_w_branch"), ws.get(f"l{l}_w_in")
        dmerged = mm_plain(f"l{l}_dmerged", dyg, woutg, "nt", F32)
        mix_grads = {("w_out", l): mm_plain_grad(f"l{l}_dw_out", sv["merged"], dyg)}
        dz = lax.empty((T, NZ), BF16)
        dys, dbm = [], []
        for br in range(N_BRANCH):
            dl, dz, db = merge_bwd(f"l{l}_merge_bwd{br}", dmerged, sv["z"], sv["lifted"][br], bm[l], br, dz, dm)
            dys.append(mm_branch_t(f"l{l}_dy{br}", dl, wbg, br))
            mix_grads[("w_branch", l, br)] = mm_branch_grad(f"l{l}_dwb{br}", sv["ys"][br], dl)
            dbm.append(db)
        small["b_merge"][l] = jnp.concatenate(dbm, axis=0)

        qr, kr, vv = sv["qkv"]
        dq, dk, dv, dsink = attention_bwd(f"l{l}_attn_bwd", qr, kr, vv, sink[l], sv["ys"][2], sv["lse"], dys[2], dm)
        dz = qkv_bwd(f"l{l}_qkv_bwd", dq, dk, dv, cos, sin, dz, dm)
        small["attn_sink"][l] = dsink[0, :H]

        dz, dscw = short_conv_bwd(f"l{l}_short_conv_bwd", dys[1], sv["z"], scw[l], dz, dm)
        small["sc_conv_w"][l] = dscw[:, 0]

        (a0, u0, h0), (a1, u1, h1) = sv["scans"]
        dhs, drg = rnn_out_bwd(f"l{l}_rnn_out_bwd", dys[0], h0, h1, sv["z"], dm)
        dxa, lru_sums = [], []
        for d, (a_d, u_d, h_d) in enumerate(sv["scans"]):
            lam_d, dla_d = lru_scan_bwd(f"l{l}_lru_scan_bwd{d}", a_d, u_d, h_d, dhs, False, d == 0, dm)
            outs = lru_gates_bwd(f"l{l}_lru_gates_bwd{d}", sv["xa"], lw, l, d, lam_d, dla_d, dm)
            dxa.append(outs[0])
            lru_sums.append(outs[1:])
        dz, drcw, drcb = rnn_conv_bwd(f"l{l}_rnn_conv_bwd", dxa[0], dxa[1], drg, sv["z"], rcw[l], dz, dm)
        small["rnn_conv_w"][l] = drcw[:, 0]
        small["rnn_conv_b"][l] = drcb[0]
        for i, n in enumerate(["lru_w_a", "lru_b_a", "lru_w_x", "lru_b_x", "lru_lambda"]):
            small[n][l] = jnp.stack([lru_sums[0][i], lru_sums[1][i]]).reshape((2,) + weights[n].shape[2:-1] + (-1,))

        du = mm_cols_t(f"l{l}_du_mix", dz, wing, flat=True)
        mix_grads[("w_in", l)] = mm_cols_grad(f"l{l}_dw_in", sv["u"], dz, flat=True)
        dh, dss1, dgn1 = norm_mod_bwd(f"l{l}_mix_norm_bwd", du, sv["h_mix"], gn[l, 1:2], mt, 1, dh, dm)
        tok = (scatter(f"l{l}_scatter_mix", mix_grads),)

        dh, dw13, dw2, dss0, dgate0, dgn0 = _ffn_backward(f"l{l}_ffn1", dh, sv["ffn1"], gn[l, 0:1], mt, 0,
                                                         ws, f"l{l}_ffn1_w13", f"l{l}_ffn1_w2", dm, deps=tok)
        tok = (scatter(f"l{l}_scatter_ffn1", {("ffn1_w13", l): dw13, ("ffn1_w2", l): dw2}),)
        small["norm_g"][l] = jnp.concatenate([dgn0, dgn1, dgn2], axis=0)
        dmods[l] = jnp.concatenate([dss0, dgate0, dss1, dgate1, dss2, dgate2], axis=1).reshape(2, A)

    grad_x = dh[C:][None]

    pack_mod = jnp.stack([jnp.stack([dmods[l][1], dmods[l][0]]) for l in range(2)])
    g3 = allgather8("gather_dmod", pack_mod.reshape(-1, 128)).reshape(N_DEV, 2, 2, 1, A)
    dmod_full, d_ada_b = dmod_assemble("dmod_assemble", g3)
    dmod_q = lax.dynamic_slice_in_dim(dmod_full, j_me * Aq, Aq, axis=2)
    dcond_q = ada_cond_grad("ada_cond_grad", dmod_q, ada_w)

    small_names = list(small)
    small_parts = [jnp.stack(small[n]) for n in small_names] + [d_final_g, dcond_q[N_DEV]]
    small_shapes = [p.shape for p in small_parts]
    lru_big = [small_names.index("lru_w_a"), small_names.index("lru_w_x")]
    rest_idx = [i for i in range(len(small_parts)) if i not in lru_big]
    summed = [None] * len(small_parts)
    for i in lru_big:
        buf = _pack([small_parts[i]])
        tot, _ = sum_devices(f"sum_{small_names[i]}", allgather8(f"gather_{small_names[i]}", buf).reshape(N_DEV, -1, 128))
        summed[i] = _unpack(tot, [small_shapes[i]])[0]
    buf = _pack([small_parts[i] for i in rest_idx])
    tot, chip_tot = sum_devices("sum_small_grads", allgather8("gather_small_grads", buf).reshape(N_DEV, -1, 128))
    for i, val in zip(rest_idx, _unpack(tot, [small_shapes[i] for i in rest_idx])):
        summed[i] = val
    dcond_ctx = _unpack(chip_tot, [small_shapes[i] for i in rest_idx])[-1]
    sg = jax.nn.sigmoid(c_ctx)
    grads = dict(zip(small_names, summed[:len(small_names)]))
    grads["final_norm_g"] = summed[len(small_names)][0]
    grads["c_ctx"] = dcond_ctx * (sg * (1.0 + c_ctx * (1.0 - sg)))
    grads["ada_b"] = d_ada_b[:, 0]
    for n in small_sharded:
        g = grads[n]
        q = g.shape[-1] // N_CHIP
        grads[n] = lax.dynamic_slice_in_dim(g, j_me * q, q, axis=g.ndim - 1)

    arrived = {}

    def collect(idx, after):
        name, ss, rs, bufs, keys = scatters[idx]
        done = exchange_wait(f"{name}_wait", ss, rs, bufs, _scatter_plan(len(keys)), after)
        for i, key in enumerate(keys):
            arrived[key] = (done[i], done[len(keys) + i])

    for idx in range(len(scatters) - 1):
        collect(idx, dh)
    results = {}
    last_done = dh

    def finish(swap, after):
        n, ss, rs, bufs = swap
        own, other = exchange_wait(f"swap_{n}_wait", ss, rs, bufs, _sibling_plan, after)
        results[n] = adamw(f"adamw_{n}", weights[n], mom_m[n], mom_v[n], [own, other])
        return results[n][0]

    in_flight = None
    for n in ["ffn2_w13", "ffn2_w2", "w_out", "w_branch", "w_in", "ffn1_w13", "ffn1_w2"]:
        if not any(k[0] == n and k[1] == 0 for k in arrived):
            collect(len(scatters) - 1, last_done)
        keys = sorted((k for k in arrived if k[0] == n), key=lambda k: k[1:])
        part = sum_parts(f"sum_{n}", [arrived[k] for k in keys], j_idx).reshape(weights[n].shape)
        ss, rs, bufs, token = exchange_start(f"swap_{n}_start", [part, lax.empty(part.shape, F32)], _sibling_plan)
        if in_flight is not None:
            last_done = finish(in_flight, token)
        in_flight = (n, ss, rs, bufs)
    finish(in_flight, last_done)
    results["ada_w"] = ada_update("ada_update", cond, dmod_q, ada_w, m_ada_w, v_ada_w)
    small_all = [n for n in order if n not in results]
    pk = lambda d: _pack([d[n] for n in small_all])
    outs = adamw("adamw_small", pk(weights), pk(mom_m), pk(mom_v), [pk(grads)])
    shapes_small = [weights[n].shape for n in small_all]
    unpacked = [_unpack(o, shapes_small) for o in outs]
    for i, n in enumerate(small_all):
        results[n] = tuple(unpacked[k][i] for k in range(4))

    return (loss, grad_x, *[results[n][0] for n in order], *[results[n][1] for n in order],
            *[results[n][2] for n in order], *[results[n][3] for n in order])
```

```python
import functools
import math

import jax
import jax.numpy as jnp
from jax import lax
from jax.experimental import pallas as pl
from jax.experimental.pallas import tpu as pltpu

F32 = jnp.float32
BF16 = jnp.bfloat16
MESH = pl.DeviceIdType.MESH

HEAD_DIM = 128
GRID_W = 64
WINDOW = 128
Q_BLOCK = 128
ROPE_BASE = 10000.0
LRU_C = 8.0
EPS = 1e-6
NEG_INF = -1e30
N_MOD = 9
N_BRANCH = 3
RNN_BLOCK = 128
HALO = 8
LSE_W = 128

ADAM_LR = 0.001
ADAM_B1 = 0.9
ADAM_B2 = 0.999
ADAM_EPS = 1e-08
ADAM_WD = 0.01
ADAM_STEP = 10

VMEM_LIMIT_BYTES = 48 * 1024 * 1024
N_DEV = 8
N_CHIP = 4


def _pick(n, cands):
    for c in cands:
        if c <= n and n % c == 0:
            return c
    return n


def _cparams(sem):
    return pltpu.CompilerParams(dimension_semantics=sem, vmem_limit_bytes=VMEM_LIMIT_BYTES)


def _silu(x):
    return x * jax.nn.sigmoid(x)


def _dsilu(x):
    s = jax.nn.sigmoid(x)
    return s * (1.0 + x * (1.0 - s))


_GELU_K = math.sqrt(2.0 / math.pi)


def _gelu(x):
    return 0.5 * x * (1.0 + jnp.tanh(_GELU_K * (x + 0.044715 * x * x * x)))


def _dgelu(x):
    t = jnp.tanh(_GELU_K * (x + 0.044715 * x * x * x))
    return 0.5 * (1.0 + t) + 0.5 * x * (1.0 - t * t) * _GELU_K * (1.0 + 3.0 * 0.044715 * x * x)


def _expm1(x):
    series = x * (1.0 + x * (0.5 + x * (1.0 / 6.0 + x * (1.0 / 24.0 + x * (1.0 / 120.0)))))
    return jnp.where(jnp.abs(x) < 0.1, series, jnp.exp(x) - 1.0)


def _my_pos():
    return lax.axis_index("x"), lax.axis_index("y"), lax.axis_index("c")


_DIMS = {"nn": (((1,), (0,)), ((), ())), "nt": (((1,), (1,)), ((), ())), "tn": (((0,), (0,)), ((), ()))}


def _mm(name, a, b, *, mode, grid, a_blk, a_map, b_blk, b_map, o_blk, o_map, out_shape, out_dtype, deps=(), resid=None):
    nk = grid[-1]
    nax = len(grid)
    acc_shape = tuple(d for d in o_blk if d is not None)

    def product(a_ref, b_ref):
        return lax.dot_general(a_ref[...].astype(BF16), b_ref[...].astype(BF16), _DIMS[mode], preferred_element_type=F32)

    def write(res, rest):
        if resid is None:
            o_ref = rest[-1] if nk == 1 else rest[-2]
            o_ref[...] = res.astype(o_ref.dtype)
            return
        _, _, s, coef, n_ctx = resid
        h_ref, m_ref = rest[len(deps)], rest[len(deps) + 1]
        o_ref, f_ref = rest[len(deps) + 2], rest[len(deps) + 3]
        tm = acc_shape[0]
        row = pl.program_id(0) * tm + lax.broadcasted_iota(jnp.int32, acc_shape, 0)
        gate = jnp.where(row < n_ctx, m_ref[0, 3 * s + 2], m_ref[1, 3 * s + 2])
        f_ref[...] = res.astype(f_ref.dtype)
        o_ref[...] = h_ref[...] + (coef * gate) * res

    def body_one_step(a_ref, b_ref, *rest):
        write(product(a_ref, b_ref), rest)

    def body(a_ref, b_ref, *rest):
        acc_ref = rest[-1]
        k = pl.program_id(nax - 1)

        @pl.when(k == 0)
        def _():
            acc_ref[...] = jnp.zeros_like(acc_ref)

        acc_ref[...] += product(a_ref, b_ref)

        @pl.when(k == nk - 1)
        def _():
            write(acc_ref[...], rest)

    in_specs = [pl.BlockSpec(a_blk, a_map), pl.BlockSpec(b_blk, b_map)] + [pl.BlockSpec(memory_space=pl.ANY)] * len(deps)
    out_specs = pl.BlockSpec(o_blk, o_map)
    out_shapes = jax.ShapeDtypeStruct(out_shape, out_dtype)
    operands = (a, b, *deps)
    if resid is not None:
        h, modtab = resid[0], resid[1]
        tn = o_blk[-1]
        in_specs += [pl.BlockSpec(o_blk, o_map),
                     pl.BlockSpec((2, N_MOD, 1, tn), lambda *idx: (0, 0, 0, o_map(*idx)[-1]))]
        out_specs = [pl.BlockSpec(o_blk, o_map), pl.BlockSpec(o_blk, o_map)]
        out_shapes = [jax.ShapeDtypeStruct(out_shape, F32), jax.ShapeDtypeStruct(out_shape, BF16)]
        operands += (h, modtab)
    return pl.pallas_call(
        body_one_step if nk == 1 else body, name=name, grid=grid,
        in_specs=in_specs, out_specs=out_specs, out_shape=out_shapes,
        scratch_shapes=[] if nk == 1 else [pltpu.VMEM(acc_shape, F32)],
        compiler_params=_cparams(("parallel",) * (nax - 1) + ("arbitrary",)),
    )(*operands)


def _tiles(n):
    return _pick(n, (768, 512, 384, 256, 128, 64, 32, 16))


def _tiles_long(n):
    return _pick(n, (1408, 768, 512, 384, 256, 128, 64, 32, 16))


VMEM_TILE_BUDGET = 40 * 1024 * 1024


def _fit(n, nbytes):
    for c in (768, 512, 384, 256, 128, 64, 32, 16):
        if c <= n and n % c == 0 and nbytes(c) <= VMEM_TILE_BUDGET:
            return c
    return _pick(n, (16, 8))


def _whole(n, cap=2048):
    return n if n <= cap else _ktile(n)


def _ktile(n):
    return _pick(n, (512, 256, 128))


def mm_cols(name, a, wg, out_dtype, flat, deps=()):
    T, K = a.shape
    Nq = wg.shape[-1]
    tk = _whole(K)
    osize = jnp.dtype(out_dtype).itemsize
    tm = _fit(T, lambda t: 2 * t * Nq * osize + 4 * tk * Nq + 4 * t * tk + (4 * t * Nq if tk < K else 0))
    if flat:
        o_blk, o_map, o_shape = (tm, Nq), (lambda j, i, k: (i, j)), (T, N_CHIP * Nq)
    else:
        o_blk, o_map, o_shape = (None, tm, Nq), (lambda j, i, k: (j, i, 0)), (N_CHIP, T, Nq)
    return _mm(name, a, wg, mode="nn", grid=(N_CHIP, T // tm, K // tk),
               a_blk=(tm, tk), a_map=lambda j, i, k: (i, k),
               b_blk=(None, tk, Nq), b_map=lambda j, i, k: (j, k, 0),
               o_blk=o_blk, o_map=o_map, out_shape=o_shape, out_dtype=out_dtype, deps=deps)


def mm_cols_t(name, d, wg, flat):
    K, Nq = wg.shape[-2:]
    T = d.shape[-2]
    tm, tn = _tiles_long(T), _ktile(K)
    if flat:
        a_blk, a_map = (tm, Nq), (lambda i, j, k: (i, k))
    else:
        a_blk, a_map = (None, tm, Nq), (lambda i, j, k: (k, i, 0))
    return _mm(name, d, wg, mode="nt", grid=(T // tm, K // tn, N_CHIP),
               a_blk=a_blk, a_map=a_map,
               b_blk=(None, tn, Nq), b_map=lambda i, j, k: (k, j, 0),
               o_blk=(tm, tn), o_map=lambda i, j, k: (i, j), out_shape=(T, K), out_dtype=F32)


def mm_cols_grad(name, a, d, flat):
    T, K = a.shape
    Nq = d.shape[-1] // N_CHIP if flat else d.shape[-1]
    tt, br = _tiles_long(T), _ktile(K)
    if flat:
        b_blk, b_map = (tt, Nq), (lambda j, r, t: (t, j))
    else:
        b_blk, b_map = (None, tt, Nq), (lambda j, r, t: (j, t, 0))
    return _mm(name, a, d, mode="tn", grid=(N_CHIP, K // br, T // tt),
               a_blk=(tt, br), a_map=lambda j, r, t: (t, r),
               b_blk=b_blk, b_map=b_map,
               o_blk=(None, br, Nq), o_map=lambda j, r, t: (j, r, 0),
               out_shape=(N_CHIP, K, Nq), out_dtype=BF16)


def mm_rows(name, a, wg, deps=(), resid=None):
    G, T, Kg = a.shape
    N = wg.shape[-1]
    tm, tn = _tiles(T), _pick(N, (1024, 512, 256, 128))
    return _mm(name, a, wg, mode="nn", grid=(T // tm, N // tn, G),
               a_blk=(None, tm, Kg), a_map=lambda i, j, k: (k, i, 0),
               b_blk=(None, Kg, tn), b_map=lambda i, j, k: (k, 0, j),
               o_blk=(tm, tn), o_map=lambda i, j, k: (i, j), out_shape=(T, N), out_dtype=F32, deps=deps, resid=resid)


def mm_rows_t(name, d, wg):
    T, N = d.shape
    G, Kg = wg.shape[0], wg.shape[1]
    tk = _whole(N)
    tm = _fit(T, lambda t: 4 * t * Kg + 4 * Kg * tk + 4 * t * tk + (4 * t * Kg if tk < N else 0))
    return _mm(name, d, wg, mode="nt", grid=(G, T // tm, N // tk),
               a_blk=(tm, tk), a_map=lambda j, i, k: (i, k),
               b_blk=(None, Kg, tk), b_map=lambda j, i, k: (j, 0, k),
               o_blk=(None, tm, Kg), o_map=lambda j, i, k: (j, i, 0), out_shape=(G, T, Kg), out_dtype=BF16)


def mm_rows_grad(name, a, d):
    G, T, Kg = a.shape
    N = d.shape[-1]
    tt, tn = _tiles_long(T), _ktile(N)
    return _mm(name, a, d, mode="tn", grid=(G, N // tn, T // tt),
               a_blk=(None, tt, Kg), a_map=lambda g, j, t: (g, t, 0),
               b_blk=(tt, tn), b_map=lambda g, j, t: (t, j),
               o_blk=(None, Kg, tn), o_map=lambda g, j, t: (g, 0, j), out_shape=(G, Kg, N), out_dtype=BF16)


def mm_plain(name, a, w, mode, out_dtype, deps=(), resid=None):
    T = a.shape[0]
    K, N = w.shape[-2:]
    tm = _tiles(T)
    if mode == "nn":
        tn, tk = (_whole(N) if resid is None else _pick(N, (1024, 512, 256, 128))), _whole(K)
        return _mm(name, a, w, mode="nn", grid=(T // tm, N // tn, K // tk),
                   a_blk=(tm, tk), a_map=lambda i, j, k: (i, k),
                   b_blk=(tk, tn), b_map=lambda i, j, k: (k, j),
                   o_blk=(tm, tn), o_map=lambda i, j, k: (i, j), out_shape=(T, N), out_dtype=out_dtype, deps=deps,
                   resid=resid)
    tn, tk = _whole(K), _whole(N)
    return _mm(name, a, w, mode="nt", grid=(T // tm, K // tn, N // tk),
               a_blk=(tm, tk), a_map=lambda i, j, k: (i, k),
               b_blk=(tn, tk), b_map=lambda i, j, k: (j, k),
               o_blk=(tm, tn), o_map=lambda i, j, k: (i, j), out_shape=(T, K), out_dtype=out_dtype)


def mm_plain_grad(name, a, d):
    T, K = a.shape
    N = d.shape[-1]
    tt, br, tn = _tiles_long(T), _ktile(K), _whole(N)
    return _mm(name, a, d, mode="tn", grid=(K // br, N // tn, T // tt),
               a_blk=(tt, br), a_map=lambda r, j, t: (t, r),
               b_blk=(tt, tn), b_map=lambda r, j, t: (t, j),
               o_blk=(br, tn), o_map=lambda r, j, t: (r, j), out_shape=(K, N), out_dtype=BF16)


def mm_branch(name, y, wbg, br, deps=()):
    T, RW = y.shape
    Dq = wbg.shape[-1]
    tm = _tiles_long(T)

    def body(y_ref, w_ref, *rest):
        o_ref = rest[-1]
        lhs = y_ref[...]
        for j in range(N_CHIP):
            o_ref[:, j * Dq:(j + 1) * Dq] = jnp.dot(lhs, w_ref[j], preferred_element_type=F32).astype(o_ref.dtype)

    return pl.pallas_call(
        body, name=name, grid=(T // tm,),
        in_specs=[pl.BlockSpec((tm, RW), lambda i: (i, 0)),
                  pl.BlockSpec((N_CHIP, None, RW, Dq), lambda i: (0, br, 0, 0))] + [pl.BlockSpec(memory_space=pl.ANY)] * len(deps),
        out_specs=pl.BlockSpec((tm, N_CHIP * Dq), lambda i: (i, 0)),
        out_shape=jax.ShapeDtypeStruct((T, N_CHIP * Dq), BF16), compiler_params=_cparams(("parallel",)),
    )(y, wbg, *deps)


def mm_branch_t(name, d, wbg, br):
    T = d.shape[0]
    RW, Dq = wbg.shape[-2:]
    tm = _tiles_long(T)

    def body(d_ref, w_ref, o_ref):
        for j in range(N_CHIP):
            term = lax.dot_general(d_ref[:, j * Dq:(j + 1) * Dq], w_ref[j], _DIMS["nt"], preferred_element_type=F32)
            if j == 0:
                o_ref[...] = term
            else:
                o_ref[...] += term

    return pl.pallas_call(
        body, name=name, grid=(T // tm,),
        in_specs=[pl.BlockSpec((tm, N_CHIP * Dq), lambda i: (i, 0)),
                  pl.BlockSpec((N_CHIP, None, RW, Dq), lambda i: (0, br, 0, 0))],
        out_specs=pl.BlockSpec((tm, RW), lambda i: (i, 0)),
        out_shape=jax.ShapeDtypeStruct((T, RW), F32), compiler_params=_cparams(("parallel",)),
    )(d, wbg)


def mm_branch_grad(name, y, d):
    T, RW = y.shape
    D = d.shape[-1]
    Dq = D // N_CHIP
    tt = _tiles_long(T)
    nt = T // tt

    def body(y_ref, d_ref, o_ref, acc_ref):
        t = pl.program_id(0)

        @pl.when(t == 0)
        def _():
            acc_ref[...] = jnp.zeros_like(acc_ref)

        acc_ref[...] += lax.dot_general(y_ref[...], d_ref[...], _DIMS["tn"], preferred_element_type=F32)

        @pl.when(t == nt - 1)
        def _():
            for j in range(N_CHIP):
                o_ref[j] = acc_ref[:, j * Dq:(j + 1) * Dq].astype(o_ref.dtype)

    return pl.pallas_call(
        body, name=name, grid=(nt,),
        in_specs=[pl.BlockSpec((tt, RW), lambda t: (t, 0)), pl.BlockSpec((tt, D), lambda t: (t, 0))],
        out_specs=pl.BlockSpec((N_CHIP, RW, Dq), lambda t: (0, 0, 0)),
        out_shape=jax.ShapeDtypeStruct((N_CHIP, RW, Dq), BF16),
        scratch_shapes=[pltpu.VMEM((RW, D), F32)], compiler_params=_cparams(("arbitrary",)),
    )(y, d)


def allgather8(name, x_shard):
    m_per, n = x_shard.shape

    def body(x_ref, out_ref, send_sems, recv_sems, local_sem):
        x, y, c = _my_pos()
        me, sibling = (x, y, c), (x, y, 1 - c)
        chips = [(1 - x, y), (x, 1 - y), (1 - x, 1 - y)]

        def rows(px, py, pc):
            return out_ref.at[pl.ds((4 * px + 2 * py + pc) * m_per, m_per), :]

        def copy(k, block, to, src=None):
            return pltpu.make_async_remote_copy(
                src_ref=rows(*block) if src is None else src, dst_ref=rows(*block),
                send_sem=send_sems.at[k], recv_sem=recv_sems.at[k], device_id=to, device_id_type=MESH)

        mine = pltpu.make_async_copy(x_ref, rows(*me), local_sem)
        mine.start()
        first = [copy(0, me, sibling, src=x_ref)]
        first += [copy(1 + j, me, (*chip, c), src=x_ref) for j, chip in enumerate(chips)]
        for cp in first:
            cp.start()
        passed = [copy(4 + j, (*chip, c), sibling) for j, chip in enumerate(chips)]
        for j, chip in enumerate(chips):
            copy(1 + j, (*chip, c), me).wait_recv()
            passed[j].start()
        copy(0, sibling, me).wait_recv()
        for j, chip in enumerate(chips):
            copy(4 + j, (*chip, 1 - c), me).wait_recv()
        for cp in first + passed:
            cp.wait_send()
        mine.wait()

    return pl.pallas_call(
        body, name=name,
        out_shape=jax.ShapeDtypeStruct((N_DEV * m_per, n), x_shard.dtype),
        in_specs=[pl.BlockSpec(memory_space=pltpu.VMEM)],
        out_specs=pl.BlockSpec(memory_space=pltpu.VMEM),
        scratch_shapes=[pltpu.SemaphoreType.DMA((7,)), pltpu.SemaphoreType.DMA((7,)), pltpu.SemaphoreType.DMA],
        compiler_params=pltpu.CompilerParams(vmem_limit_bytes=VMEM_LIMIT_BYTES),
    )(x_shard)


def _other_chips(x, y):
    return [(1 - x, y), (x, 1 - y), (1 - x, 1 - y)]


_HBM = pl.BlockSpec(memory_space=pltpu.HBM)
_SEM = pl.BlockSpec(memory_space=pltpu.SEMAPHORE)
_ANY = pl.BlockSpec(memory_space=pl.ANY)
_EFFECT = pltpu.SideEffectType.DATAFLOW_SIDE_EFFECTING
TOKEN_SHAPE = (8, 128)


def _in_hbm(a):
    return pltpu.with_memory_space_constraint(a, pltpu.HBM)


def exchange_start(name, bufs, plan):
    n = len(bufs)
    n_copies = len(plan([None] * n, 0, 0, 0, dry=True))

    def body(*refs):
        ins = refs[:n]
        send_sems, recv_sems = refs[n], refs[n + 1]
        token = refs[-1]
        x, y, c = _my_pos()
        for i, (src, dst, to) in enumerate(plan(ins, x, y, c)):
            pltpu.make_async_remote_copy(src_ref=src, dst_ref=dst, send_sem=send_sems.at[i], recv_sem=recv_sems.at[i],
                                         device_id=to, device_id_type=MESH).start()
        token[...] = jnp.zeros_like(token)

    outs = pl.pallas_call(
        body, name=name,
        out_shape=(pltpu.SemaphoreType.DMA((n_copies,)), pltpu.SemaphoreType.DMA((n_copies,)),
                   *[pltpu.HBM(b.shape, b.dtype) for b in bufs], jax.ShapeDtypeStruct(TOKEN_SHAPE, F32)),
        in_specs=[_HBM] * n,
        out_specs=(_SEM, _SEM, *[_HBM] * n, pl.BlockSpec(memory_space=pltpu.VMEM)),
        input_output_aliases={i: 2 + i for i in range(n)},
        compiler_params=pltpu.CompilerParams(has_side_effects=_EFFECT),
    )(*[_in_hbm(b) for b in bufs])
    return outs[0], outs[1], list(outs[2:2 + n]), outs[-1]


def exchange_wait(name, send_sems, recv_sems, bufs, plan, after):
    n = len(bufs)

    def body(*refs):
        ins = refs[:n]
        send_sems, recv_sems = refs[n], refs[n + 1]
        x, y, c = _my_pos()
        for i, (src, dst, to) in enumerate(plan(ins, x, y, c, arriving=True)):
            cp = pltpu.make_async_remote_copy(src_ref=src, dst_ref=dst, send_sem=send_sems.at[i],
                                              recv_sem=recv_sems.at[i], device_id=to, device_id_type=MESH)
            cp.wait_send()
            cp.wait_recv()

    outs = pl.pallas_call(
        body, name=name,
        out_shape=tuple(pltpu.HBM(b.shape, b.dtype) for b in bufs),
        in_specs=[_HBM] * n + [_SEM, _SEM, _ANY],
        out_specs=tuple([_HBM] * n),
        input_output_aliases={i: i for i in range(n)},
        compiler_params=pltpu.CompilerParams(has_side_effects=_EFFECT),
    )(*bufs, send_sems, recv_sems, after)
    return list(outs)


def _gather_plan(refs, x, y, c, dry=False, arriving=False):
    if dry:
        return [None] * 3
    (land,) = refs
    j_me = 2 * x + y
    return [(land.at[j_me], land.at[(2 * px + py) if arriving else j_me], (px, py, c)) for px, py in _other_chips(x, y)]


def _sibling_plan(refs, x, y, c, dry=False, arriving=False):
    if dry:
        return [None]
    src, land = refs
    return [(src, land, (x, y, 1 - c))]


def _scatter_plan(n_pieces):
    def plan(refs, x, y, c, dry=False, arriving=False):
        if dry:
            return [None] * (3 * n_pieces)
        grads, lands = refs[:n_pieces], refs[n_pieces:]
        return [(grads[p].at[2 * px + py], lands[p].at[k], (px, py, c))
                for p in range(n_pieces) for k, (px, py) in enumerate(_other_chips(x, y))]
    return plan


def _view2d(a):
    return a.reshape(-1, a.shape[-1])


def _row_tile(rows, width, itemsize=4, budget=1 << 20):
    t = 8
    for cand in (1024, 512, 256, 128, 64, 32, 16, 8):
        if rows % cand == 0 and cand * width * itemsize <= budget:
            t = cand
            break
    return t if rows % t == 0 else rows


def cast_into_slot(name, w, l, j_idx, deps=()):
    w3 = w.reshape(w.shape[0], -1, w.shape[-1])
    _, R, W = w3.shape
    tr = _row_tile(R, W)

    def body(j_ref, a_ref, *rest):
        o_ref = rest[-1]
        o_ref[...] = a_ref[...].astype(BF16)

    return pl.pallas_call(
        body, name=name,
        grid_spec=pltpu.PrefetchScalarGridSpec(
            num_scalar_prefetch=1, grid=(R // tr,),
            in_specs=[pl.BlockSpec((None, tr, W), lambda i, j: (l, i, 0))] + [pl.BlockSpec(memory_space=pl.ANY)] * len(deps),
            out_specs=pl.BlockSpec((None, tr, W), lambda i, j: (j[0], i, 0))),
        out_shape=jax.ShapeDtypeStruct((N_CHIP, R, W), BF16), compiler_params=_cparams(("parallel",)),
    )(j_idx, w3, *deps)


def sum_parts(name, groups, j_idx):
    n = len(groups)
    _, R, W = groups[0][0].shape
    tr = _row_tile(R, W)

    def body(j_ref, *refs):
        o_ref = refs[-1]
        g = pl.program_id(0)
        for q in range(n):
            @pl.when(g == q)
            def _(q=q):
                own, got = refs[2 * q], refs[2 * q + 1]
                o_ref[...] = ((own[...].astype(F32) + got[0].astype(F32)) + got[1].astype(F32)) + got[2].astype(F32)

    in_specs = []
    for q in range(n):
        in_specs.append(pl.BlockSpec((None, tr, W), lambda g, i, j, q=q: (j[0], jnp.where(g == q, i, 0), 0)))
        in_specs.append(pl.BlockSpec((3, tr, W), lambda g, i, j, q=q: (0, jnp.where(g == q, i, 0), 0)))
    return pl.pallas_call(
        body, name=name,
        grid_spec=pltpu.PrefetchScalarGridSpec(
            num_scalar_prefetch=1, grid=(n, R // tr), in_specs=in_specs,
            out_specs=pl.BlockSpec((None, tr, W), lambda g, i, j: (g, i, 0))),
        out_shape=jax.ShapeDtypeStruct((n, R, W), F32), compiler_params=_cparams(("arbitrary", "arbitrary")),
    )(j_idx, *[a for pair in groups for a in pair])


def adamw(name, w, m, v, g_parts):
    shape = w.shape
    w2, m2, v2 = _view2d(w), _view2d(m), _view2d(v)
    gs = [_view2d(g) for g in g_parts]
    R, W = w2.shape
    tr = _row_tile(R, W, budget=1 << 19)
    ng = len(gs)
    bc1 = 1.0 - ADAM_B1 ** ADAM_STEP
    bc2 = 1.0 - ADAM_B2 ** ADAM_STEP

    def body(*refs):
        w_ref, m_ref, v_ref = refs[:3]
        g_refs = refs[3:3 + ng]
        go_ref, d_ref, mo_ref, vo_ref = refs[3 + ng:]
        g = g_refs[0][...]
        for r in g_refs[1:]:
            g = g + r[...]
        mn = ADAM_B1 * m_ref[...] + (1.0 - ADAM_B1) * g
        vn = ADAM_B2 * v_ref[...] + (1.0 - ADAM_B2) * (g * g)
        m_hat = mn / bc1
        v_hat = vn / bc2
        go_ref[...] = g
        d_ref[...] = -ADAM_LR * (m_hat / (jnp.sqrt(v_hat) + ADAM_EPS) + ADAM_WD * w_ref[...])
        mo_ref[...] = mn
        vo_ref[...] = vn

    spec = pl.BlockSpec((tr, W), lambda i: (i, 0))
    outs = pl.pallas_call(
        body, name=name, grid=(R // tr,),
        in_specs=[spec] * (3 + ng), out_specs=[spec] * 4,
        out_shape=[jax.ShapeDtypeStruct((R, W), F32)] * 4, compiler_params=_cparams(("parallel",)),
    )(w2, m2, v2, *gs)
    return tuple(o.reshape(shape) for o in outs)


class Dims:
    pass


def _sel(dm):
    return (pl.program_id(0) >= dm.nctx).astype(jnp.int32)


def norm_mod(name, h, gn, modtab, s, dm, deps=()):
    T, D = h.shape
    tm = dm.tme

    def body(h_ref, g_ref, m_ref, *rest):
        u_ref = rest[-1]
        sel = _sel(dm)
        x = h_ref[...]
        r = lax.rsqrt(jnp.mean(x * x, axis=-1, keepdims=True) + EPS)
        ng = x * r * g_ref[...]
        u_ref[...] = (ng * (1.0 + m_ref[sel, 3 * s + 1]) + m_ref[sel, 3 * s]).astype(u_ref.dtype)

    return pl.pallas_call(
        body, name=name, grid=(T // tm,),
        in_specs=[pl.BlockSpec((tm, D), lambda i: (i, 0)), pl.BlockSpec((1, D), lambda i: (0, 0)),
                  pl.BlockSpec((2, N_MOD, 1, D), lambda i: (0, 0, 0, 0))] + [_ANY] * len(deps),
        out_specs=pl.BlockSpec((tm, D), lambda i: (i, 0)),
        out_shape=jax.ShapeDtypeStruct((T, D), BF16), compiler_params=_cparams(("parallel",)),
    )(h, gn, modtab, *deps)


def norm_mod_bwd(name, du, h, gn, modtab, s, dh_in, dm):
    T, D = h.shape
    tm = dm.tme

    def body(du_ref, h_ref, g_ref, m_ref, dhi_ref, dh_ref, dmod_ref, dg_ref):
        i = pl.program_id(0)
        sel = _sel(dm)

        @pl.when(i == 0)
        def _():
            dmod_ref[...] = jnp.zeros_like(dmod_ref)
            dg_ref[...] = jnp.zeros_like(dg_ref)

        x = h_ref[...]
        r = lax.rsqrt(jnp.mean(x * x, axis=-1, keepdims=True) + EPS)
        n = x * r
        g = g_ref[...]
        du = du_ref[...]
        dmod_ref[sel, 0] += jnp.sum(du, axis=0, keepdims=True)
        dmod_ref[sel, 1] += jnp.sum(du * (n * g), axis=0, keepdims=True)
        dng = du * (1.0 + m_ref[sel, 3 * s + 1])
        dg_ref[...] += jnp.sum(dng * n, axis=0, keepdims=True)
        dn = dng * g
        dh_ref[...] = dhi_ref[...] + r * (dn - n * jnp.mean(dn * n, axis=-1, keepdims=True))

    row = pl.BlockSpec((tm, D), lambda i: (i, 0))
    return pl.pallas_call(
        body, name=name, grid=(T // tm,),
        in_specs=[row, row, pl.BlockSpec((1, D), lambda i: (0, 0)),
                  pl.BlockSpec((2, N_MOD, 1, D), lambda i: (0, 0, 0, 0)), row],
        out_specs=[row, pl.BlockSpec((2, 2, 1, D), lambda i: (0, 0, 0, 0)), pl.BlockSpec((1, D), lambda i: (0, 0))],
        out_shape=[jax.ShapeDtypeStruct((T, D), F32), jax.ShapeDtypeStruct((2, 2, 1, D), F32),
                   jax.ShapeDtypeStruct((1, D), F32)],
        compiler_params=_cparams(("arbitrary",)),
    )(du, h, gn, modtab, dh_in)


def resid_bwd(name, dh, f, modtab, s, coef, dm, deps=()):
    T, D = dh.shape
    tm = dm.tme

    def body(dh_ref, f_ref, m_ref, *rest):
        df_ref, dg_ref = rest[-2:]
        sel = _sel(dm)

        @pl.when(pl.program_id(0) == 0)
        def _():
            dg_ref[...] = jnp.zeros_like(dg_ref)

        d = coef * dh_ref[...]
        df_ref[...] = (d * m_ref[sel, 3 * s + 2]).astype(df_ref.dtype)
        dg_ref[sel, 0] += jnp.sum(d * f_ref[...], axis=0, keepdims=True)

    row = pl.BlockSpec((tm, D), lambda i: (i, 0))
    return pl.pallas_call(
        body, name=name, grid=(T // tm,),
        in_specs=[row, row, pl.BlockSpec((2, N_MOD, 1, D), lambda i: (0, 0, 0, 0))] + [_ANY] * len(deps),
        out_specs=[row, pl.BlockSpec((2, 1, 1, D), lambda i: (0, 0, 0, 0))],
        out_shape=[jax.ShapeDtypeStruct((T, D), BF16), jax.ShapeDtypeStruct((2, 1, 1, D), F32)],
        compiler_params=_cparams(("arbitrary",)),
    )(dh, f, modtab, *deps)


def swiglu(name, gu, dm):
    _, T, Nq = gu.shape
    tm = dm.tme
    gu4 = gu.reshape(2, 2, T, Nq)

    def body(gu_ref, o_ref):
        g = gu_ref[0].astype(F32)
        o_ref[...] = (_silu(g) * gu_ref[1].astype(F32)).astype(o_ref.dtype)

    return pl.pallas_call(
        body, name=name, grid=(2, T // tm),
        in_specs=[pl.BlockSpec((2, None, tm, Nq), lambda k, i: (0, k, i, 0))],
        out_specs=pl.BlockSpec((None, tm, Nq), lambda k, i: (k, i, 0)),
        out_shape=jax.ShapeDtypeStruct((2, T, Nq), BF16), compiler_params=_cparams(("parallel", "parallel")),
    )(gu4)


def swiglu_bwd(name, dact, gu, dm):
    _, T, Nq = gu.shape
    tm = dm.tme
    gu4 = gu.reshape(2, 2, T, Nq)

    def body(da_ref, gu_ref, o_ref):
        g = gu_ref[0].astype(F32)
        da = da_ref[...].astype(F32)
        o_ref[0] = (da * gu_ref[1].astype(F32) * _dsilu(g)).astype(o_ref.dtype)
        o_ref[1] = (da * _silu(g)).astype(o_ref.dtype)

    out = pl.pallas_call(
        body, name=name, grid=(2, T // tm),
        in_specs=[pl.BlockSpec((None, tm, Nq), lambda k, i: (k, i, 0)),
                  pl.BlockSpec((2, None, tm, Nq), lambda k, i: (0, k, i, 0))],
        out_specs=pl.BlockSpec((2, None, tm, Nq), lambda k, i: (0, k, i, 0)),
        out_shape=jax.ShapeDtypeStruct((2, 2, T, Nq), BF16), compiler_params=_cparams(("parallel", "parallel")),
    )(dact, gu4)
    return out.reshape(4, T, Nq)


def _halo_specs(dm, width, col):
    tm = dm.tme
    per = tm // HALO
    last = dm.T // HALO - 1
    return [pl.BlockSpec((tm, width), lambda i: (i, col)),
            pl.BlockSpec((HALO, width), lambda i: (jnp.maximum(i * per - 1, 0), col)),
            pl.BlockSpec((HALO, width), lambda i: (jnp.minimum((i + 1) * per, last), col))]


def _segment_edges(dm, i):
    first = jnp.logical_or(i == 0, i == dm.nctx)
    last = jnp.logical_or(i == dm.nctx - 1, i == dm.nt - 1)
    return first, last


def _extend(main, prev, nxt, first, last):
    return jnp.concatenate([jnp.where(first, 0.0, prev), main, jnp.where(last, 0.0, nxt)], axis=0)


def _shift(ext, o, tm):
    n = ext.shape[0]
    rolled = ext if o == 0 else pltpu.roll(ext, (-o) % n, 0)
    return rolled[HALO:HALO + tm]


def _load_ext(refs, first, last):
    main, prev, nxt = refs
    return _extend(main[...].astype(F32), prev[...].astype(F32), nxt[...].astype(F32), first, last)


def rnn_conv(name, z, w, b, dm):
    T, RW, tm = dm.T, dm.RW, dm.tme

    def body(main, prev, nxt, w_ref, b_ref, o_ref):
        first, last = _segment_edges(dm, pl.program_id(0))
        ext = _load_ext((main, prev, nxt), first, last)
        acc = jnp.zeros((tm, RW), F32) + b_ref[...]
        for k in range(4):
            acc = acc + w_ref[k] * _shift(ext, k - 2, tm)
        o_ref[...] = acc

    return pl.pallas_call(
        body, name=name, grid=(dm.nt,),
        in_specs=_halo_specs(dm, RW, 0) + [pl.BlockSpec((4, 1, RW), lambda i: (0, 0, 0)),
                                           pl.BlockSpec((1, RW), lambda i: (0, 0))],
        out_specs=pl.BlockSpec((tm, RW), lambda i: (i, 0)),
        out_shape=jax.ShapeDtypeStruct((T, RW), F32), compiler_params=_cparams(("parallel",)),
    )(z, z, z, w, b)


def _blockdiag(x, w_ref):
    nb = w_ref.shape[0]
    outs = [jnp.dot(x[:, n * RNN_BLOCK:(n + 1) * RNN_BLOCK], w_ref[n].astype(BF16), preferred_element_type=F32)
            for n in range(nb)]
    return jnp.concatenate(outs, axis=-1)


def _lru_gates(xa, wa_ref, ba_ref, wx_ref, bx_ref, lam_ref):
    xb = xa.astype(BF16)
    r = jax.nn.sigmoid(_blockdiag(xb, wa_ref) + ba_ref[...])
    ig = jax.nn.sigmoid(_blockdiag(xb, wx_ref) + bx_ref[...])
    nl = -lam_ref[...]
    sp = jnp.maximum(nl, 0.0) + jnp.log(1.0 + jnp.exp(-jnp.abs(nl)))
    log_a = -LRU_C * r * sp
    a = jnp.exp(log_a)
    m = jnp.sqrt(-_expm1(2.0 * log_a))
    return r, ig, sp, a, m


def _lru_specs(l, d, nb, RW):
    wspec = pl.BlockSpec((None, None, nb, RNN_BLOCK, RNN_BLOCK), lambda i: (l, d, 0, 0, 0))
    vspec = pl.BlockSpec((None, None, 1, RW), lambda i: (l, d, 0, 0))
    return [wspec, vspec, wspec, vspec, vspec]


def lru_gates(name, xa, lw, l, d, dm):
    T, RW, tm = dm.T, dm.RW, dm.tme

    def body(xa_ref, wa_ref, ba_ref, wx_ref, bx_ref, lam_ref, a_ref, u_ref):
        xa_v = xa_ref[...]
        r, ig, sp, a, m = _lru_gates(xa_v, wa_ref, ba_ref, wx_ref, bx_ref, lam_ref)
        a_ref[...] = a
        u_ref[...] = m * (ig * xa_v)

    row = pl.BlockSpec((tm, RW), lambda i: (i, 0))
    return pl.pallas_call(
        body, name=name, grid=(dm.nt,),
        in_specs=[row] + _lru_specs(l, d, dm.NB, RW), out_specs=[row, row],
        out_shape=[jax.ShapeDtypeStruct((T, RW), F32)] * 2, compiler_params=_cparams(("parallel",)),
    )(xa, lw["w_a"], lw["b_a"], lw["w_x"], lw["b_x"], lw["lam"])


def lru_gates_bwd(name, xa, lw, l, d, du, dloga, dm):
    T, RW, tm, NB = dm.T, dm.RW, dm.tme, dm.NB

    def body(xa_ref, wa_ref, ba_ref, wx_ref, bx_ref, lam_ref, du_ref, dla_ref,
             dxa_ref, dwa_ref, dba_ref, dwx_ref, dbx_ref, dlam_ref):
        @pl.when(pl.program_id(0) == 0)
        def _():
            for ref in (dwa_ref, dba_ref, dwx_ref, dbx_ref, dlam_ref):
                ref[...] = jnp.zeros_like(ref)

        xa_v = xa_ref[...]
        r, ig, sp, a, m = _lru_gates(xa_v, wa_ref, ba_ref, wx_ref, bx_ref, lam_ref)
        duu = du_ref[...]
        dm_ = duu * (ig * xa_v)
        dig = duu * m * xa_v
        dxa = duu * m * ig
        dla = dla_ref[...] - dm_ * (a * a) / m
        dr = dla * (-LRU_C * sp)
        dsp = jnp.sum(dla * (-LRU_C * r), axis=0, keepdims=True)
        dlam_ref[...] += dsp * (-jax.nn.sigmoid(-lam_ref[...]))
        dpa = dr * r * (1.0 - r)
        dpx = dig * ig * (1.0 - ig)
        dba_ref[...] += jnp.sum(dpa, axis=0, keepdims=True)
        dbx_ref[...] += jnp.sum(dpx, axis=0, keepdims=True)
        xb, dpab, dpxb = xa_v.astype(BF16), dpa.astype(BF16), dpx.astype(BF16)
        back = []
        for n in range(NB):
            sl = slice(n * RNN_BLOCK, (n + 1) * RNN_BLOCK)
            dwa_ref[n] += lax.dot_general(xb[:, sl], dpab[:, sl], _DIMS["tn"], preferred_element_type=F32)
            dwx_ref[n] += lax.dot_general(xb[:, sl], dpxb[:, sl], _DIMS["tn"], preferred_element_type=F32)
            back.append(lax.dot_general(dpab[:, sl], wa_ref[n].astype(BF16), _DIMS["nt"], preferred_element_type=F32)
                        + lax.dot_general(dpxb[:, sl], wx_ref[n].astype(BF16), _DIMS["nt"], preferred_element_type=F32))
        dxa_ref[...] = dxa + jnp.concatenate(back, axis=-1)

    row = pl.BlockSpec((tm, RW), lambda i: (i, 0))
    wacc = pl.BlockSpec((NB, RNN_BLOCK, RNN_BLOCK), lambda i: (0, 0, 0))
    vacc = pl.BlockSpec((1, RW), lambda i: (0, 0))
    wshape = jax.ShapeDtypeStruct((NB, RNN_BLOCK, RNN_BLOCK), F32)
    vshape = jax.ShapeDtypeStruct((1, RW), F32)
    return pl.pallas_call(
        body, name=name, grid=(dm.nt,),
        in_specs=[row] + _lru_specs(l, d, NB, RW) + [row, row],
        out_specs=[row, wacc, vacc, wacc, vacc, vacc],
        out_shape=[jax.ShapeDtypeStruct((T, RW), F32), wshape, vshape, wshape, vshape, vshape],
        compiler_params=_cparams(("arbitrary",)),
    )(xa, lw["w_a"], lw["b_a"], lw["w_x"], lw["b_x"], lw["lam"], du, dloga)


def _chunk_order(dm, ctx_first, descending):
    nch, nctx = dm.nt, dm.nctx
    nlat = nch - nctx

    def order(s):
        if ctx_first and not descending:
            return s
        if not ctx_first and descending:
            return nch - 1 - s
        if ctx_first:
            return jnp.where(s < nctx, nctx - 1 - s, nch - 1 - (s - nctx))
        return jnp.where(s < nlat, nctx + s, s - nlat)

    return order


def _tile_scan(a, b, carry, descending):
    row = lax.broadcasted_iota(jnp.int32, a.shape, 0)
    for s in (1, 2, 4):
        sh = (HALO - s) if descending else s
        keep = (row < HALO - s) if descending else (row >= s)
        ap = pltpu.roll(a, sh, 0)
        bp = pltpu.roll(b, sh, 0)
        b = jnp.where(keep, b + a * bp, b)
        a = jnp.where(keep, a * ap, a)
    h = b + a * carry
    edge = 0 if descending else HALO - 1
    new_carry = jnp.sum(jnp.where(row == edge, h, 0.0), axis=0, keepdims=True)
    return h, new_carry


def lru_scan(name, a, u, ctx_first, descending, dm):
    T, RW, ch = dm.T, dm.RW, dm.tme
    order = _chunk_order(dm, ctx_first, descending)
    ngrp = ch // HALO

    def body(a_ref, u_ref, h_ref, carry_ref):
        @pl.when(pl.program_id(0) == 0)
        def _():
            carry_ref[...] = jnp.zeros_like(carry_ref)

        def step(g, carry):
            g = (ngrp - 1 - g) if descending else g
            rows = pl.ds(pl.multiple_of(g * HALO, HALO), HALO)
            h, carry = _tile_scan(a_ref[rows, :], u_ref[rows, :], carry, descending)
            h_ref[rows, :] = h
            return carry

        carry_ref[...] = lax.fori_loop(0, ngrp, step, carry_ref[...])

    row = pl.BlockSpec((ch, RW), lambda s: (order(s), 0))
    return pl.pallas_call(
        body, name=name, grid=(dm.nt,),
        in_specs=[row, row], out_specs=row,
        out_shape=jax.ShapeDtypeStruct((T, RW), F32),
        scratch_shapes=[pltpu.VMEM((1, RW), F32)], compiler_params=_cparams(("arbitrary",)),
    )(a, u)


def lru_scan_bwd(name, a, u, h, dh, ctx_first, descending, dm):
    T, RW, ch = dm.T, dm.RW, dm.tme
    order = _chunk_order(dm, ctx_first, descending)
    ngrp = ch // HALO

    def body(a_ref, u_ref, h_ref, dh_ref, lam_ref, dla_ref, carry_ref):
        @pl.when(pl.program_id(0) == 0)
        def _():
            carry_ref[...] = jnp.zeros_like(carry_ref)

        def step(g, carry):
            g = (ngrp - 1 - g) if descending else g
            rows = pl.ds(pl.multiple_of(g * HALO, HALO), HALO)
            a_v, dh_v = a_ref[rows, :], dh_ref[rows, :]
            mu, new_carry = _tile_scan(a_v, a_v * dh_v, carry, descending)
            row = lax.broadcasted_iota(jnp.int32, mu.shape, 0)
            if descending:
                nxt = jnp.where(row == HALO - 1, carry, pltpu.roll(mu, HALO - 1, 0))
            else:
                nxt = jnp.where(row == 0, carry, pltpu.roll(mu, 1, 0))
            lam = dh_v + nxt
            lam_ref[rows, :] = lam
            dla_ref[rows, :] = lam * (h_ref[rows, :] - u_ref[rows, :])
            return new_carry

        carry_ref[...] = lax.fori_loop(0, ngrp, step, carry_ref[...])

    row = pl.BlockSpec((ch, RW), lambda s: (order(s), 0))
    return pl.pallas_call(
        body, name=name, grid=(dm.nt,),
        in_specs=[row] * 4, out_specs=[row, row],
        out_shape=[jax.ShapeDtypeStruct((T, RW), F32)] * 2,
        scratch_shapes=[pltpu.VMEM((1, RW), F32)], compiler_params=_cparams(("arbitrary",)),
    )(a, u, h, dh)


def rnn_out(name, hf, hb, z, dm):
    T, RW, tm = dm.T, dm.RW, dm.tme

    def body(hf_ref, hb_ref, rg_ref, o_ref):
        o_ref[...] = ((hf_ref[...] + hb_ref[...]) * _gelu(rg_ref[...])).astype(o_ref.dtype)

    row = pl.BlockSpec((tm, RW), lambda i: (i, 0))
    return pl.pallas_call(
        body, name=name, grid=(dm.nt,),
        in_specs=[row, row, pl.BlockSpec((tm, RW), lambda i: (i, 1))], out_specs=row,
        out_shape=jax.ShapeDtypeStruct((T, RW), BF16), compiler_params=_cparams(("parallel",)),
    )(hf, hb, z)


def rnn_out_bwd(name, dya, hf, hb, z, dm):
    T, RW, tm = dm.T, dm.RW, dm.tme

    def body(d_ref, hf_ref, hb_ref, rg_ref, dh_ref, drg_ref):
        d, rg = d_ref[...], rg_ref[...]
        dh_ref[...] = d * _gelu(rg)
        drg_ref[...] = d * (hf_ref[...] + hb_ref[...]) * _dgelu(rg)

    row = pl.BlockSpec((tm, RW), lambda i: (i, 0))
    return pl.pallas_call(
        body, name=name, grid=(dm.nt,),
        in_specs=[row, row, row, pl.BlockSpec((tm, RW), lambda i: (i, 1))], out_specs=[row, row],
        out_shape=[jax.ShapeDtypeStruct((T, RW), F32)] * 2, compiler_params=_cparams(("parallel",)),
    )(dya, hf, hb, z)


def rnn_conv_bwd(name, dxa_f, dxa_b, drg, z, w, dz, dm):
    T, RW, tm = dm.T, dm.RW, dm.tme

    def body(f0, f1, f2, b0, b1, b2, x0, x1, x2, drg_ref, w_ref, dz_in, dz_ref, dw_ref, db_ref):
        i = pl.program_id(0)

        @pl.when(i == 0)
        def _():
            dw_ref[...] = jnp.zeros_like(dw_ref)
            db_ref[...] = jnp.zeros_like(db_ref)

        first, last = _segment_edges(dm, i)
        dext = _load_ext((f0, f1, f2), first, last) + _load_ext((b0, b1, b2), first, last)
        xext = _load_ext((x0, x1, x2), first, last)
        dmain = dext[HALO:HALO + tm]
        drx = jnp.zeros((tm, RW), F32)
        for k in range(4):
            drx = drx + w_ref[k] * _shift(dext, -(k - 2), tm)
            dw_ref[k] += jnp.sum(dmain * _shift(xext, k - 2, tm), axis=0, keepdims=True)
        db_ref[...] += jnp.sum(dmain, axis=0, keepdims=True)
        dz_ref[:, :RW] = drx.astype(dz_ref.dtype)
        dz_ref[:, RW:] = drg_ref[...].astype(dz_ref.dtype)

    return pl.pallas_call(
        body, name=name, grid=(dm.nt,),
        in_specs=_halo_specs(dm, RW, 0) * 3 + [pl.BlockSpec((tm, RW), lambda i: (i, 0)),
                                               pl.BlockSpec((4, 1, RW), lambda i: (0, 0, 0)),
                                               pl.BlockSpec(memory_space=pl.ANY)],
        out_specs=[pl.BlockSpec((tm, 2 * RW), lambda i: (i, 0)), pl.BlockSpec((4, 1, RW), lambda i: (0, 0, 0)),
                   pl.BlockSpec((1, RW), lambda i: (0, 0))],
        out_shape=[jax.ShapeDtypeStruct(dz.shape, dz.dtype), jax.ShapeDtypeStruct((4, 1, RW), F32),
                   jax.ShapeDtypeStruct((1, RW), F32)],
        input_output_aliases={11: 0}, compiler_params=_cparams(("arbitrary",)),
    )(dxa_f, dxa_f, dxa_f, dxa_b, dxa_b, dxa_b, z, z, z, drg, w, dz)


def short_conv(name, z, w, dm):
    T, RW, tm = dm.T, dm.RW, dm.tme

    def body(sb_ref, g0, g1, g2, x0, x1, x2, w_ref, o_ref):
        first, last = _segment_edges(dm, pl.program_id(0))
        pext = _load_ext((g0, g1, g2), first, last) * _load_ext((x0, x1, x2), first, last)
        cp = jnp.zeros((tm, RW), F32)
        for k in range(3):
            cp = cp + w_ref[k] * _shift(pext, k - 1, tm)
        o_ref[...] = (sb_ref[...] * cp).astype(o_ref.dtype)

    return pl.pallas_call(
        body, name=name, grid=(dm.nt,),
        in_specs=[pl.BlockSpec((tm, RW), lambda i: (i, 2))] + _halo_specs(dm, RW, 3) + _halo_specs(dm, RW, 4)
        + [pl.BlockSpec((3, 1, RW), lambda i: (0, 0, 0))],
        out_specs=pl.BlockSpec((tm, RW), lambda i: (i, 0)),
        out_shape=jax.ShapeDtypeStruct((T, RW), BF16), compiler_params=_cparams(("parallel",)),
    )(z, z, z, z, z, z, z, w)


def short_conv_bwd(name, dyb, z, w, dz, dm):
    T, RW, tm = dm.T, dm.RW, dm.tme

    def spec3(col):
        per = tm // HALO
        last = T // HALO - 1
        return [pl.BlockSpec((tm, RW), lambda i, p: (i, col)),
                pl.BlockSpec((HALO, RW), lambda i, p: (jnp.maximum(i * per - 1, 0), col)),
                pl.BlockSpec((HALO, RW), lambda i, p: (jnp.minimum((i + 1) * per, last), col))]

    def body(d0, d1, d2, s0, s1, s2, g0, g1, g2, x0, x1, x2, w_ref, dz_in, dz_ref, dw_ref, parts_ref):
        i, p = pl.program_id(0), pl.program_id(1)

        @pl.when(jnp.logical_and(i == 0, p == 0))
        def _():
            dw_ref[...] = jnp.zeros_like(dw_ref)

        @pl.when(p == 0)
        def _():
            first, last = _segment_edges(dm, i)
            gext = _load_ext((g0, g1, g2), first, last)
            xext = _load_ext((x0, x1, x2), first, last)
            pext = gext * xext
            dyext = _load_ext((d0, d1, d2), first, last)
            dcext = dyext * _load_ext((s0, s1, s2), first, last)
            dcmain = dcext[HALO:HALO + tm]
            cp = jnp.zeros((tm, RW), F32)
            dp = jnp.zeros((tm, RW), F32)
            for k in range(3):
                pk = _shift(pext, k - 1, tm)
                cp = cp + w_ref[k] * pk
                dp = dp + w_ref[k] * _shift(dcext, -(k - 1), tm)
                dw_ref[k] += jnp.sum(dcmain * pk, axis=0, keepdims=True)
            parts_ref[0] = (dyext[HALO:HALO + tm] * cp).astype(parts_ref.dtype)
            parts_ref[1] = (dp * xext[HALO:HALO + tm]).astype(parts_ref.dtype)
            parts_ref[2] = (dp * gext[HALO:HALO + tm]).astype(parts_ref.dtype)

        dz_ref[...] = parts_ref[p]

    return pl.pallas_call(
        body, name=name, grid=(dm.nt, 3),
        in_specs=spec3(0) + spec3(2) + spec3(3) + spec3(4)
        + [pl.BlockSpec((3, 1, RW), lambda i, p: (0, 0, 0)), pl.BlockSpec(memory_space=pl.ANY)],
        out_specs=[pl.BlockSpec((tm, RW), lambda i, p: (i, 2 + p)), pl.BlockSpec((3, 1, RW), lambda i, p: (0, 0, 0))],
        out_shape=[jax.ShapeDtypeStruct(dz.shape, dz.dtype), jax.ShapeDtypeStruct((3, 1, RW), F32)],
        scratch_shapes=[pltpu.VMEM((3, tm, RW), dz.dtype)],
        input_output_aliases={13: 0}, compiler_params=_cparams(("arbitrary", "arbitrary")),
    )(dyb, dyb, dyb, z, z, z, z, z, z, z, z, z, w, dz)


def _rope_tables(dm):
    L, C = dm.L, dm.C
    half = HEAD_DIM // 2
    pos = jnp.arange(L)
    row = (pos // GRID_W).astype(F32)
    col = (pos % GRID_W).astype(F32)
    inv = ROPE_BASE ** (-jnp.arange(0, half, 2, dtype=F32) / half)
    ar, ac = row[:, None] * inv, col[:, None] * inv
    cos = jnp.concatenate([jnp.cos(ar), jnp.cos(ar), jnp.cos(ac), jnp.cos(ac)], axis=-1)
    sin = jnp.concatenate([-jnp.sin(ar), jnp.sin(ar), -jnp.sin(ac), jnp.sin(ac)], axis=-1)
    cos = jnp.concatenate([jnp.ones((C, HEAD_DIM), F32), cos], axis=0)
    sin = jnp.concatenate([jnp.zeros((C, HEAD_DIM), F32), sin], axis=0)
    return cos, sin


def _swap_pairs(x):
    quarter = HEAD_DIM // 4
    lane = lax.broadcasted_iota(jnp.int32, x.shape, 1)
    return jnp.where(lane % (2 * quarter) < quarter, pltpu.roll(x, HEAD_DIM - quarter, 1), pltpu.roll(x, quarter, 1))


def _rope(x, cos, sin):
    return x * cos + _swap_pairs(x) * sin


def _unrope(d, cos, sin):
    return d * cos + _swap_pairs(d * sin)


def qkv_prep(name, z, cos, sin, dm):
    T, tm, HQ, KW = dm.T, dm.tme, dm.HQ, dm.KW
    qcol, kcol = dm.off_q // HQ, dm.off_k // KW

    def body(q_ref, k_ref, v_ref, c_ref, s_ref, qo, ko, vo):
        cos_v, sin_v = c_ref[...], s_ref[...]
        for hd in range(HQ // HEAD_DIM):
            sl = slice(hd * HEAD_DIM, (hd + 1) * HEAD_DIM)
            qo[:, sl] = _rope(q_ref[:, sl], cos_v, sin_v).astype(qo.dtype)
        for hd in range(KW // HEAD_DIM):
            sl = slice(hd * HEAD_DIM, (hd + 1) * HEAD_DIM)
            ko[:, sl] = _rope(k_ref[:, sl], cos_v, sin_v).astype(ko.dtype)
        vo[...] = v_ref[...].astype(vo.dtype)

    tab = pl.BlockSpec((tm, HEAD_DIM), lambda i: (i, 0))
    return pl.pallas_call(
        body, name=name, grid=(dm.nt,),
        in_specs=[pl.BlockSpec((tm, HQ), lambda i: (i, qcol)), pl.BlockSpec((tm, KW), lambda i: (i, kcol)),
                  pl.BlockSpec((tm, KW), lambda i: (i, kcol + 1)), tab, tab],
        out_specs=[pl.BlockSpec((tm, HQ), lambda i: (i, 0)), pl.BlockSpec((tm, KW), lambda i: (i, 0)),
                   pl.BlockSpec((tm, KW), lambda i: (i, 0))],
        out_shape=[jax.ShapeDtypeStruct((T, HQ), BF16), jax.ShapeDtypeStruct((T, KW), BF16),
                   jax.ShapeDtypeStruct((T, KW), BF16)],
        compiler_params=_cparams(("parallel",)),
    )(z, z, z, cos, sin)


def qkv_bwd(name, dq, dk, dv, cos, sin, dz, dm):
    T, tm, HQ, KW = dm.T, dm.tme, dm.HQ, dm.KW
    nq = HQ // KW
    base = dm.off_q // KW

    def body(dq_ref, dk_ref, dv_ref, c_ref, s_ref, dz_in, dz_ref):
        p = pl.program_id(1)
        src = jnp.where(p < nq, dq_ref[...], jnp.where(p == nq, dk_ref[...], dv_ref[...]))
        cos_v, sin_v = c_ref[...], s_ref[...]
        is_v = p == nq + 1
        for hd in range(KW // HEAD_DIM):
            sl = slice(hd * HEAD_DIM, (hd + 1) * HEAD_DIM)
            dz_ref[:, sl] = jnp.where(is_v, src[:, sl], _unrope(src[:, sl], cos_v, sin_v)).astype(dz_ref.dtype)

    tab = pl.BlockSpec((tm, HEAD_DIM), lambda i, p: (i, 0))
    blk = pl.BlockSpec((tm, KW), lambda i, p: (i, 0))
    return pl.pallas_call(
        body, name=name, grid=(dm.nt, nq + 2),
        in_specs=[pl.BlockSpec((tm, KW), lambda i, p: (i, jnp.minimum(p, nq - 1))), blk, blk, tab, tab,
                  pl.BlockSpec(memory_space=pl.ANY)],
        out_specs=pl.BlockSpec((tm, KW), lambda i, p: (i, base + p)),
        out_shape=jax.ShapeDtypeStruct(dz.shape, dz.dtype),
        input_output_aliases={5: 0}, compiler_params=_cparams(("parallel", "arbitrary")),
    )(dq, dk, dv, cos, sin, dz)


def _attn_specs(dm):
    nC, nB, C, KW = dm.C // Q_BLOCK, dm.T // Q_BLOCK, dm.C, dm.KW

    def near(o):
        return lambda b: (jnp.clip(b + o, nC, nB - 1), 0)

    kv = [pl.BlockSpec((Q_BLOCK, KW), near(o)) for o in (-1, 0, 1)] + [pl.BlockSpec((C, KW), lambda b: (0, 0))]
    return kv


def _attn_mask(dm, b):
    nC, C, L = dm.C // Q_BLOCK, dm.C, dm.L
    span = 3 * Q_BLOCK
    n = b - nC
    iq = lax.broadcasted_iota(jnp.int32, (Q_BLOCK, span + C), 0)
    ik = lax.broadcasted_iota(jnp.int32, (Q_BLOCK, span + C), 1)
    kpos = n * Q_BLOCK + ik - Q_BLOCK
    qpos = n * Q_BLOCK + iq
    local = (b >= nC) & (jnp.abs(qpos - kpos) <= WINDOW) & (kpos >= 0) & (kpos < L)
    return jnp.logical_or(ik >= span, local)


def attention(name, q, k, v, sink, dm):
    T, HQ, KW = dm.T, dm.HQ, dm.KW
    H, KV = HQ // HEAD_DIM, KW // HEAD_DIM
    G = H // KV
    scale = HEAD_DIM ** -0.5

    def body(q_ref, kp, kc, kn, kx, vp, vc, vn, vx, sink_ref, o_ref, lse_ref):
        valid = _attn_mask(dm, pl.program_id(0))
        lane = lax.broadcasted_iota(jnp.int32, (Q_BLOCK, LSE_W), 1)
        lse_all = jnp.zeros((Q_BLOCK, LSE_W), F32)
        for kh in range(KV):
            ks = slice(kh * HEAD_DIM, (kh + 1) * HEAD_DIM)
            k_all = jnp.concatenate([kp[:, ks], kc[:, ks], kn[:, ks], kx[:, ks]], axis=0)
            v_all = jnp.concatenate([vp[:, ks], vc[:, ks], vn[:, ks], vx[:, ks]], axis=0)
            for g in range(G):
                hd = kh * G + g
                hs = slice(hd * HEAD_DIM, (hd + 1) * HEAD_DIM)
                s = lax.dot_general(q_ref[:, hs], k_all, _DIMS["nt"], preferred_element_type=F32) * scale
                s = jnp.where(valid, s, NEG_INF)
                snk = sink_ref[0, hd]
                mx = jnp.maximum(jnp.max(s, axis=-1, keepdims=True), snk)
                p = jnp.exp(s - mx)
                den = jnp.sum(p, axis=-1, keepdims=True) + jnp.exp(snk - mx)
                o = jnp.dot(p.astype(BF16), v_all, preferred_element_type=F32) / den
                o_ref[:, hs] = o.astype(o_ref.dtype)
                lse_all = jnp.where(lane == hd, mx + jnp.log(den), lse_all)
        lse_ref[...] = lse_all

    kv = _attn_specs(dm)
    return pl.pallas_call(
        body, name=name, grid=(T // Q_BLOCK,),
        in_specs=[pl.BlockSpec((Q_BLOCK, HQ), lambda b: (b, 0))] + kv + kv + [pl.BlockSpec(memory_space=pltpu.SMEM)],
        out_specs=[pl.BlockSpec((Q_BLOCK, HQ), lambda b: (b, 0)), pl.BlockSpec((Q_BLOCK, LSE_W), lambda b: (b, 0))],
        out_shape=[jax.ShapeDtypeStruct((T, HQ), BF16), jax.ShapeDtypeStruct((T, LSE_W), F32)],
        compiler_params=_cparams(("parallel",)),
    )(q, k, k, k, k, v, v, v, v, sink)


def attention_bwd(name, q, k, v, sink, o, lse, do, dm):
    T, HQ, KW, C = dm.T, dm.HQ, dm.KW, dm.C
    H, KV = HQ // HEAD_DIM, KW // HEAD_DIM
    G = H // KV
    nC, nB = C // Q_BLOCK, T // Q_BLOCK
    scale = HEAD_DIM ** -0.5
    span = 3 * Q_BLOCK

    def body(q_ref, kp, kc, kn, kx, vp, vc, vn, vx, sink_ref, o_ref, lse_ref, do_ref,
             dq_ref, dk_ref, dv_ref, ds_ref):
        b = pl.program_id(0)

        @pl.when(b == 0)
        def _():
            dk_ref[...] = jnp.zeros_like(dk_ref)
            dv_ref[...] = jnp.zeros_like(dv_ref)
            ds_ref[...] = jnp.zeros_like(ds_ref)

        valid = _attn_mask(dm, b)
        starts = [pl.multiple_of(jnp.clip(b + off, nC, nB - 1) * Q_BLOCK, Q_BLOCK) for off in (-1, 0, 1)]
        lane = lax.broadcasted_iota(jnp.int32, (Q_BLOCK, LSE_W), 1)
        lse_all = lse_ref[...]
        dsink = jnp.zeros((1, LSE_W), F32)
        for kh in range(KV):
            ks = slice(kh * HEAD_DIM, (kh + 1) * HEAD_DIM)
            k_all = jnp.concatenate([kp[:, ks], kc[:, ks], kn[:, ks], kx[:, ks]], axis=0)
            v_all = jnp.concatenate([vp[:, ks], vc[:, ks], vn[:, ks], vx[:, ks]], axis=0)
            dk_all = jnp.zeros((span + C, HEAD_DIM), F32)
            dv_all = jnp.zeros((span + C, HEAD_DIM), F32)
            for g in range(G):
                hd = kh * G + g
                hs = slice(hd * HEAD_DIM, (hd + 1) * HEAD_DIM)
                qh = q_ref[:, hs]
                doh = do_ref[:, hs]
                s = lax.dot_general(qh, k_all, _DIMS["nt"], preferred_element_type=F32) * scale
                s = jnp.where(valid, s, NEG_INF)
                lse_h = jnp.sum(jnp.where(lane == hd, lse_all, 0.0), axis=-1, keepdims=True)
                p = jnp.exp(s - lse_h)
                delta = jnp.sum(doh * o_ref[:, hs].astype(F32), axis=-1, keepdims=True)
                dob = doh.astype(BF16)
                dp = lax.dot_general(dob, v_all, _DIMS["nt"], preferred_element_type=F32)
                dsc = (p * (dp - delta) * scale).astype(BF16)
                dq_ref[:, hs] = jnp.dot(dsc, k_all, preferred_element_type=F32)
                dk_all = dk_all + lax.dot_general(dsc, qh, _DIMS["tn"], preferred_element_type=F32)
                dv_all = dv_all + lax.dot_general(p.astype(BF16), dob, _DIMS["tn"], preferred_element_type=F32)
                p_sink = jnp.exp(sink_ref[0, hd] - lse_h)
                dsink = dsink + jnp.where(lane[0:1] == hd, -jnp.sum(p_sink * delta), 0.0)
            for j, st in enumerate(starts):
                rows = pl.ds(st, Q_BLOCK)
                dk_ref[rows, ks] += dk_all[j * Q_BLOCK:(j + 1) * Q_BLOCK]
                dv_ref[rows, ks] += dv_all[j * Q_BLOCK:(j + 1) * Q_BLOCK]
            dk_ref[0:C, ks] += dk_all[span:]
            dv_ref[0:C, ks] += dv_all[span:]
        ds_ref[...] += dsink

    kv = _attn_specs(dm)
    qspec = pl.BlockSpec((Q_BLOCK, HQ), lambda b: (b, 0))
    full = pl.BlockSpec((T, KW), lambda b: (0, 0))
    return pl.pallas_call(
        body, name=name, grid=(nB,),
        in_specs=[qspec] + kv + kv + [pl.BlockSpec(memory_space=pltpu.SMEM), qspec,
                                      pl.BlockSpec((Q_BLOCK, LSE_W), lambda b: (b, 0)), qspec],
        out_specs=[qspec, full, full, pl.BlockSpec((1, LSE_W), lambda b: (0, 0))],
        out_shape=[jax.ShapeDtypeStruct((T, HQ), F32), jax.ShapeDtypeStruct((T, KW), F32),
                   jax.ShapeDtypeStruct((T, KW), F32), jax.ShapeDtypeStruct((1, LSE_W), F32)],
        compiler_params=_cparams(("arbitrary",)),
    )(q, k, k, k, k, v, v, v, v, sink, o, lse, do)


def merge(name, z, lifted, b_merge, dm):
    T, D, tm, cw = dm.T, dm.D, dm.tme, dm.cw
    gcol = dm.off_g // cw
    per = D // cw

    def body(g0, g1, g2, l0, l1, l2, b_ref, o_ref):
        acc = jnp.zeros((tm, cw), F32)
        for i, (g, lf) in enumerate(((g0, l0), (g1, l1), (g2, l2))):
            acc = acc + jax.nn.sigmoid(g[...] + b_ref[i]) * lf[...]
        o_ref[...] = acc.astype(o_ref.dtype)

    gspecs = [pl.BlockSpec((tm, cw), lambda i, j, br=br: (i, gcol + br * per + j)) for br in range(N_BRANCH)]
    blk = pl.BlockSpec((tm, cw), lambda i, j: (i, j))
    return pl.pallas_call(
        body, name=name, grid=(T // tm, per),
        in_specs=gspecs + [blk] * 3 + [pl.BlockSpec((N_BRANCH, 1, cw), lambda i, j: (0, 0, j))], out_specs=blk,
        out_shape=jax.ShapeDtypeStruct((T, D), BF16), compiler_params=_cparams(("parallel", "parallel")),
    )(z, z, z, *lifted, b_merge)


def merge_bwd(name, dmerged, z, lifted, b_merge, dz, dm):
    T, D, tm, cw = dm.T, dm.D, dm.tme, dm.cw
    gcol = dm.off_g // cw
    per = D // cw

    def body(d_ref, g_ref, l0, l1, l2, b_ref, dz_in, o0, o1, o2, dz_ref, db_ref):
        br = pl.program_id(2)

        @pl.when(jnp.logical_and(pl.program_id(1) == 0, br == 0))
        def _():
            db_ref[...] = jnp.zeros_like(db_ref)

        d = d_ref[...]
        gate = jax.nn.sigmoid(g_ref[...] + b_ref[br])
        dlift = (d * gate).astype(o0.dtype)
        for b, (l_ref, o_ref) in enumerate(((l0, o0), (l1, o1), (l2, o2))):
            @pl.when(br == b)
            def _(l_ref=l_ref, o_ref=o_ref):
                o_ref[...] = dlift
                dg = d * l_ref[...] * gate * (1.0 - gate)
                dz_ref[...] = dg.astype(dz_ref.dtype)
                db_ref[b] += jnp.sum(dg, axis=0, keepdims=True)

    blk = pl.BlockSpec((tm, cw), lambda j, i, b: (i, j))
    zblk = pl.BlockSpec((tm, cw), lambda j, i, b: (i, gcol + b * per + j))
    return pl.pallas_call(
        body, name=name, grid=(per, T // tm, N_BRANCH),
        in_specs=[blk, zblk, blk, blk, blk, pl.BlockSpec((N_BRANCH, 1, cw), lambda j, i, b: (0, 0, j)),
                  pl.BlockSpec(memory_space=pl.ANY)],
        out_specs=[blk, blk, blk, zblk, pl.BlockSpec((N_BRANCH, 1, cw), lambda j, i, b: (0, 0, j))],
        out_shape=[jax.ShapeDtypeStruct((T, D), BF16)] * 3 + [jax.ShapeDtypeStruct(dz.shape, dz.dtype),
                                                             jax.ShapeDtypeStruct((N_BRANCH, 1, D), F32)],
        input_output_aliases={6: 3}, compiler_params=_cparams(("parallel", "arbitrary", "arbitrary")),
    )(dmerged, z, *lifted, b_merge, dz)


def loss_head(name, h, gf, target, dm):
    T, D, tm, nctx = dm.T, dm.D, dm.tme, dm.nctx

    def body(h_ref, g_ref, t_ref, dh_ref, loss_ref, dg_ref):
        i = pl.program_id(0)

        @pl.when(i == 0)
        def _():
            loss_ref[...] = jnp.zeros_like(loss_ref)
            dg_ref[...] = jnp.zeros_like(dg_ref)

        @pl.when(i < nctx)
        def _():
            dh_ref[...] = jnp.zeros_like(dh_ref)

        @pl.when(i >= nctx)
        def _():
            x = h_ref[...]
            r = lax.rsqrt(jnp.mean(x * x, axis=-1, keepdims=True) + EPS)
            n = x * r
            g = g_ref[...]
            err = n * g - t_ref[...]
            loss_ref[...] += jnp.sum(err * err) * (0.5 / D)
            dy = err * (1.0 / D)
            dg_ref[...] += jnp.sum(dy * n, axis=0, keepdims=True)
            dn = dy * g
            dh_ref[...] = r * (dn - n * jnp.mean(dn * n, axis=-1, keepdims=True))

    row = pl.BlockSpec((tm, D), lambda i: (i, 0))
    return pl.pallas_call(
        body, name=name, grid=(T // tm,),
        in_specs=[row, pl.BlockSpec((1, D), lambda i: (0, 0)),
                  pl.BlockSpec((tm, D), lambda i: (jnp.maximum(i - nctx, 0), 0))],
        out_specs=[row, pl.BlockSpec((1, 128), lambda i: (0, 0)), pl.BlockSpec((1, D), lambda i: (0, 0))],
        out_shape=[jax.ShapeDtypeStruct((T, D), F32), jax.ShapeDtypeStruct((1, 128), F32),
                   jax.ShapeDtypeStruct((1, D), F32)],
        compiler_params=_cparams(("arbitrary",)),
    )(h, gf, target)


_HI = lax.Precision.HIGHEST
ADA_ROWS = 16


def ada_forward(name, cond, ada_w, bias):
    _, D, Aq = ada_w.shape
    tc = _pick(Aq, (1536, 1152, 768, 512, 384, 256, 128))
    tk = _ktile(D)
    nk = D // tk

    def body(c_ref, w_ref, b_ref, o_ref):
        k = pl.program_id(2)

        @pl.when(k == 0)
        def _():
            o_ref[...] = jnp.zeros_like(o_ref) + b_ref[...]

        o_ref[...] += jnp.dot(_silu(c_ref[...]), w_ref[...], precision=_HI, preferred_element_type=F32)

    return pl.pallas_call(
        body, name=name, grid=(2, Aq // tc, nk),
        in_specs=[pl.BlockSpec((ADA_ROWS, tk), lambda l, j, k: (0, k)),
                  pl.BlockSpec((None, tk, tc), lambda l, j, k: (l, k, j)),
                  pl.BlockSpec((None, 1, tc), lambda l, j, k: (l, 0, j))],
        out_specs=pl.BlockSpec((None, ADA_ROWS, tc), lambda l, j, k: (l, 0, j)),
        out_shape=jax.ShapeDtypeStruct((2, ADA_ROWS, Aq), F32),
        compiler_params=_cparams(("parallel", "parallel", "arbitrary")),
    )(cond, ada_w, bias)


def ada_cond_grad(name, dmod, ada_w):
    _, D, Aq = ada_w.shape
    tc = _pick(Aq, (1536, 1152, 768, 512, 384, 256, 128))
    tn = _ktile(D)
    nc = Aq // tc

    def body(d_ref, w_ref, o_ref):
        @pl.when(jnp.logical_and(pl.program_id(1) == 0, pl.program_id(2) == 0))
        def _():
            o_ref[...] = jnp.zeros_like(o_ref)

        o_ref[...] += lax.dot_general(d_ref[...], w_ref[...], _DIMS["nt"], precision=_HI, preferred_element_type=F32)

    return pl.pallas_call(
        body, name=name, grid=(D // tn, 2, nc),
        in_specs=[pl.BlockSpec((None, ADA_ROWS, tc), lambda j, l, c: (l, 0, c)),
                  pl.BlockSpec((None, tn, tc), lambda j, l, c: (l, j, c))],
        out_specs=pl.BlockSpec((ADA_ROWS, tn), lambda j, l, c: (0, j)),
        out_shape=jax.ShapeDtypeStruct((ADA_ROWS, D), F32),
        compiler_params=_cparams(("parallel", "arbitrary", "arbitrary")),
    )(dmod, ada_w)


def ada_update(name, cond, dmod, w, m, v):
    _, D, Aq = w.shape
    tc = _pick(Aq, (1536, 1152, 768, 512, 384, 256, 128))
    tr = 128 if D % 128 == 0 else D
    bc1 = 1.0 - ADAM_B1 ** ADAM_STEP
    bc2 = 1.0 - ADAM_B2 ** ADAM_STEP

    def body(c_ref, d_ref, w_ref, m_ref, v_ref, go_ref, dl_ref, mo_ref, vo_ref):
        g = lax.dot_general(_silu(c_ref[...]), d_ref[...], _DIMS["tn"], precision=_HI, preferred_element_type=F32)
        mn = ADAM_B1 * m_ref[...] + (1.0 - ADAM_B1) * g
        vn = ADAM_B2 * v_ref[...] + (1.0 - ADAM_B2) * (g * g)
        go_ref[...] = g
        dl_ref[...] = -ADAM_LR * ((mn / bc1) / (jnp.sqrt(vn / bc2) + ADAM_EPS) + ADAM_WD * w_ref[...])
        mo_ref[...] = mn
        vo_ref[...] = vn

    blk = pl.BlockSpec((None, tr, tc), lambda l, i, j: (l, i, j))
    return pl.pallas_call(
        body, name=name, grid=(2, D // tr, Aq // tc),
        in_specs=[pl.BlockSpec((ADA_ROWS, tr), lambda l, i, j: (0, i)),
                  pl.BlockSpec((None, ADA_ROWS, tc), lambda l, i, j: (l, 0, j)), blk, blk, blk],
        out_specs=[blk] * 4, out_shape=[jax.ShapeDtypeStruct(w.shape, F32)] * 4,
        compiler_params=_cparams(("parallel", "parallel", "parallel")),
    )(cond, dmod, w, m, v)


def dmod_assemble(name, gathered):
    A = gathered.shape[-1]
    tc = _pick(A, (2048, 1024, 512, 256, 128))

    def body(g_ref, o_ref, b_ref):
        ctx = g_ref[0, 1]
        for dev in range(1, N_DEV):
            ctx = ctx + g_ref[dev, 1]
        tot = ctx
        for dev in range(N_DEV):
            o_ref[dev:dev + 1, :] = g_ref[dev, 0]
            tot = tot + g_ref[dev, 0]
        o_ref[N_DEV:N_DEV + 1, :] = ctx
        o_ref[N_DEV + 1:, :] = jnp.zeros((ADA_ROWS - N_DEV - 1, tc), F32)
        b_ref[...] = tot

    return pl.pallas_call(
        body, name=name, grid=(2, A // tc),
        in_specs=[pl.BlockSpec((N_DEV, None, 2, 1, tc), lambda l, j: (0, l, 0, 0, j))],
        out_specs=[pl.BlockSpec((None, ADA_ROWS, tc), lambda l, j: (l, 0, j)),
                   pl.BlockSpec((None, 1, tc), lambda l, j: (l, 0, j))],
        out_shape=[jax.ShapeDtypeStruct((2, ADA_ROWS, A), F32), jax.ShapeDtypeStruct((2, 1, A), F32)],
        compiler_params=_cparams(("parallel", "parallel")),
    )(gathered)


def sum_devices(name, gathered):
    _, R, W = gathered.shape
    tr = _row_tile(R, W, budget=1 << 18)

    def body(g_ref, all_ref, chip_ref):
        even = g_ref[0]
        odd = g_ref[1]
        for dev in range(2, N_DEV, 2):
            even = even + g_ref[dev]
            odd = odd + g_ref[dev + 1]
        all_ref[...] = even + odd
        chip_ref[...] = even

    blk = pl.BlockSpec((tr, W), lambda i: (i, 0))
    return pl.pallas_call(
        body, name=name, grid=(R // tr,),
        in_specs=[pl.BlockSpec((N_DEV, tr, W), lambda i: (0, i, 0))], out_specs=[blk, blk],
        out_shape=[jax.ShapeDtypeStruct((R, W), F32)] * 2, compiler_params=_cparams(("parallel",)),
    )(gathered)


PACK_ROWS = 1024


def _pack(arrays):
    flat = jnp.concatenate([a.reshape(-1).astype(F32) for a in arrays])
    pad = (-flat.shape[0]) % (PACK_ROWS * 128)
    return jnp.pad(flat, (0, pad)).reshape(-1, 128)


def _unpack(buf, shapes, lead=()):
    flat = buf.reshape(lead + (-1,))
    out, start = [], 0
    for s in shapes:
        n = math.prod(s)
        out.append(flat[..., start:start + n].reshape(lead + tuple(s)))
        start += n
    return out


def _unshard_last(g):
    g = jnp.moveaxis(g, 0, -2)
    return g.reshape(g.shape[:-2] + (g.shape[-2] * g.shape[-1],))


class WeightStream:
    AHEAD = 2

    def __init__(self, keys, make_land):
        self.keys, self.make_land = list(keys), make_land
        self.pending, self.values, self.tokens, self.started = {}, {}, [], 0
        for _ in range(self.AHEAD):
            self._start_next(())

    def _start_next(self, deps):
        if self.started < len(self.keys):
            key = self.keys[self.started]
            self.started += 1
            land, view = self.make_land(key, deps)
            ss, rs, (land,), token = exchange_start(f"{key}_start", [land], _gather_plan)
            self.pending[key] = (ss, rs, land, view)
            self.tokens.append(token)

    def get(self, key, after=None):
        if key not in self.values:
            ss, rs, land, view = self.pending.pop(key)
            (full,) = exchange_wait(f"{key}_wait", ss, rs, [land], _gather_plan, after)
            self.values[key] = full if view is None else full.reshape(view)
            self._start_next((full,))
        return self.values[key]

    def take_tokens(self):
        out, self.tokens = tuple(self.tokens), []
        return out


def _ffn_forward(tag, h, gn, modtab, s, ws, k13, k2, dm):
    u = norm_mod(f"{tag}_norm", h, gn, modtab, s, dm)
    w13g = ws.get(k13, u)
    gu = mm_cols(f"{tag}_w13", u, w13g, BF16, flat=False, deps=ws.take_tokens())
    act = swiglu(f"{tag}_act", gu, dm)
    w2g = ws.get(k2, act)
    h_out, f = mm_rows(f"{tag}_w2", act, w2g, deps=ws.take_tokens(), resid=(h, modtab, s, 0.5, dm.C))
    return h_out, (h, u, gu, act, f)


def _ffn_backward(tag, dh, saved, gn, modtab, s, ws, k13, k2, dm, deps=()):
    h, u, gu, act, f = saved
    w13g, w2g = ws.get(k13), ws.get(k2)
    df, dgate = resid_bwd(f"{tag}_res_bwd", dh, f, modtab, s, 0.5, dm, deps)
    dact = mm_rows_t(f"{tag}_dact", df, w2g)
    dw2 = mm_rows_grad(f"{tag}_dw2", act, df).reshape(N_CHIP, -1, df.shape[-1])
    dgu = swiglu_bwd(f"{tag}_act_bwd", dact, gu, dm)
    du = mm_cols_t(f"{tag}_du", dgu, w13g, flat=False)
    dw13 = mm_cols_grad(f"{tag}_dw13", u, dgu, flat=False)
    dh_in, dss, dgn = norm_mod_bwd(f"{tag}_norm_bwd", du, h, gn, modtab, s, dh, dm)
    return dh_in, dw13, dw2, dss, dgate, dgn


def kernel(x, c, ctx, c_ctx, ada_w, ada_b, norm_g, ffn1_w13, ffn1_w2, w_in, b_merge, rnn_conv_w, rnn_conv_b, lru_w_a, lru_b_a, lru_w_x, lru_b_x, lru_lambda, sc_conv_w, attn_sink, w_branch, w_out, ffn2_w13, ffn2_w2, final_norm_g, loss_target, m_c_ctx, m_ada_w, m_ada_b, m_norm_g, m_ffn1_w13, m_ffn1_w2, m_w_in, m_b_merge, m_rnn_conv_w, m_rnn_conv_b, m_lru_w_a, m_lru_b_a, m_lru_w_x, m_lru_b_x, m_lru_lambda, m_sc_conv_w, m_attn_sink, m_w_branch, m_w_out, m_ffn2_w13, m_ffn2_w2, m_final_norm_g, v_c_ctx, v_ada_w, v_ada_b, v_norm_g, v_ffn1_w13, v_ffn1_w2, v_w_in, v_b_merge, v_rnn_conv_w, v_rnn_conv_b, v_lru_w_a, v_lru_b_a, v_lru_w_x, v_lru_b_x, v_lru_lambda, v_sc_conv_w, v_attn_sink, v_w_branch, v_w_out, v_ffn2_w13, v_ffn2_w2, v_final_norm_g):
    weights = dict(c_ctx=c_ctx, ada_w=ada_w, ada_b=ada_b, norm_g=norm_g, ffn1_w13=ffn1_w13, ffn1_w2=ffn1_w2, w_in=w_in,
                   b_merge=b_merge, rnn_conv_w=rnn_conv_w, rnn_conv_b=rnn_conv_b, lru_w_a=lru_w_a, lru_b_a=lru_b_a,
                   lru_w_x=lru_w_x, lru_b_x=lru_b_x, lru_lambda=lru_lambda, sc_conv_w=sc_conv_w, attn_sink=attn_sink,
                   w_branch=w_branch, w_out=w_out, ffn2_w13=ffn2_w13, ffn2_w2=ffn2_w2, final_norm_g=final_norm_g)
    mom_m = dict(c_ctx=m_c_ctx, ada_w=m_ada_w, ada_b=m_ada_b, norm_g=m_norm_g, ffn1_w13=m_ffn1_w13, ffn1_w2=m_ffn1_w2,
                 w_in=m_w_in, b_merge=m_b_merge, rnn_conv_w=m_rnn_conv_w, rnn_conv_b=m_rnn_conv_b, lru_w_a=m_lru_w_a,
                 lru_b_a=m_lru_b_a, lru_w_x=m_lru_w_x, lru_b_x=m_lru_b_x, lru_lambda=m_lru_lambda,
                 sc_conv_w=m_sc_conv_w, attn_sink=m_attn_sink, w_branch=m_w_branch, w_out=m_w_out,
                 ffn2_w13=m_ffn2_w13, ffn2_w2=m_ffn2_w2, final_norm_g=m_final_norm_g)
    mom_v = dict(c_ctx=v_c_ctx, ada_w=v_ada_w, ada_b=v_ada_b, norm_g=v_norm_g, ffn1_w13=v_ffn1_w13, ffn1_w2=v_ffn1_w2,
                 w_in=v_w_in, b_merge=v_b_merge, rnn_conv_w=v_rnn_conv_w, rnn_conv_b=v_rnn_conv_b, lru_w_a=v_lru_w_a,
                 lru_b_a=v_lru_b_a, lru_w_x=v_lru_w_x, lru_b_x=v_lru_b_x, lru_lambda=v_lru_lambda,
                 sc_conv_w=v_sc_conv_w, attn_sink=v_attn_sink, w_branch=v_w_branch, w_out=v_w_out,
                 ffn2_w13=v_ffn2_w13, ffn2_w2=v_ffn2_w2, final_norm_g=v_final_norm_g)
    order = list(weights)

    dm = Dims()
    dm.D = D = x.shape[-1]
    dm.L = L = x.shape[1]
    dm.C = C = ctx.shape[1]
    dm.T = T = L + C
    dm.RW = RW = rnn_conv_b.shape[-1]
    dm.NB = lru_w_a.shape[2]
    H = attn_sink.shape[-1]
    dm.HQ = HQ = H * HEAD_DIM
    NZ = w_in.shape[-1] * N_CHIP
    dm.KW = KW = (NZ - 5 * RW - HQ - N_BRANCH * D) // 2
    dm.off_q = 5 * RW
    dm.off_k = dm.off_q + HQ
    dm.off_g = dm.off_k + 2 * KW
    dm.tme = _pick(C, (256, 128))
    dm.nt = T // dm.tme
    dm.nctx = C // dm.tme
    dm.cw = next(w for w in (512, 256, 128) if dm.off_g % w == 0 and D % w == 0)
    A = ada_b.shape[-1]
    Aq = ada_w.shape[-1]
    assert dm.off_q % HQ == 0 and dm.off_k % KW == 0 and HQ % KW == 0 and L % dm.tme == 0 and RW == HQ
    assert C % Q_BLOCK == 0 and L % Q_BLOCK == 0 and D % N_CHIP == 0 and A == N_MOD * D

    mx, my, mc = _my_pos()
    j_me = 2 * mx + my
    b_me = 4 * mx + 2 * my + mc

    big = ["ffn1_w13", "ffn1_w2", "w_in", "w_branch", "w_out", "ffn2_w13", "ffn2_w2"]
    j_idx = jnp.reshape(j_me, (1,)).astype(jnp.int32)
    FFq = ffn1_w2.shape[1]
    views = {"ffn1_w2": (2, 2 * FFq, D), "ffn2_w2": (2, 2 * FFq, D), "w_out": (D, D),
             "w_branch": (N_CHIP, N_BRANCH, RW, D // N_CHIP)}

    def make_land(key, deps):
        l, n = int(key[1]), key[3:]
        return cast_into_slot(f"{key}_cast", weights[n], l, j_idx, deps), views.get(n)

    ws = WeightStream([f"l{l}_{n}" for l in range(2) for n in big], make_land)
    first_tokens = ws.take_tokens()

    small_sharded = ["norm_g", "b_merge", "rnn_conv_w", "lru_b_a", "lru_b_x", "lru_lambda", "sc_conv_w"]
    pack1 = _pack([c] + [weights[n] for n in small_sharded])
    pack1 = pack1 + sum(t[0, 0] for t in first_tokens)
    g1 = allgather8("gather_small_params", pack1).reshape(N_DEV, -1, 128)
    parts = _unpack(g1, [c.shape] + [weights[n].shape for n in small_sharded], lead=(N_DEV,))
    c_all = parts[0].reshape(N_DEV, D)
    full = {n: _unshard_last(p[0::2]) for n, p in zip(small_sharded, parts[1:])}
    cond = jnp.concatenate([c_all, c_ctx[None, :], jnp.zeros((ADA_ROWS - N_DEV - 1, D), F32)], axis=0)

    bias_q = lax.dynamic_slice_in_dim(ada_b, j_me * Aq, Aq, axis=1)[:, None, :]
    mod_q = ada_forward("ada_forward", cond, ada_w, bias_q)
    g2 = allgather8("gather_mod", mod_q.reshape(-1, 128)).reshape(N_DEV, 2, ADA_ROWS, Aq)
    mod_full = _unshard_last(g2[0::2])
    mod_lat = lax.dynamic_index_in_dim(mod_full, b_me, axis=1, keepdims=False)
    mod_ctx = mod_full[:, N_DEV]
    modtabs = [jnp.stack([mod_ctx[l], mod_lat[l]]).reshape(2, N_MOD, 1, D) for l in range(2)]

    cos, sin = _rope_tables(dm)
    sink = attn_sink.reshape(2, 1, H)
    lw = dict(w_a=lru_w_a, w_x=lru_w_x,
              b_a=full["lru_b_a"][:, :, None, :], b_x=full["lru_b_x"][:, :, None, :],
              lam=full["lru_lambda"][:, :, None, :])
    gn = full["norm_g"]
    bm = full["b_merge"][:, :, None, :]
    rcw = full["rnn_conv_w"][:, :, None, :]
    scw = full["sc_conv_w"][:, :, None, :]

    h = jnp.concatenate([ctx[0], x[0]], axis=0)
    saved = []
    for l in range(2):
        mt = modtabs[l]
        sv = {}
        h, sv["ffn1"] = _ffn_forward(f"l{l}_ffn1", h, gn[l, 0:1], mt, 0, ws, f"l{l}_ffn1_w13", f"l{l}_ffn1_w2", dm)
        sv["h_mix"] = h
        u = norm_mod(f"l{l}_mix_norm", h, gn[l, 1:2], mt, 1, dm)
        wing = ws.get(f"l{l}_w_in", u)
        z = mm_cols(f"l{l}_w_in", u, wing, F32, flat=True, deps=ws.take_tokens())
        xa = rnn_conv(f"l{l}_rnn_conv", z, rcw[l], rnn_conv_b[l][None, :], dm)
        scans = []
        for d in range(2):
            a_d, u_d = lru_gates(f"l{l}_lru_gates{d}", xa, lw, l, d, dm)
            h_d = lru_scan(f"l{l}_lru_scan{d}", a_d, u_d, True, d == 1, dm)
            scans.append((a_d, u_d, h_d))
        ya = rnn_out(f"l{l}_rnn_out", scans[0][2], scans[1][2], z, dm)
        yb = short_conv(f"l{l}_short_conv", z, scw[l], dm)
        qr, kr, vv = qkv_prep(f"l{l}_qkv", z, cos, sin, dm)
        yatt, lse = attention(f"l{l}_attn", qr, kr, vv, sink[l], dm)
        ys = (ya, yb, yatt)
        wbg = ws.get(f"l{l}_w_branch", yatt)
        lifted = [mm_branch(f"l{l}_lift{br}", ys[br], wbg, br, deps=ws.take_tokens()) for br in range(N_BRANCH)]
        merged = merge(f"l{l}_merge", z, lifted, bm[l], dm)
        woutg = ws.get(f"l{l}_w_out", merged)
        h, y = mm_plain(f"l{l}_w_out", merged, woutg, "nn", F32, deps=ws.take_tokens(), resid=(h, mt, 1, 1.0, dm.C))
        sv.update(u=u, z=z, xa=xa, scans=scans, ys=ys, qkv=(qr, kr, vv), lse=lse, lifted=lifted, merged=merged, y=y)
        h, sv["ffn2"] = _ffn_forward(f"l{l}_ffn2", h, gn[l, 2:3], mt, 2, ws, f"l{l}_ffn2_w13", f"l{l}_ffn2_w2", dm)
        saved.append(sv)

    dh, loss_vec, d_final_g = loss_head("loss_head", h, final_norm_g[None, :], loss_target[0], dm)
    loss = lax.psum(loss_vec[0, 0], ("x", "y", "c"))

    small = {n: [None, None] for n in ["norm_g", "b_merge", "rnn_conv_w", "rnn_conv_b", "lru_w_a", "lru_b_a", "lru_w_x",
                                       "lru_b_x", "lru_lambda", "sc_conv_w", "attn_sink"]}
    dmods = [None, None]
    scatters = []

    def scatter(name, keyed):
        grads3 = [g.reshape(N_CHIP, -1, g.shape[-1]) for g in keyed.values()]
        lands = [lax.empty((3,) + g.shape[1:], BF16) for g in grads3]
        ss, rs, bufs, token = exchange_start(f"{name}_start", grads3 + lands, _scatter_plan(len(grads3)))
        scatters.append((name, ss, rs, bufs, list(keyed)))
        return token

    tok = ()
    for l in (1, 0):
        mt = modtabs[l]
        sv = saved[l]
        dh, dw13, dw2, dss2, dgate2, dgn2 = _ffn_backward(f"l{l}_ffn2", dh, sv["ffn2"], gn[l, 2:3], mt, 2,
                                                         ws, f"l{l}_ffn2_w13", f"l{l}_ffn2_w2", dm, deps=tok)
        tok = (scatter(f"l{l}_scatter_ffn2", {("ffn2_w13", l): dw13, ("ffn2_w2", l): dw2}),)

        dyg, dgate1 = resid_bwd(f"l{l}_mix_res_bwd", dh, sv["y"], mt, 1, 1.0, dm, deps=tok)
        woutg, wbg, wing = ws.get(f"l{l}_w_out"), ws.get(f"l{l}_w_branch"), ws.get(f"l{l}_w_in")
        dmerged = mm_plain(f"l{l}_dmerged", dyg, woutg, "nt", F32)
        mix_grads = {("w_out", l): mm_plain_grad(f"l{l}_dw_out", sv["merged"], dyg)}
        dz = lax.empty((T, NZ), BF16)
        dl0, dl1, dl2, dz, dbm = merge_bwd(f"l{l}_merge_bwd", dmerged, sv["z"], sv["lifted"], bm[l], dz, dm)
        dys = []
        for br, dl in enumerate((dl0, dl1, dl2)):
            dys.append(mm_branch_t(f"l{l}_dy{br}", dl, wbg, br))
            mix_grads[("w_branch", l, br)] = mm_branch_grad(f"l{l}_dwb{br}", sv["ys"][br], dl)
        small["b_merge"][l] = dbm[:, 0]

        qr, kr, vv = sv["qkv"]
        dq, dk, dv, dsink = attention_bwd(f"l{l}_attn_bwd", qr, kr, vv, sink[l], sv["ys"][2], sv["lse"], dys[2], dm)
        dz = qkv_bwd(f"l{l}_qkv_bwd", dq, dk, dv, cos, sin, dz, dm)
        small["attn_sink"][l] = dsink[0, :H]

        dz, dscw = short_conv_bwd(f"l{l}_short_conv_bwd", dys[1], sv["z"], scw[l], dz, dm)
        small["sc_conv_w"][l] = dscw[:, 0]

        (a0, u0, h0), (a1, u1, h1) = sv["scans"]
        dhs, drg = rnn_out_bwd(f"l{l}_rnn_out_bwd", dys[0], h0, h1, sv["z"], dm)
        dxa, lru_sums = [], []
        for d, (a_d, u_d, h_d) in enumerate(sv["scans"]):
            lam_d, dla_d = lru_scan_bwd(f"l{l}_lru_scan_bwd{d}", a_d, u_d, h_d, dhs, False, d == 0, dm)
            outs = lru_gates_bwd(f"l{l}_lru_gates_bwd{d}", sv["xa"], lw, l, d, lam_d, dla_d, dm)
            dxa.append(outs[0])
            lru_sums.append(outs[1:])
        dz, drcw, drcb = rnn_conv_bwd(f"l{l}_rnn_conv_bwd", dxa[0], dxa[1], drg, sv["z"], rcw[l], dz, dm)
        small["rnn_conv_w"][l] = drcw[:, 0]
        small["rnn_conv_b"][l] = drcb[0]
        for i, n in enumerate(["lru_w_a", "lru_b_a", "lru_w_x", "lru_b_x", "lru_lambda"]):
            small[n][l] = jnp.stack([lru_sums[0][i], lru_sums[1][i]]).reshape((2,) + weights[n].shape[2:-1] + (-1,))

        du = mm_cols_t(f"l{l}_du_mix", dz, wing, flat=True)
        mix_grads[("w_in", l)] = mm_cols_grad(f"l{l}_dw_in", sv["u"], dz, flat=True)
        dh, dss1, dgn1 = norm_mod_bwd(f"l{l}_mix_norm_bwd", du, sv["h_mix"], gn[l, 1:2], mt, 1, dh, dm)
        tok = (scatter(f"l{l}_scatter_mix", mix_grads),)

        dh, dw13, dw2, dss0, dgate0, dgn0 = _ffn_backward(f"l{l}_ffn1", dh, sv["ffn1"], gn[l, 0:1], mt, 0,
                                                         ws, f"l{l}_ffn1_w13", f"l{l}_ffn1_w2", dm, deps=tok)
        tok = (scatter(f"l{l}_scatter_ffn1", {("ffn1_w13", l): dw13, ("ffn1_w2", l): dw2}),)
        small["norm_g"][l] = jnp.concatenate([dgn0, dgn1, dgn2], axis=0)
        dmods[l] = jnp.concatenate([dss0, dgate0, dss1, dgate1, dss2, dgate2], axis=1).reshape(2, A)

    grad_x = dh[C:][None]

    pack_mod = jnp.stack([jnp.stack([dmods[l][1], dmods[l][0]]) for l in range(2)])
    g3 = allgather8("gather_dmod", pack_mod.reshape(-1, 128)).reshape(N_DEV, 2, 2, 1, A)
    dmod_full, d_ada_b = dmod_assemble("dmod_assemble", g3)
    dmod_q = lax.dynamic_slice_in_dim(dmod_full, j_me * Aq, Aq, axis=2)
    dcond_q = ada_cond_grad("ada_cond_grad", dmod_q, ada_w)

    small_names = list(small)
    small_parts = [jnp.stack(small[n]) for n in small_names] + [d_final_g, dcond_q[N_DEV]]
    small_shapes = [p.shape for p in small_parts]
    lru_big = [small_names.index("lru_w_a"), small_names.index("lru_w_x")]
    rest_idx = [i for i in range(len(small_parts)) if i not in lru_big]
    summed = [None] * len(small_parts)
    for i in lru_big:
        buf = _pack([small_parts[i]])
        tot, _ = sum_devices(f"sum_{small_names[i]}", allgather8(f"gather_{small_names[i]}", buf).reshape(N_DEV, -1, 128))
        summed[i] = _unpack(tot, [small_shapes[i]])[0]
    buf = _pack([small_parts[i] for i in rest_idx])
    tot, chip_tot = sum_devices("sum_small_grads", allgather8("gather_small_grads", buf).reshape(N_DEV, -1, 128))
    for i, val in zip(rest_idx, _unpack(tot, [small_shapes[i] for i in rest_idx])):
        summed[i] = val
    dcond_ctx = _unpack(chip_tot, [small_shapes[i] for i in rest_idx])[-1]
    sg = jax.nn.sigmoid(c_ctx)
    grads = dict(zip(small_names, summed[:len(small_names)]))
    grads["final_norm_g"] = summed[len(small_names)][0]
    grads["c_ctx"] = dcond_ctx * (sg * (1.0 + c_ctx * (1.0 - sg)))
    grads["ada_b"] = d_ada_b[:, 0]
    for n in small_sharded:
        g = grads[n]
        q = g.shape[-1] // N_CHIP
        grads[n] = lax.dynamic_slice_in_dim(g, j_me * q, q, axis=g.ndim - 1)

    arrived = {}

    def collect(idx, after):
        name, ss, rs, bufs, keys = scatters[idx]
        done = exchange_wait(f"{name}_wait", ss, rs, bufs, _scatter_plan(len(keys)), after)
        for i, key in enumerate(keys):
            arrived[key] = (done[i], done[len(keys) + i])

    for idx in range(len(scatters) - 1):
        collect(idx, dh)
    results = {}
    last_done = dh

    def finish(swap, after):
        n, ss, rs, bufs = swap
        own, other = exchange_wait(f"swap_{n}_wait", ss, rs, bufs, _sibling_plan, after)
        results[n] = adamw(f"adamw_{n}", weights[n], mom_m[n], mom_v[n], [own, other])
        return results[n][0]

    in_flight = None
    for n in ["ffn2_w13", "ffn2_w2", "w_out", "w_branch", "w_in", "ffn1_w13", "ffn1_w2"]:
        if not any(k[0] == n and k[1] == 0 for k in arrived):
            collect(len(scatters) - 1, last_done)
        keys = sorted((k for k in arrived if k[0] == n), key=lambda k: k[1:])
        part = sum_parts(f"sum_{n}", [arrived[k] for k in keys], j_idx).reshape(weights[n].shape)
        ss, rs, bufs, token = exchange_start(f"swap_{n}_start", [part, lax.empty(part.shape, F32)], _sibling_plan)
        if in_flight is not None:
            last_done = finish(in_flight, token)
        in_flight = (n, ss, rs, bufs)
    finish(in_flight, last_done)
    results["ada_w"] = ada_update("ada_update", cond, dmod_q, ada_w, m_ada_w, v_ada_w)
    small_all = [n for n in order if n not in results]
    pk = lambda d: _pack([d[n] for n in small_all])
    outs = adamw("adamw_small", pk(weights), pk(mom_m), pk(mom_v), [pk(grads)])
    shapes_small = [weights[n].shape for n in small_all]
    unpacked = [_unpack(o, shapes_small) for o in outs]
    for i, n in enumerate(small_all):
        results[n] = tuple(unpacked[k][i] for k in range(4))

    return (loss, grad_x, *[results[n][0] for n in order], *[results[n][1] for n in order],
            *[results[n][2] for n in order], *[results[n][3] for n in order])
```

```python
import functools
import math

import jax
import jax.numpy as jnp
from jax import lax
from jax.experimental import pallas as pl
from jax.experimental.pallas import tpu as pltpu

F32 = jnp.float32
BF16 = jnp.bfloat16
MESH = pl.DeviceIdType.MESH

HEAD_DIM = 128
GRID_W = 64
WINDOW = 128
Q_BLOCK = 128
ROPE_BASE = 10000.0
LRU_C = 8.0
EPS = 1e-6
NEG_INF = -1e30
N_MOD = 9
N_BRANCH = 3
RNN_BLOCK = 128
HALO = 8
LSE_W = 128

ADAM_LR = 0.001
ADAM_B1 = 0.9
ADAM_B2 = 0.999
ADAM_EPS = 1e-08
ADAM_WD = 0.01
ADAM_STEP = 10

VMEM_LIMIT_BYTES = 48 * 1024 * 1024
N_DEV = 8
N_CHIP = 4


def _pick(n, cands):
    for c in cands:
        if c <= n and n % c == 0:
            return c
    return n


def _cparams(sem):
    return pltpu.CompilerParams(dimension_semantics=sem, vmem_limit_bytes=VMEM_LIMIT_BYTES)


def _silu(x):
    return x * jax.nn.sigmoid(x)


def _dsilu(x):
    s = jax.nn.sigmoid(x)
    return s * (1.0 + x * (1.0 - s))


_GELU_K = math.sqrt(2.0 / math.pi)


def _gelu(x):
    return 0.5 * x * (1.0 + jnp.tanh(_GELU_K * (x + 0.044715 * x * x * x)))


def _dgelu(x):
    t = jnp.tanh(_GELU_K * (x + 0.044715 * x * x * x))
    return 0.5 * (1.0 + t) + 0.5 * x * (1.0 - t * t) * _GELU_K * (1.0 + 3.0 * 0.044715 * x * x)


def _expm1(x):
    series = x * (1.0 + x * (0.5 + x * (1.0 / 6.0 + x * (1.0 / 24.0 + x * (1.0 / 120.0)))))
    return jnp.where(jnp.abs(x) < 0.1, series, jnp.exp(x) - 1.0)


def _my_pos():
    return lax.axis_index("x"), lax.axis_index("y"), lax.axis_index("c")


_DIMS = {"nn": (((1,), (0,)), ((), ())), "nt": (((1,), (1,)), ((), ())), "tn": (((0,), (0,)), ((), ()))}


def _mm(name, a, b, *, mode, grid, a_blk, a_map, b_blk, b_map, o_blk, o_map, out_shape, out_dtype, deps=(), resid=None):
    nk = grid[-1]
    nax = len(grid)
    acc_shape = tuple(d for d in o_blk if d is not None)

    def product(a_ref, b_ref):
        return lax.dot_general(a_ref[...].astype(BF16), b_ref[...].astype(BF16), _DIMS[mode], preferred_element_type=F32)

    def write(res, rest):
        if resid is None:
            o_ref = rest[-1] if nk == 1 else rest[-2]
            o_ref[...] = res.astype(o_ref.dtype)
            return
        _, _, s, coef, n_ctx = resid
        h_ref, m_ref = rest[len(deps)], rest[len(deps) + 1]
        o_ref, f_ref = rest[len(deps) + 2], rest[len(deps) + 3]
        tm = acc_shape[0]
        row = pl.program_id(0) * tm + lax.broadcasted_iota(jnp.int32, acc_shape, 0)
        gate = jnp.where(row < n_ctx, m_ref[0, 3 * s + 2], m_ref[1, 3 * s + 2])
        f_ref[...] = res.astype(f_ref.dtype)
        o_ref[...] = h_ref[...] + (coef * gate) * res

    def body_one_step(a_ref, b_ref, *rest):
        write(product(a_ref, b_ref), rest)

    def body(a_ref, b_ref, *rest):
        acc_ref = rest[-1]
        k = pl.program_id(nax - 1)

        @pl.when(k == 0)
        def _():
            acc_ref[...] = jnp.zeros_like(acc_ref)

        acc_ref[...] += product(a_ref, b_ref)

        @pl.when(k == nk - 1)
        def _():
            write(acc_ref[...], rest)

    in_specs = [pl.BlockSpec(a_blk, a_map), pl.BlockSpec(b_blk, b_map)] + [pl.BlockSpec(memory_space=pl.ANY)] * len(deps)
    out_specs = pl.BlockSpec(o_blk, o_map)
    out_shapes = jax.ShapeDtypeStruct(out_shape, out_dtype)
    operands = (a, b, *deps)
    if resid is not None:
        h, modtab = resid[0], resid[1]
        tn = o_blk[-1]
        in_specs += [pl.BlockSpec(o_blk, o_map),
                     pl.BlockSpec((2, N_MOD, 1, tn), lambda *idx: (0, 0, 0, o_map(*idx)[-1]))]
        out_specs = [pl.BlockSpec(o_blk, o_map), pl.BlockSpec(o_blk, o_map)]
        out_shapes = [jax.ShapeDtypeStruct(out_shape, F32), jax.ShapeDtypeStruct(out_shape, BF16)]
        operands += (h, modtab)
    return pl.pallas_call(
        body_one_step if nk == 1 else body, name=name, grid=grid,
        in_specs=in_specs, out_specs=out_specs, out_shape=out_shapes,
        scratch_shapes=[] if nk == 1 else [pltpu.VMEM(acc_shape, F32)],
        compiler_params=_cparams(("parallel",) * (nax - 1) + ("arbitrary",)),
    )(*operands)


def _tiles(n):
    return _pick(n, (768, 512, 384, 256, 128, 64, 32, 16))


def _tiles_long(n):
    return _pick(n, (1408, 768, 512, 384, 256, 128, 64, 32, 16))


VMEM_TILE_BUDGET = 40 * 1024 * 1024


def _fit(n, nbytes):
    for c in (768, 512, 384, 256, 128, 64, 32, 16):
        if c <= n and n % c == 0 and nbytes(c) <= VMEM_TILE_BUDGET:
            return c
    return _pick(n, (16, 8))


def _whole(n, cap=2048):
    return n if n <= cap else _ktile(n)


def _ktile(n):
    return _pick(n, (512, 256, 128))


def mm_cols(name, a, wg, out_dtype, flat, deps=()):
    T, K = a.shape
    Nq = wg.shape[-1]
    tk = _whole(K)
    osize = jnp.dtype(out_dtype).itemsize
    tm = _fit(T, lambda t: 2 * t * Nq * osize + 4 * tk * Nq + 4 * t * tk + (4 * t * Nq if tk < K else 0))
    if flat:
        o_blk, o_map, o_shape = (tm, Nq), (lambda j, i, k: (i, j)), (T, N_CHIP * Nq)
    else:
        o_blk, o_map, o_shape = (None, tm, Nq), (lambda j, i, k: (j, i, 0)), (N_CHIP, T, Nq)
    return _mm(name, a, wg, mode="nn", grid=(N_CHIP, T // tm, K // tk),
               a_blk=(tm, tk), a_map=lambda j, i, k: (i, k),
               b_blk=(None, tk, Nq), b_map=lambda j, i, k: (j, k, 0),
               o_blk=o_blk, o_map=o_map, out_shape=o_shape, out_dtype=out_dtype, deps=deps)


def mm_cols_t(name, d, wg, flat, deps=()):
    K, Nq = wg.shape[-2:]
    T = d.shape[-2]
    tm, tn = _tiles_long(T), _ktile(K)
    if flat:
        a_blk, a_map = (tm, Nq), (lambda i, j, k: (i, k))
    else:
        a_blk, a_map = (None, tm, Nq), (lambda i, j, k: (k, i, 0))
    return _mm(name, d, wg, mode="nt", grid=(T // tm, K // tn, N_CHIP),
               a_blk=a_blk, a_map=a_map,
               b_blk=(None, tn, Nq), b_map=lambda i, j, k: (k, j, 0),
               o_blk=(tm, tn), o_map=lambda i, j, k: (i, j), out_shape=(T, K), out_dtype=F32, deps=deps)


def mm_cols_grad(name, a, d, flat):
    T, K = a.shape
    Nq = d.shape[-1] // N_CHIP if flat else d.shape[-1]
    tt, br = _tiles_long(T), _ktile(K)
    if flat:
        b_blk, b_map = (tt, Nq), (lambda j, r, t: (t, j))
    else:
        b_blk, b_map = (None, tt, Nq), (lambda j, r, t: (j, t, 0))
    return _mm(name, a, d, mode="tn", grid=(N_CHIP, K // br, T // tt),
               a_blk=(tt, br), a_map=lambda j, r, t: (t, r),
               b_blk=b_blk, b_map=b_map,
               o_blk=(None, br, Nq), o_map=lambda j, r, t: (j, r, 0),
               out_shape=(N_CHIP, K, Nq), out_dtype=BF16)


def mm_rows(name, a, wg, deps=(), resid=None):
    G, T, Kg = a.shape
    N = wg.shape[-1]
    tm, tn = _tiles(T), _pick(N, (1024, 512, 256, 128))
    return _mm(name, a, wg, mode="nn", grid=(T // tm, N // tn, G),
               a_blk=(None, tm, Kg), a_map=lambda i, j, k: (k, i, 0),
               b_blk=(None, Kg, tn), b_map=lambda i, j, k: (k, 0, j),
               o_blk=(tm, tn), o_map=lambda i, j, k: (i, j), out_shape=(T, N), out_dtype=F32, deps=deps, resid=resid)


def mm_rows_t(name, d, wg):
    T, N = d.shape
    G, Kg = wg.shape[0], wg.shape[1]
    tk = _whole(N)
    tm = _fit(T, lambda t: 4 * t * Kg + 4 * Kg * tk + 4 * t * tk + (4 * t * Kg if tk < N else 0))
    return _mm(name, d, wg, mode="nt", grid=(G, T // tm, N // tk),
               a_blk=(tm, tk), a_map=lambda j, i, k: (i, k),
               b_blk=(None, Kg, tk), b_map=lambda j, i, k: (j, 0, k),
               o_blk=(None, tm, Kg), o_map=lambda j, i, k: (j, i, 0), out_shape=(G, T, Kg), out_dtype=BF16)


def mm_rows_grad(name, a, d):
    G, T, Kg = a.shape
    N = d.shape[-1]
    tt, tn = _tiles_long(T), _ktile(N)
    return _mm(name, a, d, mode="tn", grid=(G, N // tn, T // tt),
               a_blk=(None, tt, Kg), a_map=lambda g, j, t: (g, t, 0),
               b_blk=(tt, tn), b_map=lambda g, j, t: (t, j),
               o_blk=(None, Kg, tn), o_map=lambda g, j, t: (g, 0, j), out_shape=(G, Kg, N), out_dtype=BF16)


def mm_plain(name, a, w, mode, out_dtype, deps=(), resid=None):
    T = a.shape[0]
    K, N = w.shape[-2:]
    tm = _tiles(T)
    if mode == "nn":
        tn, tk = (_whole(N) if resid is None else _pick(N, (1024, 512, 256, 128))), _whole(K)
        return _mm(name, a, w, mode="nn", grid=(T // tm, N // tn, K // tk),
                   a_blk=(tm, tk), a_map=lambda i, j, k: (i, k),
                   b_blk=(tk, tn), b_map=lambda i, j, k: (k, j),
                   o_blk=(tm, tn), o_map=lambda i, j, k: (i, j), out_shape=(T, N), out_dtype=out_dtype, deps=deps,
                   resid=resid)
    tn, tk = _whole(K), _whole(N)
    return _mm(name, a, w, mode="nt", grid=(T // tm, K // tn, N // tk),
               a_blk=(tm, tk), a_map=lambda i, j, k: (i, k),
               b_blk=(tn, tk), b_map=lambda i, j, k: (j, k),
               o_blk=(tm, tn), o_map=lambda i, j, k: (i, j), out_shape=(T, K), out_dtype=out_dtype)


def mm_plain_grad(name, a, d):
    T, K = a.shape
    N = d.shape[-1]
    tt, br, tn = _tiles_long(T), _ktile(K), _whole(N)
    return _mm(name, a, d, mode="tn", grid=(K // br, N // tn, T // tt),
               a_blk=(tt, br), a_map=lambda r, j, t: (t, r),
               b_blk=(tt, tn), b_map=lambda r, j, t: (t, j),
               o_blk=(br, tn), o_map=lambda r, j, t: (r, j), out_shape=(K, N), out_dtype=BF16)


def mm_branch(name, y, wbg, br, deps=()):
    T, RW = y.shape
    Dq = wbg.shape[-1]
    tm = _tiles_long(T)

    def body(y_ref, w_ref, *rest):
        o_ref = rest[-1]
        lhs = y_ref[...]
        for j in range(N_CHIP):
            o_ref[:, j * Dq:(j + 1) * Dq] = jnp.dot(lhs, w_ref[j], preferred_element_type=F32).astype(o_ref.dtype)

    return pl.pallas_call(
        body, name=name, grid=(T // tm,),
        in_specs=[pl.BlockSpec((tm, RW), lambda i: (i, 0)),
                  pl.BlockSpec((N_CHIP, None, RW, Dq), lambda i: (0, br, 0, 0))] + [pl.BlockSpec(memory_space=pl.ANY)] * len(deps),
        out_specs=pl.BlockSpec((tm, N_CHIP * Dq), lambda i: (i, 0)),
        out_shape=jax.ShapeDtypeStruct((T, N_CHIP * Dq), BF16), compiler_params=_cparams(("parallel",)),
    )(y, wbg, *deps)


def mm_branch_t(name, d, wbg, br):
    T = d.shape[0]
    RW, Dq = wbg.shape[-2:]
    tm = _tiles_long(T)

    def body(d_ref, w_ref, o_ref):
        for j in range(N_CHIP):
            term = lax.dot_general(d_ref[:, j * Dq:(j + 1) * Dq], w_ref[j], _DIMS["nt"], preferred_element_type=F32)
            if j == 0:
                o_ref[...] = term
            else:
                o_ref[...] += term

    return pl.pallas_call(
        body, name=name, grid=(T // tm,),
        in_specs=[pl.BlockSpec((tm, N_CHIP * Dq), lambda i: (i, 0)),
                  pl.BlockSpec((N_CHIP, None, RW, Dq), lambda i: (0, br, 0, 0))],
        out_specs=pl.BlockSpec((tm, RW), lambda i: (i, 0)),
        out_shape=jax.ShapeDtypeStruct((T, RW), F32), compiler_params=_cparams(("parallel",)),
    )(d, wbg)


def mm_branch_grad(name, y, d):
    T, RW = y.shape
    D = d.shape[-1]
    Dq = D // N_CHIP
    tt = _tiles_long(T)
    nt = T // tt

    def body(y_ref, d_ref, o_ref, acc_ref):
        t = pl.program_id(0)

        @pl.when(t == 0)
        def _():
            acc_ref[...] = jnp.zeros_like(acc_ref)

        acc_ref[...] += lax.dot_general(y_ref[...], d_ref[...], _DIMS["tn"], preferred_element_type=F32)

        @pl.when(t == nt - 1)
        def _():
            for j in range(N_CHIP):
                o_ref[j] = acc_ref[:, j * Dq:(j + 1) * Dq].astype(o_ref.dtype)

    return pl.pallas_call(
        body, name=name, grid=(nt,),
        in_specs=[pl.BlockSpec((tt, RW), lambda t: (t, 0)), pl.BlockSpec((tt, D), lambda t: (t, 0))],
        out_specs=pl.BlockSpec((N_CHIP, RW, Dq), lambda t: (0, 0, 0)),
        out_shape=jax.ShapeDtypeStruct((N_CHIP, RW, Dq), BF16),
        scratch_shapes=[pltpu.VMEM((RW, D), F32)], compiler_params=_cparams(("arbitrary",)),
    )(y, d)


def allgather8(name, x_shard):
    m_per, n = x_shard.shape

    def body(x_ref, out_ref, send_sems, recv_sems, local_sem):
        x, y, c = _my_pos()
        me, sibling = (x, y, c), (x, y, 1 - c)
        chips = [(1 - x, y), (x, 1 - y), (1 - x, 1 - y)]

        def rows(px, py, pc):
            return out_ref.at[pl.ds((4 * px + 2 * py + pc) * m_per, m_per), :]

        def copy(k, block, to, src=None):
            return pltpu.make_async_remote_copy(
                src_ref=rows(*block) if src is None else src, dst_ref=rows(*block),
                send_sem=send_sems.at[k], recv_sem=recv_sems.at[k], device_id=to, device_id_type=MESH)

        mine = pltpu.make_async_copy(x_ref, rows(*me), local_sem)
        mine.start()
        first = [copy(0, me, sibling, src=x_ref)]
        first += [copy(1 + j, me, (*chip, c), src=x_ref) for j, chip in enumerate(chips)]
        for cp in first:
            cp.start()
        passed = [copy(4 + j, (*chip, c), sibling) for j, chip in enumerate(chips)]
        for j, chip in enumerate(chips):
            copy(1 + j, (*chip, c), me).wait_recv()
            passed[j].start()
        copy(0, sibling, me).wait_recv()
        for j, chip in enumerate(chips):
            copy(4 + j, (*chip, 1 - c), me).wait_recv()
        for cp in first + passed:
            cp.wait_send()
        mine.wait()

    return pl.pallas_call(
        body, name=name,
        out_shape=jax.ShapeDtypeStruct((N_DEV * m_per, n), x_shard.dtype),
        in_specs=[pl.BlockSpec(memory_space=pltpu.VMEM)],
        out_specs=pl.BlockSpec(memory_space=pltpu.VMEM),
        scratch_shapes=[pltpu.SemaphoreType.DMA((7,)), pltpu.SemaphoreType.DMA((7,)), pltpu.SemaphoreType.DMA],
        compiler_params=pltpu.CompilerParams(vmem_limit_bytes=VMEM_LIMIT_BYTES),
    )(x_shard)


def _other_chips(x, y):
    return [(1 - x, y), (x, 1 - y), (1 - x, 1 - y)]


_HBM = pl.BlockSpec(memory_space=pltpu.HBM)
_SEM = pl.BlockSpec(memory_space=pltpu.SEMAPHORE)
_ANY = pl.BlockSpec(memory_space=pl.ANY)
_EFFECT = pltpu.SideEffectType.DATAFLOW_SIDE_EFFECTING
TOKEN_SHAPE = (8, 128)


def _in_hbm(a):
    return pltpu.with_memory_space_constraint(a, pltpu.HBM)


def exchange_start(name, bufs, plan):
    n = len(bufs)
    n_copies = len(plan([None] * n, 0, 0, 0, dry=True))

    def body(*refs):
        ins = refs[:n]
        send_sems, recv_sems = refs[n], refs[n + 1]
        token = refs[-1]
        x, y, c = _my_pos()
        for i, (src, dst, to) in enumerate(plan(ins, x, y, c)):
            pltpu.make_async_remote_copy(src_ref=src, dst_ref=dst, send_sem=send_sems.at[i], recv_sem=recv_sems.at[i],
                                         device_id=to, device_id_type=MESH).start()
        token[...] = jnp.zeros_like(token)

    outs = pl.pallas_call(
        body, name=name,
        out_shape=(pltpu.SemaphoreType.DMA((n_copies,)), pltpu.SemaphoreType.DMA((n_copies,)),
                   *[pltpu.HBM(b.shape, b.dtype) for b in bufs], jax.ShapeDtypeStruct(TOKEN_SHAPE, F32)),
        in_specs=[_HBM] * n,
        out_specs=(_SEM, _SEM, *[_HBM] * n, pl.BlockSpec(memory_space=pltpu.VMEM)),
        input_output_aliases={i: 2 + i for i in range(n)},
        compiler_params=pltpu.CompilerParams(has_side_effects=_EFFECT),
    )(*[_in_hbm(b) for b in bufs])
    return outs[0], outs[1], list(outs[2:2 + n]), outs[-1]


def exchange_wait(name, send_sems, recv_sems, bufs, plan, after):
    n = len(bufs)

    def body(*refs):
        ins = refs[:n]
        send_sems, recv_sems = refs[n], refs[n + 1]
        x, y, c = _my_pos()
        for i, (src, dst, to) in enumerate(plan(ins, x, y, c, arriving=True)):
            cp = pltpu.make_async_remote_copy(src_ref=src, dst_ref=dst, send_sem=send_sems.at[i],
                                              recv_sem=recv_sems.at[i], device_id=to, device_id_type=MESH)
            cp.wait_send()
            cp.wait_recv()

    outs = pl.pallas_call(
        body, name=name,
        out_shape=tuple(pltpu.HBM(b.shape, b.dtype) for b in bufs),
        in_specs=[_HBM] * n + [_SEM, _SEM, _ANY],
        out_specs=tuple([_HBM] * n),
        input_output_aliases={i: i for i in range(n)},
        compiler_params=pltpu.CompilerParams(has_side_effects=_EFFECT),
    )(*bufs, send_sems, recv_sems, after)
    return list(outs)


def _gather_plan(refs, x, y, c, dry=False, arriving=False):
    if dry:
        return [None] * 3
    (land,) = refs
    j_me = 2 * x + y
    return [(land.at[j_me], land.at[(2 * px + py) if arriving else j_me], (px, py, c)) for px, py in _other_chips(x, y)]


def _sibling_plan(refs, x, y, c, dry=False, arriving=False):
    if dry:
        return [None]
    src, land = refs
    return [(src, land, (x, y, 1 - c))]


def _scatter_plan(n_pieces):
    def plan(refs, x, y, c, dry=False, arriving=False):
        if dry:
            return [None] * (3 * n_pieces)
        grads, lands = refs[:n_pieces], refs[n_pieces:]
        return [(grads[p].at[2 * px + py], lands[p].at[k], (px, py, c))
                for p in range(n_pieces) for k, (px, py) in enumerate(_other_chips(x, y))]
    return plan


def _view2d(a):
    return a.reshape(-1, a.shape[-1])


def _row_tile(rows, width, itemsize=4, budget=1 << 20):
    t = 8
    for cand in (1024, 512, 256, 128, 64, 32, 16, 8):
        if rows % cand == 0 and cand * width * itemsize <= budget:
            t = cand
            break
    return t if rows % t == 0 else rows


def cast_into_slot(name, w, l, j_idx, deps=()):
    w3 = w.reshape(w.shape[0], -1, w.shape[-1])
    _, R, W = w3.shape
    tr = _row_tile(R, W)

    def body(j_ref, a_ref, *rest):
        o_ref = rest[-1]
        o_ref[...] = a_ref[...].astype(BF16)

    return pl.pallas_call(
        body, name=name,
        grid_spec=pltpu.PrefetchScalarGridSpec(
            num_scalar_prefetch=1, grid=(R // tr,),
            in_specs=[pl.BlockSpec((None, tr, W), lambda i, j: (l, i, 0))] + [pl.BlockSpec(memory_space=pl.ANY)] * len(deps),
            out_specs=pl.BlockSpec((None, tr, W), lambda i, j: (j[0], i, 0))),
        out_shape=jax.ShapeDtypeStruct((N_CHIP, R, W), BF16), compiler_params=_cparams(("parallel",)),
    )(j_idx, w3, *deps)


def sum_parts(name, groups, j_idx):
    n = len(groups)
    _, R, W = groups[0][0].shape
    tr = _row_tile(R, W)

    def body(j_ref, *refs):
        o_ref = refs[-1]
        g = pl.program_id(0)
        for q in range(n):
            @pl.when(g == q)
            def _(q=q):
                own, got = refs[2 * q], refs[2 * q + 1]
                o_ref[...] = ((own[...].astype(F32) + got[0].astype(F32)) + got[1].astype(F32)) + got[2].astype(F32)

    in_specs = []
    for q in range(n):
        in_specs.append(pl.BlockSpec((None, tr, W), lambda g, i, j, q=q: (j[0], jnp.where(g == q, i, 0), 0)))
        in_specs.append(pl.BlockSpec((3, tr, W), lambda g, i, j, q=q: (0, jnp.where(g == q, i, 0), 0)))
    return pl.pallas_call(
        body, name=name,
        grid_spec=pltpu.PrefetchScalarGridSpec(
            num_scalar_prefetch=1, grid=(n, R // tr), in_specs=in_specs,
            out_specs=pl.BlockSpec((None, tr, W), lambda g, i, j: (g, i, 0))),
        out_shape=jax.ShapeDtypeStruct((n, R, W), F32), compiler_params=_cparams(("arbitrary", "arbitrary")),
    )(j_idx, *[a for pair in groups for a in pair])


def adamw(name, w, m, v, g_parts):
    shape = w.shape
    w2, m2, v2 = _view2d(w), _view2d(m), _view2d(v)
    gs = [_view2d(g) for g in g_parts]
    R, W = w2.shape
    tr = _row_tile(R, W, budget=1 << 19)
    ng = len(gs)
    bc1 = 1.0 - ADAM_B1 ** ADAM_STEP
    bc2 = 1.0 - ADAM_B2 ** ADAM_STEP

    def body(*refs):
        w_ref, m_ref, v_ref = refs[:3]
        g_refs = refs[3:3 + ng]
        go_ref, d_ref, mo_ref, vo_ref = refs[3 + ng:]
        g = g_refs[0][...]
        for r in g_refs[1:]:
            g = g + r[...]
        mn = ADAM_B1 * m_ref[...] + (1.0 - ADAM_B1) * g
        vn = ADAM_B2 * v_ref[...] + (1.0 - ADAM_B2) * (g * g)
        m_hat = mn / bc1
        v_hat = vn / bc2
        go_ref[...] = g
        d_ref[...] = -ADAM_LR * (m_hat / (jnp.sqrt(v_hat) + ADAM_EPS) + ADAM_WD * w_ref[...])
        mo_ref[...] = mn
        vo_ref[...] = vn

    spec = pl.BlockSpec((tr, W), lambda i: (i, 0))
    outs = pl.pallas_call(
        body, name=name, grid=(R // tr,),
        in_specs=[spec] * (3 + ng), out_specs=[spec] * 4,
        out_shape=[jax.ShapeDtypeStruct((R, W), F32)] * 4, compiler_params=_cparams(("parallel",)),
    )(w2, m2, v2, *gs)
    return tuple(o.reshape(shape) for o in outs)


class Dims:
    pass


def _sel(dm):
    return (pl.program_id(0) >= dm.nctx).astype(jnp.int32)


def norm_mod(name, h, gn, modtab, s, dm, deps=()):
    T, D = h.shape
    tm = dm.tme

    def body(h_ref, g_ref, m_ref, *rest):
        u_ref = rest[-1]
        sel = _sel(dm)
        x = h_ref[...]
        r = lax.rsqrt(jnp.mean(x * x, axis=-1, keepdims=True) + EPS)
        ng = x * r * g_ref[...]
        u_ref[...] = (ng * (1.0 + m_ref[sel, 3 * s + 1]) + m_ref[sel, 3 * s]).astype(u_ref.dtype)

    return pl.pallas_call(
        body, name=name, grid=(T // tm,),
        in_specs=[pl.BlockSpec((tm, D), lambda i: (i, 0)), pl.BlockSpec((1, D), lambda i: (0, 0)),
                  pl.BlockSpec((2, N_MOD, 1, D), lambda i: (0, 0, 0, 0))] + [_ANY] * len(deps),
        out_specs=pl.BlockSpec((tm, D), lambda i: (i, 0)),
        out_shape=jax.ShapeDtypeStruct((T, D), BF16), compiler_params=_cparams(("parallel",)),
    )(h, gn, modtab, *deps)


def norm_mod_bwd(name, du, h, gn, modtab, s, dh_in, dm):
    T, D = h.shape
    tm = dm.tme

    def body(du_ref, h_ref, g_ref, m_ref, dhi_ref, dh_ref, dmod_ref, dg_ref):
        i = pl.program_id(0)
        sel = _sel(dm)

        @pl.when(i == 0)
        def _():
            dmod_ref[...] = jnp.zeros_like(dmod_ref)
            dg_ref[...] = jnp.zeros_like(dg_ref)

        x = h_ref[...]
        r = lax.rsqrt(jnp.mean(x * x, axis=-1, keepdims=True) + EPS)
        n = x * r
        g = g_ref[...]
        du = du_ref[...]
        dmod_ref[sel, 0] += jnp.sum(du, axis=0, keepdims=True)
        dmod_ref[sel, 1] += jnp.sum(du * (n * g), axis=0, keepdims=True)
        dng = du * (1.0 + m_ref[sel, 3 * s + 1])
        dg_ref[...] += jnp.sum(dng * n, axis=0, keepdims=True)
        dn = dng * g
        dh_ref[...] = dhi_ref[...] + r * (dn - n * jnp.mean(dn * n, axis=-1, keepdims=True))

    row = pl.BlockSpec((tm, D), lambda i: (i, 0))
    return pl.pallas_call(
        body, name=name, grid=(T // tm,),
        in_specs=[row, row, pl.BlockSpec((1, D), lambda i: (0, 0)),
                  pl.BlockSpec((2, N_MOD, 1, D), lambda i: (0, 0, 0, 0)), row],
        out_specs=[row, pl.BlockSpec((2, 2, 1, D), lambda i: (0, 0, 0, 0)), pl.BlockSpec((1, D), lambda i: (0, 0))],
        out_shape=[jax.ShapeDtypeStruct((T, D), F32), jax.ShapeDtypeStruct((2, 2, 1, D), F32),
                   jax.ShapeDtypeStruct((1, D), F32)],
        compiler_params=_cparams(("arbitrary",)),
    )(du, h, gn, modtab, dh_in)


def resid_bwd(name, dh, f, modtab, s, coef, dm, deps=()):
    T, D = dh.shape
    tm = dm.tme

    def body(dh_ref, f_ref, m_ref, *rest):
        df_ref, dg_ref = rest[-2:]
        sel = _sel(dm)

        @pl.when(pl.program_id(0) == 0)
        def _():
            dg_ref[...] = jnp.zeros_like(dg_ref)

        d = coef * dh_ref[...]
        df_ref[...] = (d * m_ref[sel, 3 * s + 2]).astype(df_ref.dtype)
        dg_ref[sel, 0] += jnp.sum(d * f_ref[...], axis=0, keepdims=True)

    row = pl.BlockSpec((tm, D), lambda i: (i, 0))
    return pl.pallas_call(
        body, name=name, grid=(T // tm,),
        in_specs=[row, row, pl.BlockSpec((2, N_MOD, 1, D), lambda i: (0, 0, 0, 0))] + [_ANY] * len(deps),
        out_specs=[row, pl.BlockSpec((2, 1, 1, D), lambda i: (0, 0, 0, 0))],
        out_shape=[jax.ShapeDtypeStruct((T, D), BF16), jax.ShapeDtypeStruct((2, 1, 1, D), F32)],
        compiler_params=_cparams(("arbitrary",)),
    )(dh, f, modtab, *deps)


def swiglu(name, gu, dm):
    _, T, Nq = gu.shape
    tm = dm.tme
    gu4 = gu.reshape(2, 2, T, Nq)

    def body(gu_ref, o_ref):
        g = gu_ref[0].astype(F32)
        o_ref[...] = (_silu(g) * gu_ref[1].astype(F32)).astype(o_ref.dtype)

    return pl.pallas_call(
        body, name=name, grid=(2, T // tm),
        in_specs=[pl.BlockSpec((2, None, tm, Nq), lambda k, i: (0, k, i, 0))],
        out_specs=pl.BlockSpec((None, tm, Nq), lambda k, i: (k, i, 0)),
        out_shape=jax.ShapeDtypeStruct((2, T, Nq), BF16), compiler_params=_cparams(("parallel", "parallel")),
    )(gu4)


def swiglu_bwd(name, dact, gu, dm):
    _, T, Nq = gu.shape
    tm = dm.tme
    gu4 = gu.reshape(2, 2, T, Nq)

    def body(da_ref, gu_ref, o_ref):
        g = gu_ref[0].astype(F32)
        da = da_ref[...].astype(F32)
        o_ref[0] = (da * gu_ref[1].astype(F32) * _dsilu(g)).astype(o_ref.dtype)
        o_ref[1] = (da * _silu(g)).astype(o_ref.dtype)

    out = pl.pallas_call(
        body, name=name, grid=(2, T // tm),
        in_specs=[pl.BlockSpec((None, tm, Nq), lambda k, i: (k, i, 0)),
                  pl.BlockSpec((2, None, tm, Nq), lambda k, i: (0, k, i, 0))],
        out_specs=pl.BlockSpec((2, None, tm, Nq), lambda k, i: (0, k, i, 0)),
        out_shape=jax.ShapeDtypeStruct((2, 2, T, Nq), BF16), compiler_params=_cparams(("parallel", "parallel")),
    )(dact, gu4)
    return out.reshape(4, T, Nq)


def _halo_specs(dm, width, col):
    tm = dm.tme
    per = tm // HALO
    last = dm.T // HALO - 1
    return [pl.BlockSpec((tm, width), lambda i: (i, col)),
            pl.BlockSpec((HALO, width), lambda i: (jnp.maximum(i * per - 1, 0), col)),
            pl.BlockSpec((HALO, width), lambda i: (jnp.minimum((i + 1) * per, last), col))]


def _segment_edges(dm, i):
    first = jnp.logical_or(i == 0, i == dm.nctx)
    last = jnp.logical_or(i == dm.nctx - 1, i == dm.nt - 1)
    return first, last


def _extend(main, prev, nxt, first, last):
    return jnp.concatenate([jnp.where(first, 0.0, prev), main, jnp.where(last, 0.0, nxt)], axis=0)


def _shift(ext, o, tm):
    n = ext.shape[0]
    rolled = ext if o == 0 else pltpu.roll(ext, (-o) % n, 0)
    return rolled[HALO:HALO + tm]


def _load_ext(refs, first, last):
    main, prev, nxt = refs
    return _extend(main[...].astype(F32), prev[...].astype(F32), nxt[...].astype(F32), first, last)


def rnn_conv(name, z, w, b, dm):
    T, RW, tm = dm.T, dm.RW, dm.tme

    def body(main, prev, nxt, w_ref, b_ref, o_ref):
        first, last = _segment_edges(dm, pl.program_id(0))
        ext = _load_ext((main, prev, nxt), first, last)
        acc = jnp.zeros((tm, RW), F32) + b_ref[...]
        for k in range(4):
            acc = acc + w_ref[k] * _shift(ext, k - 2, tm)
        o_ref[...] = acc

    return pl.pallas_call(
        body, name=name, grid=(dm.nt,),
        in_specs=_halo_specs(dm, RW, 0) + [pl.BlockSpec((4, 1, RW), lambda i: (0, 0, 0)),
                                           pl.BlockSpec((1, RW), lambda i: (0, 0))],
        out_specs=pl.BlockSpec((tm, RW), lambda i: (i, 0)),
        out_shape=jax.ShapeDtypeStruct((T, RW), F32), compiler_params=_cparams(("parallel",)),
    )(z, z, z, w, b)


def _blockdiag(x, w_ref):
    nb = w_ref.shape[0]
    outs = [jnp.dot(x[:, n * RNN_BLOCK:(n + 1) * RNN_BLOCK], w_ref[n].astype(BF16), preferred_element_type=F32)
            for n in range(nb)]
    return jnp.concatenate(outs, axis=-1)


def _lru_gates(xa, wa_ref, ba_ref, wx_ref, bx_ref, lam_ref):
    xb = xa.astype(BF16)
    r = jax.nn.sigmoid(_blockdiag(xb, wa_ref) + ba_ref[...])
    ig = jax.nn.sigmoid(_blockdiag(xb, wx_ref) + bx_ref[...])
    nl = -lam_ref[...]
    sp = jnp.maximum(nl, 0.0) + jnp.log(1.0 + jnp.exp(-jnp.abs(nl)))
    log_a = -LRU_C * r * sp
    a = jnp.exp(log_a)
    m = jnp.sqrt(-_expm1(2.0 * log_a))
    return r, ig, sp, a, m


def _lru_specs(l, d, nb, RW):
    wspec = pl.BlockSpec((None, None, nb, RNN_BLOCK, RNN_BLOCK), lambda i: (l, d, 0, 0, 0))
    vspec = pl.BlockSpec((None, None, 1, RW), lambda i: (l, d, 0, 0))
    return [wspec, vspec, wspec, vspec, vspec]


def lru_gates(name, xa, lw, l, d, dm):
    T, RW, tm = dm.T, dm.RW, dm.tme

    def body(xa_ref, wa_ref, ba_ref, wx_ref, bx_ref, lam_ref, a_ref, u_ref):
        xa_v = xa_ref[...]
        r, ig, sp, a, m = _lru_gates(xa_v, wa_ref, ba_ref, wx_ref, bx_ref, lam_ref)
        a_ref[...] = a
        u_ref[...] = m * (ig * xa_v)

    row = pl.BlockSpec((tm, RW), lambda i: (i, 0))
    return pl.pallas_call(
        body, name=name, grid=(dm.nt,),
        in_specs=[row] + _lru_specs(l, d, dm.NB, RW), out_specs=[row, row],
        out_shape=[jax.ShapeDtypeStruct((T, RW), F32)] * 2, compiler_params=_cparams(("parallel",)),
    )(xa, lw["w_a"], lw["b_a"], lw["w_x"], lw["b_x"], lw["lam"])


def lru_gates_bwd(name, xa, lw, l, d, du, dloga, dm):
    T, RW, tm, NB = dm.T, dm.RW, dm.tme, dm.NB

    def body(xa_ref, wa_ref, ba_ref, wx_ref, bx_ref, lam_ref, du_ref, dla_ref,
             dxa_ref, dwa_ref, dba_ref, dwx_ref, dbx_ref, dlam_ref):
        @pl.when(pl.program_id(0) == 0)
        def _():
            for ref in (dwa_ref, dba_ref, dwx_ref, dbx_ref, dlam_ref):
                ref[...] = jnp.zeros_like(ref)

        xa_v = xa_ref[...]
        r, ig, sp, a, m = _lru_gates(xa_v, wa_ref, ba_ref, wx_ref, bx_ref, lam_ref)
        duu = du_ref[...]
        dm_ = duu * (ig * xa_v)
        dig = duu * m * xa_v
        dxa = duu * m * ig
        dla = dla_ref[...] - dm_ * (a * a) / m
        dr = dla * (-LRU_C * sp)
        dsp = jnp.sum(dla * (-LRU_C * r), axis=0, keepdims=True)
        dlam_ref[...] += dsp * (-jax.nn.sigmoid(-lam_ref[...]))
        dpa = dr * r * (1.0 - r)
        dpx = dig * ig * (1.0 - ig)
        dba_ref[...] += jnp.sum(dpa, axis=0, keepdims=True)
        dbx_ref[...] += jnp.sum(dpx, axis=0, keepdims=True)
        xb, dpab, dpxb = xa_v.astype(BF16), dpa.astype(BF16), dpx.astype(BF16)
        back = []
        for n in range(NB):
            sl = slice(n * RNN_BLOCK, (n + 1) * RNN_BLOCK)
            dwa_ref[n] += lax.dot_general(xb[:, sl], dpab[:, sl], _DIMS["tn"], preferred_element_type=F32)
            dwx_ref[n] += lax.dot_general(xb[:, sl], dpxb[:, sl], _DIMS["tn"], preferred_element_type=F32)
            back.append(lax.dot_general(dpab[:, sl], wa_ref[n].astype(BF16), _DIMS["nt"], preferred_element_type=F32)
                        + lax.dot_general(dpxb[:, sl], wx_ref[n].astype(BF16), _DIMS["nt"], preferred_element_type=F32))
        dxa_ref[...] = dxa + jnp.concatenate(back, axis=-1)

    row = pl.BlockSpec((tm, RW), lambda i: (i, 0))
    wacc = pl.BlockSpec((NB, RNN_BLOCK, RNN_BLOCK), lambda i: (0, 0, 0))
    vacc = pl.BlockSpec((1, RW), lambda i: (0, 0))
    wshape = jax.ShapeDtypeStruct((NB, RNN_BLOCK, RNN_BLOCK), F32)
    vshape = jax.ShapeDtypeStruct((1, RW), F32)
    return pl.pallas_call(
        body, name=name, grid=(dm.nt,),
        in_specs=[row] + _lru_specs(l, d, NB, RW) + [row, row],
        out_specs=[row, wacc, vacc, wacc, vacc, vacc],
        out_shape=[jax.ShapeDtypeStruct((T, RW), F32), wshape, vshape, wshape, vshape, vshape],
        compiler_params=_cparams(("arbitrary",)),
    )(xa, lw["w_a"], lw["b_a"], lw["w_x"], lw["b_x"], lw["lam"], du, dloga)


def _chunk_order(dm, ctx_first, descending):
    nch, nctx = dm.nt, dm.nctx
    nlat = nch - nctx

    def order(s):
        if ctx_first and not descending:
            return s
        if not ctx_first and descending:
            return nch - 1 - s
        if ctx_first:
            return jnp.where(s < nctx, nctx - 1 - s, nch - 1 - (s - nctx))
        return jnp.where(s < nlat, nctx + s, s - nlat)

    return order


def _tile_scan(a, b, carry, descending):
    row = lax.broadcasted_iota(jnp.int32, a.shape, 0)
    for s in (1, 2, 4):
        sh = (HALO - s) if descending else s
        keep = (row < HALO - s) if descending else (row >= s)
        ap = pltpu.roll(a, sh, 0)
        bp = pltpu.roll(b, sh, 0)
        b = jnp.where(keep, b + a * bp, b)
        a = jnp.where(keep, a * ap, a)
    h = b + a * carry
    edge = 0 if descending else HALO - 1
    new_carry = jnp.sum(jnp.where(row == edge, h, 0.0), axis=0, keepdims=True)
    return h, new_carry


def lru_scan(name, a, u, ctx_first, descending, dm):
    T, RW, ch = dm.T, dm.RW, dm.tme
    order = _chunk_order(dm, ctx_first, descending)
    ngrp = ch // HALO

    def body(a_ref, u_ref, h_ref, carry_ref):
        @pl.when(pl.program_id(0) == 0)
        def _():
            carry_ref[...] = jnp.zeros_like(carry_ref)

        def step(g, carry):
            g = (ngrp - 1 - g) if descending else g
            rows = pl.ds(pl.multiple_of(g * HALO, HALO), HALO)
            h, carry = _tile_scan(a_ref[rows, :], u_ref[rows, :], carry, descending)
            h_ref[rows, :] = h
            return carry

        carry_ref[...] = lax.fori_loop(0, ngrp, step, carry_ref[...])

    row = pl.BlockSpec((ch, RW), lambda s: (order(s), 0))
    return pl.pallas_call(
        body, name=name, grid=(dm.nt,),
        in_specs=[row, row], out_specs=row,
        out_shape=jax.ShapeDtypeStruct((T, RW), F32),
        scratch_shapes=[pltpu.VMEM((1, RW), F32)], compiler_params=_cparams(("arbitrary",)),
    )(a, u)


def lru_scan_bwd(name, a, u, h, dh, ctx_first, descending, dm):
    T, RW, ch = dm.T, dm.RW, dm.tme
    order = _chunk_order(dm, ctx_first, descending)
    ngrp = ch // HALO

    def body(a_ref, u_ref, h_ref, dh_ref, lam_ref, dla_ref, carry_ref):
        @pl.when(pl.program_id(0) == 0)
        def _():
            carry_ref[...] = jnp.zeros_like(carry_ref)

        def step(g, carry):
            g = (ngrp - 1 - g) if descending else g
            rows = pl.ds(pl.multiple_of(g * HALO, HALO), HALO)
            a_v, dh_v = a_ref[rows, :], dh_ref[rows, :]
            mu, new_carry = _tile_scan(a_v, a_v * dh_v, carry, descending)
            row = lax.broadcasted_iota(jnp.int32, mu.shape, 0)
            if descending:
                nxt = jnp.where(row == HALO - 1, carry, pltpu.roll(mu, HALO - 1, 0))
            else:
                nxt = jnp.where(row == 0, carry, pltpu.roll(mu, 1, 0))
            lam = dh_v + nxt
            lam_ref[rows, :] = lam
            dla_ref[rows, :] = lam * (h_ref[rows, :] - u_ref[rows, :])
            return new_carry

        carry_ref[...] = lax.fori_loop(0, ngrp, step, carry_ref[...])

    row = pl.BlockSpec((ch, RW), lambda s: (order(s), 0))
    return pl.pallas_call(
        body, name=name, grid=(dm.nt,),
        in_specs=[row] * 4, out_specs=[row, row],
        out_shape=[jax.ShapeDtypeStruct((T, RW), F32)] * 2,
        scratch_shapes=[pltpu.VMEM((1, RW), F32)], compiler_params=_cparams(("arbitrary",)),
    )(a, u, h, dh)


def rnn_out(name, hf, hb, z, dm):
    T, RW, tm = dm.T, dm.RW, dm.tme

    def body(hf_ref, hb_ref, rg_ref, o_ref):
        o_ref[...] = ((hf_ref[...] + hb_ref[...]) * _gelu(rg_ref[...])).astype(o_ref.dtype)

    row = pl.BlockSpec((tm, RW), lambda i: (i, 0))
    return pl.pallas_call(
        body, name=name, grid=(dm.nt,),
        in_specs=[row, row, pl.BlockSpec((tm, RW), lambda i: (i, 1))], out_specs=row,
        out_shape=jax.ShapeDtypeStruct((T, RW), BF16), compiler_params=_cparams(("parallel",)),
    )(hf, hb, z)


def rnn_out_bwd(name, dya, hf, hb, z, dm):
    T, RW, tm = dm.T, dm.RW, dm.tme

    def body(d_ref, hf_ref, hb_ref, rg_ref, dh_ref, drg_ref):
        d, rg = d_ref[...], rg_ref[...]
        dh_ref[...] = d * _gelu(rg)
        drg_ref[...] = d * (hf_ref[...] + hb_ref[...]) * _dgelu(rg)

    row = pl.BlockSpec((tm, RW), lambda i: (i, 0))
    return pl.pallas_call(
        body, name=name, grid=(dm.nt,),
        in_specs=[row, row, row, pl.BlockSpec((tm, RW), lambda i: (i, 1))], out_specs=[row, row],
        out_shape=[jax.ShapeDtypeStruct((T, RW), F32)] * 2, compiler_params=_cparams(("parallel",)),
    )(dya, hf, hb, z)


def rnn_conv_bwd(name, dxa_f, dxa_b, drg, z, w, dz, dm):
    T, RW, tm = dm.T, dm.RW, dm.tme

    def body(f0, f1, f2, b0, b1, b2, x0, x1, x2, drg_ref, w_ref, dz_in, dz_ref, dw_ref, db_ref):
        i = pl.program_id(0)

        @pl.when(i == 0)
        def _():
            dw_ref[...] = jnp.zeros_like(dw_ref)
            db_ref[...] = jnp.zeros_like(db_ref)

        first, last = _segment_edges(dm, i)
        dext = _load_ext((f0, f1, f2), first, last) + _load_ext((b0, b1, b2), first, last)
        xext = _load_ext((x0, x1, x2), first, last)
        dmain = dext[HALO:HALO + tm]
        drx = jnp.zeros((tm, RW), F32)
        for k in range(4):
            drx = drx + w_ref[k] * _shift(dext, -(k - 2), tm)
            dw_ref[k] += jnp.sum(dmain * _shift(xext, k - 2, tm), axis=0, keepdims=True)
        db_ref[...] += jnp.sum(dmain, axis=0, keepdims=True)
        dz_ref[:, :RW] = drx.astype(dz_ref.dtype)
        dz_ref[:, RW:] = drg_ref[...].astype(dz_ref.dtype)

    return pl.pallas_call(
        body, name=name, grid=(dm.nt,),
        in_specs=_halo_specs(dm, RW, 0) * 3 + [pl.BlockSpec((tm, RW), lambda i: (i, 0)),
                                               pl.BlockSpec((4, 1, RW), lambda i: (0, 0, 0)),
                                               pl.BlockSpec(memory_space=pl.ANY)],
        out_specs=[pl.BlockSpec((tm, 2 * RW), lambda i: (i, 0)), pl.BlockSpec((4, 1, RW), lambda i: (0, 0, 0)),
                   pl.BlockSpec((1, RW), lambda i: (0, 0))],
        out_shape=[jax.ShapeDtypeStruct(dz.shape, dz.dtype), jax.ShapeDtypeStruct((4, 1, RW), F32),
                   jax.ShapeDtypeStruct((1, RW), F32)],
        input_output_aliases={11: 0}, compiler_params=_cparams(("arbitrary",)),
    )(dxa_f, dxa_f, dxa_f, dxa_b, dxa_b, dxa_b, z, z, z, drg, w, dz)


def short_conv(name, z, w, dm):
    T, RW, tm = dm.T, dm.RW, dm.tme

    def body(sb_ref, g0, g1, g2, x0, x1, x2, w_ref, o_ref):
        first, last = _segment_edges(dm, pl.program_id(0))
        pext = _load_ext((g0, g1, g2), first, last) * _load_ext((x0, x1, x2), first, last)
        cp = jnp.zeros((tm, RW), F32)
        for k in range(3):
            cp = cp + w_ref[k] * _shift(pext, k - 1, tm)
        o_ref[...] = (sb_ref[...] * cp).astype(o_ref.dtype)

    return pl.pallas_call(
        body, name=name, grid=(dm.nt,),
        in_specs=[pl.BlockSpec((tm, RW), lambda i: (i, 2))] + _halo_specs(dm, RW, 3) + _halo_specs(dm, RW, 4)
        + [pl.BlockSpec((3, 1, RW), lambda i: (0, 0, 0))],
        out_specs=pl.BlockSpec((tm, RW), lambda i: (i, 0)),
        out_shape=jax.ShapeDtypeStruct((T, RW), BF16), compiler_params=_cparams(("parallel",)),
    )(z, z, z, z, z, z, z, w)


def short_conv_bwd(name, dyb, z, w, dz, dm):
    T, RW, tm = dm.T, dm.RW, dm.tme

    def spec3(col):
        per = tm // HALO
        last = T // HALO - 1
        return [pl.BlockSpec((tm, RW), lambda i, p: (i, col)),
                pl.BlockSpec((HALO, RW), lambda i, p: (jnp.maximum(i * per - 1, 0), col)),
                pl.BlockSpec((HALO, RW), lambda i, p: (jnp.minimum((i + 1) * per, last), col))]

    def body(d0, d1, d2, s0, s1, s2, g0, g1, g2, x0, x1, x2, w_ref, dz_in, dz_ref, dw_ref, parts_ref):
        i, p = pl.program_id(0), pl.program_id(1)

        @pl.when(jnp.logical_and(i == 0, p == 0))
        def _():
            dw_ref[...] = jnp.zeros_like(dw_ref)

        @pl.when(p == 0)
        def _():
            first, last = _segment_edges(dm, i)
            gext = _load_ext((g0, g1, g2), first, last)
            xext = _load_ext((x0, x1, x2), first, last)
            pext = gext * xext
            dyext = _load_ext((d0, d1, d2), first, last)
            dcext = dyext * _load_ext((s0, s1, s2), first, last)
            dcmain = dcext[HALO:HALO + tm]
            cp = jnp.zeros((tm, RW), F32)
            dp = jnp.zeros((tm, RW), F32)
            for k in range(3):
                pk = _shift(pext, k - 1, tm)
                cp = cp + w_ref[k] * pk
                dp = dp + w_ref[k] * _shift(dcext, -(k - 1), tm)
                dw_ref[k] += jnp.sum(dcmain * pk, axis=0, keepdims=True)
            parts_ref[0] = (dyext[HALO:HALO + tm] * cp).astype(parts_ref.dtype)
            parts_ref[1] = (dp * xext[HALO:HALO + tm]).astype(parts_ref.dtype)
            parts_ref[2] = (dp * gext[HALO:HALO + tm]).astype(parts_ref.dtype)

        dz_ref[...] = parts_ref[p]

    return pl.pallas_call(
        body, name=name, grid=(dm.nt, 3),
        in_specs=spec3(0) + spec3(2) + spec3(3) + spec3(4)
        + [pl.BlockSpec((3, 1, RW), lambda i, p: (0, 0, 0)), pl.BlockSpec(memory_space=pl.ANY)],
        out_specs=[pl.BlockSpec((tm, RW), lambda i, p: (i, 2 + p)), pl.BlockSpec((3, 1, RW), lambda i, p: (0, 0, 0))],
        out_shape=[jax.ShapeDtypeStruct(dz.shape, dz.dtype), jax.ShapeDtypeStruct((3, 1, RW), F32)],
        scratch_shapes=[pltpu.VMEM((3, tm, RW), dz.dtype)],
        input_output_aliases={13: 0}, compiler_params=_cparams(("arbitrary", "arbitrary")),
    )(dyb, dyb, dyb, z, z, z, z, z, z, z, z, z, w, dz)


def _rope_tables(dm):
    L, C = dm.L, dm.C
    half = HEAD_DIM // 2
    pos = jnp.arange(L)
    row = (pos // GRID_W).astype(F32)
    col = (pos % GRID_W).astype(F32)
    inv = ROPE_BASE ** (-jnp.arange(0, half, 2, dtype=F32) / half)
    ar, ac = row[:, None] * inv, col[:, None] * inv
    cos = jnp.concatenate([jnp.cos(ar), jnp.cos(ar), jnp.cos(ac), jnp.cos(ac)], axis=-1)
    sin = jnp.concatenate([-jnp.sin(ar), jnp.sin(ar), -jnp.sin(ac), jnp.sin(ac)], axis=-1)
    cos = jnp.concatenate([jnp.ones((C, HEAD_DIM), F32), cos], axis=0)
    sin = jnp.concatenate([jnp.zeros((C, HEAD_DIM), F32), sin], axis=0)
    return cos, sin


def _swap_pairs(x):
    quarter = HEAD_DIM // 4
    lane = lax.broadcasted_iota(jnp.int32, x.shape, 1)
    return jnp.where(lane % (2 * quarter) < quarter, pltpu.roll(x, HEAD_DIM - quarter, 1), pltpu.roll(x, quarter, 1))


def _rope(x, cos, sin):
    return x * cos + _swap_pairs(x) * sin


def _unrope(d, cos, sin):
    return d * cos + _swap_pairs(d * sin)


def qkv_prep(name, z, cos, sin, dm):
    T, tm, HQ, KW = dm.T, dm.tme, dm.HQ, dm.KW
    qcol, kcol = dm.off_q // HQ, dm.off_k // KW

    def body(q_ref, k_ref, v_ref, c_ref, s_ref, qo, ko, vo):
        cos_v, sin_v = c_ref[...], s_ref[...]
        for hd in range(HQ // HEAD_DIM):
            sl = slice(hd * HEAD_DIM, (hd + 1) * HEAD_DIM)
            qo[:, sl] = _rope(q_ref[:, sl], cos_v, sin_v).astype(qo.dtype)
        for hd in range(KW // HEAD_DIM):
            sl = slice(hd * HEAD_DIM, (hd + 1) * HEAD_DIM)
            ko[:, sl] = _rope(k_ref[:, sl], cos_v, sin_v).astype(ko.dtype)
        vo[...] = v_ref[...].astype(vo.dtype)

    tab = pl.BlockSpec((tm, HEAD_DIM), lambda i: (i, 0))
    return pl.pallas_call(
        body, name=name, grid=(dm.nt,),
        in_specs=[pl.BlockSpec((tm, HQ), lambda i: (i, qcol)), pl.BlockSpec((tm, KW), lambda i: (i, kcol)),
                  pl.BlockSpec((tm, KW), lambda i: (i, kcol + 1)), tab, tab],
        out_specs=[pl.BlockSpec((tm, HQ), lambda i: (i, 0)), pl.BlockSpec((tm, KW), lambda i: (i, 0)),
                   pl.BlockSpec((tm, KW), lambda i: (i, 0))],
        out_shape=[jax.ShapeDtypeStruct((T, HQ), BF16), jax.ShapeDtypeStruct((T, KW), BF16),
                   jax.ShapeDtypeStruct((T, KW), BF16)],
        compiler_params=_cparams(("parallel",)),
    )(z, z, z, cos, sin)


def qkv_bwd(name, dq, dk, dv, cos, sin, dz, dm):
    T, tm, HQ, KW = dm.T, dm.tme, dm.HQ, dm.KW
    nq = HQ // KW
    base = dm.off_q // KW

    def body(dq_ref, dk_ref, dv_ref, c_ref, s_ref, dz_in, dz_ref):
        p = pl.program_id(1)
        src = jnp.where(p < nq, dq_ref[...], jnp.where(p == nq, dk_ref[...], dv_ref[...]))
        cos_v, sin_v = c_ref[...], s_ref[...]
        is_v = p == nq + 1
        for hd in range(KW // HEAD_DIM):
            sl = slice(hd * HEAD_DIM, (hd + 1) * HEAD_DIM)
            dz_ref[:, sl] = jnp.where(is_v, src[:, sl], _unrope(src[:, sl], cos_v, sin_v)).astype(dz_ref.dtype)

    tab = pl.BlockSpec((tm, HEAD_DIM), lambda i, p: (i, 0))
    blk = pl.BlockSpec((tm, KW), lambda i, p: (i, 0))
    return pl.pallas_call(
        body, name=name, grid=(dm.nt, nq + 2),
        in_specs=[pl.BlockSpec((tm, KW), lambda i, p: (i, jnp.minimum(p, nq - 1))), blk, blk, tab, tab,
                  pl.BlockSpec(memory_space=pl.ANY)],
        out_specs=pl.BlockSpec((tm, KW), lambda i, p: (i, base + p)),
        out_shape=jax.ShapeDtypeStruct(dz.shape, dz.dtype),
        input_output_aliases={5: 0}, compiler_params=_cparams(("parallel", "arbitrary")),
    )(dq, dk, dv, cos, sin, dz)


def _attn_specs(dm):
    nC, nB, C, KW = dm.C // Q_BLOCK, dm.T // Q_BLOCK, dm.C, dm.KW

    def near(o):
        return lambda b: (jnp.clip(b + o, nC, nB - 1), 0)

    kv = [pl.BlockSpec((Q_BLOCK, KW), near(o)) for o in (-1, 0, 1)] + [pl.BlockSpec((C, KW), lambda b: (0, 0))]
    return kv


def _attn_mask(dm, b):
    nC, C, L = dm.C // Q_BLOCK, dm.C, dm.L
    span = 3 * Q_BLOCK
    n = b - nC
    iq = lax.broadcasted_iota(jnp.int32, (Q_BLOCK, span + C), 0)
    ik = lax.broadcasted_iota(jnp.int32, (Q_BLOCK, span + C), 1)
    kpos = n * Q_BLOCK + ik - Q_BLOCK
    qpos = n * Q_BLOCK + iq
    local = (b >= nC) & (jnp.abs(qpos - kpos) <= WINDOW) & (kpos >= 0) & (kpos < L)
    return jnp.logical_or(ik >= span, local)


def attention(name, q, k, v, sink, dm):
    T, HQ, KW = dm.T, dm.HQ, dm.KW
    H, KV = HQ // HEAD_DIM, KW // HEAD_DIM
    G = H // KV
    scale = HEAD_DIM ** -0.5

    def body(q_ref, kp, kc, kn, kx, vp, vc, vn, vx, sink_ref, o_ref, lse_ref):
        valid = _attn_mask(dm, pl.program_id(0))
        lane = lax.broadcasted_iota(jnp.int32, (Q_BLOCK, LSE_W), 1)
        lse_all = jnp.zeros((Q_BLOCK, LSE_W), F32)
        for kh in range(KV):
            ks = slice(kh * HEAD_DIM, (kh + 1) * HEAD_DIM)
            k_all = jnp.concatenate([kp[:, ks], kc[:, ks], kn[:, ks], kx[:, ks]], axis=0)
            v_all = jnp.concatenate([vp[:, ks], vc[:, ks], vn[:, ks], vx[:, ks]], axis=0)
            for g in range(G):
                hd = kh * G + g
                hs = slice(hd * HEAD_DIM, (hd + 1) * HEAD_DIM)
                s = lax.dot_general(q_ref[:, hs], k_all, _DIMS["nt"], preferred_element_type=F32) * scale
                s = jnp.where(valid, s, NEG_INF)
                snk = sink_ref[0, hd]
                mx = jnp.maximum(jnp.max(s, axis=-1, keepdims=True), snk)
                p = jnp.exp(s - mx)
                den = jnp.sum(p, axis=-1, keepdims=True) + jnp.exp(snk - mx)
                o = jnp.dot(p.astype(BF16), v_all, preferred_element_type=F32) / den
                o_ref[:, hs] = o.astype(o_ref.dtype)
                lse_all = jnp.where(lane == hd, mx + jnp.log(den), lse_all)
        lse_ref[...] = lse_all

    kv = _attn_specs(dm)
    return pl.pallas_call(
        body, name=name, grid=(T // Q_BLOCK,),
        in_specs=[pl.BlockSpec((Q_BLOCK, HQ), lambda b: (b, 0))] + kv + kv + [pl.BlockSpec(memory_space=pltpu.SMEM)],
        out_specs=[pl.BlockSpec((Q_BLOCK, HQ), lambda b: (b, 0)), pl.BlockSpec((Q_BLOCK, LSE_W), lambda b: (b, 0))],
        out_shape=[jax.ShapeDtypeStruct((T, HQ), BF16), jax.ShapeDtypeStruct((T, LSE_W), F32)],
        compiler_params=_cparams(("parallel",)),
    )(q, k, k, k, k, v, v, v, v, sink)


def attention_bwd(name, q, k, v, sink, o, lse, do, dm):
    T, HQ, KW, C = dm.T, dm.HQ, dm.KW, dm.C
    H, KV = HQ // HEAD_DIM, KW // HEAD_DIM
    G = H // KV
    nC, nB = C // Q_BLOCK, T // Q_BLOCK
    scale = HEAD_DIM ** -0.5
    span = 3 * Q_BLOCK

    def body(q_ref, kp, kc, kn, kx, vp, vc, vn, vx, sink_ref, o_ref, lse_ref, do_ref,
             dq_ref, dk_ref, dv_ref, ds_ref):
        b = pl.program_id(0)

        @pl.when(b == 0)
        def _():
            dk_ref[...] = jnp.zeros_like(dk_ref)
            dv_ref[...] = jnp.zeros_like(dv_ref)
            ds_ref[...] = jnp.zeros_like(ds_ref)

        valid = _attn_mask(dm, b)
        starts = [pl.multiple_of(jnp.clip(b + off, nC, nB - 1) * Q_BLOCK, Q_BLOCK) for off in (-1, 0, 1)]
        lane = lax.broadcasted_iota(jnp.int32, (Q_BLOCK, LSE_W), 1)
        lse_all = lse_ref[...]
        dsink = jnp.zeros((1, LSE_W), F32)
        for kh in range(KV):
            ks = slice(kh * HEAD_DIM, (kh + 1) * HEAD_DIM)
            k_all = jnp.concatenate([kp[:, ks], kc[:, ks], kn[:, ks], kx[:, ks]], axis=0)
            v_all = jnp.concatenate([vp[:, ks], vc[:, ks], vn[:, ks], vx[:, ks]], axis=0)
            dk_all = jnp.zeros((span + C, HEAD_DIM), F32)
            dv_all = jnp.zeros((span + C, HEAD_DIM), F32)
            for g in range(G):
                hd = kh * G + g
                hs = slice(hd * HEAD_DIM, (hd + 1) * HEAD_DIM)
                qh = q_ref[:, hs]
                doh = do_ref[:, hs]
                s = lax.dot_general(qh, k_all, _DIMS["nt"], preferred_element_type=F32) * scale
                s = jnp.where(valid, s, NEG_INF)
                lse_h = jnp.sum(jnp.where(lane == hd, lse_all, 0.0), axis=-1, keepdims=True)
                p = jnp.exp(s - lse_h)
                delta = jnp.sum(doh * o_ref[:, hs].astype(F32), axis=-1, keepdims=True)
                dob = doh.astype(BF16)
                dp = lax.dot_general(dob, v_all, _DIMS["nt"], preferred_element_type=F32)
                dsc = (p * (dp - delta) * scale).astype(BF16)
                dq_ref[:, hs] = jnp.dot(dsc, k_all, preferred_element_type=F32)
                dk_all = dk_all + lax.dot_general(dsc, qh, _DIMS["tn"], preferred_element_type=F32)
                dv_all = dv_all + lax.dot_general(p.astype(BF16), dob, _DIMS["tn"], preferred_element_type=F32)
                p_sink = jnp.exp(sink_ref[0, hd] - lse_h)
                dsink = dsink + jnp.where(lane[0:1] == hd, -jnp.sum(p_sink * delta), 0.0)
            for j, st in enumerate(starts):
                rows = pl.ds(st, Q_BLOCK)
                dk_ref[rows, ks] += dk_all[j * Q_BLOCK:(j + 1) * Q_BLOCK]
                dv_ref[rows, ks] += dv_all[j * Q_BLOCK:(j + 1) * Q_BLOCK]
            dk_ref[0:C, ks] += dk_all[span:]
            dv_ref[0:C, ks] += dv_all[span:]
        ds_ref[...] += dsink

    kv = _attn_specs(dm)
    qspec = pl.BlockSpec((Q_BLOCK, HQ), lambda b: (b, 0))
    full = pl.BlockSpec((T, KW), lambda b: (0, 0))
    return pl.pallas_call(
        body, name=name, grid=(nB,),
        in_specs=[qspec] + kv + kv + [pl.BlockSpec(memory_space=pltpu.SMEM), qspec,
                                      pl.BlockSpec((Q_BLOCK, LSE_W), lambda b: (b, 0)), qspec],
        out_specs=[qspec, full, full, pl.BlockSpec((1, LSE_W), lambda b: (0, 0))],
        out_shape=[jax.ShapeDtypeStruct((T, HQ), F32), jax.ShapeDtypeStruct((T, KW), F32),
                   jax.ShapeDtypeStruct((T, KW), F32), jax.ShapeDtypeStruct((1, LSE_W), F32)],
        compiler_params=_cparams(("arbitrary",)),
    )(q, k, k, k, k, v, v, v, v, sink, o, lse, do)


def merge(name, z, lifted, b_merge, dm):
    T, D, tm, cw = dm.T, dm.D, _tiles(dm.T), dm.cw
    gcol = dm.off_g // cw
    per = D // cw

    def body(g0, g1, g2, l0, l1, l2, b_ref, o_ref):
        acc = jnp.zeros((tm, cw), F32)
        for i, (g, lf) in enumerate(((g0, l0), (g1, l1), (g2, l2))):
            acc = acc + jax.nn.sigmoid(g[...] + b_ref[i]) * lf[...]
        o_ref[...] = acc.astype(o_ref.dtype)

    gspecs = [pl.BlockSpec((tm, cw), lambda i, j, br=br: (i, gcol + br * per + j)) for br in range(N_BRANCH)]
    blk = pl.BlockSpec((tm, cw), lambda i, j: (i, j))
    return pl.pallas_call(
        body, name=name, grid=(T // tm, per),
        in_specs=gspecs + [blk] * 3 + [pl.BlockSpec((N_BRANCH, 1, cw), lambda i, j: (0, 0, j))], out_specs=blk,
        out_shape=jax.ShapeDtypeStruct((T, D), BF16), compiler_params=_cparams(("parallel", "parallel")),
    )(z, z, z, *lifted, b_merge)


def merge_bwd(name, dmerged, z, lifted, b_merge, dz, dm):
    T, D, tm, cw = dm.T, dm.D, _tiles(dm.T), dm.cw
    gcol = dm.off_g // cw
    per = D // cw

    def body(d_ref, g_ref, l0, l1, l2, b_ref, dz_in, o0, o1, o2, dz_ref, db_ref):
        br = pl.program_id(2)

        @pl.when(jnp.logical_and(pl.program_id(1) == 0, br == 0))
        def _():
            db_ref[...] = jnp.zeros_like(db_ref)

        d = d_ref[...]
        gate = jax.nn.sigmoid(g_ref[...] + b_ref[br])
        dlift = (d * gate).astype(o0.dtype)
        for b, (l_ref, o_ref) in enumerate(((l0, o0), (l1, o1), (l2, o2))):
            @pl.when(br == b)
            def _(l_ref=l_ref, o_ref=o_ref):
                o_ref[...] = dlift
                dg = d * l_ref[...] * gate * (1.0 - gate)
                dz_ref[...] = dg.astype(dz_ref.dtype)
                db_ref[b] += jnp.sum(dg, axis=0, keepdims=True)

    blk = pl.BlockSpec((tm, cw), lambda j, i, b: (i, j))
    zblk = pl.BlockSpec((tm, cw), lambda j, i, b: (i, gcol + b * per + j))
    return pl.pallas_call(
        body, name=name, grid=(per, T // tm, N_BRANCH),
        in_specs=[blk, zblk, blk, blk, blk, pl.BlockSpec((N_BRANCH, 1, cw), lambda j, i, b: (0, 0, j)),
                  pl.BlockSpec(memory_space=pl.ANY)],
        out_specs=[blk, blk, blk, zblk, pl.BlockSpec((N_BRANCH, 1, cw), lambda j, i, b: (0, 0, j))],
        out_shape=[jax.ShapeDtypeStruct((T, D), BF16)] * 3 + [jax.ShapeDtypeStruct(dz.shape, dz.dtype),
                                                             jax.ShapeDtypeStruct((N_BRANCH, 1, D), F32)],
        input_output_aliases={6: 3}, compiler_params=_cparams(("parallel", "arbitrary", "arbitrary")),
    )(dmerged, z, *lifted, b_merge, dz)


def loss_head(name, h, gf, target, dm):
    T, D, tm, nctx = dm.T, dm.D, dm.tme, dm.nctx

    def body(h_ref, g_ref, t_ref, dh_ref, loss_ref, dg_ref):
        i = pl.program_id(0)

        @pl.when(i == 0)
        def _():
            loss_ref[...] = jnp.zeros_like(loss_ref)
            dg_ref[...] = jnp.zeros_like(dg_ref)

        @pl.when(i < nctx)
        def _():
            dh_ref[...] = jnp.zeros_like(dh_ref)

        @pl.when(i >= nctx)
        def _():
            x = h_ref[...]
            r = lax.rsqrt(jnp.mean(x * x, axis=-1, keepdims=True) + EPS)
            n = x * r
            g = g_ref[...]
            err = n * g - t_ref[...]
            loss_ref[...] += jnp.sum(err * err) * (0.5 / D)
            dy = err * (1.0 / D)
            dg_ref[...] += jnp.sum(dy * n, axis=0, keepdims=True)
            dn = dy * g
            dh_ref[...] = r * (dn - n * jnp.mean(dn * n, axis=-1, keepdims=True))

    row = pl.BlockSpec((tm, D), lambda i: (i, 0))
    return pl.pallas_call(
        body, name=name, grid=(T // tm,),
        in_specs=[row, pl.BlockSpec((1, D), lambda i: (0, 0)),
                  pl.BlockSpec((tm, D), lambda i: (jnp.maximum(i - nctx, 0), 0))],
        out_specs=[row, pl.BlockSpec((1, 128), lambda i: (0, 0)), pl.BlockSpec((1, D), lambda i: (0, 0))],
        out_shape=[jax.ShapeDtypeStruct((T, D), F32), jax.ShapeDtypeStruct((1, 128), F32),
                   jax.ShapeDtypeStruct((1, D), F32)],
        compiler_params=_cparams(("arbitrary",)),
    )(h, gf, target)


_HI = lax.Precision.HIGHEST
ADA_ROWS = 16


def ada_forward(name, cond, ada_w, bias):
    _, D, Aq = ada_w.shape
    tc = _pick(Aq, (1536, 1152, 768, 512, 384, 256, 128))
    tk = _ktile(D)
    nk = D // tk

    def body(c_ref, w_ref, b_ref, o_ref):
        k = pl.program_id(2)

        @pl.when(k == 0)
        def _():
            o_ref[...] = jnp.zeros_like(o_ref) + b_ref[...]

        o_ref[...] += jnp.dot(_silu(c_ref[...]), w_ref[...], precision=_HI, preferred_element_type=F32)

    return pl.pallas_call(
        body, name=name, grid=(2, Aq // tc, nk),
        in_specs=[pl.BlockSpec((ADA_ROWS, tk), lambda l, j, k: (0, k)),
                  pl.BlockSpec((None, tk, tc), lambda l, j, k: (l, k, j)),
                  pl.BlockSpec((None, 1, tc), lambda l, j, k: (l, 0, j))],
        out_specs=pl.BlockSpec((None, ADA_ROWS, tc), lambda l, j, k: (l, 0, j)),
        out_shape=jax.ShapeDtypeStruct((2, ADA_ROWS, Aq), F32),
        compiler_params=_cparams(("parallel", "parallel", "arbitrary")),
    )(cond, ada_w, bias)


def ada_cond_grad(name, dmod, ada_w):
    _, D, Aq = ada_w.shape
    tc = _pick(Aq, (1536, 1152, 768, 512, 384, 256, 128))
    tn = _ktile(D)
    nc = Aq // tc

    def body(d_ref, w_ref, o_ref):
        @pl.when(jnp.logical_and(pl.program_id(1) == 0, pl.program_id(2) == 0))
        def _():
            o_ref[...] = jnp.zeros_like(o_ref)

        o_ref[...] += lax.dot_general(d_ref[...], w_ref[...], _DIMS["nt"], precision=_HI, preferred_element_type=F32)

    return pl.pallas_call(
        body, name=name, grid=(D // tn, 2, nc),
        in_specs=[pl.BlockSpec((None, ADA_ROWS, tc), lambda j, l, c: (l, 0, c)),
                  pl.BlockSpec((None, tn, tc), lambda j, l, c: (l, j, c))],
        out_specs=pl.BlockSpec((ADA_ROWS, tn), lambda j, l, c: (0, j)),
        out_shape=jax.ShapeDtypeStruct((ADA_ROWS, D), F32),
        compiler_params=_cparams(("parallel", "arbitrary", "arbitrary")),
    )(dmod, ada_w)


def ada_update(name, cond, dmod, w, m, v):
    _, D, Aq = w.shape
    tc = _pick(Aq, (1536, 1152, 768, 512, 384, 256, 128))
    tr = 128 if D % 128 == 0 else D
    bc1 = 1.0 - ADAM_B1 ** ADAM_STEP
    bc2 = 1.0 - ADAM_B2 ** ADAM_STEP

    def body(c_ref, d_ref, w_ref, m_ref, v_ref, go_ref, dl_ref, mo_ref, vo_ref):
        g = lax.dot_general(_silu(c_ref[...]), d_ref[...], _DIMS["tn"], precision=_HI, preferred_element_type=F32)
        mn = ADAM_B1 * m_ref[...] + (1.0 - ADAM_B1) * g
        vn = ADAM_B2 * v_ref[...] + (1.0 - ADAM_B2) * (g * g)
        go_ref[...] = g
        dl_ref[...] = -ADAM_LR * ((mn / bc1) / (jnp.sqrt(vn / bc2) + ADAM_EPS) + ADAM_WD * w_ref[...])
        mo_ref[...] = mn
        vo_ref[...] = vn

    blk = pl.BlockSpec((None, tr, tc), lambda l, i, j: (l, i, j))
    return pl.pallas_call(
        body, name=name, grid=(2, D // tr, Aq // tc),
        in_specs=[pl.BlockSpec((ADA_ROWS, tr), lambda l, i, j: (0, i)),
                  pl.BlockSpec((None, ADA_ROWS, tc), lambda l, i, j: (l, 0, j)), blk, blk, blk],
        out_specs=[blk] * 4, out_shape=[jax.ShapeDtypeStruct(w.shape, F32)] * 4,
        compiler_params=_cparams(("parallel", "parallel", "parallel")),
    )(cond, dmod, w, m, v)


def dmod_assemble(name, gathered):
    A = gathered.shape[-1]
    tc = _pick(A, (2048, 1024, 512, 256, 128))

    def body(g_ref, o_ref, b_ref):
        ctx = g_ref[0, 1]
        for dev in range(1, N_DEV):
            ctx = ctx + g_ref[dev, 1]
        tot = ctx
        for dev in range(N_DEV):
            o_ref[dev:dev + 1, :] = g_ref[dev, 0]
            tot = tot + g_ref[dev, 0]
        o_ref[N_DEV:N_DEV + 1, :] = ctx
        o_ref[N_DEV + 1:, :] = jnp.zeros((ADA_ROWS - N_DEV - 1, tc), F32)
        b_ref[...] = tot

    return pl.pallas_call(
        body, name=name, grid=(2, A // tc),
        in_specs=[pl.BlockSpec((N_DEV, None, 2, 1, tc), lambda l, j: (0, l, 0, 0, j))],
        out_specs=[pl.BlockSpec((None, ADA_ROWS, tc), lambda l, j: (l, 0, j)),
                   pl.BlockSpec((None, 1, tc), lambda l, j: (l, 0, j))],
        out_shape=[jax.ShapeDtypeStruct((2, ADA_ROWS, A), F32), jax.ShapeDtypeStruct((2, 1, A), F32)],
        compiler_params=_cparams(("parallel", "parallel")),
    )(gathered)


def sum_devices(name, gathered):
    _, R, W = gathered.shape
    tr = _row_tile(R, W, budget=1 << 18)

    def body(g_ref, all_ref, chip_ref):
        even = g_ref[0]
        odd = g_ref[1]
        for dev in range(2, N_DEV, 2):
            even = even + g_ref[dev]
            odd = odd + g_ref[dev + 1]
        all_ref[...] = even + odd
        chip_ref[...] = even

    blk = pl.BlockSpec((tr, W), lambda i: (i, 0))
    return pl.pallas_call(
        body, name=name, grid=(R // tr,),
        in_specs=[pl.BlockSpec((N_DEV, tr, W), lambda i: (0, i, 0))], out_specs=[blk, blk],
        out_shape=[jax.ShapeDtypeStruct((R, W), F32)] * 2, compiler_params=_cparams(("parallel",)),
    )(gathered)


PACK_ROWS = 1024


def _pack(arrays):
    flat = jnp.concatenate([a.reshape(-1).astype(F32) for a in arrays])
    pad = (-flat.shape[0]) % (PACK_ROWS * 128)
    return jnp.pad(flat, (0, pad)).reshape(-1, 128)


def _unpack(buf, shapes, lead=()):
    flat = buf.reshape(lead + (-1,))
    out, start = [], 0
    for s in shapes:
        n = math.prod(s)
        out.append(flat[..., start:start + n].reshape(lead + tuple(s)))
        start += n
    return out


def _unshard_last(g):
    g = jnp.moveaxis(g, 0, -2)
    return g.reshape(g.shape[:-2] + (g.shape[-2] * g.shape[-1],))


class WeightStream:
    AHEAD = 2

    def __init__(self, keys, make_land, first_deps=()):
        self.keys, self.make_land = list(keys), make_land
        self.pending, self.values, self.tokens, self.started = {}, {}, [], 0
        for _ in range(self.AHEAD):
            self._start_next(first_deps)

    def _start_next(self, deps):
        if self.started < len(self.keys):
            key = self.keys[self.started]
            self.started += 1
            land, view = self.make_land(key, deps)
            ss, rs, (land,), token = exchange_start(f"{key}_start", [land], _gather_plan)
            self.pending[key] = (ss, rs, land, view)
            self.tokens.append(token)

    def get(self, key, after=None):
        if key not in self.values:
            ss, rs, land, view = self.pending.pop(key)
            (full,) = exchange_wait(f"{key}_wait", ss, rs, [land], _gather_plan, after)
            self.values[key] = full if view is None else full.reshape(view)
            self._start_next((full,))
        return self.values[key]

    def take_tokens(self):
        out, self.tokens = tuple(self.tokens), []
        return out


def _ffn_forward(tag, h, gn, modtab, s, ws, k13, k2, dm):
    u = norm_mod(f"{tag}_norm", h, gn, modtab, s, dm)
    w13g = ws.get(k13, u)
    gu = mm_cols(f"{tag}_w13", u, w13g, BF16, flat=False, deps=ws.take_tokens())
    act = swiglu(f"{tag}_act", gu, dm)
    w2g = ws.get(k2, act)
    h_out, f = mm_rows(f"{tag}_w2", act, w2g, deps=ws.take_tokens(), resid=(h, modtab, s, 0.5, dm.C))
    return h_out, (h, u, gu, act, f)


def _ffn_backward(tag, dh, saved, gn, modtab, s, ws, k13, k2, dm, scatter, keys, deps=()):
    h, u, gu, act, f = saved
    w13g, w2g = ws.get(k13), ws.get(k2)
    df, dgate = resid_bwd(f"{tag}_res_bwd", dh, f, modtab, s, 0.5, dm, deps)
    dact = mm_rows_t(f"{tag}_dact", df, w2g)
    dw2 = mm_rows_grad(f"{tag}_dw2", act, df).reshape(N_CHIP, -1, df.shape[-1])
    tok2 = scatter(f"{tag}_scatter_w2", {keys[1]: dw2})
    dgu = swiglu_bwd(f"{tag}_act_bwd", dact, gu, dm)
    du = mm_cols_t(f"{tag}_du", dgu, w13g, flat=False, deps=(tok2,))
    dw13 = mm_cols_grad(f"{tag}_dw13", u, dgu, flat=False)
    dh_in, dss, dgn = norm_mod_bwd(f"{tag}_norm_bwd", du, h, gn, modtab, s, dh, dm)
    tok13 = scatter(f"{tag}_scatter_w13", {keys[0]: dw13})
    return dh_in, (tok13,), dss, dgate, dgn


def kernel(x, c, ctx, c_ctx, ada_w, ada_b, norm_g, ffn1_w13, ffn1_w2, w_in, b_merge, rnn_conv_w, rnn_conv_b, lru_w_a, lru_b_a, lru_w_x, lru_b_x, lru_lambda, sc_conv_w, attn_sink, w_branch, w_out, ffn2_w13, ffn2_w2, final_norm_g, loss_target, m_c_ctx, m_ada_w, m_ada_b, m_norm_g, m_ffn1_w13, m_ffn1_w2, m_w_in, m_b_merge, m_rnn_conv_w, m_rnn_conv_b, m_lru_w_a, m_lru_b_a, m_lru_w_x, m_lru_b_x, m_lru_lambda, m_sc_conv_w, m_attn_sink, m_w_branch, m_w_out, m_ffn2_w13, m_ffn2_w2, m_final_norm_g, v_c_ctx, v_ada_w, v_ada_b, v_norm_g, v_ffn1_w13, v_ffn1_w2, v_w_in, v_b_merge, v_rnn_conv_w, v_rnn_conv_b, v_lru_w_a, v_lru_b_a, v_lru_w_x, v_lru_b_x, v_lru_lambda, v_sc_conv_w, v_attn_sink, v_w_branch, v_w_out, v_ffn2_w13, v_ffn2_w2, v_final_norm_g):
    weights = dict(c_ctx=c_ctx, ada_w=ada_w, ada_b=ada_b, norm_g=norm_g, ffn1_w13=ffn1_w13, ffn1_w2=ffn1_w2, w_in=w_in,
                   b_merge=b_merge, rnn_conv_w=rnn_conv_w, rnn_conv_b=rnn_conv_b, lru_w_a=lru_w_a, lru_b_a=lru_b_a,
                   lru_w_x=lru_w_x, lru_b_x=lru_b_x, lru_lambda=lru_lambda, sc_conv_w=sc_conv_w, attn_sink=attn_sink,
                   w_branch=w_branch, w_out=w_out, ffn2_w13=ffn2_w13, ffn2_w2=ffn2_w2, final_norm_g=final_norm_g)
    mom_m = dict(c_ctx=m_c_ctx, ada_w=m_ada_w, ada_b=m_ada_b, norm_g=m_norm_g, ffn1_w13=m_ffn1_w13, ffn1_w2=m_ffn1_w2,
                 w_in=m_w_in, b_merge=m_b_merge, rnn_conv_w=m_rnn_conv_w, rnn_conv_b=m_rnn_conv_b, lru_w_a=m_lru_w_a,
                 lru_b_a=m_lru_b_a, lru_w_x=m_lru_w_x, lru_b_x=m_lru_b_x, lru_lambda=m_lru_lambda,
                 sc_conv_w=m_sc_conv_w, attn_sink=m_attn_sink, w_branch=m_w_branch, w_out=m_w_out,
                 ffn2_w13=m_ffn2_w13, ffn2_w2=m_ffn2_w2, final_norm_g=m_final_norm_g)
    mom_v = dict(c_ctx=v_c_ctx, ada_w=v_ada_w, ada_b=v_ada_b, norm_g=v_norm_g, ffn1_w13=v_ffn1_w13, ffn1_w2=v_ffn1_w2,
                 w_in=v_w_in, b_merge=v_b_merge, rnn_conv_w=v_rnn_conv_w, rnn_conv_b=v_rnn_conv_b, lru_w_a=v_lru_w_a,
                 lru_b_a=v_lru_b_a, lru_w_x=v_lru_w_x, lru_b_x=v_lru_b_x, lru_lambda=v_lru_lambda,
                 sc_conv_w=v_sc_conv_w, attn_sink=v_attn_sink, w_branch=v_w_branch, w_out=v_w_out,
                 ffn2_w13=v_ffn2_w13, ffn2_w2=v_ffn2_w2, final_norm_g=v_final_norm_g)
    order = list(weights)

    dm = Dims()
    dm.D = D = x.shape[-1]
    dm.L = L = x.shape[1]
    dm.C = C = ctx.shape[1]
    dm.T = T = L + C
    dm.RW = RW = rnn_conv_b.shape[-1]
    dm.NB = lru_w_a.shape[2]
    H = attn_sink.shape[-1]
    dm.HQ = HQ = H * HEAD_DIM
    NZ = w_in.shape[-1] * N_CHIP
    dm.KW = KW = (NZ - 5 * RW - HQ - N_BRANCH * D) // 2
    dm.off_q = 5 * RW
    dm.off_k = dm.off_q + HQ
    dm.off_g = dm.off_k + 2 * KW
    dm.tme = _pick(C, (256, 128))
    dm.nt = T // dm.tme
    dm.nctx = C // dm.tme
    dm.cw = next(w for w in (512, 256, 128) if dm.off_g % w == 0 and D % w == 0)
    A = ada_b.shape[-1]
    Aq = ada_w.shape[-1]
    assert dm.off_q % HQ == 0 and dm.off_k % KW == 0 and HQ % KW == 0 and L % dm.tme == 0 and RW == HQ
    assert C % Q_BLOCK == 0 and L % Q_BLOCK == 0 and D % N_CHIP == 0 and A == N_MOD * D

    mx, my, mc = _my_pos()
    j_me = 2 * mx + my
    b_me = 4 * mx + 2 * my + mc

    big = ["ffn1_w13", "ffn1_w2", "w_in", "w_branch", "w_out", "ffn2_w13", "ffn2_w2"]
    j_idx = jnp.reshape(j_me, (1,)).astype(jnp.int32)
    FFq = ffn1_w2.shape[1]
    views = {"ffn1_w2": (2, 2 * FFq, D), "ffn2_w2": (2, 2 * FFq, D), "w_out": (D, D),
             "w_branch": (N_CHIP, N_BRANCH, RW, D // N_CHIP)}

    def make_land(key, deps):
        l, n = int(key[1]), key[3:]
        return cast_into_slot(f"{key}_cast", weights[n], l, j_idx, deps), views.get(n)

    small_sharded = ["norm_g", "b_merge", "rnn_conv_w", "lru_b_a", "lru_b_x", "lru_lambda", "sc_conv_w"]
    pack1 = _pack([c] + [weights[n] for n in small_sharded])
    g1 = allgather8("gather_small_params", pack1).reshape(N_DEV, -1, 128)
    parts = _unpack(g1, [c.shape] + [weights[n].shape for n in small_sharded], lead=(N_DEV,))
    c_all = parts[0].reshape(N_DEV, D)
    full = {n: _unshard_last(p[0::2]) for n, p in zip(small_sharded, parts[1:])}
    cond = jnp.concatenate([c_all, c_ctx[None, :], jnp.zeros((ADA_ROWS - N_DEV - 1, D), F32)], axis=0)

    bias_q = lax.dynamic_slice_in_dim(ada_b, j_me * Aq, Aq, axis=1)[:, None, :]
    mod_q = ada_forward("ada_forward", cond, ada_w, bias_q)
    g2 = allgather8("gather_mod", mod_q.reshape(-1, 128)).reshape(N_DEV, 2, ADA_ROWS, Aq)
    mod_full = _unshard_last(g2[0::2])
    mod_lat = lax.dynamic_index_in_dim(mod_full, b_me, axis=1, keepdims=False)
    mod_ctx = mod_full[:, N_DEV]
    modtabs = [jnp.stack([mod_ctx[l], mod_lat[l]]).reshape(2, N_MOD, 1, D) for l in range(2)]
    ws = WeightStream([f"l{l}_{n}" for l in range(2) for n in big], make_land, first_deps=(g2,))

    cos, sin = _rope_tables(dm)
    sink = attn_sink.reshape(2, 1, H)
    lw = dict(w_a=lru_w_a, w_x=lru_w_x,
              b_a=full["lru_b_a"][:, :, None, :], b_x=full["lru_b_x"][:, :, None, :],
              lam=full["lru_lambda"][:, :, None, :])
    gn = full["norm_g"]
    bm = full["b_merge"][:, :, None, :]
    rcw = full["rnn_conv_w"][:, :, None, :]
    scw = full["sc_conv_w"][:, :, None, :]

    h = jnp.concatenate([ctx[0], x[0]], axis=0)
    saved = []
    for l in range(2):
        mt = modtabs[l]
        sv = {}
        h, sv["ffn1"] = _ffn_forward(f"l{l}_ffn1", h, gn[l, 0:1], mt, 0, ws, f"l{l}_ffn1_w13", f"l{l}_ffn1_w2", dm)
        sv["h_mix"] = h
        u = norm_mod(f"l{l}_mix_norm", h, gn[l, 1:2], mt, 1, dm)
        wing = ws.get(f"l{l}_w_in", u)
        z = mm_cols(f"l{l}_w_in", u, wing, F32, flat=True, deps=ws.take_tokens())
        xa = rnn_conv(f"l{l}_rnn_conv", z, rcw[l], rnn_conv_b[l][None, :], dm)
        scans = []
        for d in range(2):
            a_d, u_d = lru_gates(f"l{l}_lru_gates{d}", xa, lw, l, d, dm)
            h_d = lru_scan(f"l{l}_lru_scan{d}", a_d, u_d, True, d == 1, dm)
            scans.append((a_d, u_d, h_d))
        ya = rnn_out(f"l{l}_rnn_out", scans[0][2], scans[1][2], z, dm)
        yb = short_conv(f"l{l}_short_conv", z, scw[l], dm)
        qr, kr, vv = qkv_prep(f"l{l}_qkv", z, cos, sin, dm)
        yatt, lse = attention(f"l{l}_attn", qr, kr, vv, sink[l], dm)
        ys = (ya, yb, yatt)
        wbg = ws.get(f"l{l}_w_branch", yatt)
        lifted = [mm_branch(f"l{l}_lift{br}", ys[br], wbg, br, deps=ws.take_tokens()) for br in range(N_BRANCH)]
        merged = merge(f"l{l}_merge", z, lifted, bm[l], dm)
        woutg = ws.get(f"l{l}_w_out", merged)
        h, y = mm_plain(f"l{l}_w_out", merged, woutg, "nn", F32, deps=ws.take_tokens(), resid=(h, mt, 1, 1.0, dm.C))
        sv.update(u=u, z=z, xa=xa, scans=scans, ys=ys, qkv=(qr, kr, vv), lse=lse, lifted=lifted, merged=merged, y=y)
        h, sv["ffn2"] = _ffn_forward(f"l{l}_ffn2", h, gn[l, 2:3], mt, 2, ws, f"l{l}_ffn2_w13", f"l{l}_ffn2_w2", dm)
        saved.append(sv)

    dh, loss_vec, d_final_g = loss_head("loss_head", h, final_norm_g[None, :], loss_target[0], dm)
    loss = lax.psum(loss_vec[0, 0], ("x", "y", "c"))

    small = {n: [None, None] for n in ["norm_g", "b_merge", "rnn_conv_w", "rnn_conv_b", "lru_w_a", "lru_b_a", "lru_w_x",
                                       "lru_b_x", "lru_lambda", "sc_conv_w", "attn_sink"]}
    dmods = [None, None]
    scatters = []

    def scatter(name, keyed):
        grads3 = [g.reshape(N_CHIP, -1, g.shape[-1]) for g in keyed.values()]
        lands = [lax.empty((3,) + g.shape[1:], BF16) for g in grads3]
        ss, rs, bufs, token = exchange_start(f"{name}_start", grads3 + lands, _scatter_plan(len(grads3)))
        scatters.append((name, ss, rs, bufs, list(keyed)))
        return token

    tok = ()
    for l in (1, 0):
        mt = modtabs[l]
        sv = saved[l]
        dh, tok, dss2, dgate2, dgn2 = _ffn_backward(f"l{l}_ffn2", dh, sv["ffn2"], gn[l, 2:3], mt, 2,
                                                    ws, f"l{l}_ffn2_w13", f"l{l}_ffn2_w2", dm, scatter,
                                                    (("ffn2_w13", l), ("ffn2_w2", l)), deps=tok)

        dyg, dgate1 = resid_bwd(f"l{l}_mix_res_bwd", dh, sv["y"], mt, 1, 1.0, dm, deps=tok)
        woutg, wbg, wing = ws.get(f"l{l}_w_out"), ws.get(f"l{l}_w_branch"), ws.get(f"l{l}_w_in")
        dmerged = mm_plain(f"l{l}_dmerged", dyg, woutg, "nt", F32)
        mix_grads = {("w_out", l): mm_plain_grad(f"l{l}_dw_out", sv["merged"], dyg)}
        dz = lax.empty((T, NZ), BF16)
        dl0, dl1, dl2, dz, dbm = merge_bwd(f"l{l}_merge_bwd", dmerged, sv["z"], sv["lifted"], bm[l], dz, dm)
        dys = []
        for br, dl in enumerate((dl0, dl1, dl2)):
            dys.append(mm_branch_t(f"l{l}_dy{br}", dl, wbg, br))
            mix_grads[("w_branch", l, br)] = mm_branch_grad(f"l{l}_dwb{br}", sv["ys"][br], dl)
        small["b_merge"][l] = dbm[:, 0]

        qr, kr, vv = sv["qkv"]
        dq, dk, dv, dsink = attention_bwd(f"l{l}_attn_bwd", qr, kr, vv, sink[l], sv["ys"][2], sv["lse"], dys[2], dm)
        dz = qkv_bwd(f"l{l}_qkv_bwd", dq, dk, dv, cos, sin, dz, dm)
        small["attn_sink"][l] = dsink[0, :H]

        dz, dscw = short_conv_bwd(f"l{l}_short_conv_bwd", dys[1], sv["z"], scw[l], dz, dm)
        small["sc_conv_w"][l] = dscw[:, 0]

        (a0, u0, h0), (a1, u1, h1) = sv["scans"]
        dhs, drg = rnn_out_bwd(f"l{l}_rnn_out_bwd", dys[0], h0, h1, sv["z"], dm)
        dxa, lru_sums = [], []
        for d, (a_d, u_d, h_d) in enumerate(sv["scans"]):
            lam_d, dla_d = lru_scan_bwd(f"l{l}_lru_scan_bwd{d}", a_d, u_d, h_d, dhs, False, d == 0, dm)
            outs = lru_gates_bwd(f"l{l}_lru_gates_bwd{d}", sv["xa"], lw, l, d, lam_d, dla_d, dm)
            dxa.append(outs[0])
            lru_sums.append(outs[1:])
        dz, drcw, drcb = rnn_conv_bwd(f"l{l}_rnn_conv_bwd", dxa[0], dxa[1], drg, sv["z"], rcw[l], dz, dm)
        small["rnn_conv_w"][l] = drcw[:, 0]
        small["rnn_conv_b"][l] = drcb[0]
        for i, n in enumerate(["lru_w_a", "lru_b_a", "lru_w_x", "lru_b_x", "lru_lambda"]):
            small[n][l] = jnp.stack([lru_sums[0][i], lru_sums[1][i]]).reshape((2,) + weights[n].shape[2:-1] + (-1,))

        du = mm_cols_t(f"l{l}_du_mix", dz, wing, flat=True)
        mix_grads[("w_in", l)] = mm_cols_grad(f"l{l}_dw_in", sv["u"], dz, flat=True)
        dh, dss1, dgn1 = norm_mod_bwd(f"l{l}_mix_norm_bwd", du, sv["h_mix"], gn[l, 1:2], mt, 1, dh, dm)
        tok = (scatter(f"l{l}_scatter_mix", mix_grads),)

        dh, tok, dss0, dgate0, dgn0 = _ffn_backward(f"l{l}_ffn1", dh, sv["ffn1"], gn[l, 0:1], mt, 0,
                                                    ws, f"l{l}_ffn1_w13", f"l{l}_ffn1_w2", dm, scatter,
                                                    (("ffn1_w13", l), ("ffn1_w2", l)), deps=tok)
        small["norm_g"][l] = jnp.concatenate([dgn0, dgn1, dgn2], axis=0)
        dmods[l] = jnp.concatenate([dss0, dgate0, dss1, dgate1, dss2, dgate2], axis=1).reshape(2, A)

    grad_x = dh[C:][None]

    pack_mod = jnp.stack([jnp.stack([dmods[l][1], dmods[l][0]]) for l in range(2)])
    g3 = allgather8("gather_dmod", pack_mod.reshape(-1, 128)).reshape(N_DEV, 2, 2, 1, A)
    dmod_full, d_ada_b = dmod_assemble("dmod_assemble", g3)
    dmod_q = lax.dynamic_slice_in_dim(dmod_full, j_me * Aq, Aq, axis=2)
    dcond_q = ada_cond_grad("ada_cond_grad", dmod_q, ada_w)

    small_names = list(small)
    small_parts = [jnp.stack(small[n]) for n in small_names] + [d_final_g, dcond_q[N_DEV]]
    small_shapes = [p.shape for p in small_parts]
    lru_big = [small_names.index("lru_w_a"), small_names.index("lru_w_x")]
    rest_idx = [i for i in range(len(small_parts)) if i not in lru_big]
    summed = [None] * len(small_parts)
    for i in lru_big:
        buf = _pack([small_parts[i]])
        tot, _ = sum_devices(f"sum_{small_names[i]}", allgather8(f"gather_{small_names[i]}", buf).reshape(N_DEV, -1, 128))
        summed[i] = _unpack(tot, [small_shapes[i]])[0]
    buf = _pack([small_parts[i] for i in rest_idx])
    tot, chip_tot = sum_devices("sum_small_grads", allgather8("gather_small_grads", buf).reshape(N_DEV, -1, 128))
    for i, val in zip(rest_idx, _unpack(tot, [small_shapes[i] for i in rest_idx])):
        summed[i] = val
    dcond_ctx = _unpack(chip_tot, [small_shapes[i] for i in rest_idx])[-1]
    sg = jax.nn.sigmoid(c_ctx)
    grads = dict(zip(small_names, summed[:len(small_names)]))
    grads["final_norm_g"] = summed[len(small_names)][0]
    grads["c_ctx"] = dcond_ctx * (sg * (1.0 + c_ctx * (1.0 - sg)))
    grads["ada_b"] = d_ada_b[:, 0]
    for n in small_sharded:
        g = grads[n]
        q = g.shape[-1] // N_CHIP
        grads[n] = lax.dynamic_slice_in_dim(g, j_me * q, q, axis=g.ndim - 1)

    arrived = {}

    def collect(idx, after):
        name, ss, rs, bufs, keys = scatters[idx]
        done = exchange_wait(f"{name}_wait", ss, rs, bufs, _scatter_plan(len(keys)), after)
        for i, key in enumerate(keys):
            arrived[key] = (done[i], done[len(keys) + i])

    for idx in range(len(scatters) - 1):
        collect(idx, dh)
    results = {}
    last_done = dh

    def finish(swap, after):
        n, ss, rs, bufs = swap
        own, other = exchange_wait(f"swap_{n}_wait", ss, rs, bufs, _sibling_plan, after)
        results[n] = adamw(f"adamw_{n}", weights[n], mom_m[n], mom_v[n], [own, other])
        return results[n][0]

    in_flight = None
    for n in ["ffn2_w13", "ffn2_w2", "w_out", "w_branch", "w_in", "ffn1_w2", "ffn1_w13"]:
        if not any(k[0] == n and k[1] == 0 for k in arrived):
            collect(len(scatters) - 1, last_done)
        keys = sorted((k for k in arrived if k[0] == n), key=lambda k: k[1:])
        part = sum_parts(f"sum_{n}", [arrived[k] for k in keys], j_idx).reshape(weights[n].shape)
        ss, rs, bufs, token = exchange_start(f"swap_{n}_start", [part, lax.empty(part.shape, F32)], _sibling_plan)
        if in_flight is not None:
            last_done = finish(in_flight, token)
        in_flight = (n, ss, rs, bufs)
    finish(in_flight, last_done)
    results["ada_w"] = ada_update("ada_update", cond, dmod_q, ada_w, m_ada_w, v_ada_w)
    small_all = [n for n in order if n not in results]
    pk = lambda d: _pack([d[n] for n in small_all])
    outs = adamw("adamw_small", pk(weights), pk(mom_m), pk(mom_v), [pk(grads)])
    shapes_small = [weights[n].shape for n in small_all]
    unpacked = [_unpack(o, shapes_small) for o in outs]
    for i, n in enumerate(small_all):
        results[n] = tuple(unpacked[k][i] for k in range(4))

    return (loss, grad_x, *[results[n][0] for n in order], *[results[n][1] for n in order],
            *[results[n][2] for n in order], *[results[n][3] for n in order])
```

```python
import functools
import math

import jax
import jax.numpy as jnp
from jax import lax
from jax.experimental import pallas as pl
from jax.experimental.pallas import tpu as pltpu

F32 = jnp.float32
BF16 = jnp.bfloat16
MESH = pl.DeviceIdType.MESH

HEAD_DIM = 128
GRID_W = 64
WINDOW = 128
Q_BLOCK = 128
ROPE_BASE = 10000.0
LRU_C = 8.0
EPS = 1e-6
NEG_INF = -1e30
N_MOD = 9
N_BRANCH = 3
RNN_BLOCK = 128
HALO = 16
SCAN_ROWS = 8
LSE_W = 128

ADAM_LR = 0.001
ADAM_B1 = 0.9
ADAM_B2 = 0.999
ADAM_EPS = 1e-08
ADAM_WD = 0.01
ADAM_STEP = 10

VMEM_LIMIT_BYTES = 48 * 1024 * 1024
N_DEV = 8
N_CHIP = 4


def _pick(n, cands):
    for c in cands:
        if c <= n and n % c == 0:
            return c
    return n


def _cparams(sem):
    return pltpu.CompilerParams(dimension_semantics=sem, vmem_limit_bytes=VMEM_LIMIT_BYTES)


def _silu(x):
    return x * jax.nn.sigmoid(x)


def _dsilu(x):
    s = jax.nn.sigmoid(x)
    return s * (1.0 + x * (1.0 - s))


_GELU_K = math.sqrt(2.0 / math.pi)


def _gelu(x):
    return 0.5 * x * (1.0 + jnp.tanh(_GELU_K * (x + 0.044715 * x * x * x)))


def _dgelu(x):
    t = jnp.tanh(_GELU_K * (x + 0.044715 * x * x * x))
    return 0.5 * (1.0 + t) + 0.5 * x * (1.0 - t * t) * _GELU_K * (1.0 + 3.0 * 0.044715 * x * x)


def _expm1(x):
    series = x * (1.0 + x * (0.5 + x * (1.0 / 6.0 + x * (1.0 / 24.0 + x * (1.0 / 120.0)))))
    return jnp.where(jnp.abs(x) < 0.1, series, jnp.exp(x) - 1.0)


def _my_pos():
    return lax.axis_index("x"), lax.axis_index("y"), lax.axis_index("c")


_DIMS = {"nn": (((1,), (0,)), ((), ())), "nt": (((1,), (1,)), ((), ())), "tn": (((0,), (0,)), ((), ()))}


def _mm(name, a, b, *, mode, grid, a_blk, a_map, b_blk, b_map, o_blk, o_map, out_shape, out_dtype, deps=(), resid=None):
    nk = grid[-1]
    nax = len(grid)
    acc_shape = tuple(d for d in o_blk if d is not None)

    def product(a_ref, b_ref):
        return lax.dot_general(a_ref[...].astype(BF16), b_ref[...].astype(BF16), _DIMS[mode], preferred_element_type=F32)

    def write(res, rest):
        if resid is None:
            o_ref = rest[-1] if nk == 1 else rest[-2]
            o_ref[...] = res.astype(o_ref.dtype)
            return
        _, _, s, coef, n_ctx = resid
        h_ref, m_ref = rest[len(deps)], rest[len(deps) + 1]
        o_ref, f_ref = rest[len(deps) + 2], rest[len(deps) + 3]
        tm = acc_shape[0]
        row = pl.program_id(0) * tm + lax.broadcasted_iota(jnp.int32, acc_shape, 0)
        gate = jnp.where(row < n_ctx, m_ref[0, 3 * s + 2], m_ref[1, 3 * s + 2])
        f_ref[...] = res.astype(f_ref.dtype)
        o_ref[...] = h_ref[...] + (coef * gate) * res

    def body_one_step(a_ref, b_ref, *rest):
        write(product(a_ref, b_ref), rest)

    def body(a_ref, b_ref, *rest):
        acc_ref = rest[-1]
        k = pl.program_id(nax - 1)

        @pl.when(k == 0)
        def _():
            acc_ref[...] = jnp.zeros_like(acc_ref)

        acc_ref[...] += product(a_ref, b_ref)

        @pl.when(k == nk - 1)
        def _():
            write(acc_ref[...], rest)

    in_specs = [pl.BlockSpec(a_blk, a_map), pl.BlockSpec(b_blk, b_map)] + [pl.BlockSpec(memory_space=pl.ANY)] * len(deps)
    out_specs = pl.BlockSpec(o_blk, o_map)
    out_shapes = jax.ShapeDtypeStruct(out_shape, out_dtype)
    operands = (a, b, *deps)
    if resid is not None:
        h, modtab = resid[0], resid[1]
        tn = o_blk[-1]
        in_specs += [pl.BlockSpec(o_blk, o_map),
                     pl.BlockSpec((2, N_MOD, 1, tn), lambda *idx: (0, 0, 0, o_map(*idx)[-1]))]
        out_specs = [pl.BlockSpec(o_blk, o_map), pl.BlockSpec(o_blk, o_map)]
        out_shapes = [jax.ShapeDtypeStruct(out_shape, F32), jax.ShapeDtypeStruct(out_shape, BF16)]
        operands += (h, modtab)
    return pl.pallas_call(
        body_one_step if nk == 1 else body, name=name, grid=grid,
        in_specs=in_specs, out_specs=out_specs, out_shape=out_shapes,
        scratch_shapes=[] if nk == 1 else [pltpu.VMEM(acc_shape, F32)],
        compiler_params=_cparams(("parallel",) * (nax - 1) + ("arbitrary",)),
    )(*operands)


def _tiles(n):
    return _pick(n, (768, 512, 384, 256, 128, 64, 32, 16))


def _tiles_long(n):
    return _pick(n, (1408, 768, 512, 384, 256, 128, 64, 32, 16))


VMEM_TILE_BUDGET = 40 * 1024 * 1024


def _fit(n, nbytes):
    for c in (768, 512, 384, 256, 128, 64, 32, 16):
        if c <= n and n % c == 0 and nbytes(c) <= VMEM_TILE_BUDGET:
            return c
    return _pick(n, (16, 8))


def _whole(n, cap=2048):
    return n if n <= cap else _ktile(n)


def _ktile(n):
    return _pick(n, (512, 256, 128))


def mm_cols(name, a, wg, out_dtype, flat, deps=()):
    T, K = a.shape
    Nq = wg.shape[-1]
    tk = _whole(K)
    osize = jnp.dtype(out_dtype).itemsize
    tm = _fit(T, lambda t: 2 * t * Nq * osize + 4 * tk * Nq + 4 * t * tk + (4 * t * Nq if tk < K else 0))
    if flat:
        o_blk, o_map, o_shape = (tm, Nq), (lambda j, i, k: (i, j)), (T, N_CHIP * Nq)
    else:
        o_blk, o_map, o_shape = (None, tm, Nq), (lambda j, i, k: (j, i, 0)), (N_CHIP, T, Nq)
    return _mm(name, a, wg, mode="nn", grid=(N_CHIP, T // tm, K // tk),
               a_blk=(tm, tk), a_map=lambda j, i, k: (i, k),
               b_blk=(None, tk, Nq), b_map=lambda j, i, k: (j, k, 0),
               o_blk=o_blk, o_map=o_map, out_shape=o_shape, out_dtype=out_dtype, deps=deps)


def mm_cols_t(name, d, wg, flat, deps=()):
    K, Nq = wg.shape[-2:]
    T = d.shape[-2]
    tm, tn = _tiles_long(T), _ktile(K)
    if flat:
        a_blk, a_map = (tm, Nq), (lambda i, j, k: (i, k))
    else:
        a_blk, a_map = (None, tm, Nq), (lambda i, j, k: (k, i, 0))
    return _mm(name, d, wg, mode="nt", grid=(T // tm, K // tn, N_CHIP),
               a_blk=a_blk, a_map=a_map,
               b_blk=(None, tn, Nq), b_map=lambda i, j, k: (k, j, 0),
               o_blk=(tm, tn), o_map=lambda i, j, k: (i, j), out_shape=(T, K), out_dtype=F32, deps=deps)


def mm_cols_grad(name, a, d, flat):
    T, K = a.shape
    Nq = d.shape[-1] // N_CHIP if flat else d.shape[-1]
    tt, br = _tiles_long(T), _ktile(K)
    if flat:
        b_blk, b_map = (tt, Nq), (lambda j, r, t: (t, j))
    else:
        b_blk, b_map = (None, tt, Nq), (lambda j, r, t: (j, t, 0))
    return _mm(name, a, d, mode="tn", grid=(N_CHIP, K // br, T // tt),
               a_blk=(tt, br), a_map=lambda j, r, t: (t, r),
               b_blk=b_blk, b_map=b_map,
               o_blk=(None, br, Nq), o_map=lambda j, r, t: (j, r, 0),
               out_shape=(N_CHIP, K, Nq), out_dtype=BF16)


def mm_rows(name, a, wg, deps=(), resid=None):
    G, T, Kg = a.shape
    N = wg.shape[-1]
    tm, tn = _tiles(T), _pick(N, (1024, 512, 256, 128))
    return _mm(name, a, wg, mode="nn", grid=(T // tm, N // tn, G),
               a_blk=(None, tm, Kg), a_map=lambda i, j, k: (k, i, 0),
               b_blk=(None, Kg, tn), b_map=lambda i, j, k: (k, 0, j),
               o_blk=(tm, tn), o_map=lambda i, j, k: (i, j), out_shape=(T, N), out_dtype=F32, deps=deps, resid=resid)


def mm_rows_t(name, d, wg):
    T, N = d.shape
    G, Kg = wg.shape[0], wg.shape[1]
    tk = _whole(N)
    tm = _fit(T, lambda t: 4 * t * Kg + 4 * Kg * tk + 4 * t * tk + (4 * t * Kg if tk < N else 0))
    return _mm(name, d, wg, mode="nt", grid=(G, T // tm, N // tk),
               a_blk=(tm, tk), a_map=lambda j, i, k: (i, k),
               b_blk=(None, Kg, tk), b_map=lambda j, i, k: (j, 0, k),
               o_blk=(None, tm, Kg), o_map=lambda j, i, k: (j, i, 0), out_shape=(G, T, Kg), out_dtype=BF16)


def mm_rows_grad(name, a, d):
    G, T, Kg = a.shape
    N = d.shape[-1]
    tt, tn = _tiles_long(T), _ktile(N)
    return _mm(name, a, d, mode="tn", grid=(G, N // tn, T // tt),
               a_blk=(None, tt, Kg), a_map=lambda g, j, t: (g, t, 0),
               b_blk=(tt, tn), b_map=lambda g, j, t: (t, j),
               o_blk=(None, Kg, tn), o_map=lambda g, j, t: (g, 0, j), out_shape=(G, Kg, N), out_dtype=BF16)


def mm_plain(name, a, w, mode, out_dtype, deps=(), resid=None):
    T = a.shape[0]
    K, N = w.shape[-2:]
    tm = _tiles(T)
    if mode == "nn":
        tn, tk = (_whole(N) if resid is None else _pick(N, (1024, 512, 256, 128))), _whole(K)
        return _mm(name, a, w, mode="nn", grid=(T // tm, N // tn, K // tk),
                   a_blk=(tm, tk), a_map=lambda i, j, k: (i, k),
                   b_blk=(tk, tn), b_map=lambda i, j, k: (k, j),
                   o_blk=(tm, tn), o_map=lambda i, j, k: (i, j), out_shape=(T, N), out_dtype=out_dtype, deps=deps,
                   resid=resid)
    tn, tk = _whole(K), _whole(N)
    return _mm(name, a, w, mode="nt", grid=(T // tm, K // tn, N // tk),
               a_blk=(tm, tk), a_map=lambda i, j, k: (i, k),
               b_blk=(tn, tk), b_map=lambda i, j, k: (j, k),
               o_blk=(tm, tn), o_map=lambda i, j, k: (i, j), out_shape=(T, K), out_dtype=out_dtype)


def mm_plain_grad(name, a, d):
    T, K = a.shape
    N = d.shape[-1]
    tt, br, tn = _tiles_long(T), _ktile(K), _whole(N)
    return _mm(name, a, d, mode="tn", grid=(K // br, N // tn, T // tt),
               a_blk=(tt, br), a_map=lambda r, j, t: (t, r),
               b_blk=(tt, tn), b_map=lambda r, j, t: (t, j),
               o_blk=(br, tn), o_map=lambda r, j, t: (r, j), out_shape=(K, N), out_dtype=BF16)


def mm_branch(name, y, wbg, br, deps=()):
    T, RW = y.shape
    Dq = wbg.shape[-1]
    tm = _tiles_long(T)

    def body(y_ref, w_ref, *rest):
        o_ref = rest[-1]
        lhs = y_ref[...]
        for j in range(N_CHIP):
            o_ref[:, j * Dq:(j + 1) * Dq] = jnp.dot(lhs, w_ref[j], preferred_element_type=F32).astype(o_ref.dtype)

    return pl.pallas_call(
        body, name=name, grid=(T // tm,),
        in_specs=[pl.BlockSpec((tm, RW), lambda i: (i, 0)),
                  pl.BlockSpec((N_CHIP, None, RW, Dq), lambda i: (0, br, 0, 0))] + [pl.BlockSpec(memory_space=pl.ANY)] * len(deps),
        out_specs=pl.BlockSpec((tm, N_CHIP * Dq), lambda i: (i, 0)),
        out_shape=jax.ShapeDtypeStruct((T, N_CHIP * Dq), BF16), compiler_params=_cparams(("parallel",)),
    )(y, wbg, *deps)


def mm_branch_t(name, d, wbg, br):
    T = d.shape[0]
    RW, Dq = wbg.shape[-2:]
    tm = _tiles_long(T)

    def body(d_ref, w_ref, o_ref):
        for j in range(N_CHIP):
            term = lax.dot_general(d_ref[:, j * Dq:(j + 1) * Dq], w_ref[j], _DIMS["nt"], preferred_element_type=F32)
            if j == 0:
                o_ref[...] = term
            else:
                o_ref[...] += term

    return pl.pallas_call(
        body, name=name, grid=(T // tm,),
        in_specs=[pl.BlockSpec((tm, N_CHIP * Dq), lambda i: (i, 0)),
                  pl.BlockSpec((N_CHIP, None, RW, Dq), lambda i: (0, br, 0, 0))],
        out_specs=pl.BlockSpec((tm, RW), lambda i: (i, 0)),
        out_shape=jax.ShapeDtypeStruct((T, RW), F32), compiler_params=_cparams(("parallel",)),
    )(d, wbg)


def mm_branch_grad(name, y, d):
    T, RW = y.shape
    D = d.shape[-1]
    Dq = D // N_CHIP
    tt = _tiles_long(T)
    nt = T // tt

    def body(y_ref, d_ref, o_ref, acc_ref):
        t = pl.program_id(0)

        @pl.when(t == 0)
        def _():
            acc_ref[...] = jnp.zeros_like(acc_ref)

        acc_ref[...] += lax.dot_general(y_ref[...], d_ref[...], _DIMS["tn"], preferred_element_type=F32)

        @pl.when(t == nt - 1)
        def _():
            for j in range(N_CHIP):
                o_ref[j] = acc_ref[:, j * Dq:(j + 1) * Dq].astype(o_ref.dtype)

    return pl.pallas_call(
        body, name=name, grid=(nt,),
        in_specs=[pl.BlockSpec((tt, RW), lambda t: (t, 0)), pl.BlockSpec((tt, D), lambda t: (t, 0))],
        out_specs=pl.BlockSpec((N_CHIP, RW, Dq), lambda t: (0, 0, 0)),
        out_shape=jax.ShapeDtypeStruct((N_CHIP, RW, Dq), BF16),
        scratch_shapes=[pltpu.VMEM((RW, D), F32)], compiler_params=_cparams(("arbitrary",)),
    )(y, d)


def allgather8(name, x_shard):
    m_per, n = x_shard.shape

    def body(x_ref, out_ref, send_sems, recv_sems, local_sem):
        x, y, c = _my_pos()
        me, sibling = (x, y, c), (x, y, 1 - c)
        chips = [(1 - x, y), (x, 1 - y), (1 - x, 1 - y)]

        def rows(px, py, pc):
            return out_ref.at[pl.ds((4 * px + 2 * py + pc) * m_per, m_per), :]

        def copy(k, block, to, src=None):
            return pltpu.make_async_remote_copy(
                src_ref=rows(*block) if src is None else src, dst_ref=rows(*block),
                send_sem=send_sems.at[k], recv_sem=recv_sems.at[k], device_id=to, device_id_type=MESH)

        mine = pltpu.make_async_copy(x_ref, rows(*me), local_sem)
        mine.start()
        first = [copy(0, me, sibling, src=x_ref)]
        first += [copy(1 + j, me, (*chip, c), src=x_ref) for j, chip in enumerate(chips)]
        for cp in first:
            cp.start()
        passed = [copy(4 + j, (*chip, c), sibling) for j, chip in enumerate(chips)]
        for j, chip in enumerate(chips):
            copy(1 + j, (*chip, c), me).wait_recv()
            passed[j].start()
        copy(0, sibling, me).wait_recv()
        for j, chip in enumerate(chips):
            copy(4 + j, (*chip, 1 - c), me).wait_recv()
        for cp in first + passed:
            cp.wait_send()
        mine.wait()

    return pl.pallas_call(
        body, name=name,
        out_shape=jax.ShapeDtypeStruct((N_DEV * m_per, n), x_shard.dtype),
        in_specs=[pl.BlockSpec(memory_space=pltpu.VMEM)],
        out_specs=pl.BlockSpec(memory_space=pltpu.VMEM),
        scratch_shapes=[pltpu.SemaphoreType.DMA((7,)), pltpu.SemaphoreType.DMA((7,)), pltpu.SemaphoreType.DMA],
        compiler_params=pltpu.CompilerParams(vmem_limit_bytes=VMEM_LIMIT_BYTES),
    )(x_shard)


def _other_chips(x, y):
    return [(1 - x, y), (x, 1 - y), (1 - x, 1 - y)]


_HBM = pl.BlockSpec(memory_space=pltpu.HBM)
_SEM = pl.BlockSpec(memory_space=pltpu.SEMAPHORE)
_ANY = pl.BlockSpec(memory_space=pl.ANY)
_EFFECT = pltpu.SideEffectType.DATAFLOW_SIDE_EFFECTING
TOKEN_SHAPE = (8, 128)


def _in_hbm(a):
    return pltpu.with_memory_space_constraint(a, pltpu.HBM)


def exchange_start(name, bufs, plan):
    n = len(bufs)
    n_copies = len(plan([None] * n, 0, 0, 0, dry=True))

    def body(*refs):
        ins = refs[:n]
        send_sems, recv_sems = refs[n], refs[n + 1]
        token = refs[-1]
        x, y, c = _my_pos()
        for i, (src, dst, to) in enumerate(plan(ins, x, y, c)):
            pltpu.make_async_remote_copy(src_ref=src, dst_ref=dst, send_sem=send_sems.at[i], recv_sem=recv_sems.at[i],
                                         device_id=to, device_id_type=MESH).start()
        token[...] = jnp.zeros_like(token)

    outs = pl.pallas_call(
        body, name=name,
        out_shape=(pltpu.SemaphoreType.DMA((n_copies,)), pltpu.SemaphoreType.DMA((n_copies,)),
                   *[pltpu.HBM(b.shape, b.dtype) for b in bufs], jax.ShapeDtypeStruct(TOKEN_SHAPE, F32)),
        in_specs=[_HBM] * n,
        out_specs=(_SEM, _SEM, *[_HBM] * n, pl.BlockSpec(memory_space=pltpu.VMEM)),
        input_output_aliases={i: 2 + i for i in range(n)},
        compiler_params=pltpu.CompilerParams(has_side_effects=_EFFECT),
    )(*[_in_hbm(b) for b in bufs])
    return outs[0], outs[1], list(outs[2:2 + n]), outs[-1]


def exchange_wait(name, send_sems, recv_sems, bufs, plan, after):
    n = len(bufs)

    def body(*refs):
        ins = refs[:n]
        send_sems, recv_sems = refs[n], refs[n + 1]
        x, y, c = _my_pos()
        for i, (src, dst, to) in enumerate(plan(ins, x, y, c, arriving=True)):
            cp = pltpu.make_async_remote_copy(src_ref=src, dst_ref=dst, send_sem=send_sems.at[i],
                                              recv_sem=recv_sems.at[i], device_id=to, device_id_type=MESH)
            cp.wait_send()
            cp.wait_recv()

    outs = pl.pallas_call(
        body, name=name,
        out_shape=tuple(pltpu.HBM(b.shape, b.dtype) for b in bufs),
        in_specs=[_HBM] * n + [_SEM, _SEM, _ANY],
        out_specs=tuple([_HBM] * n),
        input_output_aliases={i: i for i in range(n)},
        compiler_params=pltpu.CompilerParams(has_side_effects=_EFFECT),
    )(*bufs, send_sems, recv_sems, after)
    return list(outs)


def _gather_plan(refs, x, y, c, dry=False, arriving=False):
    if dry:
        return [None] * 3
    (land,) = refs
    j_me = 2 * x + y
    return [(land.at[j_me], land.at[(2 * px + py) if arriving else j_me], (px, py, c)) for px, py in _other_chips(x, y)]


def _sibling_plan(refs, x, y, c, dry=False, arriving=False):
    if dry:
        return [None]
    src, land = refs
    return [(src, land, (x, y, 1 - c))]


def _scatter_plan(n_pieces):
    def plan(refs, x, y, c, dry=False, arriving=False):
        if dry:
            return [None] * (3 * n_pieces)
        grads, lands = refs[:n_pieces], refs[n_pieces:]
        return [(grads[p].at[2 * px + py], lands[p].at[k], (px, py, c))
                for p in range(n_pieces) for k, (px, py) in enumerate(_other_chips(x, y))]
    return plan


def _view2d(a):
    return a.reshape(-1, a.shape[-1])


def _row_tile(rows, width, itemsize=4, budget=1 << 20):
    t = 8
    for cand in (1024, 512, 256, 128, 64, 32, 16, 8):
        if rows % cand == 0 and cand * width * itemsize <= budget:
            t = cand
            break
    return t if rows % t == 0 else rows


def cast_into_slot(name, w, l, j_idx, deps=()):
    w3 = w.reshape(w.shape[0], -1, w.shape[-1])
    _, R, W = w3.shape
    tr = _row_tile(R, W)

    def body(j_ref, a_ref, *rest):
        o_ref = rest[-1]
        o_ref[...] = a_ref[...].astype(BF16)

    return pl.pallas_call(
        body, name=name,
        grid_spec=pltpu.PrefetchScalarGridSpec(
            num_scalar_prefetch=1, grid=(R // tr,),
            in_specs=[pl.BlockSpec((None, tr, W), lambda i, j: (l, i, 0))] + [pl.BlockSpec(memory_space=pl.ANY)] * len(deps),
            out_specs=pl.BlockSpec((None, tr, W), lambda i, j: (j[0], i, 0))),
        out_shape=jax.ShapeDtypeStruct((N_CHIP, R, W), BF16), compiler_params=_cparams(("parallel",)),
    )(j_idx, w3, *deps)


def sum_parts(name, groups, j_idx):
    n = len(groups)
    _, R, W = groups[0][0].shape
    tr = _row_tile(R, W)

    def body(j_ref, *refs):
        o_ref = refs[-1]
        g = pl.program_id(0)
        for q in range(n):
            @pl.when(g == q)
            def _(q=q):
                own, got = refs[2 * q], refs[2 * q + 1]
                o_ref[...] = ((own[...].astype(F32) + got[0].astype(F32)) + got[1].astype(F32)) + got[2].astype(F32)

    in_specs = []
    for q in range(n):
        in_specs.append(pl.BlockSpec((None, tr, W), lambda g, i, j, q=q: (j[0], jnp.where(g == q, i, 0), 0)))
        in_specs.append(pl.BlockSpec((3, tr, W), lambda g, i, j, q=q: (0, jnp.where(g == q, i, 0), 0)))
    return pl.pallas_call(
        body, name=name,
        grid_spec=pltpu.PrefetchScalarGridSpec(
            num_scalar_prefetch=1, grid=(n, R // tr), in_specs=in_specs,
            out_specs=pl.BlockSpec((None, tr, W), lambda g, i, j: (g, i, 0))),
        out_shape=jax.ShapeDtypeStruct((n, R, W), F32), compiler_params=_cparams(("arbitrary", "arbitrary")),
    )(j_idx, *[a for pair in groups for a in pair])


def adamw(name, w, m, v, g_parts):
    shape = w.shape
    w2, m2, v2 = _view2d(w), _view2d(m), _view2d(v)
    gs = [_view2d(g) for g in g_parts]
    R, W = w2.shape
    tr = _row_tile(R, W, budget=1 << 19)
    ng = len(gs)
    bc1 = 1.0 - ADAM_B1 ** ADAM_STEP
    bc2 = 1.0 - ADAM_B2 ** ADAM_STEP

    def body(*refs):
        w_ref, m_ref, v_ref = refs[:3]
        g_refs = refs[3:3 + ng]
        go_ref, d_ref, mo_ref, vo_ref = refs[3 + ng:]
        g = g_refs[0][...]
        for r in g_refs[1:]:
            g = g + r[...]
        mn = ADAM_B1 * m_ref[...] + (1.0 - ADAM_B1) * g
        vn = ADAM_B2 * v_ref[...] + (1.0 - ADAM_B2) * (g * g)
        m_hat = mn / bc1
        v_hat = vn / bc2
        go_ref[...] = g
        d_ref[...] = -ADAM_LR * (m_hat / (jnp.sqrt(v_hat) + ADAM_EPS) + ADAM_WD * w_ref[...])
        mo_ref[...] = mn
        vo_ref[...] = vn

    spec = pl.BlockSpec((tr, W), lambda i: (i, 0))
    outs = pl.pallas_call(
        body, name=name, grid=(R // tr,),
        in_specs=[spec] * (3 + ng), out_specs=[spec] * 4,
        out_shape=[jax.ShapeDtypeStruct((R, W), F32)] * 4, compiler_params=_cparams(("parallel",)),
    )(w2, m2, v2, *gs)
    return tuple(o.reshape(shape) for o in outs)


class Dims:
    pass


def _sel(dm):
    return (pl.program_id(0) >= dm.nctx).astype(jnp.int32)


def norm_mod(name, h, gn, modtab, s, dm, deps=()):
    T, D = h.shape
    tm = dm.tme

    def body(h_ref, g_ref, m_ref, *rest):
        u_ref = rest[-1]
        sel = _sel(dm)
        x = h_ref[...]
        r = lax.rsqrt(jnp.mean(x * x, axis=-1, keepdims=True) + EPS)
        ng = x * r * g_ref[...]
        u_ref[...] = (ng * (1.0 + m_ref[sel, 3 * s + 1]) + m_ref[sel, 3 * s]).astype(u_ref.dtype)

    return pl.pallas_call(
        body, name=name, grid=(T // tm,),
        in_specs=[pl.BlockSpec((tm, D), lambda i: (i, 0)), pl.BlockSpec((1, D), lambda i: (0, 0)),
                  pl.BlockSpec((2, N_MOD, 1, D), lambda i: (0, 0, 0, 0))] + [_ANY] * len(deps),
        out_specs=pl.BlockSpec((tm, D), lambda i: (i, 0)),
        out_shape=jax.ShapeDtypeStruct((T, D), BF16), compiler_params=_cparams(("parallel",)),
    )(h, gn, modtab, *deps)


def norm_mod_bwd(name, du, h, gn, modtab, s, dh_in, dm):
    T, D = h.shape
    tm = dm.tme

    def body(du_ref, h_ref, g_ref, m_ref, dhi_ref, dh_ref, dmod_ref, dg_ref):
        i = pl.program_id(0)
        sel = _sel(dm)

        @pl.when(i == 0)
        def _():
            dmod_ref[...] = jnp.zeros_like(dmod_ref)
            dg_ref[...] = jnp.zeros_like(dg_ref)

        x = h_ref[...]
        r = lax.rsqrt(jnp.mean(x * x, axis=-1, keepdims=True) + EPS)
        n = x * r
        g = g_ref[...]
        du = du_ref[...]
        dmod_ref[sel, 0] += jnp.sum(du, axis=0, keepdims=True)
        dmod_ref[sel, 1] += jnp.sum(du * (n * g), axis=0, keepdims=True)
        dng = du * (1.0 + m_ref[sel, 3 * s + 1])
        dg_ref[...] += jnp.sum(dng * n, axis=0, keepdims=True)
        dn = dng * g
        dh_ref[...] = dhi_ref[...] + r * (dn - n * jnp.mean(dn * n, axis=-1, keepdims=True))

    row = pl.BlockSpec((tm, D), lambda i: (i, 0))
    return pl.pallas_call(
        body, name=name, grid=(T // tm,),
        in_specs=[row, row, pl.BlockSpec((1, D), lambda i: (0, 0)),
                  pl.BlockSpec((2, N_MOD, 1, D), lambda i: (0, 0, 0, 0)), row],
        out_specs=[row, pl.BlockSpec((2, 2, 1, D), lambda i: (0, 0, 0, 0)), pl.BlockSpec((1, D), lambda i: (0, 0))],
        out_shape=[jax.ShapeDtypeStruct((T, D), F32), jax.ShapeDtypeStruct((2, 2, 1, D), F32),
                   jax.ShapeDtypeStruct((1, D), F32)],
        compiler_params=_cparams(("arbitrary",)),
    )(du, h, gn, modtab, dh_in)


def resid_bwd(name, dh, f, modtab, s, coef, dm, deps=()):
    T, D = dh.shape
    tm = dm.tme

    def body(dh_ref, f_ref, m_ref, *rest):
        df_ref, dg_ref = rest[-2:]
        sel = _sel(dm)

        @pl.when(pl.program_id(0) == 0)
        def _():
            dg_ref[...] = jnp.zeros_like(dg_ref)

        d = coef * dh_ref[...]
        df_ref[...] = (d * m_ref[sel, 3 * s + 2]).astype(df_ref.dtype)
        dg_ref[sel, 0] += jnp.sum(d * f_ref[...], axis=0, keepdims=True)

    row = pl.BlockSpec((tm, D), lambda i: (i, 0))
    return pl.pallas_call(
        body, name=name, grid=(T // tm,),
        in_specs=[row, row, pl.BlockSpec((2, N_MOD, 1, D), lambda i: (0, 0, 0, 0))] + [_ANY] * len(deps),
        out_specs=[row, pl.BlockSpec((2, 1, 1, D), lambda i: (0, 0, 0, 0))],
        out_shape=[jax.ShapeDtypeStruct((T, D), BF16), jax.ShapeDtypeStruct((2, 1, 1, D), F32)],
        compiler_params=_cparams(("arbitrary",)),
    )(dh, f, modtab, *deps)


def swiglu(name, gu, dm):
    _, T, Nq = gu.shape
    tm = dm.tme
    gu4 = gu.reshape(2, 2, T, Nq)

    def body(gu_ref, o_ref):
        g = gu_ref[0].astype(F32)
        o_ref[...] = (_silu(g) * gu_ref[1].astype(F32)).astype(o_ref.dtype)

    return pl.pallas_call(
        body, name=name, grid=(2, T // tm),
        in_specs=[pl.BlockSpec((2, None, tm, Nq), lambda k, i: (0, k, i, 0))],
        out_specs=pl.BlockSpec((None, tm, Nq), lambda k, i: (k, i, 0)),
        out_shape=jax.ShapeDtypeStruct((2, T, Nq), BF16), compiler_params=_cparams(("parallel", "parallel")),
    )(gu4)


def swiglu_bwd(name, dact, gu, dm):
    _, T, Nq = gu.shape
    tm = dm.tme
    gu4 = gu.reshape(2, 2, T, Nq)

    def body(da_ref, gu_ref, o_ref):
        g = gu_ref[0].astype(F32)
        da = da_ref[...].astype(F32)
        o_ref[0] = (da * gu_ref[1].astype(F32) * _dsilu(g)).astype(o_ref.dtype)
        o_ref[1] = (da * _silu(g)).astype(o_ref.dtype)

    out = pl.pallas_call(
        body, name=name, grid=(2, T // tm),
        in_specs=[pl.BlockSpec((None, tm, Nq), lambda k, i: (k, i, 0)),
                  pl.BlockSpec((2, None, tm, Nq), lambda k, i: (0, k, i, 0))],
        out_specs=pl.BlockSpec((2, None, tm, Nq), lambda k, i: (0, k, i, 0)),
        out_shape=jax.ShapeDtypeStruct((2, 2, T, Nq), BF16), compiler_params=_cparams(("parallel", "parallel")),
    )(dact, gu4)
    return out.reshape(4, T, Nq)


def _halo_specs(dm, width, col):
    tm = dm.tme
    per = tm // HALO
    last = dm.T // HALO - 1
    return [pl.BlockSpec((tm, width), lambda i: (i, col)),
            pl.BlockSpec((HALO, width), lambda i: (jnp.maximum(i * per - 1, 0), col)),
            pl.BlockSpec((HALO, width), lambda i: (jnp.minimum((i + 1) * per, last), col))]


def _segment_edges(dm, i):
    first = jnp.logical_or(i == 0, i == dm.nctx)
    last = jnp.logical_or(i == dm.nctx - 1, i == dm.nt - 1)
    return first, last


def _extend(main, prev, nxt, first, last):
    return jnp.concatenate([jnp.where(first, 0.0, prev), main, jnp.where(last, 0.0, nxt)], axis=0)


def _shift(ext, o, tm):
    n = ext.shape[0]
    rolled = ext if o == 0 else pltpu.roll(ext, (-o) % n, 0)
    return rolled[HALO:HALO + tm]


def _load_ext(refs, first, last):
    main, prev, nxt = refs
    return _extend(main[...].astype(F32), prev[...].astype(F32), nxt[...].astype(F32), first, last)


def rnn_conv(name, z, w, b, dm):
    T, RW, tm = dm.T, dm.RW, dm.tme

    def body(main, prev, nxt, w_ref, b_ref, o_ref):
        first, last = _segment_edges(dm, pl.program_id(0))
        ext = _load_ext((main, prev, nxt), first, last)
        acc = jnp.zeros((tm, RW), F32) + b_ref[...]
        for k in range(4):
            acc = acc + w_ref[k] * _shift(ext, k - 2, tm)
        o_ref[...] = acc

    return pl.pallas_call(
        body, name=name, grid=(dm.nt,),
        in_specs=_halo_specs(dm, RW, 0) + [pl.BlockSpec((4, 1, RW), lambda i: (0, 0, 0)),
                                           pl.BlockSpec((1, RW), lambda i: (0, 0))],
        out_specs=pl.BlockSpec((tm, RW), lambda i: (i, 0)),
        out_shape=jax.ShapeDtypeStruct((T, RW), F32), compiler_params=_cparams(("parallel",)),
    )(z, z, z, w, b)


def _blockdiag(x, w_ref):
    nb = w_ref.shape[0]
    outs = [jnp.dot(x[:, n * RNN_BLOCK:(n + 1) * RNN_BLOCK], w_ref[n].astype(BF16), preferred_element_type=F32)
            for n in range(nb)]
    return jnp.concatenate(outs, axis=-1)


def _lru_gates(xa, wa_ref, ba_ref, wx_ref, bx_ref, lam_ref):
    xb = xa.astype(BF16)
    r = jax.nn.sigmoid(_blockdiag(xb, wa_ref) + ba_ref[...])
    ig = jax.nn.sigmoid(_blockdiag(xb, wx_ref) + bx_ref[...])
    nl = -lam_ref[...]
    sp = jnp.maximum(nl, 0.0) + jnp.log(1.0 + jnp.exp(-jnp.abs(nl)))
    log_a = -LRU_C * r * sp
    a = jnp.exp(log_a)
    m = jnp.sqrt(-_expm1(2.0 * log_a))
    return r, ig, sp, a, m


def _lru_specs(l, d, nb, RW):
    wspec = pl.BlockSpec((None, None, nb, RNN_BLOCK, RNN_BLOCK), lambda i: (l, d, 0, 0, 0))
    vspec = pl.BlockSpec((None, None, 1, RW), lambda i: (l, d, 0, 0))
    return [wspec, vspec, wspec, vspec, vspec]


def lru_gates(name, xa, lw, l, d, dm):
    T, RW, tm = dm.T, dm.RW, dm.tme

    def body(xa_ref, wa_ref, ba_ref, wx_ref, bx_ref, lam_ref, a_ref, u_ref):
        xa_v = xa_ref[...]
        r, ig, sp, a, m = _lru_gates(xa_v, wa_ref, ba_ref, wx_ref, bx_ref, lam_ref)
        a_ref[...] = a
        u_ref[...] = m * (ig * xa_v)

    row = pl.BlockSpec((tm, RW), lambda i: (i, 0))
    return pl.pallas_call(
        body, name=name, grid=(dm.nt,),
        in_specs=[row] + _lru_specs(l, d, dm.NB, RW), out_specs=[row, row],
        out_shape=[jax.ShapeDtypeStruct((T, RW), F32)] * 2, compiler_params=_cparams(("parallel",)),
    )(xa, lw["w_a"], lw["b_a"], lw["w_x"], lw["b_x"], lw["lam"])


def lru_gates_bwd(name, xa, lw, l, d, du, dloga, dm):
    T, RW, tm, NB = dm.T, dm.RW, dm.tme, dm.NB

    def body(xa_ref, wa_ref, ba_ref, wx_ref, bx_ref, lam_ref, du_ref, dla_ref,
             dxa_ref, dwa_ref, dba_ref, dwx_ref, dbx_ref, dlam_ref):
        @pl.when(pl.program_id(0) == 0)
        def _():
            for ref in (dwa_ref, dba_ref, dwx_ref, dbx_ref, dlam_ref):
                ref[...] = jnp.zeros_like(ref)

        xa_v = xa_ref[...]
        r, ig, sp, a, m = _lru_gates(xa_v, wa_ref, ba_ref, wx_ref, bx_ref, lam_ref)
        duu = du_ref[...]
        dm_ = duu * (ig * xa_v)
        dig = duu * m * xa_v
        dxa = duu * m * ig
        dla = dla_ref[...] - dm_ * (a * a) / m
        dr = dla * (-LRU_C * sp)
        dsp = jnp.sum(dla * (-LRU_C * r), axis=0, keepdims=True)
        dlam_ref[...] += dsp * (-jax.nn.sigmoid(-lam_ref[...]))
        dpa = dr * r * (1.0 - r)
        dpx = dig * ig * (1.0 - ig)
        dba_ref[...] += jnp.sum(dpa, axis=0, keepdims=True)
        dbx_ref[...] += jnp.sum(dpx, axis=0, keepdims=True)
        xb, dpab, dpxb = xa_v.astype(BF16), dpa.astype(BF16), dpx.astype(BF16)
        back = []
        for n in range(NB):
            sl = slice(n * RNN_BLOCK, (n + 1) * RNN_BLOCK)
            dwa_ref[n] += lax.dot_general(xb[:, sl], dpab[:, sl], _DIMS["tn"], preferred_element_type=F32)
            dwx_ref[n] += lax.dot_general(xb[:, sl], dpxb[:, sl], _DIMS["tn"], preferred_element_type=F32)
            back.append(lax.dot_general(dpab[:, sl], wa_ref[n].astype(BF16), _DIMS["nt"], preferred_element_type=F32)
                        + lax.dot_general(dpxb[:, sl], wx_ref[n].astype(BF16), _DIMS["nt"], preferred_element_type=F32))
        dxa_ref[...] = dxa + jnp.concatenate(back, axis=-1)

    row = pl.BlockSpec((tm, RW), lambda i: (i, 0))
    wacc = pl.BlockSpec((NB, RNN_BLOCK, RNN_BLOCK), lambda i: (0, 0, 0))
    vacc = pl.BlockSpec((1, RW), lambda i: (0, 0))
    wshape = jax.ShapeDtypeStruct((NB, RNN_BLOCK, RNN_BLOCK), F32)
    vshape = jax.ShapeDtypeStruct((1, RW), F32)
    return pl.pallas_call(
        body, name=name, grid=(dm.nt,),
        in_specs=[row] + _lru_specs(l, d, NB, RW) + [row, row],
        out_specs=[row, wacc, vacc, wacc, vacc, vacc],
        out_shape=[jax.ShapeDtypeStruct((T, RW), F32), wshape, vshape, wshape, vshape, vshape],
        compiler_params=_cparams(("arbitrary",)),
    )(xa, lw["w_a"], lw["b_a"], lw["w_x"], lw["b_x"], lw["lam"], du, dloga)


def _chunk_order(dm, ctx_first, descending):
    nch, nctx = dm.nt, dm.nctx
    nlat = nch - nctx

    def order(s):
        if ctx_first and not descending:
            return s
        if not ctx_first and descending:
            return nch - 1 - s
        if ctx_first:
            return jnp.where(s < nctx, nctx - 1 - s, nch - 1 - (s - nctx))
        return jnp.where(s < nlat, nctx + s, s - nlat)

    return order


def _tile_scan(a, b, carry, descending):
    row = lax.broadcasted_iota(jnp.int32, a.shape, 0)
    for s in (1, 2, 4):
        sh = (SCAN_ROWS - s) if descending else s
        keep = (row < SCAN_ROWS - s) if descending else (row >= s)
        ap = pltpu.roll(a, sh, 0)
        bp = pltpu.roll(b, sh, 0)
        b = jnp.where(keep, b + a * bp, b)
        a = jnp.where(keep, a * ap, a)
    h = b + a * carry
    edge = 0 if descending else SCAN_ROWS - 1
    new_carry = jnp.sum(jnp.where(row == edge, h, 0.0), axis=0, keepdims=True)
    return h, new_carry


def lru_scan(name, a, u, ctx_first, descending, dm):
    T, RW, ch = dm.T, dm.RW, dm.tme
    order = _chunk_order(dm, ctx_first, descending)
    ngrp = ch // SCAN_ROWS

    def body(a_ref, u_ref, h_ref, carry_ref):
        @pl.when(pl.program_id(0) == 0)
        def _():
            carry_ref[...] = jnp.zeros_like(carry_ref)

        def step(g, carry):
            g = (ngrp - 1 - g) if descending else g
            rows = pl.ds(pl.multiple_of(g * SCAN_ROWS, SCAN_ROWS), SCAN_ROWS)
            h, carry = _tile_scan(a_ref[rows, :], u_ref[rows, :], carry, descending)
            h_ref[rows, :] = h
            return carry

        carry_ref[...] = lax.fori_loop(0, ngrp, step, carry_ref[...])

    row = pl.BlockSpec((ch, RW), lambda s: (order(s), 0))
    return pl.pallas_call(
        body, name=name, grid=(dm.nt,),
        in_specs=[row, row], out_specs=row,
        out_shape=jax.ShapeDtypeStruct((T, RW), F32),
        scratch_shapes=[pltpu.VMEM((1, RW), F32)], compiler_params=_cparams(("arbitrary",)),
    )(a, u)


def lru_scan_bwd(name, a, u, h, dh, ctx_first, descending, dm):
    T, RW, ch = dm.T, dm.RW, dm.tme
    order = _chunk_order(dm, ctx_first, descending)
    ngrp = ch // SCAN_ROWS

    def body(a_ref, u_ref, h_ref, dh_ref, lam_ref, dla_ref, carry_ref):
        @pl.when(pl.program_id(0) == 0)
        def _():
            carry_ref[...] = jnp.zeros_like(carry_ref)

        def step(g, carry):
            g = (ngrp - 1 - g) if descending else g
            rows = pl.ds(pl.multiple_of(g * SCAN_ROWS, SCAN_ROWS), SCAN_ROWS)
            a_v, dh_v = a_ref[rows, :], dh_ref[rows, :]
            mu, new_carry = _tile_scan(a_v, a_v * dh_v, carry, descending)
            row = lax.broadcasted_iota(jnp.int32, mu.shape, 0)
            if descending:
                nxt = jnp.where(row == SCAN_ROWS - 1, carry, pltpu.roll(mu, SCAN_ROWS - 1, 0))
            else:
                nxt = jnp.where(row == 0, carry, pltpu.roll(mu, 1, 0))
            lam = dh_v + nxt
            lam_ref[rows, :] = lam
            dla_ref[rows, :] = lam * (h_ref[rows, :] - u_ref[rows, :])
            return new_carry

        carry_ref[...] = lax.fori_loop(0, ngrp, step, carry_ref[...])

    row = pl.BlockSpec((ch, RW), lambda s: (order(s), 0))
    return pl.pallas_call(
        body, name=name, grid=(dm.nt,),
        in_specs=[row] * 4, out_specs=[row, row],
        out_shape=[jax.ShapeDtypeStruct((T, RW), F32)] * 2,
        scratch_shapes=[pltpu.VMEM((1, RW), F32)], compiler_params=_cparams(("arbitrary",)),
    )(a, u, h, dh)


def rnn_out(name, hf, hb, z, dm):
    T, RW, tm = dm.T, dm.RW, dm.tme

    def body(hf_ref, hb_ref, rg_ref, o_ref):
        o_ref[...] = ((hf_ref[...] + hb_ref[...]) * _gelu(rg_ref[...].astype(F32))).astype(o_ref.dtype)

    row = pl.BlockSpec((tm, RW), lambda i: (i, 0))
    return pl.pallas_call(
        body, name=name, grid=(dm.nt,),
        in_specs=[row, row, pl.BlockSpec((tm, RW), lambda i: (i, 1))], out_specs=row,
        out_shape=jax.ShapeDtypeStruct((T, RW), BF16), compiler_params=_cparams(("parallel",)),
    )(hf, hb, z)


def rnn_out_bwd(name, dya, hf, hb, z, dm):
    T, RW, tm = dm.T, dm.RW, dm.tme

    def body(d_ref, hf_ref, hb_ref, rg_ref, dh_ref, drg_ref):
        d, rg = d_ref[...], rg_ref[...].astype(F32)
        dh_ref[...] = d * _gelu(rg)
        drg_ref[...] = d * (hf_ref[...] + hb_ref[...]) * _dgelu(rg)

    row = pl.BlockSpec((tm, RW), lambda i: (i, 0))
    return pl.pallas_call(
        body, name=name, grid=(dm.nt,),
        in_specs=[row, row, row, pl.BlockSpec((tm, RW), lambda i: (i, 1))], out_specs=[row, row],
        out_shape=[jax.ShapeDtypeStruct((T, RW), F32)] * 2, compiler_params=_cparams(("parallel",)),
    )(dya, hf, hb, z)


def rnn_conv_bwd(name, dxa_f, dxa_b, drg, z, w, dz, dm):
    T, RW, tm = dm.T, dm.RW, dm.tme

    def body(f0, f1, f2, b0, b1, b2, x0, x1, x2, drg_ref, w_ref, dz_in, dz_ref, dw_ref, db_ref):
        i = pl.program_id(0)

        @pl.when(i == 0)
        def _():
            dw_ref[...] = jnp.zeros_like(dw_ref)
            db_ref[...] = jnp.zeros_like(db_ref)

        first, last = _segment_edges(dm, i)
        dext = _load_ext((f0, f1, f2), first, last) + _load_ext((b0, b1, b2), first, last)
        xext = _load_ext((x0, x1, x2), first, last)
        dmain = dext[HALO:HALO + tm]
        drx = jnp.zeros((tm, RW), F32)
        for k in range(4):
            drx = drx + w_ref[k] * _shift(dext, -(k - 2), tm)
            dw_ref[k] += jnp.sum(dmain * _shift(xext, k - 2, tm), axis=0, keepdims=True)
        db_ref[...] += jnp.sum(dmain, axis=0, keepdims=True)
        dz_ref[:, :RW] = drx.astype(dz_ref.dtype)
        dz_ref[:, RW:] = drg_ref[...].astype(dz_ref.dtype)

    return pl.pallas_call(
        body, name=name, grid=(dm.nt,),
        in_specs=_halo_specs(dm, RW, 0) * 3 + [pl.BlockSpec((tm, RW), lambda i: (i, 0)),
                                               pl.BlockSpec((4, 1, RW), lambda i: (0, 0, 0)),
                                               pl.BlockSpec(memory_space=pl.ANY)],
        out_specs=[pl.BlockSpec((tm, 2 * RW), lambda i: (i, 0)), pl.BlockSpec((4, 1, RW), lambda i: (0, 0, 0)),
                   pl.BlockSpec((1, RW), lambda i: (0, 0))],
        out_shape=[jax.ShapeDtypeStruct(dz.shape, dz.dtype), jax.ShapeDtypeStruct((4, 1, RW), F32),
                   jax.ShapeDtypeStruct((1, RW), F32)],
        input_output_aliases={11: 0}, compiler_params=_cparams(("arbitrary",)),
    )(dxa_f, dxa_f, dxa_f, dxa_b, dxa_b, dxa_b, z, z, z, drg, w, dz)


def short_conv(name, z, w, dm):
    T, RW, tm = dm.T, dm.RW, dm.tme

    def body(sb_ref, g0, g1, g2, x0, x1, x2, w_ref, o_ref):
        first, last = _segment_edges(dm, pl.program_id(0))
        pext = _load_ext((g0, g1, g2), first, last) * _load_ext((x0, x1, x2), first, last)
        cp = jnp.zeros((tm, RW), F32)
        for k in range(3):
            cp = cp + w_ref[k] * _shift(pext, k - 1, tm)
        o_ref[...] = (sb_ref[...].astype(F32) * cp).astype(o_ref.dtype)

    return pl.pallas_call(
        body, name=name, grid=(dm.nt,),
        in_specs=[pl.BlockSpec((tm, RW), lambda i: (i, 2))] + _halo_specs(dm, RW, 3) + _halo_specs(dm, RW, 4)
        + [pl.BlockSpec((3, 1, RW), lambda i: (0, 0, 0))],
        out_specs=pl.BlockSpec((tm, RW), lambda i: (i, 0)),
        out_shape=jax.ShapeDtypeStruct((T, RW), BF16), compiler_params=_cparams(("parallel",)),
    )(z, z, z, z, z, z, z, w)


def short_conv_bwd(name, dyb, z, w, dz, dm):
    T, RW, tm = dm.T, dm.RW, dm.tme

    def spec3(col):
        per = tm // HALO
        last = T // HALO - 1
        return [pl.BlockSpec((tm, RW), lambda i, p: (i, col)),
                pl.BlockSpec((HALO, RW), lambda i, p: (jnp.maximum(i * per - 1, 0), col)),
                pl.BlockSpec((HALO, RW), lambda i, p: (jnp.minimum((i + 1) * per, last), col))]

    def body(d0, d1, d2, s0, s1, s2, g0, g1, g2, x0, x1, x2, w_ref, dz_in, dz_ref, dw_ref, parts_ref):
        i, p = pl.program_id(0), pl.program_id(1)

        @pl.when(jnp.logical_and(i == 0, p == 0))
        def _():
            dw_ref[...] = jnp.zeros_like(dw_ref)

        @pl.when(p == 0)
        def _():
            first, last = _segment_edges(dm, i)
            gext = _load_ext((g0, g1, g2), first, last)
            xext = _load_ext((x0, x1, x2), first, last)
            pext = gext * xext
            dyext = _load_ext((d0, d1, d2), first, last)
            dcext = dyext * _load_ext((s0, s1, s2), first, last)
            dcmain = dcext[HALO:HALO + tm]
            cp = jnp.zeros((tm, RW), F32)
            dp = jnp.zeros((tm, RW), F32)
            for k in range(3):
                pk = _shift(pext, k - 1, tm)
                cp = cp + w_ref[k] * pk
                dp = dp + w_ref[k] * _shift(dcext, -(k - 1), tm)
                dw_ref[k] += jnp.sum(dcmain * pk, axis=0, keepdims=True)
            parts_ref[0] = (dyext[HALO:HALO + tm] * cp).astype(parts_ref.dtype)
            parts_ref[1] = (dp * xext[HALO:HALO + tm]).astype(parts_ref.dtype)
            parts_ref[2] = (dp * gext[HALO:HALO + tm]).astype(parts_ref.dtype)

        dz_ref[...] = parts_ref[p]

    return pl.pallas_call(
        body, name=name, grid=(dm.nt, 3),
        in_specs=spec3(0) + spec3(2) + spec3(3) + spec3(4)
        + [pl.BlockSpec((3, 1, RW), lambda i, p: (0, 0, 0)), pl.BlockSpec(memory_space=pl.ANY)],
        out_specs=[pl.BlockSpec((tm, RW), lambda i, p: (i, 2 + p)), pl.BlockSpec((3, 1, RW), lambda i, p: (0, 0, 0))],
        out_shape=[jax.ShapeDtypeStruct(dz.shape, dz.dtype), jax.ShapeDtypeStruct((3, 1, RW), F32)],
        scratch_shapes=[pltpu.VMEM((3, tm, RW), dz.dtype)],
        input_output_aliases={13: 0}, compiler_params=_cparams(("arbitrary", "arbitrary")),
    )(dyb, dyb, dyb, z, z, z, z, z, z, z, z, z, w, dz)


def _rope_tables(dm):
    L, C = dm.L, dm.C
    half = HEAD_DIM // 2
    pos = jnp.arange(L)
    row = (pos // GRID_W).astype(F32)
    col = (pos % GRID_W).astype(F32)
    inv = ROPE_BASE ** (-jnp.arange(0, half, 2, dtype=F32) / half)
    ar, ac = row[:, None] * inv, col[:, None] * inv
    cos = jnp.concatenate([jnp.cos(ar), jnp.cos(ar), jnp.cos(ac), jnp.cos(ac)], axis=-1)
    sin = jnp.concatenate([-jnp.sin(ar), jnp.sin(ar), -jnp.sin(ac), jnp.sin(ac)], axis=-1)
    cos = jnp.concatenate([jnp.ones((C, HEAD_DIM), F32), cos], axis=0)
    sin = jnp.concatenate([jnp.zeros((C, HEAD_DIM), F32), sin], axis=0)
    return cos, sin


def _swap_pairs(x):
    quarter = HEAD_DIM // 4
    lane = lax.broadcasted_iota(jnp.int32, x.shape, 1)
    return jnp.where(lane % (2 * quarter) < quarter, pltpu.roll(x, HEAD_DIM - quarter, 1), pltpu.roll(x, quarter, 1))


def _rope(x, cos, sin):
    return x * cos + _swap_pairs(x) * sin


def _unrope(d, cos, sin):
    return d * cos + _swap_pairs(d * sin)


def qkv_prep(name, z, cos, sin, dm):
    T, tm, HQ, KW = dm.T, dm.tme, dm.HQ, dm.KW
    qcol, kcol = dm.off_q // HQ, dm.off_k // KW

    def body(q_ref, k_ref, v_ref, c_ref, s_ref, qo, ko, vo):
        cos_v, sin_v = c_ref[...], s_ref[...]
        for hd in range(HQ // HEAD_DIM):
            sl = slice(hd * HEAD_DIM, (hd + 1) * HEAD_DIM)
            qo[:, sl] = _rope(q_ref[:, sl].astype(F32), cos_v, sin_v).astype(qo.dtype)
        for hd in range(KW // HEAD_DIM):
            sl = slice(hd * HEAD_DIM, (hd + 1) * HEAD_DIM)
            ko[:, sl] = _rope(k_ref[:, sl].astype(F32), cos_v, sin_v).astype(ko.dtype)
        vo[...] = v_ref[...].astype(vo.dtype)

    tab = pl.BlockSpec((tm, HEAD_DIM), lambda i: (i, 0))
    return pl.pallas_call(
        body, name=name, grid=(dm.nt,),
        in_specs=[pl.BlockSpec((tm, HQ), lambda i: (i, qcol)), pl.BlockSpec((tm, KW), lambda i: (i, kcol)),
                  pl.BlockSpec((tm, KW), lambda i: (i, kcol + 1)), tab, tab],
        out_specs=[pl.BlockSpec((tm, HQ), lambda i: (i, 0)), pl.BlockSpec((tm, KW), lambda i: (i, 0)),
                   pl.BlockSpec((tm, KW), lambda i: (i, 0))],
        out_shape=[jax.ShapeDtypeStruct((T, HQ), BF16), jax.ShapeDtypeStruct((T, KW), BF16),
                   jax.ShapeDtypeStruct((T, KW), BF16)],
        compiler_params=_cparams(("parallel",)),
    )(z, z, z, cos, sin)


def qkv_bwd(name, dq, dk, dv, cos, sin, dz, dm):
    T, tm, HQ, KW = dm.T, _tiles(dm.T), dm.HQ, dm.KW
    nq = HQ // KW
    base = dm.off_q // KW

    def body(dq_ref, dk_ref, dv_ref, c_ref, s_ref, dz_in, dz_ref):
        p = pl.program_id(1)
        src = jnp.where(p < nq, dq_ref[...], jnp.where(p == nq, dk_ref[...], dv_ref[...]))
        cos_v, sin_v = c_ref[...], s_ref[...]
        is_v = p == nq + 1
        for hd in range(KW // HEAD_DIM):
            sl = slice(hd * HEAD_DIM, (hd + 1) * HEAD_DIM)
            dz_ref[:, sl] = jnp.where(is_v, src[:, sl], _unrope(src[:, sl], cos_v, sin_v)).astype(dz_ref.dtype)

    tab = pl.BlockSpec((tm, HEAD_DIM), lambda i, p: (i, 0))
    blk = pl.BlockSpec((tm, KW), lambda i, p: (i, 0))
    return pl.pallas_call(
        body, name=name, grid=(T // tm, nq + 2),
        in_specs=[pl.BlockSpec((tm, KW), lambda i, p: (i, jnp.minimum(p, nq - 1))), blk, blk, tab, tab,
                  pl.BlockSpec(memory_space=pl.ANY)],
        out_specs=pl.BlockSpec((tm, KW), lambda i, p: (i, base + p)),
        out_shape=jax.ShapeDtypeStruct(dz.shape, dz.dtype),
        input_output_aliases={5: 0}, compiler_params=_cparams(("parallel", "arbitrary")),
    )(dq, dk, dv, cos, sin, dz)


def _attn_specs(dm):
    nC, nB, C, KW = dm.C // Q_BLOCK, dm.T // Q_BLOCK, dm.C, dm.KW

    def near(o):
        return lambda b: (jnp.clip(b + o, nC, nB - 1), 0)

    kv = [pl.BlockSpec((Q_BLOCK, KW), near(o)) for o in (-1, 0, 1)] + [pl.BlockSpec((C, KW), lambda b: (0, 0))]
    return kv


def _attn_mask(dm, b):
    nC, C, L = dm.C // Q_BLOCK, dm.C, dm.L
    span = 3 * Q_BLOCK
    n = b - nC
    iq = lax.broadcasted_iota(jnp.int32, (Q_BLOCK, span + C), 0)
    ik = lax.broadcasted_iota(jnp.int32, (Q_BLOCK, span + C), 1)
    kpos = n * Q_BLOCK + ik - Q_BLOCK
    qpos = n * Q_BLOCK + iq
    local = (b >= nC) & (jnp.abs(qpos - kpos) <= WINDOW) & (kpos >= 0) & (kpos < L)
    return jnp.logical_or(ik >= span, local)


def attention(name, q, k, v, sink, dm):
    T, HQ, KW = dm.T, dm.HQ, dm.KW
    H, KV = HQ // HEAD_DIM, KW // HEAD_DIM
    G = H // KV
    scale = HEAD_DIM ** -0.5

    def body(q_ref, kp, kc, kn, kx, vp, vc, vn, vx, sink_ref, o_ref, lse_ref):
        valid = _attn_mask(dm, pl.program_id(0))
        lane = lax.broadcasted_iota(jnp.int32, (Q_BLOCK, LSE_W), 1)
        lse_all = jnp.zeros((Q_BLOCK, LSE_W), F32)
        for kh in range(KV):
            ks = slice(kh * HEAD_DIM, (kh + 1) * HEAD_DIM)
            k_all = jnp.concatenate([kp[:, ks], kc[:, ks], kn[:, ks], kx[:, ks]], axis=0)
            v_all = jnp.concatenate([vp[:, ks], vc[:, ks], vn[:, ks], vx[:, ks]], axis=0)
            for g in range(G):
                hd = kh * G + g
                hs = slice(hd * HEAD_DIM, (hd + 1) * HEAD_DIM)
                s = lax.dot_general(q_ref[:, hs], k_all, _DIMS["nt"], preferred_element_type=F32) * scale
                s = jnp.where(valid, s, NEG_INF)
                snk = sink_ref[0, hd]
                mx = jnp.maximum(jnp.max(s, axis=-1, keepdims=True), snk)
                p = jnp.exp(s - mx)
                den = jnp.sum(p, axis=-1, keepdims=True) + jnp.exp(snk - mx)
                o = jnp.dot(p.astype(BF16), v_all, preferred_element_type=F32) / den
                o_ref[:, hs] = o.astype(o_ref.dtype)
                lse_all = jnp.where(lane == hd, mx + jnp.log(den), lse_all)
        lse_ref[...] = lse_all

    kv = _attn_specs(dm)
    return pl.pallas_call(
        body, name=name, grid=(T // Q_BLOCK,),
        in_specs=[pl.BlockSpec((Q_BLOCK, HQ), lambda b: (b, 0))] + kv + kv + [pl.BlockSpec(memory_space=pltpu.SMEM)],
        out_specs=[pl.BlockSpec((Q_BLOCK, HQ), lambda b: (b, 0)), pl.BlockSpec((Q_BLOCK, LSE_W), lambda b: (b, 0))],
        out_shape=[jax.ShapeDtypeStruct((T, HQ), BF16), jax.ShapeDtypeStruct((T, LSE_W), F32)],
        compiler_params=_cparams(("parallel",)),
    )(q, k, k, k, k, v, v, v, v, sink)


def attention_bwd(name, q, k, v, sink, o, lse, do, dm):
    T, HQ, KW, C = dm.T, dm.HQ, dm.KW, dm.C
    H, KV = HQ // HEAD_DIM, KW // HEAD_DIM
    G = H // KV
    nC, nB = C // Q_BLOCK, T // Q_BLOCK
    scale = HEAD_DIM ** -0.5
    span = 3 * Q_BLOCK

    def body(q_ref, kp, kc, kn, kx, vp, vc, vn, vx, sink_ref, o_ref, lse_ref, do_ref,
             dq_ref, dk_ref, dv_ref, ds_ref):
        b = pl.program_id(0)

        @pl.when(b == 0)
        def _():
            dk_ref[...] = jnp.zeros_like(dk_ref)
            dv_ref[...] = jnp.zeros_like(dv_ref)
            ds_ref[...] = jnp.zeros_like(ds_ref)

        valid = _attn_mask(dm, b)
        starts = [pl.multiple_of(jnp.clip(b + off, nC, nB - 1) * Q_BLOCK, Q_BLOCK) for off in (-1, 0, 1)]
        lane = lax.broadcasted_iota(jnp.int32, (Q_BLOCK, LSE_W), 1)
        lse_all = lse_ref[...]
        dsink = jnp.zeros((1, LSE_W), F32)
        for kh in range(KV):
            ks = slice(kh * HEAD_DIM, (kh + 1) * HEAD_DIM)
            k_all = jnp.concatenate([kp[:, ks], kc[:, ks], kn[:, ks], kx[:, ks]], axis=0)
            v_all = jnp.concatenate([vp[:, ks], vc[:, ks], vn[:, ks], vx[:, ks]], axis=0)
            dk_all = jnp.zeros((span + C, HEAD_DIM), F32)
            dv_all = jnp.zeros((span + C, HEAD_DIM), F32)
            for g in range(G):
                hd = kh * G + g
                hs = slice(hd * HEAD_DIM, (hd + 1) * HEAD_DIM)
                qh = q_ref[:, hs]
                doh = do_ref[:, hs]
                s = lax.dot_general(qh, k_all, _DIMS["nt"], preferred_element_type=F32) * scale
                s = jnp.where(valid, s, NEG_INF)
                lse_h = jnp.sum(jnp.where(lane == hd, lse_all, 0.0), axis=-1, keepdims=True)
                p = jnp.exp(s - lse_h)
                delta = jnp.sum(doh * o_ref[:, hs].astype(F32), axis=-1, keepdims=True)
                dob = doh.astype(BF16)
                dp = lax.dot_general(dob, v_all, _DIMS["nt"], preferred_element_type=F32)
                dsc = (p * (dp - delta) * scale).astype(BF16)
                dq_ref[:, hs] = jnp.dot(dsc, k_all, preferred_element_type=F32)
                dk_all = dk_all + lax.dot_general(dsc, qh, _DIMS["tn"], preferred_element_type=F32)
                dv_all = dv_all + lax.dot_general(p.astype(BF16), dob, _DIMS["tn"], preferred_element_type=F32)
                p_sink = jnp.exp(sink_ref[0, hd] - lse_h)
                dsink = dsink + jnp.where(lane[0:1] == hd, -jnp.sum(p_sink * delta), 0.0)
            for j, st in enumerate(starts):
                rows = pl.ds(st, Q_BLOCK)
                dk_ref[rows, ks] += dk_all[j * Q_BLOCK:(j + 1) * Q_BLOCK]
                dv_ref[rows, ks] += dv_all[j * Q_BLOCK:(j + 1) * Q_BLOCK]
            dk_ref[0:C, ks] += dk_all[span:]
            dv_ref[0:C, ks] += dv_all[span:]
        ds_ref[...] += dsink

    kv = _attn_specs(dm)
    qspec = pl.BlockSpec((Q_BLOCK, HQ), lambda b: (b, 0))
    full = pl.BlockSpec((T, KW), lambda b: (0, 0))
    return pl.pallas_call(
        body, name=name, grid=(nB,),
        in_specs=[qspec] + kv + kv + [pl.BlockSpec(memory_space=pltpu.SMEM), qspec,
                                      pl.BlockSpec((Q_BLOCK, LSE_W), lambda b: (b, 0)), qspec],
        out_specs=[qspec, full, full, pl.BlockSpec((1, LSE_W), lambda b: (0, 0))],
        out_shape=[jax.ShapeDtypeStruct((T, HQ), F32), jax.ShapeDtypeStruct((T, KW), F32),
                   jax.ShapeDtypeStruct((T, KW), F32), jax.ShapeDtypeStruct((1, LSE_W), F32)],
        compiler_params=_cparams(("arbitrary",)),
    )(q, k, k, k, k, v, v, v, v, sink, o, lse, do)


def merge(name, z, lifted, b_merge, dm):
    T, D, tm, cw = dm.T, dm.D, _tiles(dm.T), dm.cw
    gcol = dm.off_g // cw
    per = D // cw

    def body(g0, g1, g2, l0, l1, l2, b_ref, o_ref):
        acc = jnp.zeros((tm, cw), F32)
        for i, (g, lf) in enumerate(((g0, l0), (g1, l1), (g2, l2))):
            acc = acc + jax.nn.sigmoid(g[...] + b_ref[i]) * lf[...]
        o_ref[...] = acc.astype(o_ref.dtype)

    gspecs = [pl.BlockSpec((tm, cw), lambda i, j, br=br: (i, gcol + br * per + j)) for br in range(N_BRANCH)]
    blk = pl.BlockSpec((tm, cw), lambda i, j: (i, j))
    return pl.pallas_call(
        body, name=name, grid=(T // tm, per),
        in_specs=gspecs + [blk] * 3 + [pl.BlockSpec((N_BRANCH, 1, cw), lambda i, j: (0, 0, j))], out_specs=blk,
        out_shape=jax.ShapeDtypeStruct((T, D), BF16), compiler_params=_cparams(("parallel", "parallel")),
    )(z, z, z, *lifted, b_merge)


def merge_bwd(name, dmerged, z, lifted, b_merge, dz, dm):
    T, D, tm, cw = dm.T, dm.D, _tiles(dm.T), dm.cw
    gcol = dm.off_g // cw
    per = D // cw

    def body(d_ref, g_ref, l0, l1, l2, b_ref, dz_in, o0, o1, o2, dz_ref, db_ref):
        br = pl.program_id(2)

        @pl.when(jnp.logical_and(pl.program_id(1) == 0, br == 0))
        def _():
            db_ref[...] = jnp.zeros_like(db_ref)

        d = d_ref[...]
        gate = jax.nn.sigmoid(g_ref[...] + b_ref[br])
        dlift = (d * gate).astype(o0.dtype)
        for b, (l_ref, o_ref) in enumerate(((l0, o0), (l1, o1), (l2, o2))):
            @pl.when(br == b)
            def _(l_ref=l_ref, o_ref=o_ref):
                o_ref[...] = dlift
                dg = d * l_ref[...] * gate * (1.0 - gate)
                dz_ref[...] = dg.astype(dz_ref.dtype)
                db_ref[b] += jnp.sum(dg, axis=0, keepdims=True)

    blk = pl.BlockSpec((tm, cw), lambda j, i, b: (i, j))
    zblk = pl.BlockSpec((tm, cw), lambda j, i, b: (i, gcol + b * per + j))
    return pl.pallas_call(
        body, name=name, grid=(per, T // tm, N_BRANCH),
        in_specs=[blk, zblk, blk, blk, blk, pl.BlockSpec((N_BRANCH, 1, cw), lambda j, i, b: (0, 0, j)),
                  pl.BlockSpec(memory_space=pl.ANY)],
        out_specs=[blk, blk, blk, zblk, pl.BlockSpec((N_BRANCH, 1, cw), lambda j, i, b: (0, 0, j))],
        out_shape=[jax.ShapeDtypeStruct((T, D), BF16)] * 3 + [jax.ShapeDtypeStruct(dz.shape, dz.dtype),
                                                             jax.ShapeDtypeStruct((N_BRANCH, 1, D), F32)],
        input_output_aliases={6: 3}, compiler_params=_cparams(("parallel", "arbitrary", "arbitrary")),
    )(dmerged, z, *lifted, b_merge, dz)


def loss_head(name, h, gf, target, dm):
    T, D, tm, nctx = dm.T, dm.D, dm.tme, dm.nctx

    def body(h_ref, g_ref, t_ref, dh_ref, loss_ref, dg_ref):
        i = pl.program_id(0)

        @pl.when(i == 0)
        def _():
            loss_ref[...] = jnp.zeros_like(loss_ref)
            dg_ref[...] = jnp.zeros_like(dg_ref)

        @pl.when(i < nctx)
        def _():
            dh_ref[...] = jnp.zeros_like(dh_ref)

        @pl.when(i >= nctx)
        def _():
            x = h_ref[...]
            r = lax.rsqrt(jnp.mean(x * x, axis=-1, keepdims=True) + EPS)
            n = x * r
            g = g_ref[...]
            err = n * g - t_ref[...]
            loss_ref[...] += jnp.sum(err * err) * (0.5 / D)
            dy = err * (1.0 / D)
            dg_ref[...] += jnp.sum(dy * n, axis=0, keepdims=True)
            dn = dy * g
            dh_ref[...] = r * (dn - n * jnp.mean(dn * n, axis=-1, keepdims=True))

    row = pl.BlockSpec((tm, D), lambda i: (i, 0))
    return pl.pallas_call(
        body, name=name, grid=(T // tm,),
        in_specs=[row, pl.BlockSpec((1, D), lambda i: (0, 0)),
                  pl.BlockSpec((tm, D), lambda i: (jnp.maximum(i - nctx, 0), 0))],
        out_specs=[row, pl.BlockSpec((1, 128), lambda i: (0, 0)), pl.BlockSpec((1, D), lambda i: (0, 0))],
        out_shape=[jax.ShapeDtypeStruct((T, D), F32), jax.ShapeDtypeStruct((1, 128), F32),
                   jax.ShapeDtypeStruct((1, D), F32)],
        compiler_params=_cparams(("arbitrary",)),
    )(h, gf, target)


_HI = lax.Precision.HIGHEST
ADA_ROWS = 16


def ada_forward(name, cond, ada_w, bias):
    _, D, Aq = ada_w.shape
    tc = _pick(Aq, (1536, 1152, 768, 512, 384, 256, 128))
    tk = _ktile(D)
    nk = D // tk

    def body(c_ref, w_ref, b_ref, o_ref):
        k = pl.program_id(2)

        @pl.when(k == 0)
        def _():
            o_ref[...] = jnp.zeros_like(o_ref) + b_ref[...]

        o_ref[...] += jnp.dot(_silu(c_ref[...]), w_ref[...], precision=_HI, preferred_element_type=F32)

    return pl.pallas_call(
        body, name=name, grid=(2, Aq // tc, nk),
        in_specs=[pl.BlockSpec((ADA_ROWS, tk), lambda l, j, k: (0, k)),
                  pl.BlockSpec((None, tk, tc), lambda l, j, k: (l, k, j)),
                  pl.BlockSpec((None, 1, tc), lambda l, j, k: (l, 0, j))],
        out_specs=pl.BlockSpec((None, ADA_ROWS, tc), lambda l, j, k: (l, 0, j)),
        out_shape=jax.ShapeDtypeStruct((2, ADA_ROWS, Aq), F32),
        compiler_params=_cparams(("parallel", "parallel", "arbitrary")),
    )(cond, ada_w, bias)


def ada_cond_grad(name, dmod, ada_w):
    _, D, Aq = ada_w.shape
    tc = _pick(Aq, (1536, 1152, 768, 512, 384, 256, 128))
    tn = _ktile(D)
    nc = Aq // tc

    def body(d_ref, w_ref, o_ref):
        @pl.when(jnp.logical_and(pl.program_id(1) == 0, pl.program_id(2) == 0))
        def _():
            o_ref[...] = jnp.zeros_like(o_ref)

        o_ref[...] += lax.dot_general(d_ref[...], w_ref[...], _DIMS["nt"], precision=_HI, preferred_element_type=F32)

    return pl.pallas_call(
        body, name=name, grid=(D // tn, 2, nc),
        in_specs=[pl.BlockSpec((None, ADA_ROWS, tc), lambda j, l, c: (l, 0, c)),
                  pl.BlockSpec((None, tn, tc), lambda j, l, c: (l, j, c))],
        out_specs=pl.BlockSpec((ADA_ROWS, tn), lambda j, l, c: (0, j)),
        out_shape=jax.ShapeDtypeStruct((ADA_ROWS, D), F32),
        compiler_params=_cparams(("parallel", "arbitrary", "arbitrary")),
    )(dmod, ada_w)


def ada_update(name, cond, dmod, w, m, v):
    _, D, Aq = w.shape
    tc = _pick(Aq, (1536, 1152, 768, 512, 384, 256, 128))
    tr = 128 if D % 128 == 0 else D
    bc1 = 1.0 - ADAM_B1 ** ADAM_STEP
    bc2 = 1.0 - ADAM_B2 ** ADAM_STEP

    def body(c_ref, d_ref, w_ref, m_ref, v_ref, go_ref, dl_ref, mo_ref, vo_ref):
        g = lax.dot_general(_silu(c_ref[...]), d_ref[...], _DIMS["tn"], precision=_HI, preferred_element_type=F32)
        mn = ADAM_B1 * m_ref[...] + (1.0 - ADAM_B1) * g
        vn = ADAM_B2 * v_ref[...] + (1.0 - ADAM_B2) * (g * g)
        go_ref[...] = g
        dl_ref[...] = -ADAM_LR * ((mn / bc1) / (jnp.sqrt(vn / bc2) + ADAM_EPS) + ADAM_WD * w_ref[...])
        mo_ref[...] = mn
        vo_ref[...] = vn

    blk = pl.BlockSpec((None, tr, tc), lambda l, i, j: (l, i, j))
    return pl.pallas_call(
        body, name=name, grid=(2, D // tr, Aq // tc),
        in_specs=[pl.BlockSpec((ADA_ROWS, tr), lambda l, i, j: (0, i)),
                  pl.BlockSpec((None, ADA_ROWS, tc), lambda l, i, j: (l, 0, j)), blk, blk, blk],
        out_specs=[blk] * 4, out_shape=[jax.ShapeDtypeStruct(w.shape, F32)] * 4,
        compiler_params=_cparams(("parallel", "parallel", "parallel")),
    )(cond, dmod, w, m, v)


def dmod_assemble(name, gathered):
    A = gathered.shape[-1]
    tc = _pick(A, (2048, 1024, 512, 256, 128))

    def body(g_ref, o_ref, b_ref):
        ctx = g_ref[0, 1]
        for dev in range(1, N_DEV):
            ctx = ctx + g_ref[dev, 1]
        tot = ctx
        for dev in range(N_DEV):
            o_ref[dev:dev + 1, :] = g_ref[dev, 0]
            tot = tot + g_ref[dev, 0]
        o_ref[N_DEV:N_DEV + 1, :] = ctx
        o_ref[N_DEV + 1:, :] = jnp.zeros((ADA_ROWS - N_DEV - 1, tc), F32)
        b_ref[...] = tot

    return pl.pallas_call(
        body, name=name, grid=(2, A // tc),
        in_specs=[pl.BlockSpec((N_DEV, None, 2, 1, tc), lambda l, j: (0, l, 0, 0, j))],
        out_specs=[pl.BlockSpec((None, ADA_ROWS, tc), lambda l, j: (l, 0, j)),
                   pl.BlockSpec((None, 1, tc), lambda l, j: (l, 0, j))],
        out_shape=[jax.ShapeDtypeStruct((2, ADA_ROWS, A), F32), jax.ShapeDtypeStruct((2, 1, A), F32)],
        compiler_params=_cparams(("parallel", "parallel")),
    )(gathered)


def sum_devices(name, gathered):
    _, R, W = gathered.shape
    tr = _row_tile(R, W, budget=1 << 18)

    def body(g_ref, all_ref, chip_ref):
        even = g_ref[0]
        odd = g_ref[1]
        for dev in range(2, N_DEV, 2):
            even = even + g_ref[dev]
            odd = odd + g_ref[dev + 1]
        all_ref[...] = even + odd
        chip_ref[...] = even

    blk = pl.BlockSpec((tr, W), lambda i: (i, 0))
    return pl.pallas_call(
        body, name=name, grid=(R // tr,),
        in_specs=[pl.BlockSpec((N_DEV, tr, W), lambda i: (0, i, 0))], out_specs=[blk, blk],
        out_shape=[jax.ShapeDtypeStruct((R, W), F32)] * 2, compiler_params=_cparams(("parallel",)),
    )(gathered)


PACK_ROWS = 1024


def _pack(arrays):
    flat = jnp.concatenate([a.reshape(-1).astype(F32) for a in arrays])
    pad = (-flat.shape[0]) % (PACK_ROWS * 128)
    return jnp.pad(flat, (0, pad)).reshape(-1, 128)


def _unpack(buf, shapes, lead=()):
    flat = buf.reshape(lead + (-1,))
    out, start = [], 0
    for s in shapes:
        n = math.prod(s)
        out.append(flat[..., start:start + n].reshape(lead + tuple(s)))
        start += n
    return out


def _unshard_last(g):
    g = jnp.moveaxis(g, 0, -2)
    return g.reshape(g.shape[:-2] + (g.shape[-2] * g.shape[-1],))


class WeightStream:
    AHEAD = 2

    def __init__(self, keys, make_land, first_deps=()):
        self.keys, self.make_land = list(keys), make_land
        self.pending, self.values, self.tokens, self.started = {}, {}, [], 0
        for _ in range(self.AHEAD):
            self._start_next(first_deps)

    def _start_next(self, deps):
        if self.started < len(self.keys):
            key = self.keys[self.started]
            self.started += 1
            land, view = self.make_land(key, deps)
            ss, rs, (land,), token = exchange_start(f"{key}_start", [land], _gather_plan)
            self.pending[key] = (ss, rs, land, view)
            self.tokens.append(token)

    def get(self, key, after=None):
        if key not in self.values:
            ss, rs, land, view = self.pending.pop(key)
            (full,) = exchange_wait(f"{key}_wait", ss, rs, [land], _gather_plan, after)
            self.values[key] = full if view is None else full.reshape(view)
            self._start_next((full,))
        return self.values[key]

    def take_tokens(self):
        out, self.tokens = tuple(self.tokens), []
        return out


def _ffn_forward(tag, h, gn, modtab, s, ws, k13, k2, dm):
    u = norm_mod(f"{tag}_norm", h, gn, modtab, s, dm)
    w13g = ws.get(k13, u)
    gu = mm_cols(f"{tag}_w13", u, w13g, BF16, flat=False, deps=ws.take_tokens())
    act = swiglu(f"{tag}_act", gu, dm)
    w2g = ws.get(k2, act)
    h_out, f = mm_rows(f"{tag}_w2", act, w2g, deps=ws.take_tokens(), resid=(h, modtab, s, 0.5, dm.C))
    return h_out, (h, u, gu, act, f)


def _ffn_backward(tag, dh, saved, gn, modtab, s, ws, k13, k2, dm, scatter, keys, deps=()):
    h, u, gu, act, f = saved
    w13g, w2g = ws.get(k13), ws.get(k2)
    df, dgate = resid_bwd(f"{tag}_res_bwd", dh, f, modtab, s, 0.5, dm, deps)
    dact = mm_rows_t(f"{tag}_dact", df, w2g)
    dw2 = mm_rows_grad(f"{tag}_dw2", act, df).reshape(N_CHIP, -1, df.shape[-1])
    tok2 = scatter(f"{tag}_scatter_w2", {keys[1]: dw2})
    dgu = swiglu_bwd(f"{tag}_act_bwd", dact, gu, dm)
    du = mm_cols_t(f"{tag}_du", dgu, w13g, flat=False, deps=(tok2,))
    dw13 = mm_cols_grad(f"{tag}_dw13", u, dgu, flat=False)
    dh_in, dss, dgn = norm_mod_bwd(f"{tag}_norm_bwd", du, h, gn, modtab, s, dh, dm)
    tok13 = scatter(f"{tag}_scatter_w13", {keys[0]: dw13})
    return dh_in, (tok13,), dss, dgate, dgn


def kernel(x, c, ctx, c_ctx, ada_w, ada_b, norm_g, ffn1_w13, ffn1_w2, w_in, b_merge, rnn_conv_w, rnn_conv_b, lru_w_a, lru_b_a, lru_w_x, lru_b_x, lru_lambda, sc_conv_w, attn_sink, w_branch, w_out, ffn2_w13, ffn2_w2, final_norm_g, loss_target, m_c_ctx, m_ada_w, m_ada_b, m_norm_g, m_ffn1_w13, m_ffn1_w2, m_w_in, m_b_merge, m_rnn_conv_w, m_rnn_conv_b, m_lru_w_a, m_lru_b_a, m_lru_w_x, m_lru_b_x, m_lru_lambda, m_sc_conv_w, m_attn_sink, m_w_branch, m_w_out, m_ffn2_w13, m_ffn2_w2, m_final_norm_g, v_c_ctx, v_ada_w, v_ada_b, v_norm_g, v_ffn1_w13, v_ffn1_w2, v_w_in, v_b_merge, v_rnn_conv_w, v_rnn_conv_b, v_lru_w_a, v_lru_b_a, v_lru_w_x, v_lru_b_x, v_lru_lambda, v_sc_conv_w, v_attn_sink, v_w_branch, v_w_out, v_ffn2_w13, v_ffn2_w2, v_final_norm_g):
    weights = dict(c_ctx=c_ctx, ada_w=ada_w, ada_b=ada_b, norm_g=norm_g, ffn1_w13=ffn1_w13, ffn1_w2=ffn1_w2, w_in=w_in,
                   b_merge=b_merge, rnn_conv_w=rnn_conv_w, rnn_conv_b=rnn_conv_b, lru_w_a=lru_w_a, lru_b_a=lru_b_a,
                   lru_w_x=lru_w_x, lru_b_x=lru_b_x, lru_lambda=lru_lambda, sc_conv_w=sc_conv_w, attn_sink=attn_sink,
                   w_branch=w_branch, w_out=w_out, ffn2_w13=ffn2_w13, ffn2_w2=ffn2_w2, final_norm_g=final_norm_g)
    mom_m = dict(c_ctx=m_c_ctx, ada_w=m_ada_w, ada_b=m_ada_b, norm_g=m_norm_g, ffn1_w13=m_ffn1_w13, ffn1_w2=m_ffn1_w2,
                 w_in=m_w_in, b_merge=m_b_merge, rnn_conv_w=m_rnn_conv_w, rnn_conv_b=m_rnn_conv_b, lru_w_a=m_lru_w_a,
                 lru_b_a=m_lru_b_a, lru_w_x=m_lru_w_x, lru_b_x=m_lru_b_x, lru_lambda=m_lru_lambda,
                 sc_conv_w=m_sc_conv_w, attn_sink=m_attn_sink, w_branch=m_w_branch, w_out=m_w_out,
                 ffn2_w13=m_ffn2_w13, ffn2_w2=m_ffn2_w2, final_norm_g=m_final_norm_g)
    mom_v = dict(c_ctx=v_c_ctx, ada_w=v_ada_w, ada_b=v_ada_b, norm_g=v_norm_g, ffn1_w13=v_ffn1_w13, ffn1_w2=v_ffn1_w2,
                 w_in=v_w_in, b_merge=v_b_merge, rnn_conv_w=v_rnn_conv_w, rnn_conv_b=v_rnn_conv_b, lru_w_a=v_lru_w_a,
                 lru_b_a=v_lru_b_a, lru_w_x=v_lru_w_x, lru_b_x=v_lru_b_x, lru_lambda=v_lru_lambda,
                 sc_conv_w=v_sc_conv_w, attn_sink=v_attn_sink, w_branch=v_w_branch, w_out=v_w_out,
                 ffn2_w13=v_ffn2_w13, ffn2_w2=v_ffn2_w2, final_norm_g=v_final_norm_g)
    order = list(weights)

    dm = Dims()
    dm.D = D = x.shape[-1]
    dm.L = L = x.shape[1]
    dm.C = C = ctx.shape[1]
    dm.T = T = L + C
    dm.RW = RW = rnn_conv_b.shape[-1]
    dm.NB = lru_w_a.shape[2]
    H = attn_sink.shape[-1]
    dm.HQ = HQ = H * HEAD_DIM
    NZ = w_in.shape[-1] * N_CHIP
    dm.KW = KW = (NZ - 5 * RW - HQ - N_BRANCH * D) // 2
    dm.off_q = 5 * RW
    dm.off_k = dm.off_q + HQ
    dm.off_g = dm.off_k + 2 * KW
    dm.tme = _pick(C, (256, 128))
    dm.nt = T // dm.tme
    dm.nctx = C // dm.tme
    dm.cw = next(w for w in (512, 256, 128) if dm.off_g % w == 0 and D % w == 0)
    A = ada_b.shape[-1]
    Aq = ada_w.shape[-1]
    assert dm.off_q % HQ == 0 and dm.off_k % KW == 0 and HQ % KW == 0 and L % dm.tme == 0 and RW == HQ
    assert C % Q_BLOCK == 0 and L % Q_BLOCK == 0 and D % N_CHIP == 0 and A == N_MOD * D

    mx, my, mc = _my_pos()
    j_me = 2 * mx + my
    b_me = 4 * mx + 2 * my + mc

    big = ["ffn1_w13", "ffn1_w2", "w_in", "w_branch", "w_out", "ffn2_w13", "ffn2_w2"]
    j_idx = jnp.reshape(j_me, (1,)).astype(jnp.int32)
    FFq = ffn1_w2.shape[1]
    views = {"ffn1_w2": (2, 2 * FFq, D), "ffn2_w2": (2, 2 * FFq, D), "w_out": (D, D),
             "w_branch": (N_CHIP, N_BRANCH, RW, D // N_CHIP)}

    def make_land(key, deps):
        l, n = int(key[1]), key[3:]
        return cast_into_slot(f"{key}_cast", weights[n], l, j_idx, deps), views.get(n)

    small_sharded = ["norm_g", "b_merge", "rnn_conv_w", "lru_b_a", "lru_b_x", "lru_lambda", "sc_conv_w"]
    pack1 = _pack([c] + [weights[n] for n in small_sharded])
    g1 = allgather8("gather_small_params", pack1).reshape(N_DEV, -1, 128)
    parts = _unpack(g1, [c.shape] + [weights[n].shape for n in small_sharded], lead=(N_DEV,))
    c_all = parts[0].reshape(N_DEV, D)
    full = {n: _unshard_last(p[0::2]) for n, p in zip(small_sharded, parts[1:])}
    cond = jnp.concatenate([c_all, c_ctx[None, :], jnp.zeros((ADA_ROWS - N_DEV - 1, D), F32)], axis=0)

    bias_q = lax.dynamic_slice_in_dim(ada_b, j_me * Aq, Aq, axis=1)[:, None, :]
    mod_q = ada_forward("ada_forward", cond, ada_w, bias_q)
    g2 = allgather8("gather_mod", mod_q.reshape(-1, 128)).reshape(N_DEV, 2, ADA_ROWS, Aq)
    mod_full = _unshard_last(g2[0::2])
    mod_lat = lax.dynamic_index_in_dim(mod_full, b_me, axis=1, keepdims=False)
    mod_ctx = mod_full[:, N_DEV]
    modtabs = [jnp.stack([mod_ctx[l], mod_lat[l]]).reshape(2, N_MOD, 1, D) for l in range(2)]
    ws = WeightStream([f"l{l}_{n}" for l in range(2) for n in big], make_land, first_deps=(g2,))

    cos, sin = _rope_tables(dm)
    sink = attn_sink.reshape(2, 1, H)
    lw = dict(w_a=lru_w_a, w_x=lru_w_x,
              b_a=full["lru_b_a"][:, :, None, :], b_x=full["lru_b_x"][:, :, None, :],
              lam=full["lru_lambda"][:, :, None, :])
    gn = full["norm_g"]
    bm = full["b_merge"][:, :, None, :]
    rcw = full["rnn_conv_w"][:, :, None, :]
    scw = full["sc_conv_w"][:, :, None, :]

    h = jnp.concatenate([ctx[0], x[0]], axis=0)
    saved = []
    for l in range(2):
        mt = modtabs[l]
        sv = {}
        h, sv["ffn1"] = _ffn_forward(f"l{l}_ffn1", h, gn[l, 0:1], mt, 0, ws, f"l{l}_ffn1_w13", f"l{l}_ffn1_w2", dm)
        sv["h_mix"] = h
        u = norm_mod(f"l{l}_mix_norm", h, gn[l, 1:2], mt, 1, dm)
        wing = ws.get(f"l{l}_w_in", u)
        z = mm_cols(f"l{l}_w_in", u, wing, BF16, flat=True, deps=ws.take_tokens())
        xa = rnn_conv(f"l{l}_rnn_conv", z, rcw[l], rnn_conv_b[l][None, :], dm)
        scans = []
        for d in range(2):
            a_d, u_d = lru_gates(f"l{l}_lru_gates{d}", xa, lw, l, d, dm)
            h_d = lru_scan(f"l{l}_lru_scan{d}", a_d, u_d, True, d == 1, dm)
            scans.append((a_d, u_d, h_d))
        ya = rnn_out(f"l{l}_rnn_out", scans[0][2], scans[1][2], z, dm)
        yb = short_conv(f"l{l}_short_conv", z, scw[l], dm)
        qr, kr, vv = qkv_prep(f"l{l}_qkv", z, cos, sin, dm)
        yatt, lse = attention(f"l{l}_attn", qr, kr, vv, sink[l], dm)
        ys = (ya, yb, yatt)
        wbg = ws.get(f"l{l}_w_branch", yatt)
        lifted = [mm_branch(f"l{l}_lift{br}", ys[br], wbg, br, deps=ws.take_tokens()) for br in range(N_BRANCH)]
        merged = merge(f"l{l}_merge", z, lifted, bm[l], dm)
        woutg = ws.get(f"l{l}_w_out", merged)
        h, y = mm_plain(f"l{l}_w_out", merged, woutg, "nn", F32, deps=ws.take_tokens(), resid=(h, mt, 1, 1.0, dm.C))
        sv.update(u=u, z=z, xa=xa, scans=scans, ys=ys, qkv=(qr, kr, vv), lse=lse, lifted=lifted, merged=merged, y=y)
        h, sv["ffn2"] = _ffn_forward(f"l{l}_ffn2", h, gn[l, 2:3], mt, 2, ws, f"l{l}_ffn2_w13", f"l{l}_ffn2_w2", dm)
        saved.append(sv)

    dh, loss_vec, d_final_g = loss_head("loss_head", h, final_norm_g[None, :], loss_target[0], dm)
    loss = lax.psum(loss_vec[0, 0], ("x", "y", "c"))

    small = {n: [None, None] for n in ["norm_g", "b_merge", "rnn_conv_w", "rnn_conv_b", "lru_w_a", "lru_b_a", "lru_w_x",
                                       "lru_b_x", "lru_lambda", "sc_conv_w", "attn_sink"]}
    dmods = [None, None]
    scatters = []

    def scatter(name, keyed):
        grads3 = [g.reshape(N_CHIP, -1, g.shape[-1]) for g in keyed.values()]
        lands = [lax.empty((3,) + g.shape[1:], BF16) for g in grads3]
        ss, rs, bufs, token = exchange_start(f"{name}_start", grads3 + lands, _scatter_plan(len(grads3)))
        scatters.append((name, ss, rs, bufs, list(keyed)))
        return token

    tok = ()
    for l in (1, 0):
        mt = modtabs[l]
        sv = saved[l]
        dh, tok, dss2, dgate2, dgn2 = _ffn_backward(f"l{l}_ffn2", dh, sv["ffn2"], gn[l, 2:3], mt, 2,
                                                    ws, f"l{l}_ffn2_w13", f"l{l}_ffn2_w2", dm, scatter,
                                                    (("ffn2_w13", l), ("ffn2_w2", l)), deps=tok)

        dyg, dgate1 = resid_bwd(f"l{l}_mix_res_bwd", dh, sv["y"], mt, 1, 1.0, dm, deps=tok)
        woutg, wbg, wing = ws.get(f"l{l}_w_out"), ws.get(f"l{l}_w_branch"), ws.get(f"l{l}_w_in")
        dmerged = mm_plain(f"l{l}_dmerged", dyg, woutg, "nt", F32)
        mix_grads = {("w_out", l): mm_plain_grad(f"l{l}_dw_out", sv["merged"], dyg)}
        dz = lax.empty((T, NZ), BF16)
        dl0, dl1, dl2, dz, dbm = merge_bwd(f"l{l}_merge_bwd", dmerged, sv["z"], sv["lifted"], bm[l], dz, dm)
        dys = []
        for br, dl in enumerate((dl0, dl1, dl2)):
            dys.append(mm_branch_t(f"l{l}_dy{br}", dl, wbg, br))
            mix_grads[("w_branch", l, br)] = mm_branch_grad(f"l{l}_dwb{br}", sv["ys"][br], dl)
        small["b_merge"][l] = dbm[:, 0]

        qr, kr, vv = sv["qkv"]
        dq, dk, dv, dsink = attention_bwd(f"l{l}_attn_bwd", qr, kr, vv, sink[l], sv["ys"][2], sv["lse"], dys[2], dm)
        dz = qkv_bwd(f"l{l}_qkv_bwd", dq, dk, dv, cos, sin, dz, dm)
        small["attn_sink"][l] = dsink[0, :H]

        dz, dscw = short_conv_bwd(f"l{l}_short_conv_bwd", dys[1], sv["z"], scw[l], dz, dm)
        small["sc_conv_w"][l] = dscw[:, 0]

        (a0, u0, h0), (a1, u1, h1) = sv["scans"]
        dhs, drg = rnn_out_bwd(f"l{l}_rnn_out_bwd", dys[0], h0, h1, sv["z"], dm)
        dxa, lru_sums = [], []
        for d, (a_d, u_d, h_d) in enumerate(sv["scans"]):
            lam_d, dla_d = lru_scan_bwd(f"l{l}_lru_scan_bwd{d}", a_d, u_d, h_d, dhs, False, d == 0, dm)
            outs = lru_gates_bwd(f"l{l}_lru_gates_bwd{d}", sv["xa"], lw, l, d, lam_d, dla_d, dm)
            dxa.append(outs[0])
            lru_sums.append(outs[1:])
        dz, drcw, drcb = rnn_conv_bwd(f"l{l}_rnn_conv_bwd", dxa[0], dxa[1], drg, sv["z"], rcw[l], dz, dm)
        small["rnn_conv_w"][l] = drcw[:, 0]
        small["rnn_conv_b"][l] = drcb[0]
        for i, n in enumerate(["lru_w_a", "lru_b_a", "lru_w_x", "lru_b_x", "lru_lambda"]):
            small[n][l] = jnp.stack([lru_sums[0][i], lru_sums[1][i]]).reshape((2,) + weights[n].shape[2:-1] + (-1,))

        du = mm_cols_t(f"l{l}_du_mix", dz, wing, flat=True)
        mix_grads[("w_in", l)] = mm_cols_grad(f"l{l}_dw_in", sv["u"], dz, flat=True)
        dh, dss1, dgn1 = norm_mod_bwd(f"l{l}_mix_norm_bwd", du, sv["h_mix"], gn[l, 1:2], mt, 1, dh, dm)
        tok = (scatter(f"l{l}_scatter_mix", mix_grads),)

        dh, tok, dss0, dgate0, dgn0 = _ffn_backward(f"l{l}_ffn1", dh, sv["ffn1"], gn[l, 0:1], mt, 0,
                                                    ws, f"l{l}_ffn1_w13", f"l{l}_ffn1_w2", dm, scatter,
                                                    (("ffn1_w13", l), ("ffn1_w2", l)), deps=tok)
        small["norm_g"][l] = jnp.concatenate([dgn0, dgn1, dgn2], axis=0)
        dmods[l] = jnp.concatenate([dss0, dgate0, dss1, dgate1, dss2, dgate2], axis=1).reshape(2, A)

    grad_x = dh[C:][None]

    pack_mod = jnp.stack([jnp.stack([dmods[l][1], dmods[l][0]]) for l in range(2)])
    g3 = allgather8("gather_dmod", pack_mod.reshape(-1, 128)).reshape(N_DEV, 2, 2, 1, A)
    dmod_full, d_ada_b = dmod_assemble("dmod_assemble", g3)
    dmod_q = lax.dynamic_slice_in_dim(dmod_full, j_me * Aq, Aq, axis=2)
    dcond_q = ada_cond_grad("ada_cond_grad", dmod_q, ada_w)

    small_names = list(small)
    small_parts = [jnp.stack(small[n]) for n in small_names] + [d_final_g, dcond_q[N_DEV]]
    small_shapes = [p.shape for p in small_parts]
    lru_big = [small_names.index("lru_w_a"), small_names.index("lru_w_x")]
    rest_idx = [i for i in range(len(small_parts)) if i not in lru_big]
    summed = [None] * len(small_parts)
    for i in lru_big:
        buf = _pack([small_parts[i]])
        tot, _ = sum_devices(f"sum_{small_names[i]}", allgather8(f"gather_{small_names[i]}", buf).reshape(N_DEV, -1, 128))
        summed[i] = _unpack(tot, [small_shapes[i]])[0]
    buf = _pack([small_parts[i] for i in rest_idx])
    tot, chip_tot = sum_devices("sum_small_grads", allgather8("gather_small_grads", buf).reshape(N_DEV, -1, 128))
    for i, val in zip(rest_idx, _unpack(tot, [small_shapes[i] for i in rest_idx])):
        summed[i] = val
    dcond_ctx = _unpack(chip_tot, [small_shapes[i] for i in rest_idx])[-1]
    sg = jax.nn.sigmoid(c_ctx)
    grads = dict(zip(small_names, summed[:len(small_names)]))
    grads["final_norm_g"] = summed[len(small_names)][0]
    grads["c_ctx"] = dcond_ctx * (sg * (1.0 + c_ctx * (1.0 - sg)))
    grads["ada_b"] = d_ada_b[:, 0]
    for n in small_sharded:
        g = grads[n]
        q = g.shape[-1] // N_CHIP
        grads[n] = lax.dynamic_slice_in_dim(g, j_me * q, q, axis=g.ndim - 1)

    arrived = {}

    def collect(idx, after):
        name, ss, rs, bufs, keys = scatters[idx]
        done = exchange_wait(f"{name}_wait", ss, rs, bufs, _scatter_plan(len(keys)), after)
        for i, key in enumerate(keys):
            arrived[key] = (done[i], done[len(keys) + i])

    for idx in range(len(scatters) - 1):
        collect(idx, dh)
    results = {}
    last_done = dh

    def finish(swap, after):
        n, ss, rs, bufs = swap
        own, other = exchange_wait(f"swap_{n}_wait", ss, rs, bufs, _sibling_plan, after)
        results[n] = adamw(f"adamw_{n}", weights[n], mom_m[n], mom_v[n], [own, other])
        return results[n][0]

    in_flight = None
    for n in ["ffn2_w13", "ffn2_w2", "w_out", "w_branch", "w_in", "ffn1_w2", "ffn1_w13"]:
        if not any(k[0] == n and k[1] == 0 for k in arrived):
            collect(len(scatters) - 1, last_done)
        keys = sorted((k for k in arrived if k[0] == n), key=lambda k: k[1:])
        part = sum_parts(f"sum_{n}", [arrived[k] for k in keys], j_idx).reshape(weights[n].shape)
        ss, rs, bufs, token = exchange_start(f"swap_{n}_start", [part, lax.empty(part.shape, F32)], _sibling_plan)
        if in_flight is not None:
            last_done = finish(in_flight, token)
        in_flight = (n, ss, rs, bufs)
    finish(in_flight, last_done)
    results["ada_w"] = ada_update("ada_update", cond, dmod_q, ada_w, m_ada_w, v_ada_w)
    small_all = [n for n in order if n not in results]
    pk = lambda d: _pack([d[n] for n in small_all])
    outs = adamw("adamw_small", pk(weights), pk(mom_m), pk(mom_v), [pk(grads)])
    shapes_small = [weights[n].shape for n in small_all]
    unpacked = [_unpack(o, shapes_small) for o in outs]
    for i, n in enumerate(small_all):
        results[n] = tuple(unpacked[k][i] for k in range(4))

    return (loss, grad_x, *[results[n][0] for n in order], *[results[n][1] for n in order],
            *[results[n][2] for n in order], *[results[n][3] for n in order])
```

```python
import functools
import math

import jax
import jax.numpy as jnp
from jax import lax
from jax.experimental import pallas as pl
from jax.experimental.pallas import tpu as pltpu

F32 = jnp.float32
BF16 = jnp.bfloat16
MESH = pl.DeviceIdType.MESH

HEAD_DIM = 128
GRID_W = 64
WINDOW = 128
Q_BLOCK = 128
ROPE_BASE = 10000.0
LRU_C = 8.0
EPS = 1e-6
NEG_INF = -1e30
N_MOD = 9
N_BRANCH = 3
RNN_BLOCK = 128
HALO = 16
SCAN_ROWS = 8
LSE_W = 128

ADAM_LR = 0.001
ADAM_B1 = 0.9
ADAM_B2 = 0.999
ADAM_EPS = 1e-08
ADAM_WD = 0.01
ADAM_STEP = 10

VMEM_LIMIT_BYTES = 48 * 1024 * 1024
N_DEV = 8
N_CHIP = 4


def _pick(n, cands):
    for c in cands:
        if c <= n and n % c == 0:
            return c
    return n


def _cparams(sem):
    return pltpu.CompilerParams(dimension_semantics=sem, vmem_limit_bytes=VMEM_LIMIT_BYTES)


def _silu(x):
    return x * jax.nn.sigmoid(x)


def _dsilu(x):
    s = jax.nn.sigmoid(x)
    return s * (1.0 + x * (1.0 - s))


_GELU_K = math.sqrt(2.0 / math.pi)


def _gelu(x):
    return 0.5 * x * (1.0 + jnp.tanh(_GELU_K * (x + 0.044715 * x * x * x)))


def _dgelu(x):
    t = jnp.tanh(_GELU_K * (x + 0.044715 * x * x * x))
    return 0.5 * (1.0 + t) + 0.5 * x * (1.0 - t * t) * _GELU_K * (1.0 + 3.0 * 0.044715 * x * x)


def _expm1(x):
    series = x * (1.0 + x * (0.5 + x * (1.0 / 6.0 + x * (1.0 / 24.0 + x * (1.0 / 120.0)))))
    return jnp.where(jnp.abs(x) < 0.1, series, jnp.exp(x) - 1.0)


def _my_pos():
    return lax.axis_index("x"), lax.axis_index("y"), lax.axis_index("c")


_DIMS = {"nn": (((1,), (0,)), ((), ())), "nt": (((1,), (1,)), ((), ())), "tn": (((0,), (0,)), ((), ()))}


def _mm(name, a, b, *, mode, grid, a_blk, a_map, b_blk, b_map, o_blk, o_map, out_shape, out_dtype, deps=(), resid=None,
        b_resident=False):
    nk = grid[-1]
    nax = len(grid)
    acc_shape = tuple(d for d in o_blk if d is not None)

    def product(a_ref, b_ref):
        return lax.dot_general(a_ref[...].astype(BF16), b_ref[...].astype(BF16), _DIMS[mode], preferred_element_type=F32)

    def write(res, rest):
        if resid is None:
            o_ref = rest[-1] if nk == 1 else rest[-2]
            o_ref[...] = res.astype(o_ref.dtype)
            return
        _, _, s, coef, n_ctx = resid
        h_ref, m_ref = rest[len(deps)], rest[len(deps) + 1]
        o_ref, f_ref = rest[len(deps) + 2], rest[len(deps) + 3]
        tm = acc_shape[0]
        row = pl.program_id(0) * tm + lax.broadcasted_iota(jnp.int32, acc_shape, 0)
        gate = jnp.where(row < n_ctx, m_ref[0, 3 * s + 2], m_ref[1, 3 * s + 2])
        f_ref[...] = res.astype(f_ref.dtype)
        o_ref[...] = h_ref[...] + (coef * gate) * res

    def body_one_step(a_ref, b_ref, *rest):
        write(product(a_ref, b_ref), rest)

    def body(a_ref, b_ref, *rest):
        acc_ref = rest[-1]
        k = pl.program_id(nax - 1)

        @pl.when(k == 0)
        def _():
            acc_ref[...] = jnp.zeros_like(acc_ref)

        acc_ref[...] += product(a_ref, b_ref)

        @pl.when(k == nk - 1)
        def _():
            write(acc_ref[...], rest)

    b_spec = pl.BlockSpec(b_blk, b_map, pipeline_mode=pl.Buffered(1)) if b_resident else pl.BlockSpec(b_blk, b_map)
    in_specs = [pl.BlockSpec(a_blk, a_map), b_spec] + [pl.BlockSpec(memory_space=pl.ANY)] * len(deps)
    out_specs = pl.BlockSpec(o_blk, o_map)
    out_shapes = jax.ShapeDtypeStruct(out_shape, out_dtype)
    operands = (a, b, *deps)
    if resid is not None:
        h, modtab = resid[0], resid[1]
        tn = o_blk[-1]
        in_specs += [pl.BlockSpec(o_blk, o_map),
                     pl.BlockSpec((2, N_MOD, 1, tn), lambda *idx: (0, 0, 0, o_map(*idx)[-1]))]
        out_specs = [pl.BlockSpec(o_blk, o_map), pl.BlockSpec(o_blk, o_map)]
        out_shapes = [jax.ShapeDtypeStruct(out_shape, F32), jax.ShapeDtypeStruct(out_shape, BF16)]
        operands += (h, modtab)
    return pl.pallas_call(
        body_one_step if nk == 1 else body, name=name, grid=grid,
        in_specs=in_specs, out_specs=out_specs, out_shape=out_shapes,
        scratch_shapes=[] if nk == 1 else [pltpu.VMEM(acc_shape, F32)],
        compiler_params=_cparams(("parallel",) * (nax - 1) + ("arbitrary",)),
    )(*operands)


def _tiles(n):
    return _pick(n, (768, 512, 384, 256, 128, 64, 32, 16))


def _tiles_long(n):
    return _pick(n, (1408, 768, 512, 384, 256, 128, 64, 32, 16))


VMEM_TILE_BUDGET = 40 * 1024 * 1024


def _fit(n, nbytes):
    for c in (1408, 768, 512, 384, 256, 128, 64, 32, 16):
        if c <= n and n % c == 0 and nbytes(c) <= VMEM_TILE_BUDGET:
            return c
    return _pick(n, (16, 8))


def _whole(n, cap=2048):
    return n if n <= cap else _ktile(n)


def _ktile(n):
    return _pick(n, (512, 256, 128))


def mm_cols(name, a, wg, out_dtype, flat, deps=()):
    T, K = a.shape
    Nq = wg.shape[-1]
    tk = _whole(K)
    osize = jnp.dtype(out_dtype).itemsize
    one = tk == K
    tm = _fit(T, lambda t: 2 * t * Nq * osize + (2 if one else 4) * tk * Nq + 4 * t * tk + 4 * t * Nq)
    if flat:
        o_blk, o_map, o_shape = (tm, Nq), (lambda j, i, k: (i, j)), (T, N_CHIP * Nq)
    else:
        o_blk, o_map, o_shape = (None, tm, Nq), (lambda j, i, k: (j, i, 0)), (N_CHIP, T, Nq)
    return _mm(name, a, wg, mode="nn", grid=(N_CHIP, T // tm, K // tk),
               a_blk=(tm, tk), a_map=lambda j, i, k: (i, k),
               b_blk=(None, tk, Nq), b_map=lambda j, i, k: (j, k, 0),
               o_blk=o_blk, o_map=o_map, out_shape=o_shape, out_dtype=out_dtype, deps=deps, b_resident=one)


def mm_cols_t(name, d, wg, flat, deps=()):
    K, Nq = wg.shape[-2:]
    T = d.shape[-2]
    tm, tn = _tiles_long(T), _ktile(K)
    if flat:
        a_blk, a_map = (tm, Nq), (lambda i, j, k: (i, k))
    else:
        a_blk, a_map = (None, tm, Nq), (lambda i, j, k: (k, i, 0))
    return _mm(name, d, wg, mode="nt", grid=(T // tm, K // tn, N_CHIP),
               a_blk=a_blk, a_map=a_map,
               b_blk=(None, tn, Nq), b_map=lambda i, j, k: (k, j, 0),
               o_blk=(tm, tn), o_map=lambda i, j, k: (i, j), out_shape=(T, K), out_dtype=F32, deps=deps)


def mm_cols_grad(name, a, d, flat):
    T, K = a.shape
    Nq = d.shape[-1] // N_CHIP if flat else d.shape[-1]
    tt, br = _tiles_long(T), _ktile(K)
    if flat:
        b_blk, b_map = (tt, Nq), (lambda j, r, t: (t, j))
    else:
        b_blk, b_map = (None, tt, Nq), (lambda j, r, t: (j, t, 0))
    return _mm(name, a, d, mode="tn", grid=(N_CHIP, K // br, T // tt),
               a_blk=(tt, br), a_map=lambda j, r, t: (t, r),
               b_blk=b_blk, b_map=b_map,
               o_blk=(None, br, Nq), o_map=lambda j, r, t: (j, r, 0),
               out_shape=(N_CHIP, K, Nq), out_dtype=BF16)


def mm_rows(name, a, wg, deps=(), resid=None):
    G, T, Kg = a.shape
    N = wg.shape[-1]
    tm, tn = _tiles(T), _pick(N, (1024, 512, 256, 128))
    return _mm(name, a, wg, mode="nn", grid=(T // tm, N // tn, G),
               a_blk=(None, tm, Kg), a_map=lambda i, j, k: (k, i, 0),
               b_blk=(None, Kg, tn), b_map=lambda i, j, k: (k, 0, j),
               o_blk=(tm, tn), o_map=lambda i, j, k: (i, j), out_shape=(T, N), out_dtype=F32, deps=deps, resid=resid)


def mm_rows_t(name, d, wg):
    T, N = d.shape
    G, Kg = wg.shape[0], wg.shape[1]
    tk = _whole(N)
    one = tk == N
    tm = _fit(T, lambda t: 4 * t * Kg + (2 if one else 4) * Kg * tk + 4 * t * tk + 4 * t * Kg)
    return _mm(name, d, wg, mode="nt", grid=(G, T // tm, N // tk),
               a_blk=(tm, tk), a_map=lambda j, i, k: (i, k),
               b_blk=(None, Kg, tk), b_map=lambda j, i, k: (j, 0, k),
               o_blk=(None, tm, Kg), o_map=lambda j, i, k: (j, i, 0), out_shape=(G, T, Kg), out_dtype=BF16,
               b_resident=one)


def mm_rows_grad(name, a, d):
    G, T, Kg = a.shape
    N = d.shape[-1]
    tt, tn = _tiles_long(T), _ktile(N)
    return _mm(name, a, d, mode="tn", grid=(G, N // tn, T // tt),
               a_blk=(None, tt, Kg), a_map=lambda g, j, t: (g, t, 0),
               b_blk=(tt, tn), b_map=lambda g, j, t: (t, j),
               o_blk=(None, Kg, tn), o_map=lambda g, j, t: (g, 0, j), out_shape=(G, Kg, N), out_dtype=BF16)


def mm_plain(name, a, w, mode, out_dtype, deps=(), resid=None):
    T = a.shape[0]
    K, N = w.shape[-2:]
    tm = _tiles(T)
    if mode == "nn":
        tn, tk = (_whole(N) if resid is None else _pick(N, (1024, 512, 256, 128))), _whole(K)
        return _mm(name, a, w, mode="nn", grid=(T // tm, N // tn, K // tk),
                   a_blk=(tm, tk), a_map=lambda i, j, k: (i, k),
                   b_blk=(tk, tn), b_map=lambda i, j, k: (k, j),
                   o_blk=(tm, tn), o_map=lambda i, j, k: (i, j), out_shape=(T, N), out_dtype=out_dtype, deps=deps,
                   resid=resid)
    tn, tk = _whole(K), _whole(N)
    return _mm(name, a, w, mode="nt", grid=(T // tm, K // tn, N // tk),
               a_blk=(tm, tk), a_map=lambda i, j, k: (i, k),
               b_blk=(tn, tk), b_map=lambda i, j, k: (j, k),
               o_blk=(tm, tn), o_map=lambda i, j, k: (i, j), out_shape=(T, K), out_dtype=out_dtype)


def mm_plain_grad(name, a, d):
    T, K = a.shape
    N = d.shape[-1]
    tt, br, tn = _tiles_long(T), _ktile(K), _whole(N)
    return _mm(name, a, d, mode="tn", grid=(K // br, N // tn, T // tt),
               a_blk=(tt, br), a_map=lambda r, j, t: (t, r),
               b_blk=(tt, tn), b_map=lambda r, j, t: (t, j),
               o_blk=(br, tn), o_map=lambda r, j, t: (r, j), out_shape=(K, N), out_dtype=BF16)


def mm_branch(name, y, wbg, br, deps=()):
    T, RW = y.shape
    Dq = wbg.shape[-1]
    tm = _tiles_long(T)

    def body(y_ref, w_ref, *rest):
        o_ref = rest[-1]
        lhs = y_ref[...]
        for j in range(N_CHIP):
            o_ref[:, j * Dq:(j + 1) * Dq] = jnp.dot(lhs, w_ref[j], preferred_element_type=F32).astype(o_ref.dtype)

    return pl.pallas_call(
        body, name=name, grid=(T // tm,),
        in_specs=[pl.BlockSpec((tm, RW), lambda i: (i, 0)),
                  pl.BlockSpec((N_CHIP, None, RW, Dq), lambda i: (0, br, 0, 0))] + [pl.BlockSpec(memory_space=pl.ANY)] * len(deps),
        out_specs=pl.BlockSpec((tm, N_CHIP * Dq), lambda i: (i, 0)),
        out_shape=jax.ShapeDtypeStruct((T, N_CHIP * Dq), BF16), compiler_params=_cparams(("parallel",)),
    )(y, wbg, *deps)


def mm_branch_t(name, d, wbg, br):
    T = d.shape[0]
    RW, Dq = wbg.shape[-2:]
    tm = _tiles_long(T)

    def body(d_ref, w_ref, o_ref):
        for j in range(N_CHIP):
            term = lax.dot_general(d_ref[:, j * Dq:(j + 1) * Dq], w_ref[j], _DIMS["nt"], preferred_element_type=F32)
            if j == 0:
                o_ref[...] = term
            else:
                o_ref[...] += term

    return pl.pallas_call(
        body, name=name, grid=(T // tm,),
        in_specs=[pl.BlockSpec((tm, N_CHIP * Dq), lambda i: (i, 0)),
                  pl.BlockSpec((N_CHIP, None, RW, Dq), lambda i: (0, br, 0, 0))],
        out_specs=pl.BlockSpec((tm, RW), lambda i: (i, 0)),
        out_shape=jax.ShapeDtypeStruct((T, RW), F32), compiler_params=_cparams(("parallel",)),
    )(d, wbg)


def mm_branch_grad(name, y, d):
    T, RW = y.shape
    D = d.shape[-1]
    Dq = D // N_CHIP
    tt = _tiles_long(T)
    nt = T // tt

    def body(y_ref, d_ref, o_ref, acc_ref):
        t = pl.program_id(0)

        @pl.when(t == 0)
        def _():
            acc_ref[...] = jnp.zeros_like(acc_ref)

        acc_ref[...] += lax.dot_general(y_ref[...], d_ref[...], _DIMS["tn"], preferred_element_type=F32)

        @pl.when(t == nt - 1)
        def _():
            for j in range(N_CHIP):
                o_ref[j] = acc_ref[:, j * Dq:(j + 1) * Dq].astype(o_ref.dtype)

    return pl.pallas_call(
        body, name=name, grid=(nt,),
        in_specs=[pl.BlockSpec((tt, RW), lambda t: (t, 0)), pl.BlockSpec((tt, D), lambda t: (t, 0))],
        out_specs=pl.BlockSpec((N_CHIP, RW, Dq), lambda t: (0, 0, 0)),
        out_shape=jax.ShapeDtypeStruct((N_CHIP, RW, Dq), BF16),
        scratch_shapes=[pltpu.VMEM((RW, D), F32)], compiler_params=_cparams(("arbitrary",)),
    )(y, d)


def allgather8(name, x_shard):
    m_per, n = x_shard.shape

    def body(x_ref, out_ref, send_sems, recv_sems, local_sem):
        x, y, c = _my_pos()
        me, sibling = (x, y, c), (x, y, 1 - c)
        chips = [(1 - x, y), (x, 1 - y), (1 - x, 1 - y)]

        def rows(px, py, pc):
            return out_ref.at[pl.ds((4 * px + 2 * py + pc) * m_per, m_per), :]

        def copy(k, block, to, src=None):
            return pltpu.make_async_remote_copy(
                src_ref=rows(*block) if src is None else src, dst_ref=rows(*block),
                send_sem=send_sems.at[k], recv_sem=recv_sems.at[k], device_id=to, device_id_type=MESH)

        mine = pltpu.make_async_copy(x_ref, rows(*me), local_sem)
        mine.start()
        first = [copy(0, me, sibling, src=x_ref)]
        first += [copy(1 + j, me, (*chip, c), src=x_ref) for j, chip in enumerate(chips)]
        for cp in first:
            cp.start()
        passed = [copy(4 + j, (*chip, c), sibling) for j, chip in enumerate(chips)]
        for j, chip in enumerate(chips):
            copy(1 + j, (*chip, c), me).wait_recv()
            passed[j].start()
        copy(0, sibling, me).wait_recv()
        for j, chip in enumerate(chips):
            copy(4 + j, (*chip, 1 - c), me).wait_recv()
        for cp in first + passed:
            cp.wait_send()
        mine.wait()

    return pl.pallas_call(
        body, name=name,
        out_shape=jax.ShapeDtypeStruct((N_DEV * m_per, n), x_shard.dtype),
        in_specs=[pl.BlockSpec(memory_space=pltpu.VMEM)],
        out_specs=pl.BlockSpec(memory_space=pltpu.VMEM),
        scratch_shapes=[pltpu.SemaphoreType.DMA((7,)), pltpu.SemaphoreType.DMA((7,)), pltpu.SemaphoreType.DMA],
        compiler_params=pltpu.CompilerParams(vmem_limit_bytes=VMEM_LIMIT_BYTES),
    )(x_shard)


def _other_chips(x, y):
    return [(1 - x, y), (x, 1 - y), (1 - x, 1 - y)]


_HBM = pl.BlockSpec(memory_space=pltpu.HBM)
_SEM = pl.BlockSpec(memory_space=pltpu.SEMAPHORE)
_ANY = pl.BlockSpec(memory_space=pl.ANY)
_EFFECT = pltpu.SideEffectType.DATAFLOW_SIDE_EFFECTING
TOKEN_SHAPE = (8, 128)


def _in_hbm(a):
    return pltpu.with_memory_space_constraint(a, pltpu.HBM)


def exchange_start(name, bufs, plan):
    n = len(bufs)
    n_copies = len(plan([None] * n, 0, 0, 0, dry=True))

    def body(*refs):
        ins = refs[:n]
        send_sems, recv_sems = refs[n], refs[n + 1]
        token = refs[-1]
        x, y, c = _my_pos()
        for i, (src, dst, to) in enumerate(plan(ins, x, y, c)):
            pltpu.make_async_remote_copy(src_ref=src, dst_ref=dst, send_sem=send_sems.at[i], recv_sem=recv_sems.at[i],
                                         device_id=to, device_id_type=MESH).start()
        token[...] = jnp.zeros_like(token)

    outs = pl.pallas_call(
        body, name=name,
        out_shape=(pltpu.SemaphoreType.DMA((n_copies,)), pltpu.SemaphoreType.DMA((n_copies,)),
                   *[pltpu.HBM(b.shape, b.dtype) for b in bufs], jax.ShapeDtypeStruct(TOKEN_SHAPE, F32)),
        in_specs=[_HBM] * n,
        out_specs=(_SEM, _SEM, *[_HBM] * n, pl.BlockSpec(memory_space=pltpu.VMEM)),
        input_output_aliases={i: 2 + i for i in range(n)},
        compiler_params=pltpu.CompilerParams(has_side_effects=_EFFECT),
    )(*[_in_hbm(b) for b in bufs])
    return outs[0], outs[1], list(outs[2:2 + n]), outs[-1]


def exchange_wait(name, send_sems, recv_sems, bufs, plan, after):
    n = len(bufs)

    def body(*refs):
        ins = refs[:n]
        send_sems, recv_sems = refs[n], refs[n + 1]
        x, y, c = _my_pos()
        for i, (src, dst, to) in enumerate(plan(ins, x, y, c, arriving=True)):
            cp = pltpu.make_async_remote_copy(src_ref=src, dst_ref=dst, send_sem=send_sems.at[i],
                                              recv_sem=recv_sems.at[i], device_id=to, device_id_type=MESH)
            cp.wait_send()
            cp.wait_recv()

    outs = pl.pallas_call(
        body, name=name,
        out_shape=tuple(pltpu.HBM(b.shape, b.dtype) for b in bufs),
        in_specs=[_HBM] * n + [_SEM, _SEM, _ANY],
        out_specs=tuple([_HBM] * n),
        input_output_aliases={i: i for i in range(n)},
        compiler_params=pltpu.CompilerParams(has_side_effects=_EFFECT),
    )(*bufs, send_sems, recv_sems, after)
    return list(outs)


def _gather_plan(refs, x, y, c, dry=False, arriving=False):
    if dry:
        return [None] * 3
    (land,) = refs
    j_me = 2 * x + y
    return [(land.at[j_me], land.at[(2 * px + py) if arriving else j_me], (px, py, c)) for px, py in _other_chips(x, y)]


def _sibling_plan(refs, x, y, c, dry=False, arriving=False):
    if dry:
        return [None]
    src, land = refs
    return [(src, land, (x, y, 1 - c))]


def _scatter_plan(n_pieces):
    def plan(refs, x, y, c, dry=False, arriving=False):
        if dry:
            return [None] * (3 * n_pieces)
        grads, lands = refs[:n_pieces], refs[n_pieces:]
        return [(grads[p].at[2 * px + py], lands[p].at[k], (px, py, c))
                for p in range(n_pieces) for k, (px, py) in enumerate(_other_chips(x, y))]
    return plan


def _view2d(a):
    return a.reshape(-1, a.shape[-1])


def _row_tile(rows, width, itemsize=4, budget=1 << 20):
    t = 8
    for cand in (1024, 512, 256, 128, 64, 32, 16, 8):
        if rows % cand == 0 and cand * width * itemsize <= budget:
            t = cand
            break
    return t if rows % t == 0 else rows


def cast_into_slot(name, w, l, j_idx, deps=()):
    w3 = w.reshape(w.shape[0], -1, w.shape[-1])
    _, R, W = w3.shape
    tr = _row_tile(R, W)

    def body(j_ref, a_ref, *rest):
        o_ref = rest[-1]
        o_ref[...] = a_ref[...].astype(BF16)

    return pl.pallas_call(
        body, name=name,
        grid_spec=pltpu.PrefetchScalarGridSpec(
            num_scalar_prefetch=1, grid=(R // tr,),
            in_specs=[pl.BlockSpec((None, tr, W), lambda i, j: (l, i, 0))] + [pl.BlockSpec(memory_space=pl.ANY)] * len(deps),
            out_specs=pl.BlockSpec((None, tr, W), lambda i, j: (j[0], i, 0))),
        out_shape=jax.ShapeDtypeStruct((N_CHIP, R, W), BF16), compiler_params=_cparams(("parallel",)),
    )(j_idx, w3, *deps)


def sum_parts(name, groups, j_idx):
    n = len(groups)
    _, R, W = groups[0][0].shape
    tr = _row_tile(R, W)

    def body(j_ref, *refs):
        o_ref = refs[-1]
        g = pl.program_id(0)
        for q in range(n):
            @pl.when(g == q)
            def _(q=q):
                own, got = refs[2 * q], refs[2 * q + 1]
                o_ref[...] = ((own[...].astype(F32) + got[0].astype(F32)) + got[1].astype(F32)) + got[2].astype(F32)

    in_specs = []
    for q in range(n):
        in_specs.append(pl.BlockSpec((None, tr, W), lambda g, i, j, q=q: (j[0], jnp.where(g == q, i, 0), 0)))
        in_specs.append(pl.BlockSpec((3, tr, W), lambda g, i, j, q=q: (0, jnp.where(g == q, i, 0), 0)))
    return pl.pallas_call(
        body, name=name,
        grid_spec=pltpu.PrefetchScalarGridSpec(
            num_scalar_prefetch=1, grid=(n, R // tr), in_specs=in_specs,
            out_specs=pl.BlockSpec((None, tr, W), lambda g, i, j: (g, i, 0))),
        out_shape=jax.ShapeDtypeStruct((n, R, W), F32), compiler_params=_cparams(("arbitrary", "arbitrary")),
    )(j_idx, *[a for pair in groups for a in pair])


def adamw(name, w, m, v, g_parts):
    shape = w.shape
    w2, m2, v2 = _view2d(w), _view2d(m), _view2d(v)
    gs = [_view2d(g) for g in g_parts]
    R, W = w2.shape
    tr = _row_tile(R, W, budget=1 << 19)
    ng = len(gs)
    bc1 = 1.0 - ADAM_B1 ** ADAM_STEP
    bc2 = 1.0 - ADAM_B2 ** ADAM_STEP

    def body(*refs):
        w_ref, m_ref, v_ref = refs[:3]
        g_refs = refs[3:3 + ng]
        go_ref, d_ref, mo_ref, vo_ref = refs[3 + ng:]
        g = g_refs[0][...]
        for r in g_refs[1:]:
            g = g + r[...]
        mn = ADAM_B1 * m_ref[...] + (1.0 - ADAM_B1) * g
        vn = ADAM_B2 * v_ref[...] + (1.0 - ADAM_B2) * (g * g)
        m_hat = mn / bc1
        v_hat = vn / bc2
        go_ref[...] = g
        d_ref[...] = -ADAM_LR * (m_hat / (jnp.sqrt(v_hat) + ADAM_EPS) + ADAM_WD * w_ref[...])
        mo_ref[...] = mn
        vo_ref[...] = vn

    spec = pl.BlockSpec((tr, W), lambda i: (i, 0))
    outs = pl.pallas_call(
        body, name=name, grid=(R // tr,),
        in_specs=[spec] * (3 + ng), out_specs=[spec] * 4,
        out_shape=[jax.ShapeDtypeStruct((R, W), F32)] * 4, compiler_params=_cparams(("parallel",)),
    )(w2, m2, v2, *gs)
    return tuple(o.reshape(shape) for o in outs)


class Dims:
    pass


def _sel(dm):
    return (pl.program_id(0) >= dm.nctx).astype(jnp.int32)


def norm_mod(name, h, gn, modtab, s, dm, deps=()):
    T, D = h.shape
    tm = dm.tme

    def body(h_ref, g_ref, m_ref, *rest):
        u_ref = rest[-1]
        sel = _sel(dm)
        x = h_ref[...]
        r = lax.rsqrt(jnp.mean(x * x, axis=-1, keepdims=True) + EPS)
        ng = x * r * g_ref[...]
        u_ref[...] = (ng * (1.0 + m_ref[sel, 3 * s + 1]) + m_ref[sel, 3 * s]).astype(u_ref.dtype)

    return pl.pallas_call(
        body, name=name, grid=(T // tm,),
        in_specs=[pl.BlockSpec((tm, D), lambda i: (i, 0)), pl.BlockSpec((1, D), lambda i: (0, 0)),
                  pl.BlockSpec((2, N_MOD, 1, D), lambda i: (0, 0, 0, 0))] + [_ANY] * len(deps),
        out_specs=pl.BlockSpec((tm, D), lambda i: (i, 0)),
        out_shape=jax.ShapeDtypeStruct((T, D), BF16), compiler_params=_cparams(("parallel",)),
    )(h, gn, modtab, *deps)


def norm_mod_bwd(name, du, h, gn, modtab, s, dh_in, dm):
    T, D = h.shape
    tm = dm.tme

    def body(du_ref, h_ref, g_ref, m_ref, dhi_ref, dh_ref, dmod_ref, dg_ref):
        i = pl.program_id(0)
        sel = _sel(dm)

        @pl.when(i == 0)
        def _():
            dmod_ref[...] = jnp.zeros_like(dmod_ref)
            dg_ref[...] = jnp.zeros_like(dg_ref)

        x = h_ref[...]
        r = lax.rsqrt(jnp.mean(x * x, axis=-1, keepdims=True) + EPS)
        n = x * r
        g = g_ref[...]
        du = du_ref[...]
        dmod_ref[sel, 0] += jnp.sum(du, axis=0, keepdims=True)
        dmod_ref[sel, 1] += jnp.sum(du * (n * g), axis=0, keepdims=True)
        dng = du * (1.0 + m_ref[sel, 3 * s + 1])
        dg_ref[...] += jnp.sum(dng * n, axis=0, keepdims=True)
        dn = dng * g
        dh_ref[...] = dhi_ref[...] + r * (dn - n * jnp.mean(dn * n, axis=-1, keepdims=True))

    row = pl.BlockSpec((tm, D), lambda i: (i, 0))
    return pl.pallas_call(
        body, name=name, grid=(T // tm,),
        in_specs=[row, row, pl.BlockSpec((1, D), lambda i: (0, 0)),
                  pl.BlockSpec((2, N_MOD, 1, D), lambda i: (0, 0, 0, 0)), row],
        out_specs=[row, pl.BlockSpec((2, 2, 1, D), lambda i: (0, 0, 0, 0)), pl.BlockSpec((1, D), lambda i: (0, 0))],
        out_shape=[jax.ShapeDtypeStruct((T, D), F32), jax.ShapeDtypeStruct((2, 2, 1, D), F32),
                   jax.ShapeDtypeStruct((1, D), F32)],
        compiler_params=_cparams(("arbitrary",)),
    )(du, h, gn, modtab, dh_in)


def resid_bwd(name, dh, f, modtab, s, coef, dm, deps=()):
    T, D = dh.shape
    tm = dm.tme

    def body(dh_ref, f_ref, m_ref, *rest):
        df_ref, dg_ref = rest[-2:]
        sel = _sel(dm)

        @pl.when(pl.program_id(0) == 0)
        def _():
            dg_ref[...] = jnp.zeros_like(dg_ref)

        d = coef * dh_ref[...]
        df_ref[...] = (d * m_ref[sel, 3 * s + 2]).astype(df_ref.dtype)
        dg_ref[sel, 0] += jnp.sum(d * f_ref[...], axis=0, keepdims=True)

    row = pl.BlockSpec((tm, D), lambda i: (i, 0))
    return pl.pallas_call(
        body, name=name, grid=(T // tm,),
        in_specs=[row, row, pl.BlockSpec((2, N_MOD, 1, D), lambda i: (0, 0, 0, 0))] + [_ANY] * len(deps),
        out_specs=[row, pl.BlockSpec((2, 1, 1, D), lambda i: (0, 0, 0, 0))],
        out_shape=[jax.ShapeDtypeStruct((T, D), BF16), jax.ShapeDtypeStruct((2, 1, 1, D), F32)],
        compiler_params=_cparams(("arbitrary",)),
    )(dh, f, modtab, *deps)


def swiglu(name, gu, dm):
    _, T, Nq = gu.shape
    tm = dm.tme
    gu4 = gu.reshape(2, 2, T, Nq)

    def body(gu_ref, o_ref):
        g = gu_ref[0].astype(F32)
        o_ref[...] = (_silu(g) * gu_ref[1].astype(F32)).astype(o_ref.dtype)

    return pl.pallas_call(
        body, name=name, grid=(2, T // tm),
        in_specs=[pl.BlockSpec((2, None, tm, Nq), lambda k, i: (0, k, i, 0))],
        out_specs=pl.BlockSpec((None, tm, Nq), lambda k, i: (k, i, 0)),
        out_shape=jax.ShapeDtypeStruct((2, T, Nq), BF16), compiler_params=_cparams(("parallel", "parallel")),
    )(gu4)


def swiglu_bwd(name, dact, gu, dm):
    _, T, Nq = gu.shape
    tm = dm.tme
    gu4 = gu.reshape(2, 2, T, Nq)

    def body(da_ref, gu_ref, o_ref):
        g = gu_ref[0].astype(F32)
        da = da_ref[...].astype(F32)
        o_ref[0] = (da * gu_ref[1].astype(F32) * _dsilu(g)).astype(o_ref.dtype)
        o_ref[1] = (da * _silu(g)).astype(o_ref.dtype)

    out = pl.pallas_call(
        body, name=name, grid=(2, T // tm),
        in_specs=[pl.BlockSpec((None, tm, Nq), lambda k, i: (k, i, 0)),
                  pl.BlockSpec((2, None, tm, Nq), lambda k, i: (0, k, i, 0))],
        out_specs=pl.BlockSpec((2, None, tm, Nq), lambda k, i: (0, k, i, 0)),
        out_shape=jax.ShapeDtypeStruct((2, 2, T, Nq), BF16), compiler_params=_cparams(("parallel", "parallel")),
    )(dact, gu4)
    return out.reshape(4, T, Nq)


def _halo_specs(dm, width, col):
    tm = dm.tme
    per = tm // HALO
    last = dm.T // HALO - 1
    return [pl.BlockSpec((tm, width), lambda i: (i, col)),
            pl.BlockSpec((HALO, width), lambda i: (jnp.maximum(i * per - 1, 0), col)),
            pl.BlockSpec((HALO, width), lambda i: (jnp.minimum((i + 1) * per, last), col))]


def _segment_edges(dm, i):
    first = jnp.logical_or(i == 0, i == dm.nctx)
    last = jnp.logical_or(i == dm.nctx - 1, i == dm.nt - 1)
    return first, last


def _extend(main, prev, nxt, first, last):
    return jnp.concatenate([jnp.where(first, 0.0, prev), main, jnp.where(last, 0.0, nxt)], axis=0)


def _shift(ext, o, tm):
    n = ext.shape[0]
    rolled = ext if o == 0 else pltpu.roll(ext, (-o) % n, 0)
    return rolled[HALO:HALO + tm]


def _load_ext(refs, first, last):
    main, prev, nxt = refs
    return _extend(main[...].astype(F32), prev[...].astype(F32), nxt[...].astype(F32), first, last)


def rnn_conv(name, z, w, b, dm):
    T, RW, tm = dm.T, dm.RW, dm.tme

    def body(main, prev, nxt, w_ref, b_ref, o_ref):
        first, last = _segment_edges(dm, pl.program_id(0))
        ext = _load_ext((main, prev, nxt), first, last)
        acc = jnp.zeros((tm, RW), F32) + b_ref[...]
        for k in range(4):
            acc = acc + w_ref[k] * _shift(ext, k - 2, tm)
        o_ref[...] = acc

    return pl.pallas_call(
        body, name=name, grid=(dm.nt,),
        in_specs=_halo_specs(dm, RW, 0) + [pl.BlockSpec((4, 1, RW), lambda i: (0, 0, 0)),
                                           pl.BlockSpec((1, RW), lambda i: (0, 0))],
        out_specs=pl.BlockSpec((tm, RW), lambda i: (i, 0)),
        out_shape=jax.ShapeDtypeStruct((T, RW), F32), compiler_params=_cparams(("parallel",)),
    )(z, z, z, w, b)


def _blockdiag(x, w_ref):
    nb = w_ref.shape[0]
    outs = [jnp.dot(x[:, n * RNN_BLOCK:(n + 1) * RNN_BLOCK], w_ref[n].astype(BF16), preferred_element_type=F32)
            for n in range(nb)]
    return jnp.concatenate(outs, axis=-1)


def _lru_gates(xa, wa_ref, ba_ref, wx_ref, bx_ref, lam_ref):
    xb = xa.astype(BF16)
    r = jax.nn.sigmoid(_blockdiag(xb, wa_ref) + ba_ref[...])
    ig = jax.nn.sigmoid(_blockdiag(xb, wx_ref) + bx_ref[...])
    nl = -lam_ref[...]
    sp = jnp.maximum(nl, 0.0) + jnp.log(1.0 + jnp.exp(-jnp.abs(nl)))
    log_a = -LRU_C * r * sp
    a = jnp.exp(log_a)
    m = jnp.sqrt(-_expm1(2.0 * log_a))
    return r, ig, sp, a, m


def _lru_specs(l, d, nb, RW):
    wspec = pl.BlockSpec((None, None, nb, RNN_BLOCK, RNN_BLOCK), lambda i: (l, d, 0, 0, 0))
    vspec = pl.BlockSpec((None, None, 1, RW), lambda i: (l, d, 0, 0))
    return [wspec, vspec, wspec, vspec, vspec]


def lru_gates(name, xa, lw, l, d, dm):
    T, RW, tm = dm.T, dm.RW, dm.tme

    def body(xa_ref, wa_ref, ba_ref, wx_ref, bx_ref, lam_ref, a_ref, u_ref):
        xa_v = xa_ref[...]
        r, ig, sp, a, m = _lru_gates(xa_v, wa_ref, ba_ref, wx_ref, bx_ref, lam_ref)
        a_ref[...] = a
        u_ref[...] = m * (ig * xa_v)

    row = pl.BlockSpec((tm, RW), lambda i: (i, 0))
    return pl.pallas_call(
        body, name=name, grid=(dm.nt,),
        in_specs=[row] + _lru_specs(l, d, dm.NB, RW), out_specs=[row, row],
        out_shape=[jax.ShapeDtypeStruct((T, RW), F32)] * 2, compiler_params=_cparams(("parallel",)),
    )(xa, lw["w_a"], lw["b_a"], lw["w_x"], lw["b_x"], lw["lam"])


def lru_gates_bwd(name, xa, lw, l, d, du, dloga, dm):
    T, RW, tm, NB = dm.T, dm.RW, dm.tme, dm.NB

    def body(xa_ref, wa_ref, ba_ref, wx_ref, bx_ref, lam_ref, du_ref, dla_ref,
             dxa_ref, dwa_ref, dba_ref, dwx_ref, dbx_ref, dlam_ref):
        @pl.when(pl.program_id(0) == 0)
        def _():
            for ref in (dwa_ref, dba_ref, dwx_ref, dbx_ref, dlam_ref):
                ref[...] = jnp.zeros_like(ref)

        xa_v = xa_ref[...]
        r, ig, sp, a, m = _lru_gates(xa_v, wa_ref, ba_ref, wx_ref, bx_ref, lam_ref)
        duu = du_ref[...]
        dm_ = duu * (ig * xa_v)
        dig = duu * m * xa_v
        dxa = duu * m * ig
        dla = dla_ref[...] - dm_ * (a * a) / m
        dr = dla * (-LRU_C * sp)
        dsp = jnp.sum(dla * (-LRU_C * r), axis=0, keepdims=True)
        dlam_ref[...] += dsp * (-jax.nn.sigmoid(-lam_ref[...]))
        dpa = dr * r * (1.0 - r)
        dpx = dig * ig * (1.0 - ig)
        dba_ref[...] += jnp.sum(dpa, axis=0, keepdims=True)
        dbx_ref[...] += jnp.sum(dpx, axis=0, keepdims=True)
        xb, dpab, dpxb = xa_v.astype(BF16), dpa.astype(BF16), dpx.astype(BF16)
        back = []
        for n in range(NB):
            sl = slice(n * RNN_BLOCK, (n + 1) * RNN_BLOCK)
            dwa_ref[n] += lax.dot_general(xb[:, sl], dpab[:, sl], _DIMS["tn"], preferred_element_type=F32)
            dwx_ref[n] += lax.dot_general(xb[:, sl], dpxb[:, sl], _DIMS["tn"], preferred_element_type=F32)
            back.append(lax.dot_general(dpab[:, sl], wa_ref[n].astype(BF16), _DIMS["nt"], preferred_element_type=F32)
                        + lax.dot_general(dpxb[:, sl], wx_ref[n].astype(BF16), _DIMS["nt"], preferred_element_type=F32))
        dxa_ref[...] = dxa + jnp.concatenate(back, axis=-1)

    row = pl.BlockSpec((tm, RW), lambda i: (i, 0))
    wacc = pl.BlockSpec((NB, RNN_BLOCK, RNN_BLOCK), lambda i: (0, 0, 0))
    vacc = pl.BlockSpec((1, RW), lambda i: (0, 0))
    wshape = jax.ShapeDtypeStruct((NB, RNN_BLOCK, RNN_BLOCK), F32)
    vshape = jax.ShapeDtypeStruct((1, RW), F32)
    return pl.pallas_call(
        body, name=name, grid=(dm.nt,),
        in_specs=[row] + _lru_specs(l, d, NB, RW) + [row, row],
        out_specs=[row, wacc, vacc, wacc, vacc, vacc],
        out_shape=[jax.ShapeDtypeStruct((T, RW), F32), wshape, vshape, wshape, vshape, vshape],
        compiler_params=_cparams(("arbitrary",)),
    )(xa, lw["w_a"], lw["b_a"], lw["w_x"], lw["b_x"], lw["lam"], du, dloga)


def _chunk_order(dm, ctx_first, descending):
    nch, nctx = dm.nt, dm.nctx
    nlat = nch - nctx

    def order(s):
        if ctx_first and not descending:
            return s
        if not ctx_first and descending:
            return nch - 1 - s
        if ctx_first:
            return jnp.where(s < nctx, nctx - 1 - s, nch - 1 - (s - nctx))
        return jnp.where(s < nlat, nctx + s, s - nlat)

    return order


def _tile_scan(a, b, carry, descending):
    row = lax.broadcasted_iota(jnp.int32, a.shape, 0)
    for s in (1, 2, 4):
        sh = (SCAN_ROWS - s) if descending else s
        keep = (row < SCAN_ROWS - s) if descending else (row >= s)
        ap = pltpu.roll(a, sh, 0)
        bp = pltpu.roll(b, sh, 0)
        b = jnp.where(keep, b + a * bp, b)
        a = jnp.where(keep, a * ap, a)
    h = b + a * carry
    edge = 0 if descending else SCAN_ROWS - 1
    new_carry = jnp.sum(jnp.where(row == edge, h, 0.0), axis=0, keepdims=True)
    return h, new_carry


def lru_scan(name, a, u, ctx_first, descending, dm):
    T, RW, ch = dm.T, dm.RW, dm.tme
    order = _chunk_order(dm, ctx_first, descending)
    ngrp = ch // SCAN_ROWS

    def body(a_ref, u_ref, h_ref, carry_ref):
        @pl.when(pl.program_id(0) == 0)
        def _():
            carry_ref[...] = jnp.zeros_like(carry_ref)

        def step(g, carry):
            g = (ngrp - 1 - g) if descending else g
            rows = pl.ds(pl.multiple_of(g * SCAN_ROWS, SCAN_ROWS), SCAN_ROWS)
            h, carry = _tile_scan(a_ref[rows, :], u_ref[rows, :], carry, descending)
            h_ref[rows, :] = h
            return carry

        carry_ref[...] = lax.fori_loop(0, ngrp, step, carry_ref[...])

    row = pl.BlockSpec((ch, RW), lambda s: (order(s), 0))
    return pl.pallas_call(
        body, name=name, grid=(dm.nt,),
        in_specs=[row, row], out_specs=row,
        out_shape=jax.ShapeDtypeStruct((T, RW), F32),
        scratch_shapes=[pltpu.VMEM((1, RW), F32)], compiler_params=_cparams(("arbitrary",)),
    )(a, u)


def lru_scan_bwd(name, a, u, h, dh, ctx_first, descending, dm):
    T, RW, ch = dm.T, dm.RW, dm.tme
    order = _chunk_order(dm, ctx_first, descending)
    ngrp = ch // SCAN_ROWS

    def body(a_ref, u_ref, h_ref, dh_ref, lam_ref, dla_ref, carry_ref):
        @pl.when(pl.program_id(0) == 0)
        def _():
            carry_ref[...] = jnp.zeros_like(carry_ref)

        def step(g, carry):
            g = (ngrp - 1 - g) if descending else g
            rows = pl.ds(pl.multiple_of(g * SCAN_ROWS, SCAN_ROWS), SCAN_ROWS)
            a_v, dh_v = a_ref[rows, :], dh_ref[rows, :]
            mu, new_carry = _tile_scan(a_v, a_v * dh_v, carry, descending)
            row = lax.broadcasted_iota(jnp.int32, mu.shape, 0)
            if descending:
                nxt = jnp.where(row == SCAN_ROWS - 1, carry, pltpu.roll(mu, SCAN_ROWS - 1, 0))
            else:
                nxt = jnp.where(row == 0, carry, pltpu.roll(mu, 1, 0))
            lam = dh_v + nxt
            lam_ref[rows, :] = lam
            dla_ref[rows, :] = lam * (h_ref[rows, :] - u_ref[rows, :])
            return new_carry

        carry_ref[...] = lax.fori_loop(0, ngrp, step, carry_ref[...])

    row = pl.BlockSpec((ch, RW), lambda s: (order(s), 0))
    return pl.pallas_call(
        body, name=name, grid=(dm.nt,),
        in_specs=[row] * 4, out_specs=[row, row],
        out_shape=[jax.ShapeDtypeStruct((T, RW), F32)] * 2,
        scratch_shapes=[pltpu.VMEM((1, RW), F32)], compiler_params=_cparams(("arbitrary",)),
    )(a, u, h, dh)


def rnn_out(name, hf, hb, z, dm):
    T, RW, tm = dm.T, dm.RW, dm.tme

    def body(hf_ref, hb_ref, rg_ref, o_ref):
        o_ref[...] = ((hf_ref[...] + hb_ref[...]) * _gelu(rg_ref[...].astype(F32))).astype(o_ref.dtype)

    row = pl.BlockSpec((tm, RW), lambda i: (i, 0))
    return pl.pallas_call(
        body, name=name, grid=(dm.nt,),
        in_specs=[row, row, pl.BlockSpec((tm, RW), lambda i: (i, 1))], out_specs=row,
        out_shape=jax.ShapeDtypeStruct((T, RW), BF16), compiler_params=_cparams(("parallel",)),
    )(hf, hb, z)


def rnn_out_bwd(name, dya, hf, hb, z, dm):
    T, RW, tm = dm.T, dm.RW, dm.tme

    def body(d_ref, hf_ref, hb_ref, rg_ref, dh_ref, drg_ref):
        d, rg = d_ref[...], rg_ref[...].astype(F32)
        dh_ref[...] = d * _gelu(rg)
        drg_ref[...] = d * (hf_ref[...] + hb_ref[...]) * _dgelu(rg)

    row = pl.BlockSpec((tm, RW), lambda i: (i, 0))
    return pl.pallas_call(
        body, name=name, grid=(dm.nt,),
        in_specs=[row, row, row, pl.BlockSpec((tm, RW), lambda i: (i, 1))], out_specs=[row, row],
        out_shape=[jax.ShapeDtypeStruct((T, RW), F32)] * 2, compiler_params=_cparams(("parallel",)),
    )(dya, hf, hb, z)


def rnn_conv_bwd(name, dxa_f, dxa_b, drg, z, w, dz, dm):
    T, RW, tm = dm.T, dm.RW, dm.tme

    def body(f0, f1, f2, b0, b1, b2, x0, x1, x2, drg_ref, w_ref, dz_in, dz_ref, dw_ref, db_ref):
        i = pl.program_id(0)

        @pl.when(i == 0)
        def _():
            dw_ref[...] = jnp.zeros_like(dw_ref)
            db_ref[...] = jnp.zeros_like(db_ref)

        first, last = _segment_edges(dm, i)
        dext = _load_ext((f0, f1, f2), first, last) + _load_ext((b0, b1, b2), first, last)
        xext = _load_ext((x0, x1, x2), first, last)
        dmain = dext[HALO:HALO + tm]
        drx = jnp.zeros((tm, RW), F32)
        for k in range(4):
            drx = drx + w_ref[k] * _shift(dext, -(k - 2), tm)
            dw_ref[k] += jnp.sum(dmain * _shift(xext, k - 2, tm), axis=0, keepdims=True)
        db_ref[...] += jnp.sum(dmain, axis=0, keepdims=True)
        dz_ref[:, :RW] = drx.astype(dz_ref.dtype)
        dz_ref[:, RW:] = drg_ref[...].astype(dz_ref.dtype)

    return pl.pallas_call(
        body, name=name, grid=(dm.nt,),
        in_specs=_halo_specs(dm, RW, 0) * 3 + [pl.BlockSpec((tm, RW), lambda i: (i, 0)),
                                               pl.BlockSpec((4, 1, RW), lambda i: (0, 0, 0)),
                                               pl.BlockSpec(memory_space=pl.ANY)],
        out_specs=[pl.BlockSpec((tm, 2 * RW), lambda i: (i, 0)), pl.BlockSpec((4, 1, RW), lambda i: (0, 0, 0)),
                   pl.BlockSpec((1, RW), lambda i: (0, 0))],
        out_shape=[jax.ShapeDtypeStruct(dz.shape, dz.dtype), jax.ShapeDtypeStruct((4, 1, RW), F32),
                   jax.ShapeDtypeStruct((1, RW), F32)],
        input_output_aliases={11: 0}, compiler_params=_cparams(("arbitrary",)),
    )(dxa_f, dxa_f, dxa_f, dxa_b, dxa_b, dxa_b, z, z, z, drg, w, dz)


def short_conv(name, z, w, dm):
    T, RW, tm = dm.T, dm.RW, dm.tme

    def body(sb_ref, g0, g1, g2, x0, x1, x2, w_ref, o_ref):
        first, last = _segment_edges(dm, pl.program_id(0))
        pext = _load_ext((g0, g1, g2), first, last) * _load_ext((x0, x1, x2), first, last)
        cp = jnp.zeros((tm, RW), F32)
        for k in range(3):
            cp = cp + w_ref[k] * _shift(pext, k - 1, tm)
        o_ref[...] = (sb_ref[...].astype(F32) * cp).astype(o_ref.dtype)

    return pl.pallas_call(
        body, name=name, grid=(dm.nt,),
        in_specs=[pl.BlockSpec((tm, RW), lambda i: (i, 2))] + _halo_specs(dm, RW, 3) + _halo_specs(dm, RW, 4)
        + [pl.BlockSpec((3, 1, RW), lambda i: (0, 0, 0))],
        out_specs=pl.BlockSpec((tm, RW), lambda i: (i, 0)),
        out_shape=jax.ShapeDtypeStruct((T, RW), BF16), compiler_params=_cparams(("parallel",)),
    )(z, z, z, z, z, z, z, w)


def short_conv_bwd(name, dyb, z, w, dz, dm):
    T, RW, tm = dm.T, dm.RW, dm.tme

    def spec3(col):
        per = tm // HALO
        last = T // HALO - 1
        return [pl.BlockSpec((tm, RW), lambda i, p: (i, col)),
                pl.BlockSpec((HALO, RW), lambda i, p: (jnp.maximum(i * per - 1, 0), col)),
                pl.BlockSpec((HALO, RW), lambda i, p: (jnp.minimum((i + 1) * per, last), col))]

    def body(d0, d1, d2, s0, s1, s2, g0, g1, g2, x0, x1, x2, w_ref, dz_in, dz_ref, dw_ref, parts_ref):
        i, p = pl.program_id(0), pl.program_id(1)

        @pl.when(jnp.logical_and(i == 0, p == 0))
        def _():
            dw_ref[...] = jnp.zeros_like(dw_ref)

        @pl.when(p == 0)
        def _():
            first, last = _segment_edges(dm, i)
            gext = _load_ext((g0, g1, g2), first, last)
            xext = _load_ext((x0, x1, x2), first, last)
            pext = gext * xext
            dyext = _load_ext((d0, d1, d2), first, last)
            dcext = dyext * _load_ext((s0, s1, s2), first, last)
            dcmain = dcext[HALO:HALO + tm]
            cp = jnp.zeros((tm, RW), F32)
            dp = jnp.zeros((tm, RW), F32)
            for k in range(3):
                pk = _shift(pext, k - 1, tm)
                cp = cp + w_ref[k] * pk
                dp = dp + w_ref[k] * _shift(dcext, -(k - 1), tm)
                dw_ref[k] += jnp.sum(dcmain * pk, axis=0, keepdims=True)
            parts_ref[0] = (dyext[HALO:HALO + tm] * cp).astype(parts_ref.dtype)
            parts_ref[1] = (dp * xext[HALO:HALO + tm]).astype(parts_ref.dtype)
            parts_ref[2] = (dp * gext[HALO:HALO + tm]).astype(parts_ref.dtype)

        dz_ref[...] = parts_ref[p]

    return pl.pallas_call(
        body, name=name, grid=(dm.nt, 3),
        in_specs=spec3(0) + spec3(2) + spec3(3) + spec3(4)
        + [pl.BlockSpec((3, 1, RW), lambda i, p: (0, 0, 0)), pl.BlockSpec(memory_space=pl.ANY)],
        out_specs=[pl.BlockSpec((tm, RW), lambda i, p: (i, 2 + p)), pl.BlockSpec((3, 1, RW), lambda i, p: (0, 0, 0))],
        out_shape=[jax.ShapeDtypeStruct(dz.shape, dz.dtype), jax.ShapeDtypeStruct((3, 1, RW), F32)],
        scratch_shapes=[pltpu.VMEM((3, tm, RW), dz.dtype)],
        input_output_aliases={13: 0}, compiler_params=_cparams(("arbitrary", "arbitrary")),
    )(dyb, dyb, dyb, z, z, z, z, z, z, z, z, z, w, dz)


def _rope_tables(dm):
    L, C = dm.L, dm.C
    half = HEAD_DIM // 2
    pos = jnp.arange(L)
    row = (pos // GRID_W).astype(F32)
    col = (pos % GRID_W).astype(F32)
    inv = ROPE_BASE ** (-jnp.arange(0, half, 2, dtype=F32) / half)
    ar, ac = row[:, None] * inv, col[:, None] * inv
    cos = jnp.concatenate([jnp.cos(ar), jnp.cos(ar), jnp.cos(ac), jnp.cos(ac)], axis=-1)
    sin = jnp.concatenate([-jnp.sin(ar), jnp.sin(ar), -jnp.sin(ac), jnp.sin(ac)], axis=-1)
    cos = jnp.concatenate([jnp.ones((C, HEAD_DIM), F32), cos], axis=0)
    sin = jnp.concatenate([jnp.zeros((C, HEAD_DIM), F32), sin], axis=0)
    return cos, sin


def _swap_pairs(x):
    quarter = HEAD_DIM // 4
    lane = lax.broadcasted_iota(jnp.int32, x.shape, 1)
    return jnp.where(lane % (2 * quarter) < quarter, pltpu.roll(x, HEAD_DIM - quarter, 1), pltpu.roll(x, quarter, 1))


def _rope(x, cos, sin):
    return x * cos + _swap_pairs(x) * sin


def _unrope(d, cos, sin):
    return d * cos + _swap_pairs(d * sin)


def qkv_prep(name, z, cos, sin, dm):
    T, tm, HQ, KW = dm.T, dm.tme, dm.HQ, dm.KW
    qcol, kcol = dm.off_q // HQ, dm.off_k // KW

    def body(q_ref, k_ref, v_ref, c_ref, s_ref, qo, ko, vo):
        cos_v, sin_v = c_ref[...], s_ref[...]
        for hd in range(HQ // HEAD_DIM):
            sl = slice(hd * HEAD_DIM, (hd + 1) * HEAD_DIM)
            qo[:, sl] = _rope(q_ref[:, sl].astype(F32), cos_v, sin_v).astype(qo.dtype)
        for hd in range(KW // HEAD_DIM):
            sl = slice(hd * HEAD_DIM, (hd + 1) * HEAD_DIM)
            ko[:, sl] = _rope(k_ref[:, sl].astype(F32), cos_v, sin_v).astype(ko.dtype)
        vo[...] = v_ref[...].astype(vo.dtype)

    tab = pl.BlockSpec((tm, HEAD_DIM), lambda i: (i, 0))
    return pl.pallas_call(
        body, name=name, grid=(dm.nt,),
        in_specs=[pl.BlockSpec((tm, HQ), lambda i: (i, qcol)), pl.BlockSpec((tm, KW), lambda i: (i, kcol)),
                  pl.BlockSpec((tm, KW), lambda i: (i, kcol + 1)), tab, tab],
        out_specs=[pl.BlockSpec((tm, HQ), lambda i: (i, 0)), pl.BlockSpec((tm, KW), lambda i: (i, 0)),
                   pl.BlockSpec((tm, KW), lambda i: (i, 0))],
        out_shape=[jax.ShapeDtypeStruct((T, HQ), BF16), jax.ShapeDtypeStruct((T, KW), BF16),
                   jax.ShapeDtypeStruct((T, KW), BF16)],
        compiler_params=_cparams(("parallel",)),
    )(z, z, z, cos, sin)


def qkv_bwd(name, dq, dk, dv, cos, sin, dz, dm):
    T, tm, HQ, KW = dm.T, _tiles(dm.T), dm.HQ, dm.KW
    nq = HQ // KW
    base = dm.off_q // KW

    def body(dq_ref, dk_ref, dv_ref, c_ref, s_ref, dz_in, dz_ref):
        p = pl.program_id(1)
        src = jnp.where(p < nq, dq_ref[...], jnp.where(p == nq, dk_ref[...], dv_ref[...]))
        cos_v, sin_v = c_ref[...], s_ref[...]
        is_v = p == nq + 1
        for hd in range(KW // HEAD_DIM):
            sl = slice(hd * HEAD_DIM, (hd + 1) * HEAD_DIM)
            dz_ref[:, sl] = jnp.where(is_v, src[:, sl], _unrope(src[:, sl], cos_v, sin_v)).astype(dz_ref.dtype)

    tab = pl.BlockSpec((tm, HEAD_DIM), lambda i, p: (i, 0))
    blk = pl.BlockSpec((tm, KW), lambda i, p: (i, 0))
    return pl.pallas_call(
        body, name=name, grid=(T // tm, nq + 2),
        in_specs=[pl.BlockSpec((tm, KW), lambda i, p: (i, jnp.minimum(p, nq - 1))), blk, blk, tab, tab,
                  pl.BlockSpec(memory_space=pl.ANY)],
        out_specs=pl.BlockSpec((tm, KW), lambda i, p: (i, base + p)),
        out_shape=jax.ShapeDtypeStruct(dz.shape, dz.dtype),
        input_output_aliases={5: 0}, compiler_params=_cparams(("parallel", "arbitrary")),
    )(dq, dk, dv, cos, sin, dz)


def _attn_specs(dm):
    nC, nB, C, KW = dm.C // Q_BLOCK, dm.T // Q_BLOCK, dm.C, dm.KW

    def near(o):
        return lambda b: (jnp.clip(b + o, nC, nB - 1), 0)

    kv = [pl.BlockSpec((Q_BLOCK, KW), near(o)) for o in (-1, 0, 1)] + [pl.BlockSpec((C, KW), lambda b: (0, 0))]
    return kv


def _attn_mask(dm, b):
    nC, C, L = dm.C // Q_BLOCK, dm.C, dm.L
    span = 3 * Q_BLOCK
    n = b - nC
    iq = lax.broadcasted_iota(jnp.int32, (Q_BLOCK, span + C), 0)
    ik = lax.broadcasted_iota(jnp.int32, (Q_BLOCK, span + C), 1)
    kpos = n * Q_BLOCK + ik - Q_BLOCK
    qpos = n * Q_BLOCK + iq
    local = (b >= nC) & (jnp.abs(qpos - kpos) <= WINDOW) & (kpos >= 0) & (kpos < L)
    return jnp.logical_or(ik >= span, local)


def attention(name, q, k, v, sink, dm):
    T, HQ, KW = dm.T, dm.HQ, dm.KW
    H, KV = HQ // HEAD_DIM, KW // HEAD_DIM
    G = H // KV
    scale = HEAD_DIM ** -0.5

    def body(q_ref, kp, kc, kn, kx, vp, vc, vn, vx, sink_ref, o_ref, lse_ref):
        valid = _attn_mask(dm, pl.program_id(0))
        lane = lax.broadcasted_iota(jnp.int32, (Q_BLOCK, LSE_W), 1)
        lse_all = jnp.zeros((Q_BLOCK, LSE_W), F32)
        for kh in range(KV):
            ks = slice(kh * HEAD_DIM, (kh + 1) * HEAD_DIM)
            k_all = jnp.concatenate([kp[:, ks], kc[:, ks], kn[:, ks], kx[:, ks]], axis=0)
            v_all = jnp.concatenate([vp[:, ks], vc[:, ks], vn[:, ks], vx[:, ks]], axis=0)
            for g in range(G):
                hd = kh * G + g
                hs = slice(hd * HEAD_DIM, (hd + 1) * HEAD_DIM)
                s = lax.dot_general(q_ref[:, hs], k_all, _DIMS["nt"], preferred_element_type=F32) * scale
                s = jnp.where(valid, s, NEG_INF)
                snk = sink_ref[0, hd]
                mx = jnp.maximum(jnp.max(s, axis=-1, keepdims=True), snk)
                p = jnp.exp(s - mx)
                den = jnp.sum(p, axis=-1, keepdims=True) + jnp.exp(snk - mx)
                o = jnp.dot(p.astype(BF16), v_all, preferred_element_type=F32) / den
                o_ref[:, hs] = o.astype(o_ref.dtype)
                lse_all = jnp.where(lane == hd, mx + jnp.log(den), lse_all)
        lse_ref[...] = lse_all

    kv = _attn_specs(dm)
    return pl.pallas_call(
        body, name=name, grid=(T // Q_BLOCK,),
        in_specs=[pl.BlockSpec((Q_BLOCK, HQ), lambda b: (b, 0))] + kv + kv + [pl.BlockSpec(memory_space=pltpu.SMEM)],
        out_specs=[pl.BlockSpec((Q_BLOCK, HQ), lambda b: (b, 0)), pl.BlockSpec((Q_BLOCK, LSE_W), lambda b: (b, 0))],
        out_shape=[jax.ShapeDtypeStruct((T, HQ), BF16), jax.ShapeDtypeStruct((T, LSE_W), F32)],
        compiler_params=_cparams(("parallel",)),
    )(q, k, k, k, k, v, v, v, v, sink)


def attention_bwd(name, q, k, v, sink, o, lse, do, dm):
    T, HQ, KW, C = dm.T, dm.HQ, dm.KW, dm.C
    H, KV = HQ // HEAD_DIM, KW // HEAD_DIM
    G = H // KV
    nC, nB = C // Q_BLOCK, T // Q_BLOCK
    scale = HEAD_DIM ** -0.5
    span = 3 * Q_BLOCK

    def body(q_ref, kp, kc, kn, kx, vp, vc, vn, vx, sink_ref, o_ref, lse_ref, do_ref,
             dq_ref, dk_ref, dv_ref, ds_ref):
        b = pl.program_id(0)

        @pl.when(b == 0)
        def _():
            dk_ref[...] = jnp.zeros_like(dk_ref)
            dv_ref[...] = jnp.zeros_like(dv_ref)
            ds_ref[...] = jnp.zeros_like(ds_ref)

        valid = _attn_mask(dm, b)
        starts = [pl.multiple_of(jnp.clip(b + off, nC, nB - 1) * Q_BLOCK, Q_BLOCK) for off in (-1, 0, 1)]
        lane = lax.broadcasted_iota(jnp.int32, (Q_BLOCK, LSE_W), 1)
        lse_all = lse_ref[...]
        dsink = jnp.zeros((1, LSE_W), F32)
        for kh in range(KV):
            ks = slice(kh * HEAD_DIM, (kh + 1) * HEAD_DIM)
            k_all = jnp.concatenate([kp[:, ks], kc[:, ks], kn[:, ks], kx[:, ks]], axis=0)
            v_all = jnp.concatenate([vp[:, ks], vc[:, ks], vn[:, ks], vx[:, ks]], axis=0)
            dk_all = jnp.zeros((span + C, HEAD_DIM), F32)
            dv_all = jnp.zeros((span + C, HEAD_DIM), F32)
            for g in range(G):
                hd = kh * G + g
                hs = slice(hd * HEAD_DIM, (hd + 1) * HEAD_DIM)
                qh = q_ref[:, hs]
                doh = do_ref[:, hs]
                s = lax.dot_general(qh, k_all, _DIMS["nt"], preferred_element_type=F32) * scale
                s = jnp.where(valid, s, NEG_INF)
                lse_h = jnp.sum(jnp.where(lane == hd, lse_all, 0.0), axis=-1, keepdims=True)
                p = jnp.exp(s - lse_h)
                delta = jnp.sum(doh * o_ref[:, hs].astype(F32), axis=-1, keepdims=True)
                dob = doh.astype(BF16)
                dp = lax.dot_general(dob, v_all, _DIMS["nt"], preferred_element_type=F32)
                dsc = (p * (dp - delta) * scale).astype(BF16)
                dq_ref[:, hs] = jnp.dot(dsc, k_all, preferred_element_type=F32)
                dk_all = dk_all + lax.dot_general(dsc, qh, _DIMS["tn"], preferred_element_type=F32)
                dv_all = dv_all + lax.dot_general(p.astype(BF16), dob, _DIMS["tn"], preferred_element_type=F32)
                p_sink = jnp.exp(sink_ref[0, hd] - lse_h)
                dsink = dsink + jnp.where(lane[0:1] == hd, -jnp.sum(p_sink * delta), 0.0)
            for j, st in enumerate(starts):
                rows = pl.ds(st, Q_BLOCK)
                dk_ref[rows, ks] += dk_all[j * Q_BLOCK:(j + 1) * Q_BLOCK]
                dv_ref[rows, ks] += dv_all[j * Q_BLOCK:(j + 1) * Q_BLOCK]
            dk_ref[0:C, ks] += dk_all[span:]
            dv_ref[0:C, ks] += dv_all[span:]
        ds_ref[...] += dsink

    kv = _attn_specs(dm)
    qspec = pl.BlockSpec((Q_BLOCK, HQ), lambda b: (b, 0))
    full = pl.BlockSpec((T, KW), lambda b: (0, 0))
    return pl.pallas_call(
        body, name=name, grid=(nB,),
        in_specs=[qspec] + kv + kv + [pl.BlockSpec(memory_space=pltpu.SMEM), qspec,
                                      pl.BlockSpec((Q_BLOCK, LSE_W), lambda b: (b, 0)), qspec],
        out_specs=[qspec, full, full, pl.BlockSpec((1, LSE_W), lambda b: (0, 0))],
        out_shape=[jax.ShapeDtypeStruct((T, HQ), F32), jax.ShapeDtypeStruct((T, KW), F32),
                   jax.ShapeDtypeStruct((T, KW), F32), jax.ShapeDtypeStruct((1, LSE_W), F32)],
        compiler_params=_cparams(("arbitrary",)),
    )(q, k, k, k, k, v, v, v, v, sink, o, lse, do)


def merge(name, z, lifted, b_merge, dm):
    T, D, tm, cw = dm.T, dm.D, _tiles(dm.T), dm.cw
    gcol = dm.off_g // cw
    per = D // cw

    def body(g0, g1, g2, l0, l1, l2, b_ref, o_ref):
        acc = jnp.zeros((tm, cw), F32)
        for i, (g, lf) in enumerate(((g0, l0), (g1, l1), (g2, l2))):
            acc = acc + jax.nn.sigmoid(g[...] + b_ref[i]) * lf[...]
        o_ref[...] = acc.astype(o_ref.dtype)

    gspecs = [pl.BlockSpec((tm, cw), lambda i, j, br=br: (i, gcol + br * per + j)) for br in range(N_BRANCH)]
    blk = pl.BlockSpec((tm, cw), lambda i, j: (i, j))
    return pl.pallas_call(
        body, name=name, grid=(T // tm, per),
        in_specs=gspecs + [blk] * 3 + [pl.BlockSpec((N_BRANCH, 1, cw), lambda i, j: (0, 0, j))], out_specs=blk,
        out_shape=jax.ShapeDtypeStruct((T, D), BF16), compiler_params=_cparams(("parallel", "parallel")),
    )(z, z, z, *lifted, b_merge)


def merge_bwd(name, dmerged, z, lifted, b_merge, dz, dm):
    T, D, tm, cw = dm.T, dm.D, _tiles(dm.T), dm.cw
    gcol = dm.off_g // cw
    per = D // cw

    def body(d_ref, g_ref, l0, l1, l2, b_ref, dz_in, o0, o1, o2, dz_ref, db_ref):
        br = pl.program_id(2)

        @pl.when(jnp.logical_and(pl.program_id(1) == 0, br == 0))
        def _():
            db_ref[...] = jnp.zeros_like(db_ref)

        d = d_ref[...]
        gate = jax.nn.sigmoid(g_ref[...] + b_ref[br])
        dlift = (d * gate).astype(o0.dtype)
        for b, (l_ref, o_ref) in enumerate(((l0, o0), (l1, o1), (l2, o2))):
            @pl.when(br == b)
            def _(l_ref=l_ref, o_ref=o_ref):
                o_ref[...] = dlift
                dg = d * l_ref[...] * gate * (1.0 - gate)
                dz_ref[...] = dg.astype(dz_ref.dtype)
                db_ref[b] += jnp.sum(dg, axis=0, keepdims=True)

    blk = pl.BlockSpec((tm, cw), lambda j, i, b: (i, j))
    zblk = pl.BlockSpec((tm, cw), lambda j, i, b: (i, gcol + b * per + j))
    return pl.pallas_call(
        body, name=name, grid=(per, T // tm, N_BRANCH),
        in_specs=[blk, zblk, blk, blk, blk, pl.BlockSpec((N_BRANCH, 1, cw), lambda j, i, b: (0, 0, j)),
                  pl.BlockSpec(memory_space=pl.ANY)],
        out_specs=[blk, blk, blk, zblk, pl.BlockSpec((N_BRANCH, 1, cw), lambda j, i, b: (0, 0, j))],
        out_shape=[jax.ShapeDtypeStruct((T, D), BF16)] * 3 + [jax.ShapeDtypeStruct(dz.shape, dz.dtype),
                                                             jax.ShapeDtypeStruct((N_BRANCH, 1, D), F32)],
        input_output_aliases={6: 3}, compiler_params=_cparams(("parallel", "arbitrary", "arbitrary")),
    )(dmerged, z, *lifted, b_merge, dz)


def loss_head(name, h, gf, target, dm):
    T, D, tm, nctx = dm.T, dm.D, dm.tme, dm.nctx

    def body(h_ref, g_ref, t_ref, dh_ref, loss_ref, dg_ref):
        i = pl.program_id(0)

        @pl.when(i == 0)
        def _():
            loss_ref[...] = jnp.zeros_like(loss_ref)
            dg_ref[...] = jnp.zeros_like(dg_ref)

        @pl.when(i < nctx)
        def _():
            dh_ref[...] = jnp.zeros_like(dh_ref)

        @pl.when(i >= nctx)
        def _():
            x = h_ref[...]
            r = lax.rsqrt(jnp.mean(x * x, axis=-1, keepdims=True) + EPS)
            n = x * r
            g = g_ref[...]
            err = n * g - t_ref[...]
            loss_ref[...] += jnp.sum(err * err) * (0.5 / D)
            dy = err * (1.0 / D)
            dg_ref[...] += jnp.sum(dy * n, axis=0, keepdims=True)
            dn = dy * g
            dh_ref[...] = r * (dn - n * jnp.mean(dn * n, axis=-1, keepdims=True))

    row = pl.BlockSpec((tm, D), lambda i: (i, 0))
    return pl.pallas_call(
        body, name=name, grid=(T // tm,),
        in_specs=[row, pl.BlockSpec((1, D), lambda i: (0, 0)),
                  pl.BlockSpec((tm, D), lambda i: (jnp.maximum(i - nctx, 0), 0))],
        out_specs=[row, pl.BlockSpec((1, 128), lambda i: (0, 0)), pl.BlockSpec((1, D), lambda i: (0, 0))],
        out_shape=[jax.ShapeDtypeStruct((T, D), F32), jax.ShapeDtypeStruct((1, 128), F32),
                   jax.ShapeDtypeStruct((1, D), F32)],
        compiler_params=_cparams(("arbitrary",)),
    )(h, gf, target)


_HI = lax.Precision.HIGHEST
ADA_ROWS = 16


def ada_forward(name, cond, ada_w, bias):
    _, D, Aq = ada_w.shape
    tc = _pick(Aq, (1536, 1152, 768, 512, 384, 256, 128))
    tk = _ktile(D)
    nk = D // tk

    def body(c_ref, w_ref, b_ref, o_ref):
        k = pl.program_id(2)

        @pl.when(k == 0)
        def _():
            o_ref[...] = jnp.zeros_like(o_ref) + b_ref[...]

        o_ref[...] += jnp.dot(_silu(c_ref[...]), w_ref[...], precision=_HI, preferred_element_type=F32)

    return pl.pallas_call(
        body, name=name, grid=(2, Aq // tc, nk),
        in_specs=[pl.BlockSpec((ADA_ROWS, tk), lambda l, j, k: (0, k)),
                  pl.BlockSpec((None, tk, tc), lambda l, j, k: (l, k, j)),
                  pl.BlockSpec((None, 1, tc), lambda l, j, k: (l, 0, j))],
        out_specs=pl.BlockSpec((None, ADA_ROWS, tc), lambda l, j, k: (l, 0, j)),
        out_shape=jax.ShapeDtypeStruct((2, ADA_ROWS, Aq), F32),
        compiler_params=_cparams(("parallel", "parallel", "arbitrary")),
    )(cond, ada_w, bias)


def ada_cond_grad(name, dmod, ada_w):
    _, D, Aq = ada_w.shape
    tc = _pick(Aq, (1536, 1152, 768, 512, 384, 256, 128))
    tn = _ktile(D)
    nc = Aq // tc

    def body(d_ref, w_ref, o_ref):
        @pl.when(jnp.logical_and(pl.program_id(1) == 0, pl.program_id(2) == 0))
        def _():
            o_ref[...] = jnp.zeros_like(o_ref)

        o_ref[...] += lax.dot_general(d_ref[...], w_ref[...], _DIMS["nt"], precision=_HI, preferred_element_type=F32)

    return pl.pallas_call(
        body, name=name, grid=(D // tn, 2, nc),
        in_specs=[pl.BlockSpec((None, ADA_ROWS, tc), lambda j, l, c: (l, 0, c)),
                  pl.BlockSpec((None, tn, tc), lambda j, l, c: (l, j, c))],
        out_specs=pl.BlockSpec((ADA_ROWS, tn), lambda j, l, c: (0, j)),
        out_shape=jax.ShapeDtypeStruct((ADA_ROWS, D), F32),
        compiler_params=_cparams(("parallel", "arbitrary", "arbitrary")),
    )(dmod, ada_w)


def ada_update(name, cond, dmod, w, m, v):
    _, D, Aq = w.shape
    tc = _pick(Aq, (1536, 1152, 768, 512, 384, 256, 128))
    tr = 128 if D % 128 == 0 else D
    bc1 = 1.0 - ADAM_B1 ** ADAM_STEP
    bc2 = 1.0 - ADAM_B2 ** ADAM_STEP

    def body(c_ref, d_ref, w_ref, m_ref, v_ref, go_ref, dl_ref, mo_ref, vo_ref):
        g = lax.dot_general(_silu(c_ref[...]), d_ref[...], _DIMS["tn"], precision=_HI, preferred_element_type=F32)
        mn = ADAM_B1 * m_ref[...] + (1.0 - ADAM_B1) * g
        vn = ADAM_B2 * v_ref[...] + (1.0 - ADAM_B2) * (g * g)
        go_ref[...] = g
        dl_ref[...] = -ADAM_LR * ((mn / bc1) / (jnp.sqrt(vn / bc2) + ADAM_EPS) + ADAM_WD * w_ref[...])
        mo_ref[...] = mn
        vo_ref[...] = vn

    blk = pl.BlockSpec((None, tr, tc), lambda l, i, j: (l, i, j))
    return pl.pallas_call(
        body, name=name, grid=(2, D // tr, Aq // tc),
        in_specs=[pl.BlockSpec((ADA_ROWS, tr), lambda l, i, j: (0, i)),
                  pl.BlockSpec((None, ADA_ROWS, tc), lambda l, i, j: (l, 0, j)), blk, blk, blk],
        out_specs=[blk] * 4, out_shape=[jax.ShapeDtypeStruct(w.shape, F32)] * 4,
        compiler_params=_cparams(("parallel", "parallel", "parallel")),
    )(cond, dmod, w, m, v)


def dmod_assemble(name, gathered):
    A = gathered.shape[-1]
    tc = _pick(A, (2048, 1024, 512, 256, 128))

    def body(g_ref, o_ref, b_ref):
        ctx = g_ref[0, 1]
        for dev in range(1, N_DEV):
            ctx = ctx + g_ref[dev, 1]
        tot = ctx
        for dev in range(N_DEV):
            o_ref[dev:dev + 1, :] = g_ref[dev, 0]
            tot = tot + g_ref[dev, 0]
        o_ref[N_DEV:N_DEV + 1, :] = ctx
        o_ref[N_DEV + 1:, :] = jnp.zeros((ADA_ROWS - N_DEV - 1, tc), F32)
        b_ref[...] = tot

    return pl.pallas_call(
        body, name=name, grid=(2, A // tc),
        in_specs=[pl.BlockSpec((N_DEV, None, 2, 1, tc), lambda l, j: (0, l, 0, 0, j))],
        out_specs=[pl.BlockSpec((None, ADA_ROWS, tc), lambda l, j: (l, 0, j)),
                   pl.BlockSpec((None, 1, tc), lambda l, j: (l, 0, j))],
        out_shape=[jax.ShapeDtypeStruct((2, ADA_ROWS, A), F32), jax.ShapeDtypeStruct((2, 1, A), F32)],
        compiler_params=_cparams(("parallel", "parallel")),
    )(gathered)


def sum_devices(name, gathered):
    _, R, W = gathered.shape
    tr = _row_tile(R, W, budget=1 << 18)

    def body(g_ref, all_ref, chip_ref):
        even = g_ref[0]
        odd = g_ref[1]
        for dev in range(2, N_DEV, 2):
            even = even + g_ref[dev]
            odd = odd + g_ref[dev + 1]
        all_ref[...] = even + odd
        chip_ref[...] = even

    blk = pl.BlockSpec((tr, W), lambda i: (i, 0))
    return pl.pallas_call(
        body, name=name, grid=(R // tr,),
        in_specs=[pl.BlockSpec((N_DEV, tr, W), lambda i: (0, i, 0))], out_specs=[blk, blk],
        out_shape=[jax.ShapeDtypeStruct((R, W), F32)] * 2, compiler_params=_cparams(("parallel",)),
    )(gathered)


PACK_ROWS = 1024


def _pack(arrays):
    flat = jnp.concatenate([a.reshape(-1).astype(F32) for a in arrays])
    pad = (-flat.shape[0]) % (PACK_ROWS * 128)
    return jnp.pad(flat, (0, pad)).reshape(-1, 128)


def _unpack(buf, shapes, lead=()):
    flat = buf.reshape(lead + (-1,))
    out, start = [], 0
    for s in shapes:
        n = math.prod(s)
        out.append(flat[..., start:start + n].reshape(lead + tuple(s)))
        start += n
    return out


def _unshard_last(g):
    g = jnp.moveaxis(g, 0, -2)
    return g.reshape(g.shape[:-2] + (g.shape[-2] * g.shape[-1],))


class WeightStream:
    AHEAD = 2

    def __init__(self, keys, make_land, first_deps=()):
        self.keys, self.make_land = list(keys), make_land
        self.pending, self.values, self.tokens, self.started = {}, {}, [], 0
        for _ in range(self.AHEAD):
            self._start_next(first_deps)

    def _start_next(self, deps):
        if self.started < len(self.keys):
            key = self.keys[self.started]
            self.started += 1
            land, view = self.make_land(key, deps)
            ss, rs, (land,), token = exchange_start(f"{key}_start", [land], _gather_plan)
            self.pending[key] = (ss, rs, land, view)
            self.tokens.append(token)

    def get(self, key, after=None):
        if key not in self.values:
            ss, rs, land, view = self.pending.pop(key)
            (full,) = exchange_wait(f"{key}_wait", ss, rs, [land], _gather_plan, after)
            self.values[key] = full if view is None else full.reshape(view)
            self._start_next((full,))
        return self.values[key]

    def take_tokens(self):
        out, self.tokens = tuple(self.tokens), []
        return out


def _ffn_forward(tag, h, gn, modtab, s, ws, k13, k2, dm):
    u = norm_mod(f"{tag}_norm", h, gn, modtab, s, dm)
    w13g = ws.get(k13, u)
    gu = mm_cols(f"{tag}_w13", u, w13g, BF16, flat=False, deps=ws.take_tokens())
    act = swiglu(f"{tag}_act", gu, dm)
    w2g = ws.get(k2, act)
    h_out, f = mm_rows(f"{tag}_w2", act, w2g, deps=ws.take_tokens(), resid=(h, modtab, s, 0.5, dm.C))
    return h_out, (h, u, gu, act, f)


def _ffn_backward(tag, dh, saved, gn, modtab, s, ws, k13, k2, dm, scatter, keys, deps=()):
    h, u, gu, act, f = saved
    w13g, w2g = ws.get(k13), ws.get(k2)
    df, dgate = resid_bwd(f"{tag}_res_bwd", dh, f, modtab, s, 0.5, dm, deps)
    dact = mm_rows_t(f"{tag}_dact", df, w2g)
    dw2 = mm_rows_grad(f"{tag}_dw2", act, df).reshape(N_CHIP, -1, df.shape[-1])
    tok2 = scatter(f"{tag}_scatter_w2", {keys[1]: dw2})
    dgu = swiglu_bwd(f"{tag}_act_bwd", dact, gu, dm)
    du = mm_cols_t(f"{tag}_du", dgu, w13g, flat=False, deps=(tok2,))
    dw13 = mm_cols_grad(f"{tag}_dw13", u, dgu, flat=False)
    dh_in, dss, dgn = norm_mod_bwd(f"{tag}_norm_bwd", du, h, gn, modtab, s, dh, dm)
    tok13 = scatter(f"{tag}_scatter_w13", {keys[0]: dw13})
    return dh_in, (tok13,), dss, dgate, dgn


def kernel(x, c, ctx, c_ctx, ada_w, ada_b, norm_g, ffn1_w13, ffn1_w2, w_in, b_merge, rnn_conv_w, rnn_conv_b, lru_w_a, lru_b_a, lru_w_x, lru_b_x, lru_lambda, sc_conv_w, attn_sink, w_branch, w_out, ffn2_w13, ffn2_w2, final_norm_g, loss_target, m_c_ctx, m_ada_w, m_ada_b, m_norm_g, m_ffn1_w13, m_ffn1_w2, m_w_in, m_b_merge, m_rnn_conv_w, m_rnn_conv_b, m_lru_w_a, m_lru_b_a, m_lru_w_x, m_lru_b_x, m_lru_lambda, m_sc_conv_w, m_attn_sink, m_w_branch, m_w_out, m_ffn2_w13, m_ffn2_w2, m_final_norm_g, v_c_ctx, v_ada_w, v_ada_b, v_norm_g, v_ffn1_w13, v_ffn1_w2, v_w_in, v_b_merge, v_rnn_conv_w, v_rnn_conv_b, v_lru_w_a, v_lru_b_a, v_lru_w_x, v_lru_b_x, v_lru_lambda, v_sc_conv_w, v_attn_sink, v_w_branch, v_w_out, v_ffn2_w13, v_ffn2_w2, v_final_norm_g):
    weights = dict(c_ctx=c_ctx, ada_w=ada_w, ada_b=ada_b, norm_g=norm_g, ffn1_w13=ffn1_w13, ffn1_w2=ffn1_w2, w_in=w_in,
                   b_merge=b_merge, rnn_conv_w=rnn_conv_w, rnn_conv_b=rnn_conv_b, lru_w_a=lru_w_a, lru_b_a=lru_b_a,
                   lru_w_x=lru_w_x, lru_b_x=lru_b_x, lru_lambda=lru_lambda, sc_conv_w=sc_conv_w, attn_sink=attn_sink,
                   w_branch=w_branch, w_out=w_out, ffn2_w13=ffn2_w13, ffn2_w2=ffn2_w2, final_norm_g=final_norm_g)
    mom_m = dict(c_ctx=m_c_ctx, ada_w=m_ada_w, ada_b=m_ada_b, norm_g=m_norm_g, ffn1_w13=m_ffn1_w13, ffn1_w2=m_ffn1_w2,
                 w_in=m_w_in, b_merge=m_b_merge, rnn_conv_w=m_rnn_conv_w, rnn_conv_b=m_rnn_conv_b, lru_w_a=m_lru_w_a,
                 lru_b_a=m_lru_b_a, lru_w_x=m_lru_w_x, lru_b_x=m_lru_b_x, lru_lambda=m_lru_lambda,
                 sc_conv_w=m_sc_conv_w, attn_sink=m_attn_sink, w_branch=m_w_branch, w_out=m_w_out,
                 ffn2_w13=m_ffn2_w13, ffn2_w2=m_ffn2_w2, final_norm_g=m_final_norm_g)
    mom_v = dict(c_ctx=v_c_ctx, ada_w=v_ada_w, ada_b=v_ada_b, norm_g=v_norm_g, ffn1_w13=v_ffn1_w13, ffn1_w2=v_ffn1_w2,
                 w_in=v_w_in, b_merge=v_b_merge, rnn_conv_w=v_rnn_conv_w, rnn_conv_b=v_rnn_conv_b, lru_w_a=v_lru_w_a,
                 lru_b_a=v_lru_b_a, lru_w_x=v_lru_w_x, lru_b_x=v_lru_b_x, lru_lambda=v_lru_lambda,
                 sc_conv_w=v_sc_conv_w, attn_sink=v_attn_sink, w_branch=v_w_branch, w_out=v_w_out,
                 ffn2_w13=v_ffn2_w13, ffn2_w2=v_ffn2_w2, final_norm_g=v_final_norm_g)
    order = list(weights)

    dm = Dims()
    dm.D = D = x.shape[-1]
    dm.L = L = x.shape[1]
    dm.C = C = ctx.shape[1]
    dm.T = T = L + C
    dm.RW = RW = rnn_conv_b.shape[-1]
    dm.NB = lru_w_a.shape[2]
    H = attn_sink.shape[-1]
    dm.HQ = HQ = H * HEAD_DIM
    NZ = w_in.shape[-1] * N_CHIP
    dm.KW = KW = (NZ - 5 * RW - HQ - N_BRANCH * D) // 2
    dm.off_q = 5 * RW
    dm.off_k = dm.off_q + HQ
    dm.off_g = dm.off_k + 2 * KW
    dm.tme = _pick(C, (256, 128))
    dm.nt = T // dm.tme
    dm.nctx = C // dm.tme
    dm.cw = next(w for w in (512, 256, 128) if dm.off_g % w == 0 and D % w == 0)
    A = ada_b.shape[-1]
    Aq = ada_w.shape[-1]
    assert dm.off_q % HQ == 0 and dm.off_k % KW == 0 and HQ % KW == 0 and L % dm.tme == 0 and RW == HQ
    assert C % Q_BLOCK == 0 and L % Q_BLOCK == 0 and D % N_CHIP == 0 and A == N_MOD * D

    mx, my, mc = _my_pos()
    j_me = 2 * mx + my
    b_me = 4 * mx + 2 * my + mc

    big = ["ffn1_w13", "ffn1_w2", "w_in", "w_branch", "w_out", "ffn2_w13", "ffn2_w2"]
    j_idx = jnp.reshape(j_me, (1,)).astype(jnp.int32)
    FFq = ffn1_w2.shape[1]
    views = {"ffn1_w2": (2, 2 * FFq, D), "ffn2_w2": (2, 2 * FFq, D), "w_out": (D, D),
             "w_branch": (N_CHIP, N_BRANCH, RW, D // N_CHIP)}

    def make_land(key, deps):
        l, n = int(key[1]), key[3:]
        return cast_into_slot(f"{key}_cast", weights[n], l, j_idx, deps), views.get(n)

    small_sharded = ["norm_g", "b_merge", "rnn_conv_w", "lru_b_a", "lru_b_x", "lru_lambda", "sc_conv_w"]
    pack1 = _pack([c] + [weights[n] for n in small_sharded])
    g1 = allgather8("gather_small_params", pack1).reshape(N_DEV, -1, 128)
    parts = _unpack(g1, [c.shape] + [weights[n].shape for n in small_sharded], lead=(N_DEV,))
    c_all = parts[0].reshape(N_DEV, D)
    full = {n: _unshard_last(p[0::2]) for n, p in zip(small_sharded, parts[1:])}
    cond = jnp.concatenate([c_all, c_ctx[None, :], jnp.zeros((ADA_ROWS - N_DEV - 1, D), F32)], axis=0)

    bias_q = lax.dynamic_slice_in_dim(ada_b, j_me * Aq, Aq, axis=1)[:, None, :]
    mod_q = ada_forward("ada_forward", cond, ada_w, bias_q)
    g2 = allgather8("gather_mod", mod_q.reshape(-1, 128)).reshape(N_DEV, 2, ADA_ROWS, Aq)
    mod_full = _unshard_last(g2[0::2])
    mod_lat = lax.dynamic_index_in_dim(mod_full, b_me, axis=1, keepdims=False)
    mod_ctx = mod_full[:, N_DEV]
    modtabs = [jnp.stack([mod_ctx[l], mod_lat[l]]).reshape(2, N_MOD, 1, D) for l in range(2)]
    ws = WeightStream([f"l{l}_{n}" for l in range(2) for n in big], make_land, first_deps=(g2,))

    cos, sin = _rope_tables(dm)
    sink = attn_sink.reshape(2, 1, H)
    lw = dict(w_a=lru_w_a, w_x=lru_w_x,
              b_a=full["lru_b_a"][:, :, None, :], b_x=full["lru_b_x"][:, :, None, :],
              lam=full["lru_lambda"][:, :, None, :])
    gn = full["norm_g"]
    bm = full["b_merge"][:, :, None, :]
    rcw = full["rnn_conv_w"][:, :, None, :]
    scw = full["sc_conv_w"][:, :, None, :]

    h = jnp.concatenate([ctx[0], x[0]], axis=0)
    saved = []
    for l in range(2):
        mt = modtabs[l]
        sv = {}
        h, sv["ffn1"] = _ffn_forward(f"l{l}_ffn1", h, gn[l, 0:1], mt, 0, ws, f"l{l}_ffn1_w13", f"l{l}_ffn1_w2", dm)
        sv["h_mix"] = h
        u = norm_mod(f"l{l}_mix_norm", h, gn[l, 1:2], mt, 1, dm)
        wing = ws.get(f"l{l}_w_in", u)
        z = mm_cols(f"l{l}_w_in", u, wing, BF16, flat=True, deps=ws.take_tokens())
        xa = rnn_conv(f"l{l}_rnn_conv", z, rcw[l], rnn_conv_b[l][None, :], dm)
        scans = []
        for d in range(2):
            a_d, u_d = lru_gates(f"l{l}_lru_gates{d}", xa, lw, l, d, dm)
            h_d = lru_scan(f"l{l}_lru_scan{d}", a_d, u_d, True, d == 1, dm)
            scans.append((a_d, u_d, h_d))
        ya = rnn_out(f"l{l}_rnn_out", scans[0][2], scans[1][2], z, dm)
        yb = short_conv(f"l{l}_short_conv", z, scw[l], dm)
        qr, kr, vv = qkv_prep(f"l{l}_qkv", z, cos, sin, dm)
        yatt, lse = attention(f"l{l}_attn", qr, kr, vv, sink[l], dm)
        ys = (ya, yb, yatt)
        wbg = ws.get(f"l{l}_w_branch", yatt)
        lifted = [mm_branch(f"l{l}_lift{br}", ys[br], wbg, br, deps=ws.take_tokens()) for br in range(N_BRANCH)]
        merged = merge(f"l{l}_merge", z, lifted, bm[l], dm)
        woutg = ws.get(f"l{l}_w_out", merged)
        h, y = mm_plain(f"l{l}_w_out", merged, woutg, "nn", F32, deps=ws.take_tokens(), resid=(h, mt, 1, 1.0, dm.C))
        sv.update(u=u, z=z, xa=xa, scans=scans, ys=ys, qkv=(qr, kr, vv), lse=lse, lifted=lifted, merged=merged, y=y)
        h, sv["ffn2"] = _ffn_forward(f"l{l}_ffn2", h, gn[l, 2:3], mt, 2, ws, f"l{l}_ffn2_w13", f"l{l}_ffn2_w2", dm)
        saved.append(sv)

    dh, loss_vec, d_final_g = loss_head("loss_head", h, final_norm_g[None, :], loss_target[0], dm)
    loss = lax.psum(loss_vec[0, 0], ("x", "y", "c"))

    small = {n: [None, None] for n in ["norm_g", "b_merge", "rnn_conv_w", "rnn_conv_b", "lru_w_a", "lru_b_a", "lru_w_x",
                                       "lru_b_x", "lru_lambda", "sc_conv_w", "attn_sink"]}
    dmods = [None, None]
    scatters = []

    def scatter(name, keyed):
        grads3 = [g.reshape(N_CHIP, -1, g.shape[-1]) for g in keyed.values()]
        lands = [lax.empty((3,) + g.shape[1:], BF16) for g in grads3]
        ss, rs, bufs, token = exchange_start(f"{name}_start", grads3 + lands, _scatter_plan(len(grads3)))
        scatters.append((name, ss, rs, bufs, list(keyed)))
        return token

    tok = ()
    for l in (1, 0):
        mt = modtabs[l]
        sv = saved[l]
        dh, tok, dss2, dgate2, dgn2 = _ffn_backward(f"l{l}_ffn2", dh, sv["ffn2"], gn[l, 2:3], mt, 2,
                                                    ws, f"l{l}_ffn2_w13", f"l{l}_ffn2_w2", dm, scatter,
                                                    (("ffn2_w13", l), ("ffn2_w2", l)), deps=tok)

        dyg, dgate1 = resid_bwd(f"l{l}_mix_res_bwd", dh, sv["y"], mt, 1, 1.0, dm, deps=tok)
        woutg, wbg, wing = ws.get(f"l{l}_w_out"), ws.get(f"l{l}_w_branch"), ws.get(f"l{l}_w_in")
        dmerged = mm_plain(f"l{l}_dmerged", dyg, woutg, "nt", F32)
        mix_grads = {("w_out", l): mm_plain_grad(f"l{l}_dw_out", sv["merged"], dyg)}
        dz = lax.empty((T, NZ), BF16)
        dl0, dl1, dl2, dz, dbm = merge_bwd(f"l{l}_merge_bwd", dmerged, sv["z"], sv["lifted"], bm[l], dz, dm)
        dys = []
        for br, dl in enumerate((dl0, dl1, dl2)):
            dys.append(mm_branch_t(f"l{l}_dy{br}", dl, wbg, br))
            mix_grads[("w_branch", l, br)] = mm_branch_grad(f"l{l}_dwb{br}", sv["ys"][br], dl)
        small["b_merge"][l] = dbm[:, 0]

        qr, kr, vv = sv["qkv"]
        dq, dk, dv, dsink = attention_bwd(f"l{l}_attn_bwd", qr, kr, vv, sink[l], sv["ys"][2], sv["lse"], dys[2], dm)
        dz = qkv_bwd(f"l{l}_qkv_bwd", dq, dk, dv, cos, sin, dz, dm)
        small["attn_sink"][l] = dsink[0, :H]

        dz, dscw = short_conv_bwd(f"l{l}_short_conv_bwd", dys[1], sv["z"], scw[l], dz, dm)
        small["sc_conv_w"][l] = dscw[:, 0]

        (a0, u0, h0), (a1, u1, h1) = sv["scans"]
        dhs, drg = rnn_out_bwd(f"l{l}_rnn_out_bwd", dys[0], h0, h1, sv["z"], dm)
        dxa, lru_sums = [], []
        for d, (a_d, u_d, h_d) in enumerate(sv["scans"]):
            lam_d, dla_d = lru_scan_bwd(f"l{l}_lru_scan_bwd{d}", a_d, u_d, h_d, dhs, False, d == 0, dm)
            outs = lru_gates_bwd(f"l{l}_lru_gates_bwd{d}", sv["xa"], lw, l, d, lam_d, dla_d, dm)
            dxa.append(outs[0])
            lru_sums.append(outs[1:])
        dz, drcw, drcb = rnn_conv_bwd(f"l{l}_rnn_conv_bwd", dxa[0], dxa[1], drg, sv["z"], rcw[l], dz, dm)
        small["rnn_conv_w"][l] = drcw[:, 0]
        small["rnn_conv_b"][l] = drcb[0]
        for i, n in enumerate(["lru_w_a", "lru_b_a", "lru_w_x", "lru_b_x", "lru_lambda"]):
            small[n][l] = jnp.stack([lru_sums[0][i], lru_sums[1][i]]).reshape((2,) + weights[n].shape[2:-1] + (-1,))

        du = mm_cols_t(f"l{l}_du_mix", dz, wing, flat=True)
        mix_grads[("w_in", l)] = mm_cols_grad(f"l{l}_dw_in", sv["u"], dz, flat=True)
        dh, dss1, dgn1 = norm_mod_bwd(f"l{l}_mix_norm_bwd", du, sv["h_mix"], gn[l, 1:2], mt, 1, dh, dm)
        tok = (scatter(f"l{l}_scatter_mix", mix_grads),)

        dh, tok, dss0, dgate0, dgn0 = _ffn_backward(f"l{l}_ffn1", dh, sv["ffn1"], gn[l, 0:1], mt, 0,
                                                    ws, f"l{l}_ffn1_w13", f"l{l}_ffn1_w2", dm, scatter,
                                                    (("ffn1_w13", l), ("ffn1_w2", l)), deps=tok)
        small["norm_g"][l] = jnp.concatenate([dgn0, dgn1, dgn2], axis=0)
        dmods[l] = jnp.concatenate([dss0, dgate0, dss1, dgate1, dss2, dgate2], axis=1).reshape(2, A)

    grad_x = dh[C:][None]

    pack_mod = jnp.stack([jnp.stack([dmods[l][1], dmods[l][0]]) for l in range(2)])
    g3 = allgather8("gather_dmod", pack_mod.reshape(-1, 128)).reshape(N_DEV, 2, 2, 1, A)
    dmod_full, d_ada_b = dmod_assemble("dmod_assemble", g3)
    dmod_q = lax.dynamic_slice_in_dim(dmod_full, j_me * Aq, Aq, axis=2)
    dcond_q = ada_cond_grad("ada_cond_grad", dmod_q, ada_w)

    small_names = list(small)
    small_parts = [jnp.stack(small[n]) for n in small_names] + [d_final_g, dcond_q[N_DEV]]
    small_shapes = [p.shape for p in small_parts]
    lru_big = [small_names.index("lru_w_a"), small_names.index("lru_w_x")]
    rest_idx = [i for i in range(len(small_parts)) if i not in lru_big]
    summed = [None] * len(small_parts)
    for i in lru_big:
        buf = _pack([small_parts[i]])
        tot, _ = sum_devices(f"sum_{small_names[i]}", allgather8(f"gather_{small_names[i]}", buf).reshape(N_DEV, -1, 128))
        summed[i] = _unpack(tot, [small_shapes[i]])[0]
    buf = _pack([small_parts[i] for i in rest_idx])
    tot, chip_tot = sum_devices("sum_small_grads", allgather8("gather_small_grads", buf).reshape(N_DEV, -1, 128))
    for i, val in zip(rest_idx, _unpack(tot, [small_shapes[i] for i in rest_idx])):
        summed[i] = val
    dcond_ctx = _unpack(chip_tot, [small_shapes[i] for i in rest_idx])[-1]
    sg = jax.nn.sigmoid(c_ctx)
    grads = dict(zip(small_names, summed[:len(small_names)]))
    grads["final_norm_g"] = summed[len(small_names)][0]
    grads["c_ctx"] = dcond_ctx * (sg * (1.0 + c_ctx * (1.0 - sg)))
    grads["ada_b"] = d_ada_b[:, 0]
    for n in small_sharded:
        g = grads[n]
        q = g.shape[-1] // N_CHIP
        grads[n] = lax.dynamic_slice_in_dim(g, j_me * q, q, axis=g.ndim - 1)

    arrived = {}

    def collect(idx, after):
        name, ss, rs, bufs, keys = scatters[idx]
        done = exchange_wait(f"{name}_wait", ss, rs, bufs, _scatter_plan(len(keys)), after)
        for i, key in enumerate(keys):
            arrived[key] = (done[i], done[len(keys) + i])

    for idx in range(len(scatters) - 1):
        collect(idx, dh)
    results = {}
    last_done = dh

    def finish(swap, after):
        n, ss, rs, bufs = swap
        own, other = exchange_wait(f"swap_{n}_wait", ss, rs, bufs, _sibling_plan, after)
        results[n] = adamw(f"adamw_{n}", weights[n], mom_m[n], mom_v[n], [own, other])
        return results[n][0]

    in_flight = None
    for n in ["ffn2_w13", "ffn2_w2", "w_out", "w_branch", "w_in", "ffn1_w2", "ffn1_w13"]:
        if not any(k[0] == n and k[1] == 0 for k in arrived):
            collect(len(scatters) - 1, last_done)
        keys = sorted((k for k in arrived if k[0] == n), key=lambda k: k[1:])
        part = sum_parts(f"sum_{n}", [arrived[k] for k in keys], j_idx).reshape(weights[n].shape)
        ss, rs, bufs, token = exchange_start(f"swap_{n}_start", [part, lax.empty(part.shape, F32)], _sibling_plan)
        if in_flight is not None:
            last_done = finish(in_flight, token)
        in_flight = (n, ss, rs, bufs)
    finish(in_flight, last_done)
    results["ada_w"] = ada_update("ada_update", cond, dmod_q, ada_w, m_ada_w, v_ada_w)
    small_all = [n for n in order if n not in results]
    pk = lambda d: _pack([d[n] for n in small_all])
    outs = adamw("adamw_small", pk(weights), pk(mom_m), pk(mom_v), [pk(grads)])
    shapes_small = [weights[n].shape for n in small_all]
    unpacked = [_unpack(o, shapes_small) for o in outs]
    for i, n in enumerate(small_all):
        results[n] = tuple(unpacked[k][i] for k in range(4))

    return (loss, grad_x, *[results[n][0] for n in order], *[results[n][1] for n in order],
            *[results[n][2] for n in order], *[results[n][3] for n in order])
```
